```python
import math
import jax, jax.numpy as jnp
from jax import lax
import numpy as np

D_MODEL = 1024
BATCH = 8
SEQ = 4096
DEPTH = 1

N_META = 16
CONV_WIDTH = 3
D_CONV = D_MODEL
CONV_GROUPS = 16
SB_HEAD_DIM = 64
SB_HEADS = 16
D_ATTN = SB_HEADS * SB_HEAD_DIM
N_BRANCH = 2
D_IN = 3 * D_CONV + 3 * D_ATTN + N_BRANCH * D_MODEL
D_FF = 2816
Q_BLOCK = 128
RMS_EPS = 1e-6

kernel_name = "hybrid_shortconv_stickbreaking_convffn_block"


def rms_norm(x, g):
    xf = x.astype(jnp.float32)
    y = xf * lax.rsqrt(jnp.mean(xf * xf, axis=-1, keepdims=True) + RMS_EPS)
    return (y * g.astype(jnp.float32)).astype(x.dtype)


def causal_dwconv(x, w):
    k, c = w.shape
    return lax.conv_general_dilated(
        x, w.reshape(k, 1, c).astype(x.dtype),
        window_strides=(1,), padding=[(k - 1, 0)],
        dimension_numbers=("NWC", "WIO", "NWC"),
        feature_group_count=c)


def stick_breaking_attention(q, k, v):
    seq_len = q.shape[1]
    scale = 1.0 / math.sqrt(q.shape[-1])
    bounds = [(0, N_META)] + [(s, min(s + Q_BLOCK, seq_len))
                              for s in range(N_META, seq_len, Q_BLOCK)]
    outs = []
    for q0, q1 in bounds:
        qb = q[:, q0:q1]
        kb = k[:, :q1]
        vb = v[:, :q1].astype(jnp.float32)
        z = jnp.einsum("bqhd,bkhd->bhqk", qb, kb,
                       preferred_element_type=jnp.float32) * scale
        t_idx = jnp.arange(q0, q1)[:, None]
        s_idx = jnp.arange(q1)[None, :]
        causal = s_idx < t_idx
        log_beta = jax.nn.log_sigmoid(z)
        log_fail = jnp.where(causal, log_beta - z, 0.0)
        survive = lax.cumsum(log_fail, axis=3, reverse=True) - log_fail
        a = jnp.where(causal, jnp.exp(log_beta + survive), 0.0)
        o = jnp.einsum("bhqk,bkhd->bqhd", a, vb)
        outs.append(o.astype(v.dtype))
    return jnp.concatenate(outs, axis=1)


def hybrid_mixer(xn, w_in, conv_w_mix, w_proj_conv, w_proj_attn, b_gate, w_out):
    bsz, seq_len, _ = xn.shape
    h = xn @ w_in
    splits = np.cumsum([D_CONV, D_CONV, D_CONV, D_ATTN, D_ATTN, D_ATTN, D_MODEL])
    b_c, c_c, h_c, q, k, v, g_conv, g_attn = jnp.split(h, list(splits), axis=-1)

    y_conv = b_c * causal_dwconv(c_c * h_c, conv_w_mix)
    branch_conv = y_conv @ w_proj_conv

    hd = (bsz, seq_len, SB_HEADS, SB_HEAD_DIM)
    o = stick_breaking_attention(q.reshape(hd), k.reshape(hd), v.reshape(hd))
    branch_attn = o.reshape(bsz, seq_len, D_ATTN) @ w_proj_attn

    gate_conv = jax.nn.sigmoid(g_conv + b_gate[0])
    gate_attn = jax.nn.sigmoid(g_attn + b_gate[1])
    merged = gate_conv * branch_conv + gate_attn * branch_attn
    return merged @ w_out


def conv_gated_mlp(xn, w_up_gate, conv_w_ffn, w_down):
    u, g = jnp.split(xn @ w_up_gate, 2, axis=-1)
    u = causal_dwconv(u, conv_w_ffn)
    return (jax.nn.gelu(u) * g) @ w_down


def _fwd_setup_inputs(seed: int = 0) -> dict:
    key = jax.random.key(seed)
    ks = jax.random.split(key, 16)
    f32 = jnp.float32

    def normal(k, shape, scale):
        return jax.random.normal(k, shape, f32) * scale

    def gain(k):
        return 1.0 + normal(k, (DEPTH, D_MODEL), 0.05)

    return {
        "x": normal(ks[0], (BATCH, SEQ, D_MODEL), 1.0),
        "meta_tokens": normal(ks[1], (N_META, D_MODEL), 1.0),
        "g_pre_mix": gain(ks[2]),
        "w_in": normal(ks[3], (DEPTH, D_MODEL, D_IN), D_MODEL ** -0.5),
        "conv_w_mix": normal(ks[4], (DEPTH, CONV_WIDTH, D_CONV), CONV_WIDTH ** -0.5),
        "w_proj_conv": normal(ks[5], (DEPTH, D_CONV, D_MODEL), D_CONV ** -0.5),
        "w_proj_attn": normal(ks[6], (DEPTH, D_ATTN, D_MODEL), D_ATTN ** -0.5),
        "b_gate": normal(ks[7], (DEPTH, N_BRANCH, D_MODEL), 0.02),
        "w_out": normal(ks[8], (DEPTH, D_MODEL, D_MODEL), D_MODEL ** -0.5),
        "g_post_mix": gain(ks[9]),
        "g_pre_ffn": gain(ks[10]),
        "w_up_gate": normal(ks[11], (DEPTH, D_MODEL, 2 * D_FF), D_MODEL ** -0.5),
        "conv_w_ffn": normal(ks[12], (DEPTH, CONV_WIDTH, D_FF), CONV_WIDTH ** -0.5),
        "w_down": normal(ks[13], (DEPTH, D_FF, D_MODEL), D_FF ** -0.5),
        "g_post_ffn": gain(ks[14]),
    }


def _fwd_reference(x, meta_tokens, g_pre_mix, w_in, conv_w_mix, w_proj_conv, w_proj_attn,
              b_gate, w_out, g_post_mix, g_pre_ffn, w_up_gate, conv_w_ffn, w_down,
              g_post_ffn):
    bsz = x.shape[0]
    meta = jnp.broadcast_to(meta_tokens[None].astype(x.dtype), (bsz, N_META, D_MODEL))
    h = jnp.concatenate([meta, x], axis=1)
    for layer in range(DEPTH):
        mix = hybrid_mixer(rms_norm(h, g_pre_mix[layer]), w_in[layer], conv_w_mix[layer],
                           w_proj_conv[layer], w_proj_attn[layer], b_gate[layer],
                           w_out[layer])
        h = h + rms_norm(mix, g_post_mix[layer])
        ffn = conv_gated_mlp(rms_norm(h, g_pre_ffn[layer]), w_up_gate[layer],
                             conv_w_ffn[layer], w_down[layer])
        h = h + rms_norm(ffn, g_post_ffn[layer])
    return h[:, N_META:, :]


import jax as _jax
import jax.numpy as _jnp

TWIN_FORMAT = 'train_step'
FWD_PARAMS = ['x', 'meta_tokens', 'g_pre_mix', 'w_in', 'conv_w_mix', 'w_proj_conv', 'w_proj_attn', 'b_gate', 'w_out', 'g_post_mix', 'g_pre_ffn', 'w_up_gate', 'conv_w_ffn', 'w_down', 'g_post_ffn']
TWIN_WEIGHTS = ['meta_tokens', 'g_pre_mix', 'w_in', 'conv_w_mix', 'w_proj_conv', 'w_proj_attn', 'b_gate', 'w_out', 'g_post_mix', 'g_pre_ffn', 'w_up_gate', 'conv_w_ffn', 'w_down', 'g_post_ffn']
TWIN_DIFF_INPUT = 'x'
TWIN_INPUTS = ['x', 'meta_tokens', 'g_pre_mix', 'w_in', 'conv_w_mix', 'w_proj_conv', 'w_proj_attn', 'b_gate', 'w_out', 'g_post_mix', 'g_pre_ffn', 'w_up_gate', 'conv_w_ffn', 'w_down', 'g_post_ffn', 'loss_target', 'm_meta_tokens', 'm_g_pre_mix', 'm_w_in', 'm_conv_w_mix', 'm_w_proj_conv', 'm_w_proj_attn', 'm_b_gate', 'm_w_out', 'm_g_post_mix', 'm_g_pre_ffn', 'm_w_up_gate', 'm_conv_w_ffn', 'm_w_down', 'm_g_post_ffn', 'v_meta_tokens', 'v_g_pre_mix', 'v_w_in', 'v_conv_w_mix', 'v_w_proj_conv', 'v_w_proj_attn', 'v_b_gate', 'v_w_out', 'v_g_post_mix', 'v_g_pre_ffn', 'v_w_up_gate', 'v_conv_w_ffn', 'v_w_down', 'v_g_post_ffn']
TWIN_OUTPUTS = ['loss', 'grad_x', 'grad_meta_tokens', 'grad_g_pre_mix', 'grad_w_in', 'grad_conv_w_mix', 'grad_w_proj_conv', 'grad_w_proj_attn', 'grad_b_gate', 'grad_w_out', 'grad_g_post_mix', 'grad_g_pre_ffn', 'grad_w_up_gate', 'grad_conv_w_ffn', 'grad_w_down', 'grad_g_post_ffn', 'delta_meta_tokens', 'delta_g_pre_mix', 'delta_w_in', 'delta_conv_w_mix', 'delta_w_proj_conv', 'delta_w_proj_attn', 'delta_b_gate', 'delta_w_out', 'delta_g_post_mix', 'delta_g_pre_ffn', 'delta_w_up_gate', 'delta_conv_w_ffn', 'delta_w_down', 'delta_g_post_ffn', 'new_m_meta_tokens', 'new_m_g_pre_mix', 'new_m_w_in', 'new_m_conv_w_mix', 'new_m_w_proj_conv', 'new_m_w_proj_attn', 'new_m_b_gate', 'new_m_w_out', 'new_m_g_post_mix', 'new_m_g_pre_ffn', 'new_m_w_up_gate', 'new_m_conv_w_ffn', 'new_m_w_down', 'new_m_g_post_ffn', 'new_v_meta_tokens', 'new_v_g_pre_mix', 'new_v_w_in', 'new_v_conv_w_mix', 'new_v_w_proj_conv', 'new_v_w_proj_attn', 'new_v_b_gate', 'new_v_w_out', 'new_v_g_post_mix', 'new_v_g_pre_ffn', 'new_v_w_up_gate', 'new_v_conv_w_ffn', 'new_v_w_down', 'new_v_g_post_ffn']
TWIN_LEAF_KINDS = {'loss': 'loss', 'grad_x': 'grad_x', 'grad_meta_tokens': 'grad_w', 'grad_g_pre_mix': 'grad_w', 'grad_w_in': 'grad_w', 'grad_conv_w_mix': 'grad_w', 'grad_w_proj_conv': 'grad_w', 'grad_w_proj_attn': 'grad_w', 'grad_b_gate': 'grad_w', 'grad_w_out': 'grad_w', 'grad_g_post_mix': 'grad_w', 'grad_g_pre_ffn': 'grad_w', 'grad_w_up_gate': 'grad_w', 'grad_conv_w_ffn': 'grad_w', 'grad_w_down': 'grad_w', 'grad_g_post_ffn': 'grad_w', 'delta_meta_tokens': 'delta_w', 'delta_g_pre_mix': 'delta_w', 'delta_w_in': 'delta_w', 'delta_conv_w_mix': 'delta_w', 'delta_w_proj_conv': 'delta_w', 'delta_w_proj_attn': 'delta_w', 'delta_b_gate': 'delta_w', 'delta_w_out': 'delta_w', 'delta_g_post_mix': 'delta_w', 'delta_g_pre_ffn': 'delta_w', 'delta_w_up_gate': 'delta_w', 'delta_conv_w_ffn': 'delta_w', 'delta_w_down': 'delta_w', 'delta_g_post_ffn': 'delta_w', 'new_m_meta_tokens': 'new_m', 'new_m_g_pre_mix': 'new_m', 'new_m_w_in': 'new_m', 'new_m_conv_w_mix': 'new_m', 'new_m_w_proj_conv': 'new_m', 'new_m_w_proj_attn': 'new_m', 'new_m_b_gate': 'new_m', 'new_m_w_out': 'new_m', 'new_m_g_post_mix': 'new_m', 'new_m_g_pre_ffn': 'new_m', 'new_m_w_up_gate': 'new_m', 'new_m_conv_w_ffn': 'new_m', 'new_m_w_down': 'new_m', 'new_m_g_post_ffn': 'new_m', 'new_v_meta_tokens': 'new_v', 'new_v_g_pre_mix': 'new_v', 'new_v_w_in': 'new_v', 'new_v_conv_w_mix': 'new_v', 'new_v_w_proj_conv': 'new_v', 'new_v_w_proj_attn': 'new_v', 'new_v_b_gate': 'new_v', 'new_v_w_out': 'new_v', 'new_v_g_post_mix': 'new_v', 'new_v_g_pre_ffn': 'new_v', 'new_v_w_up_gate': 'new_v', 'new_v_conv_w_ffn': 'new_v', 'new_v_w_down': 'new_v', 'new_v_g_post_ffn': 'new_v'}


def _forward(args):
    return _fwd_reference(*[args[k] for k in FWD_PARAMS])


def _output_shape():
    out = _jax.eval_shape(lambda: _forward(_fwd_setup_inputs(0)))
    return out.shape, out.dtype

N_MICROBATCH = 1
ADAM_LR = 0.001
ADAM_B1 = 0.9
ADAM_B2 = 0.999
ADAM_EPS = 1e-08
ADAM_WD = 0.01
ADAM_STEP = 10
PER_EXAMPLE_BATCH_AXIS = {'x': 0, 'loss_target': 0}
SHARED_INPUTS = []
_WEIGHT_DTYPES = {'meta_tokens': _jnp.float32, 'g_pre_mix': _jnp.float32, 'w_in': _jnp.float32, 'conv_w_mix': _jnp.float32, 'w_proj_conv': _jnp.float32, 'w_proj_attn': _jnp.float32, 'b_gate': _jnp.float32, 'w_out': _jnp.float32, 'g_post_mix': _jnp.float32, 'g_pre_ffn': _jnp.float32, 'w_up_gate': _jnp.float32, 'conv_w_ffn': _jnp.float32, 'w_down': _jnp.float32, 'g_post_ffn': _jnp.float32}
MOMENT_SCALE = {'meta_tokens': 1.125977e-02, 'g_pre_mix': 7.619618e-01, 'w_in': 2.570555e-01, 'conv_w_mix': 4.012394e-01, 'w_proj_conv': 4.144841e-01, 'w_proj_attn': 2.547495e-01, 'b_gate': 1.508470e-01, 'w_out': 5.543618e-01, 'g_post_mix': 3.193633e+01, 'g_pre_ffn': 6.501605e-01, 'w_up_gate': 2.653493e-01, 'conv_w_ffn': 2.796317e-01, 'w_down': 5.595283e-01, 'g_post_ffn': 3.210369e+01}


def _to_microbatches(a, axis):
    t = _jnp.moveaxis(a, axis, 0)
    t = t.reshape((N_MICROBATCH, t.shape[0] // N_MICROBATCH) + t.shape[1:])
    return _jnp.moveaxis(t, 1, axis + 1)


def setup_inputs(seed: int = 0) -> dict:
    inp = _fwd_setup_inputs(seed)
    key = _jax.random.fold_in(_jax.random.key(seed), 7919)
    shape, _ = _output_shape()
    out = dict(inp)
    out["loss_target"] = _jax.random.normal(_jax.random.fold_in(key, 0), shape, _jnp.float32)
    for i, name in enumerate(TWIN_WEIGHTS):
        w = inp[name].astype(_jnp.float32)
        if MOMENT_SCALE is None:
            s = _jnp.sqrt(_jnp.mean(_jnp.square(w)) + 1e-30)
        else:
            s = MOMENT_SCALE[name]
        km, kv = _jax.random.split(_jax.random.fold_in(key, i + 1))
        out[name] = w
        out["m_" + name] = s * _jax.random.normal(km, w.shape, _jnp.float32)
        out["v_" + name] = (s * s) * _jax.random.uniform(kv, w.shape, _jnp.float32, 0.5, 1.5)
    if N_MICROBATCH > 1:
        for name, axis in PER_EXAMPLE_BATCH_AXIS.items():
            out[name] = _to_microbatches(out[name], axis)
    return {'x': out['x'], 'meta_tokens': out['meta_tokens'], 'g_pre_mix': out['g_pre_mix'], 'w_in': out['w_in'], 'conv_w_mix': out['conv_w_mix'], 'w_proj_conv': out['w_proj_conv'], 'w_proj_attn': out['w_proj_attn'], 'b_gate': out['b_gate'], 'w_out': out['w_out'], 'g_post_mix': out['g_post_mix'], 'g_pre_ffn': out['g_pre_ffn'], 'w_up_gate': out['w_up_gate'], 'conv_w_ffn': out['conv_w_ffn'], 'w_down': out['w_down'], 'g_post_ffn': out['g_post_ffn'], 'loss_target': out['loss_target'], 'm_meta_tokens': out['m_meta_tokens'], 'm_g_pre_mix': out['m_g_pre_mix'], 'm_w_in': out['m_w_in'], 'm_conv_w_mix': out['m_conv_w_mix'], 'm_w_proj_conv': out['m_w_proj_conv'], 'm_w_proj_attn': out['m_w_proj_attn'], 'm_b_gate': out['m_b_gate'], 'm_w_out': out['m_w_out'], 'm_g_post_mix': out['m_g_post_mix'], 'm_g_pre_ffn': out['m_g_pre_ffn'], 'm_w_up_gate': out['m_w_up_gate'], 'm_conv_w_ffn': out['m_conv_w_ffn'], 'm_w_down': out['m_w_down'], 'm_g_post_ffn': out['m_g_post_ffn'], 'v_meta_tokens': out['v_meta_tokens'], 'v_g_pre_mix': out['v_g_pre_mix'], 'v_w_in': out['v_w_in'], 'v_conv_w_mix': out['v_conv_w_mix'], 'v_w_proj_conv': out['v_w_proj_conv'], 'v_w_proj_attn': out['v_w_proj_attn'], 'v_b_gate': out['v_b_gate'], 'v_w_out': out['v_w_out'], 'v_g_post_mix': out['v_g_post_mix'], 'v_g_pre_ffn': out['v_g_pre_ffn'], 'v_w_up_gate': out['v_w_up_gate'], 'v_conv_w_ffn': out['v_conv_w_ffn'], 'v_w_down': out['v_w_down'], 'v_g_post_ffn': out['v_g_post_ffn']}


def _loss(weights, diff, rest, loss_target):
    with _jax.named_scope("forward"):
        args = {**rest, TWIN_DIFF_INPUT: diff, **{k: w.astype(_WEIGHT_DTYPES[k]) for k, w in weights.items()}}
        y = _forward(args)
    with _jax.named_scope("loss_head"):
        err = _jnp.square(y.astype(_jnp.float32) - loss_target)
        return 0.5 * _jnp.sum(_jnp.mean(err, axis=-1)) if err.ndim else 0.5 * err


def _adamw(w, g, m, v):
    m = ADAM_B1 * m + (1.0 - ADAM_B1) * g
    v = ADAM_B2 * v + (1.0 - ADAM_B2) * _jnp.square(g)
    m_hat = m / (1.0 - ADAM_B1 ** ADAM_STEP)
    v_hat = v / (1.0 - ADAM_B2 ** ADAM_STEP)
    delta = -ADAM_LR * (m_hat / (_jnp.sqrt(v_hat) + ADAM_EPS) + ADAM_WD * w)
    return delta, m, v


def reference(x, meta_tokens, g_pre_mix, w_in, conv_w_mix, w_proj_conv, w_proj_attn, b_gate, w_out, g_post_mix, g_pre_ffn, w_up_gate, conv_w_ffn, w_down, g_post_ffn, loss_target, m_meta_tokens, m_g_pre_mix, m_w_in, m_conv_w_mix, m_w_proj_conv, m_w_proj_attn, m_b_gate, m_w_out, m_g_post_mix, m_g_pre_ffn, m_w_up_gate, m_conv_w_ffn, m_w_down, m_g_post_ffn, v_meta_tokens, v_g_pre_mix, v_w_in, v_conv_w_mix, v_w_proj_conv, v_w_proj_attn, v_b_gate, v_w_out, v_g_post_mix, v_g_pre_ffn, v_w_up_gate, v_conv_w_ffn, v_w_down, v_g_post_ffn):
    given = dict(x=x, meta_tokens=meta_tokens, g_pre_mix=g_pre_mix, w_in=w_in, conv_w_mix=conv_w_mix, w_proj_conv=w_proj_conv, w_proj_attn=w_proj_attn, b_gate=b_gate, w_out=w_out, g_post_mix=g_post_mix, g_pre_ffn=g_pre_ffn, w_up_gate=w_up_gate, conv_w_ffn=conv_w_ffn, w_down=w_down, g_post_ffn=g_post_ffn, loss_target=loss_target, m_meta_tokens=m_meta_tokens, m_g_pre_mix=m_g_pre_mix, m_w_in=m_w_in, m_conv_w_mix=m_conv_w_mix, m_w_proj_conv=m_w_proj_conv, m_w_proj_attn=m_w_proj_attn, m_b_gate=m_b_gate, m_w_out=m_w_out, m_g_post_mix=m_g_post_mix, m_g_pre_ffn=m_g_pre_ffn, m_w_up_gate=m_w_up_gate, m_conv_w_ffn=m_conv_w_ffn, m_w_down=m_w_down, m_g_post_ffn=m_g_post_ffn, v_meta_tokens=v_meta_tokens, v_g_pre_mix=v_g_pre_mix, v_w_in=v_w_in, v_conv_w_mix=v_conv_w_mix, v_w_proj_conv=v_w_proj_conv, v_w_proj_attn=v_w_proj_attn, v_b_gate=v_b_gate, v_w_out=v_w_out, v_g_post_mix=v_g_post_mix, v_g_pre_ffn=v_g_pre_ffn, v_w_up_gate=v_w_up_gate, v_conv_w_ffn=v_conv_w_ffn, v_w_down=v_w_down, v_g_post_ffn=v_g_post_ffn)
    weights = {n: given[n] for n in TWIN_WEIGHTS}
    shared = {n: given[n] for n in SHARED_INPUTS}
    per_example = {n: given[n] for n in ['x']}
    grad_fn = _jax.value_and_grad(_loss, argnums=(0, 1))

    def one_microbatch(ex, loss_target):
        ex = dict(ex)
        diff = ex.pop(TWIN_DIFF_INPUT)
        return grad_fn(weights, diff, {**shared, **ex}, loss_target)

    if N_MICROBATCH == 1:
        loss, (grad_w, grad_x) = one_microbatch(per_example, given["loss_target"])
    else:
        def body(carry, xs):
            loss_sum, grad_sum = carry
            l_k, (gw_k, gx_k) = one_microbatch(xs[0], xs[1])
            with _jax.named_scope("update"):
                return (loss_sum + l_k, _jax.tree.map(_jnp.add, grad_sum, gw_k)), gx_k

        init = (_jnp.zeros((), _jnp.float32), _jax.tree.map(_jnp.zeros_like, weights))
        (loss, grad_w), grad_x = _jax.lax.scan(body, init, (per_example, given["loss_target"]))
    with _jax.named_scope("update"):
        delta_w, new_m, new_v = {}, {}, {}
        for n in TWIN_WEIGHTS:
            delta_w[n], new_m[n], new_v[n] = _adamw(weights[n], grad_w[n], given["m_" + n], given["v_" + n])
    return (loss, grad_x, *[grad_w[n] for n in TWIN_WEIGHTS], *[delta_w[n] for n in TWIN_WEIGHTS],
            *[new_m[n] for n in TWIN_WEIGHTS], *[new_v[n] for n in TWIN_WEIGHTS])
```

```python
import functools
import math

import jax
import jax.numpy as jnp
from jax import lax
from jax.experimental import pallas as pl
from jax.experimental.pallas import tpu as pltpu

F32 = jnp.float32
BF16 = jnp.bfloat16

N_META = 16
HEAD_DIM = 64
LANES = 128
RMS_EPS = 1e-6
ATT_BLOCK = 256
VMEM_LIMIT = 56 * 1024 * 1024

ADAM_LR = 0.001
ADAM_B1 = 0.9
ADAM_B2 = 0.999
ADAM_EPS = 1e-08
ADAM_WD = 0.01
ADAM_STEP = 10

MESH = pl.DeviceIdType.MESH


def _tile(n, cap, unit=LANES):
    d = (min(cap, n) // unit) * unit
    while d >= unit:
        if n % d == 0:
            return d
        d -= unit
    raise ValueError(f"no tile for {n} under {cap}")


def _params(*sem):
    return pltpu.CompilerParams(dimension_semantics=sem, vmem_limit_bytes=VMEM_LIMIT)


def _pcall(body, **kw):
    return pl.pallas_call(body, **kw)


def _dot(a, b, dims):
    return lax.dot_general(a, b, (dims, ((), ())), preferred_element_type=F32)


NN = ((1,), (0,))
NT = ((1,), (1,))
TN = ((0,), (0,))


def _matmul(a, b, *, nt=False, out_dtype=BF16, name):
    m, kdim = a.shape
    n = b.shape[0] if nt else b.shape[1]
    tm, tn, tk = _tile(m, 640, 16), _tile(n, 1536), _tile(kdim, 1536)
    nk = kdim // tk

    def body(a_ref, b_ref, o_ref, *scratch):
        p = _dot(a_ref[...], b_ref[...], NT if nt else NN)
        if nk == 1:
            o_ref[...] = p.astype(o_ref.dtype)
            return
        acc, k = scratch[0], pl.program_id(2)

        @pl.when(k == 0)
        def _():
            acc[...] = p

        @pl.when(k > 0)
        def _():
            acc[...] += p

        @pl.when(k == nk - 1)
        def _():
            o_ref[...] = acc[...].astype(o_ref.dtype)

    b_spec = pl.BlockSpec((tn, tk), lambda j, i, k: (j, k)) if nt else pl.BlockSpec((tk, tn), lambda j, i, k: (k, j))
    return _pcall(
        body, name=name, grid=(n // tn, m // tm, nk),
        in_specs=[pl.BlockSpec((tm, tk), lambda j, i, k: (i, k)), b_spec],
        out_specs=pl.BlockSpec((tm, tn), lambda j, i, k: (i, j)),
        out_shape=jax.ShapeDtypeStruct((m, n), out_dtype),
        scratch_shapes=[pltpu.VMEM((tm, tn), F32)] if nk > 1 else [],
        compiler_params=_params("arbitrary", "arbitrary", "arbitrary"),
    )(a, b)


def _matmul_tn(a, b, *, name):
    t, ka = a.shape
    nb = b.shape[1]
    ta, tb, tt = _tile(ka, 1408), _tile(nb, 1024), _tile(t, 640, 16)

    def body(a_ref, b_ref, o_ref):
        p = _dot(a_ref[...], b_ref[...], TN)
        k = pl.program_id(2)

        @pl.when(k == 0)
        def _():
            o_ref[...] = p

        @pl.when(k > 0)
        def _():
            o_ref[...] += p

    return _pcall(
        body, name=name, grid=(ka // ta, nb // tb, t // tt),
        in_specs=[pl.BlockSpec((tt, ta), lambda i, j, k: (k, i)), pl.BlockSpec((tt, tb), lambda i, j, k: (k, j))],
        out_specs=pl.BlockSpec((ta, tb), lambda i, j, k: (i, j)),
        out_shape=jax.ShapeDtypeStruct((ka, nb), F32),
        compiler_params=_params("arbitrary", "arbitrary", "arbitrary"),
    )(a, b)


def _matmul_rows(pairs, rows, vecs, epi, outs, *, nt=False, name):
    m, kdim = pairs[0][0].shape
    n = pairs[0][1].shape[0] if nt else pairs[0][1].shape[1]
    tm, tk = _tile(m, 640, 16), _tile(kdim, 1536)
    nk, npair = kdim // tk, len(pairs)

    def body(*refs):
        a_refs, b_refs = refs[0:2 * npair:2], refs[1:2 * npair:2]
        pos = 2 * npair
        row_refs = refs[pos:pos + len(rows)]
        pos += len(rows)
        vec_refs = refs[pos:pos + len(vecs)]
        pos += len(vecs)
        out_refs = refs[pos:pos + len(outs)]
        accs = refs[pos + len(outs):]
        i, k = pl.program_id(0), pl.program_id(1)
        prods = [_dot(a[...], b[...], NT if nt else NN) for a, b in zip(a_refs, b_refs)]
        if nk > 1:
            @pl.when(k == 0)
            def _():
                for acc, p in zip(accs, prods):
                    acc[...] = p

            @pl.when(k > 0)
            def _():
                for acc, p in zip(accs, prods):
                    acc[...] += p

        @pl.when(k == nk - 1)
        def _():
            vals = [acc[...] for acc in accs] if nk > 1 else prods
            res = epi(vals, [r[...] for r in row_refs], [v[...] for v in vec_refs], i * tm)
            for o_ref, spec, val in zip(out_refs, outs, res):
                if spec[0] == "row":
                    o_ref[...] = val.astype(o_ref.dtype)
                else:
                    @pl.when(i == 0)
                    def _():
                        o_ref[...] = val

                    @pl.when(i > 0)
                    def _():
                        o_ref[...] += val

    in_specs, args = [], []
    for a, b in pairs:
        in_specs += [pl.BlockSpec((tm, tk), lambda i, k: (i, k)),
                     pl.BlockSpec((n, tk), lambda i, k: (0, k)) if nt else pl.BlockSpec((tk, n), lambda i, k: (k, 0))]
        args += [a, b]
    for arr, width, cb in rows:
        in_specs.append(pl.BlockSpec((tm, width), functools.partial(lambda i, k, cb: (i, cb), cb=cb)))
        args.append(arr)
    for v in vecs:
        in_specs.append(pl.BlockSpec(v.shape, lambda i, k: (0, 0)))
        args.append(v)
    out_specs, out_shape = [], []
    for spec in outs:
        if spec[0] == "row":
            out_specs.append(pl.BlockSpec((tm, spec[1]), lambda i, k: (i, 0)))
            out_shape.append(jax.ShapeDtypeStruct((m, spec[1]), spec[2]))
        else:
            out_specs.append(pl.BlockSpec((spec[1], spec[2]), lambda i, k: (0, 0)))
            out_shape.append(jax.ShapeDtypeStruct((spec[1], spec[2]), F32))
    return _pcall(
        body, name=name, grid=(m // tm, nk), in_specs=in_specs, out_specs=out_specs, out_shape=out_shape,
        scratch_shapes=[pltpu.VMEM((tm, n), F32) for _ in pairs] if nk > 1 else [],
        compiler_params=_params("arbitrary", "arbitrary"),
    )(*args)


def _rms(v):
    return lax.rsqrt(jnp.mean(v * v, axis=-1, keepdims=True) + RMS_EPS)


def _rms_bwd(dz, vhat, r, g):
    t = dz * g
    dv = r * (t - vhat * jnp.mean(t * vhat, axis=-1, keepdims=True))
    return dv, jnp.sum(dz * vhat, axis=0, keepdims=True)


def _prenorm(h, g, *, name):
    m, d = h.shape
    tm = _tile(m, 640, 16)

    def body(h_ref, g_ref, o_ref):
        v = h_ref[...]
        o_ref[...] = (v * _rms(v) * g_ref[...]).astype(BF16)

    return _pcall(
        body, name=name, grid=(m // tm,),
        in_specs=[pl.BlockSpec((tm, d), lambda i: (i, 0)), pl.BlockSpec((1, d), lambda i: (0, 0))],
        out_specs=pl.BlockSpec((tm, d), lambda i: (i, 0)),
        out_shape=jax.ShapeDtypeStruct((m, d), BF16),
        compiler_params=_params("arbitrary"),
    )(h, g)


def _shift_down(u, k):
    rows = lax.broadcasted_iota(jnp.int32, u.shape, 0)
    return jnp.where(rows >= k, pltpu.roll(u, k, 0), 0.0)


def _shift_up(u, k):
    n = u.shape[0]
    rows = lax.broadcasted_iota(jnp.int32, u.shape, 0)
    return jnp.where(rows < n - k, pltpu.roll(u, n - k, 0), 0.0)


def _conv(u, w):
    return w[2:3] * u + w[1:2] * _shift_down(u, 1) + w[0:1] * _shift_down(u, 2)


def _conv_bwd(dcu, u, w):
    du = w[2:3] * dcu + w[1:2] * _shift_up(dcu, 1) + w[0:1] * _shift_up(dcu, 2)
    dw = [jnp.sum(dcu * _shift_down(u, 2 - k) if k < 2 else dcu * u, axis=0, keepdims=True) for k in range(3)]
    return du, dw


def _strip(arr, t, cb0):
    return pl.BlockSpec((t, LANES), functools.partial(lambda s, cb0: (0, cb0 + s), cb0=cb0))


def _mixer_conv_fwd(hin, w, d):
    t, ns = hin.shape[0], d // LANES

    def body(b_ref, c_ref, h_ref, w_ref, y_ref):
        u = c_ref[...].astype(F32) * h_ref[...].astype(F32)
        y_ref[...] = (b_ref[...].astype(F32) * _conv(u, w_ref[...])).astype(BF16)

    return _pcall(
        body, name="mixer_conv_fwd", grid=(ns,),
        in_specs=[_strip(hin, t, 0), _strip(hin, t, ns), _strip(hin, t, 2 * ns), pl.BlockSpec((3, LANES), lambda s: (0, s))],
        out_specs=pl.BlockSpec((t, LANES), lambda s: (0, s)),
        out_shape=jax.ShapeDtypeStruct((t, d), BF16),
        compiler_params=_params("arbitrary"),
    )(hin, hin, hin, w)


def _mixer_conv_bwd(hin, dy, w, d):
    t, ns = hin.shape[0], d // LANES

    def body(b_ref, c_ref, h_ref, dy_ref, w_ref, db_ref, dc_ref, dh_ref, dw_ref):
        b, c, h, g = (r[...].astype(F32) for r in (b_ref, c_ref, h_ref, dy_ref))
        wv = w_ref[...]
        u = c * h
        db_ref[...] = (g * _conv(u, wv)).astype(BF16)
        du, dw = _conv_bwd(g * b, u, wv)
        dc_ref[...] = (du * h).astype(BF16)
        dh_ref[...] = (du * c).astype(BF16)
        for k in range(3):
            dw_ref[k:k + 1, :] = dw[k]

    col = pl.BlockSpec((t, LANES), lambda s: (0, s))
    act = jax.ShapeDtypeStruct((t, d), BF16)
    return _pcall(
        body, name="mixer_conv_bwd", grid=(ns,),
        in_specs=[_strip(hin, t, 0), _strip(hin, t, ns), _strip(hin, t, 2 * ns), col, pl.BlockSpec((3, LANES), lambda s: (0, s))],
        out_specs=[col, col, col, pl.BlockSpec((3, LANES), lambda s: (0, s))],
        out_shape=[act, act, act, jax.ShapeDtypeStruct((3, d), F32)],
        compiler_params=_params("arbitrary"),
    )(hin, hin, hin, dy, w)


GELU_C = math.sqrt(2.0 / math.pi)
GELU_A = 0.044715


def _gelu_tanh(x):
    return jnp.tanh(GELU_C * (x + GELU_A * x * x * x))


def _ffn_conv_fwd(ug, w, f):
    t, ns = ug.shape[0], f // LANES

    def body(u_ref, g_ref, w_ref, o_ref):
        cu = _conv(u_ref[...].astype(F32), w_ref[...])
        o_ref[...] = (0.5 * cu * (1.0 + _gelu_tanh(cu)) * g_ref[...].astype(F32)).astype(BF16)

    return _pcall(
        body, name="ffn_conv_fwd", grid=(ns,),
        in_specs=[_strip(ug, t, 0), _strip(ug, t, ns), pl.BlockSpec((3, LANES), lambda s: (0, s))],
        out_specs=pl.BlockSpec((t, LANES), lambda s: (0, s)),
        out_shape=jax.ShapeDtypeStruct((t, f), BF16),
        compiler_params=_params("arbitrary"),
    )(ug, ug, w)


def _ffn_conv_bwd(ug, df, w, f):
    t, ns = ug.shape[0], f // LANES

    def body(u_ref, g_ref, df_ref, w_ref, du_ref, dg_ref, dw_ref):
        u, g, d = (r[...].astype(F32) for r in (u_ref, g_ref, df_ref))
        wv = w_ref[...]
        cu = _conv(u, wv)
        th = _gelu_tanh(cu)
        dg_ref[...] = (d * 0.5 * cu * (1.0 + th)).astype(BF16)
        dgelu = 0.5 * (1.0 + th) + 0.5 * cu * (1.0 - th * th) * GELU_C * (1.0 + 3.0 * GELU_A * cu * cu)
        du, dw = _conv_bwd(d * g * dgelu, u, wv)
        du_ref[...] = du.astype(BF16)
        for k in range(3):
            dw_ref[k:k + 1, :] = dw[k]

    col = pl.BlockSpec((t, LANES), lambda s: (0, s))
    act = jax.ShapeDtypeStruct((t, f), BF16)
    return _pcall(
        body, name="ffn_conv_bwd", grid=(ns,),
        in_specs=[_strip(ug, t, 0), _strip(ug, t, ns), col, pl.BlockSpec((3, LANES), lambda s: (0, s))],
        out_specs=[col, col, pl.BlockSpec((3, LANES), lambda s: (0, s))],
        out_shape=[act, act, jax.ShapeDtypeStruct((3, f), F32)],
        compiler_params=_params("arbitrary"),
    )(ug, ug, df, w)


def _log_terms(z):
    soft = jnp.log(1.0 + jnp.exp(-jnp.abs(z)))
    return jnp.minimum(z, 0.0) - soft, -jnp.maximum(z, 0.0) - soft


def _split(v):
    hi = v.astype(BF16)
    return hi, (v - hi.astype(F32)).astype(BF16)


def _head_masks():
    lane = lax.broadcasted_iota(jnp.int32, (1, LANES), 1)
    return lane < HEAD_DIM, lane >= HEAD_DIM


def _attention_fwd(hin, d):
    t, blk, nd = hin.shape[0], ATT_BLOCK, d // LANES
    nq = t // blk
    scale = 1.0 / math.sqrt(HEAD_DIM)
    tri = jnp.tril(jnp.ones((blk, blk), F32), -1).astype(BF16)

    def body(q_ref, k_ref, v_ref, tri_ref, o_ref, tot_ref):
        i = pl.program_id(1)
        masks = _head_masks()
        q = q_ref[...].astype(F32) * scale
        qh = [jnp.where(mh, q, 0.0).astype(BF16) for mh in masks]
        diff = lax.broadcasted_iota(jnp.int32, (blk, blk), 0) - lax.broadcasted_iota(jnp.int32, (blk, blk), 1)
        tri_v = tri_ref[...]

        def step(it, carry):
            cs, acc = list(carry[:2]), carry[2]
            j = i - it
            j0 = pl.multiple_of(j * blk, blk)
            kj, vj = k_ref[pl.ds(j0, blk), :], v_ref[pl.ds(j0, blk), :]
            causal = diff > (j - i) * blk
            for h in range(2):
                z = _dot(qh[h], kj, NT)
                lb, lf = _log_terms(z)
                lf = jnp.where(causal, lf, 0.0)
                hi, lo = _split(lf)
                inner = _dot(hi, tri_v, NN) + _dot(lo, tri_v, NN)
                a = jnp.where(causal, jnp.exp(lb + inner + cs[h]), 0.0)
                cs[h] = cs[h] + inner[:, 0:1] + lf[:, 0:1]
                acc = acc + _dot(a.astype(BF16), jnp.where(masks[h], vj, jnp.zeros_like(vj)), NN)
            return cs[0], cs[1], acc

        zero = jnp.zeros((blk, 1), F32)
        c0, c1, acc = lax.fori_loop(0, i + 1, step, (zero, zero, jnp.zeros((blk, LANES), F32)))
        o_ref[...] = acc.astype(BF16)
        tot_ref[...] = jnp.where(masks[0], c0, c1)

    def whole(cb0):
        return pl.BlockSpec((t, LANES), functools.partial(lambda p, i, cb0: (0, cb0 + p), cb0=cb0))

    tile = pl.BlockSpec((blk, LANES), lambda p, i: (i, p))
    return _pcall(
        body, name="attention_fwd", grid=(nd, nq),
        in_specs=[pl.BlockSpec((blk, LANES), lambda p, i: (i, 3 * nd + p)), whole(4 * nd), whole(5 * nd),
                  pl.BlockSpec((blk, blk), lambda p, i: (0, 0))],
        out_specs=[tile, tile],
        out_shape=[jax.ShapeDtypeStruct((t, d), BF16), jax.ShapeDtypeStruct((t, d), F32)],
        compiler_params=_params("arbitrary", "arbitrary"),
    )(hin, hin, hin, tri)


def _attention_bwd(hin, tot, do, d):
    t, blk, nd = hin.shape[0], ATT_BLOCK, d // LANES
    nq = t // blk
    scale = 1.0 / math.sqrt(HEAD_DIM)
    upper = jnp.triu(jnp.ones((blk, blk), F32), 1).astype(BF16)
    lower = jnp.tril(jnp.ones((blk, blk), F32), -1).astype(BF16)

    def body(q_ref, k_ref, v_ref, tot_ref, do_ref, up_ref, low_ref, dq_ref, dk_ref, dv_ref, dk_acc, dv_acc):
        i = pl.program_id(1)
        masks = _head_masks()

        @pl.when(i == 0)
        def _():
            dk_acc[...] = jnp.zeros_like(dk_acc)
            dv_acc[...] = jnp.zeros_like(dv_acc)

        q = q_ref[...].astype(F32) * scale
        dout = do_ref[...]
        qh = [jnp.where(mh, q, 0.0).astype(BF16) for mh in masks]
        doh = [jnp.where(mh, dout, jnp.zeros_like(dout)) for mh in masks]
        lane = lax.broadcasted_iota(jnp.int32, (8, LANES), 1)
        totals = [lax.dot_general(jnp.where(lane == h * HEAD_DIM, 1.0, 0.0), tot_ref[...], (NT, ((), ())),
                                  precision=lax.Precision.HIGHEST, preferred_element_type=F32)[0:1, :] for h in range(2)]
        diff = lax.broadcasted_iota(jnp.int32, (blk, blk), 1) - lax.broadcasted_iota(jnp.int32, (blk, blk), 0)
        up, low = up_ref[...], low_ref[...]

        def step(j, carry):
            rest, before, dq = list(carry[0:2]), list(carry[2:4]), carry[4]
            j0 = pl.multiple_of(j * blk, blk)
            kj, vj = k_ref[pl.ds(j0, blk), :], v_ref[pl.ds(j0, blk), :]
            causal = diff > (j - i) * blk
            dk_new = jnp.zeros((blk, LANES), F32)
            dv_new = jnp.zeros((blk, LANES), F32)
            for h in range(2):
                z = _dot(kj, qh[h], NT)
                lb, lf = _log_terms(z)
                lf = jnp.where(causal, lf, 0.0)
                hi, lo = _split(lf)
                inner = _dot(up, hi, NN) + _dot(up, lo, NN)
                rest[h] = rest[h] - (inner[0:1, :] + lf[0:1, :])
                a = jnp.where(causal, jnp.exp(lb + inner + rest[h]), 0.0)
                g = _dot(vj, doh[h], NT) * a
                earlier = _dot(low, g.astype(BF16), NN)
                dlf = earlier + before[h]
                before[h] = before[h] + earlier[blk - 1:blk, :] + g[blk - 1:blk, :]
                dz = jnp.where(causal, g - jnp.exp(lb) * (g + dlf), 0.0).astype(BF16)
                dk_new = dk_new + _dot(dz, qh[h], NN)
                dv_new = dv_new + _dot(a.astype(BF16), doh[h], NN)
                dq = dq + _dot(dz, jnp.where(masks[h], kj, jnp.zeros_like(kj)), TN)
            dk_acc[pl.ds(j0, blk), :] += dk_new
            dv_acc[pl.ds(j0, blk), :] += dv_new
            return rest[0], rest[1], before[0], before[1], dq

        zero = jnp.zeros((1, blk), F32)
        res = lax.fori_loop(0, i + 1, step, (totals[0], totals[1], zero, zero, jnp.zeros((blk, LANES), F32)))
        dq_ref[...] = (res[4] * scale).astype(BF16)

        @pl.when(i == nq - 1)
        def _():
            dk_ref[...] = dk_acc[...].astype(BF16)
            dv_ref[...] = dv_acc[...].astype(BF16)

    def whole(cb0):
        return pl.BlockSpec((t, LANES), functools.partial(lambda p, i, cb0: (0, cb0 + p), cb0=cb0))

    tile = pl.BlockSpec((blk, LANES), lambda p, i: (i, p))
    const = pl.BlockSpec((blk, blk), lambda p, i: (0, 0))
    act = jax.ShapeDtypeStruct((t, d), BF16)
    return _pcall(
        body, name="attention_bwd", grid=(nd, nq),
        in_specs=[pl.BlockSpec((blk, LANES), lambda p, i: (i, 3 * nd + p)), whole(4 * nd), whole(5 * nd), tile, tile, const, const],
        out_specs=[tile, whole(0), whole(0)],
        out_shape=[act, act, act],
        scratch_shapes=[pltpu.VMEM((t, LANES), F32), pltpu.VMEM((t, LANES), F32)],
        compiler_params=_params("arbitrary", "arbitrary"),
    )(hin, hin, hin, tot, do, upper, lower)


def _adamw(w, g, m, v, *, name):
    r, c = w.shape
    tr = _tile(r, 256, 8) if r % 8 == 0 and r * c * 4 > (1 << 20) else r
    c1, c2 = 1.0 - ADAM_B1 ** ADAM_STEP, 1.0 - ADAM_B2 ** ADAM_STEP

    def body(w_ref, g_ref, m_ref, v_ref, d_ref, nm_ref, nv_ref):
        gv = g_ref[...]
        nm = ADAM_B1 * m_ref[...] + (1.0 - ADAM_B1) * gv
        nv = ADAM_B2 * v_ref[...] + (1.0 - ADAM_B2) * (gv * gv)
        d_ref[...] = -ADAM_LR * ((nm / c1) / (jnp.sqrt(nv / c2) + ADAM_EPS) + ADAM_WD * w_ref[...])
        nm_ref[...] = nm
        nv_ref[...] = nv

    spec = pl.BlockSpec((tr, c), lambda i: (i, 0))
    shape = jax.ShapeDtypeStruct((r, c), F32)
    return _pcall(
        body, name=name, grid=(r // tr,), in_specs=[spec] * 4, out_specs=[spec] * 3, out_shape=[shape] * 3,
        compiler_params=_params("arbitrary"),
    )(w, g, m, v)


HBM = pl.BlockSpec(memory_space=pltpu.HBM)


def _place():
    x, y, c = lax.axis_index("x"), lax.axis_index("y"), lax.axis_index("c")
    return x, y, c, [(1 - x, y), (x, 1 - y), (1 - x, 1 - y)]


def _window(ref, kind, shard_shape, j, half):
    r, w = shard_shape
    if kind == "col":
        return ref.at[pl.ds(half * (r // 2), r // 2), pl.ds(j * w, w)]
    return ref.at[pl.ds(j * r + half * (r // 2), r // 2), :]


def _full_shape(kind, shard_shape):
    r, w = shard_shape
    return (r, 4 * w) if kind == "col" else (4 * r, w)


def _remote(src, dst, send_sem, recv_sem, device):
    return pltpu.make_async_remote_copy(src_ref=src, dst_ref=dst, send_sem=send_sem, recv_sem=recv_sem,
                                        device_id=device, device_id_type=MESH)


def _gather_weights(bigs, kinds, smalls):
    nb, ns = len(bigs), len(smalls)
    shapes = [b.shape for b in bigs]

    def body(*refs):
        big_in, small_in = refs[:nb], refs[nb:nb + ns]
        big_out, small_out = refs[nb + ns:2 * nb + ns], refs[2 * nb + ns:2 * (nb + ns)]
        send_sems, recv_sems, local_sems = refs[2 * (nb + ns):]
        x, y, c, chips = _place()
        me, sibling = 2 * x + y, (x, y, 1 - c)
        local, sent = [], []
        for a in range(nb):
            r, w = shapes[a]
            own = big_out[a].at[:, pl.ds(me * w, w)] if kinds[a] == "col" else big_out[a].at[pl.ds(me * r, r), :]
            local.append(pltpu.make_async_copy(big_in[a], own, local_sems.at[a]))
            for k, chip in enumerate(chips):
                sent.append(_remote(big_in[a].at[pl.ds(c * (r // 2), r // 2), :], _window(big_out[a], kinds[a], shapes[a], me, c),
                                    send_sems.at[a, k], recv_sems.at[a, k], (*chip, c)))
        for s in range(ns):
            local.append(pltpu.make_async_copy(small_in[s], small_out[s].at[me], local_sems.at[nb + s]))
            for k, chip in enumerate(chips):
                sent.append(_remote(small_in[s], small_out[s].at[me], send_sems.at[nb + s, k], recv_sems.at[nb + s, k], (*chip, c)))
        for cp in local + sent:
            cp.start()
        for a in range(nb):
            for k, (px, py) in enumerate(chips):
                win = _window(big_out[a], kinds[a], shapes[a], 2 * px + py, c)
                _remote(win, win, send_sems.at[a, k], recv_sems.at[a, k], (px, py, c)).wait_recv()
                fwd = _remote(win, win, send_sems.at[a, 3 + k], recv_sems.at[a, 3 + k], sibling)
                fwd.start()
                sent.append(fwd)
        for s in range(ns):
            for k, (px, py) in enumerate(chips):
                dst = small_out[s].at[2 * px + py]
                _remote(dst, dst, send_sems.at[nb + s, k], recv_sems.at[nb + s, k], (px, py, c)).wait_recv()
        for a in range(nb):
            for k, (px, py) in enumerate(chips):
                win = _window(big_out[a], kinds[a], shapes[a], 2 * px + py, 1 - c)
                _remote(win, win, send_sems.at[a, 3 + k], recv_sems.at[a, 3 + k], sibling).wait_recv()
        for cp in sent:
            cp.wait_send()
        for cp in local:
            cp.wait()

    out_shape = [jax.ShapeDtypeStruct(_full_shape(k, s), b.dtype) for b, k, s in zip(bigs, kinds, shapes)]
    out_shape += [jax.ShapeDtypeStruct((4, *s.shape), s.dtype) for s in smalls]
    return _pcall(
        body, name="gather_weights", in_specs=[HBM] * (nb + ns), out_specs=[HBM] * (nb + ns), out_shape=out_shape,
        scratch_shapes=[pltpu.SemaphoreType.DMA((nb + ns, 6)), pltpu.SemaphoreType.DMA((nb + ns, 6)), pltpu.SemaphoreType.DMA((nb + ns,))],
    )(*bigs, *smalls)


def _swap_other_halves(grads, kinds, shapes):
    n = len(grads)

    def body(*refs):
        g_in, land = refs[:n], refs[n:2 * n]
        send_sems, recv_sems = refs[2 * n:]
        x, y, c, _ = _place()
        sibling = (x, y, 1 - c)
        sent = []
        for a in range(n):
            for j in range(4):
                cp = _remote(_window(g_in[a], kinds[a], shapes[a], j, 1 - c), land[a].at[j], send_sems.at[a, j], recv_sems.at[a, j], sibling)
                cp.start()
                sent.append(cp)
        for a in range(n):
            for j in range(4):
                _remote(land[a].at[j], land[a].at[j], send_sems.at[a, j], recv_sems.at[a, j], sibling).wait_recv()
        for cp in sent:
            cp.wait_send()

    return _pcall(
        body, name="swap_other_halves", in_specs=[HBM] * n, out_specs=[HBM] * n,
        out_shape=[jax.ShapeDtypeStruct((4, s[0] // 2, s[1]), F32) for s in shapes],
        scratch_shapes=[pltpu.SemaphoreType.DMA((n, 4)), pltpu.SemaphoreType.DMA((n, 4))],
    )(*grads)


def _pair_sum(place, g, land, kind, shard_shape, *, name):
    r, w = shard_shape
    hr = r // 2
    tr = _tile(hr, 256, 8)
    nr = hr // tr

    def body(place_ref, g_ref, l_ref, o_ref):
        o_ref[...] = g_ref[...] + l_ref[...]

    if kind == "col":
        g_spec = pl.BlockSpec((tr, w), lambda j, i, p: (p[0] * nr + i, j))
    else:
        g_spec = pl.BlockSpec((tr, w), lambda j, i, p: ((2 * j + p[0]) * nr + i, 0))
    part = pl.BlockSpec((None, tr, w), lambda j, i, p: (j, i, 0))
    return _pcall(
        body, name=name,
        grid_spec=pltpu.PrefetchScalarGridSpec(num_scalar_prefetch=1, grid=(4, nr), in_specs=[g_spec, part], out_specs=part),
        out_shape=jax.ShapeDtypeStruct((4, hr, w), F32),
        compiler_params=_params("arbitrary", "arbitrary"),
    )(place, g, land)


def _chip_exchange(parts):
    n = len(parts)

    def body(*refs):
        p_in, land = refs[:n], refs[n:2 * n]
        send_sems, recv_sems = refs[2 * n:]
        x, y, c, chips = _place()
        sent = []
        for a in range(n):
            for k, (px, py) in enumerate(chips):
                cp = _remote(p_in[a].at[2 * px + py], land[a].at[k], send_sems.at[a, k], recv_sems.at[a, k], (px, py, c))
                cp.start()
                sent.append(cp)
        for a in range(n):
            for k, (px, py) in enumerate(chips):
                _remote(land[a].at[k], land[a].at[k], send_sems.at[a, k], recv_sems.at[a, k], (px, py, c)).wait_recv()
        for cp in sent:
            cp.wait_send()

    return _pcall(
        body, name="chip_exchange", in_specs=[HBM] * n, out_specs=[HBM] * n,
        out_shape=[jax.ShapeDtypeStruct((3, *p.shape[1:]), F32) for p in parts],
        scratch_shapes=[pltpu.SemaphoreType.DMA((n, 3)), pltpu.SemaphoreType.DMA((n, 3))],
    )(*parts)


def _chip_sum(place, part, land, *, name):
    _, hr, w = part.shape
    tr = _tile(hr, 256, 8)
    nr = hr // tr

    def body(place_ref, p_ref, l_ref, o_ref):
        o_ref[...] = ((p_ref[...] + l_ref[0]) + l_ref[1]) + l_ref[2]

    return _pcall(
        body, name=name,
        grid_spec=pltpu.PrefetchScalarGridSpec(
            num_scalar_prefetch=1, grid=(nr,),
            in_specs=[pl.BlockSpec((None, tr, w), lambda i, p: (p[1], i, 0)), pl.BlockSpec((3, tr, w), lambda i, p: (0, i, 0))],
            out_specs=pl.BlockSpec((tr, w), lambda i, p: (p[0] * nr + i, 0))),
        out_shape=jax.ShapeDtypeStruct((2 * hr, w), F32),
        compiler_params=_params("arbitrary"),
    )(place, part, land)


def _swap_reduced_halves(halves):
    n = len(halves)

    def body(*refs):
        src, out = refs[:n], refs[n:2 * n]
        send_sems, recv_sems = refs[2 * n:]
        x, y, c, _ = _place()
        sibling = (x, y, 1 - c)
        sent = []
        for a in range(n):
            hr = out[a].shape[0] // 2
            cp = _remote(src[a].at[pl.ds(c * hr, hr), :], out[a].at[pl.ds(c * hr, hr), :], send_sems.at[a], recv_sems.at[a], sibling)
            cp.start()
            sent.append(cp)
        for a in range(n):
            hr = out[a].shape[0] // 2
            other = out[a].at[pl.ds((1 - c) * hr, hr), :]
            _remote(other, other, send_sems.at[a], recv_sems.at[a], sibling).wait_recv()
        for cp in sent:
            cp.wait_send()

    return _pcall(
        body, name="swap_reduced_halves", in_specs=[HBM] * n, out_specs=[HBM] * n,
        out_shape=[jax.ShapeDtypeStruct(h.shape, F32) for h in halves],
        input_output_aliases={a: a for a in range(n)},
        scratch_shapes=[pltpu.SemaphoreType.DMA((n,)), pltpu.SemaphoreType.DMA((n,))],
    )(*halves)


def _sum_small(packed):
    rows = packed.shape[0]

    def body(in_ref, out_ref, land, send_sems, recv_sems, local_sem):
        x, y, c, _ = _place()
        me = 4 * x + 2 * y + c
        own = pltpu.make_async_copy(in_ref, land.at[me], local_sem)
        own.start()
        flips = [(r >> 2, (r >> 1) & 1, r & 1) for r in range(1, 8)]
        sent = []
        for k, (fx, fy, fc) in enumerate(flips):
            cp = _remote(in_ref, land.at[me], send_sems.at[k], recv_sems.at[k], (x ^ fx, y ^ fy, c ^ fc))
            cp.start()
            sent.append(cp)
        for k, (fx, fy, fc) in enumerate(flips):
            src = land.at[4 * (x ^ fx) + 2 * (y ^ fy) + (c ^ fc)]
            _remote(src, src, send_sems.at[k], recv_sems.at[k], (x ^ fx, y ^ fy, c ^ fc)).wait_recv()
        for cp in sent:
            cp.wait_send()
        own.wait()
        total = land[0]
        for dev in range(1, 8):
            total = total + land[dev]
        out_ref[...] = total

    vmem = pl.BlockSpec(memory_space=pltpu.VMEM)
    return _pcall(
        body, name="sum_small", in_specs=[vmem], out_specs=vmem, out_shape=jax.ShapeDtypeStruct(packed.shape, F32),
        scratch_shapes=[pltpu.VMEM((8, rows, LANES), F32), pltpu.SemaphoreType.DMA((7,)), pltpu.SemaphoreType.DMA((7,)), pltpu.SemaphoreType.DMA],
    )(packed)


def _sigmoid(v):
    return 1.0 / (1.0 + jnp.exp(-v))


def kernel(x, meta_tokens, g_pre_mix, w_in, conv_w_mix, w_proj_conv, w_proj_attn, b_gate, w_out, g_post_mix, g_pre_ffn, w_up_gate, conv_w_ffn, w_down, g_post_ffn, loss_target, m_meta_tokens, m_g_pre_mix, m_w_in, m_conv_w_mix, m_w_proj_conv, m_w_proj_attn, m_b_gate, m_w_out, m_g_post_mix, m_g_pre_ffn, m_w_up_gate, m_conv_w_ffn, m_w_down, m_g_post_ffn, v_meta_tokens, v_g_pre_mix, v_w_in, v_conv_w_mix, v_w_proj_conv, v_w_proj_attn, v_b_gate, v_w_out, v_g_post_mix, v_g_pre_ffn, v_w_up_gate, v_conv_w_ffn, v_w_down, v_g_post_ffn):
    seq, d = x.shape[1], x.shape[2]
    f = w_down.shape[1] * 4
    real = N_META + seq
    t = -(-real // ATT_BLOCK) * ATT_BLOCK
    nd = d // LANES
    cx, cy, cc = lax.axis_index("x"), lax.axis_index("y"), lax.axis_index("c")
    chip = 2 * cx + cy
    place = jnp.stack([cc, chip]).astype(jnp.int32)

    big_names = ["w_in", "w_proj_conv", "w_proj_attn", "w_out", "w_up_gate", "w_down"]
    kinds = ["col", "row", "row", "row", "col", "row"]
    big_w = [w_in[0], w_proj_conv[0], w_proj_attn[0], w_out[0], w_up_gate[0], w_down[0]]
    small_w = [meta_tokens, conv_w_mix[0], b_gate[0], conv_w_ffn[0]]
    gathered = _gather_weights([w.astype(BF16) for w in big_w], kinds, small_w)
    wi, wpc, wpa, wo, wug, wd = gathered[:6]
    meta_f, cwm_f, bg_f, cwf_f = (jnp.moveaxis(s, 0, 1).reshape(s.shape[1], -1) for s in gathered[6:])

    h0 = jnp.concatenate([meta_f, x[0], jnp.zeros((t - real, d), F32)], axis=0)
    target = jnp.concatenate([jnp.zeros((N_META, d), F32), loss_target[0], jnp.zeros((t - real, d), F32)], axis=0)
    xn1 = _prenorm(h0, g_pre_mix, name="prenorm_mix")
    hin = _matmul(xn1, wi, name="in_proj")
    yconv = _mixer_conv_fwd(hin, cwm_f, d)
    o, att_tot = _attention_fwd(hin, d)

    def gates(gpre, bias):
        return _sigmoid(gpre[:, :d].astype(F32) + bias[0:1]), _sigmoid(gpre[:, d:].astype(F32) + bias[1:2])

    def merge_epi(accs, rows, vecs, row0):
        gate_c, gate_a = gates(rows[0], vecs[0])
        return [gate_c * accs[0] + gate_a * accs[1], accs[0], accs[1]]

    act = ("row", d, BF16)
    merged, bconv, battn = _matmul_rows([(yconv, wpc), (o, wpa)], [(hin, 2 * d, 3)], [bg_f], merge_epi, [act, act, act], name="branch_proj")

    def mix_epi(accs, rows, vecs, row0):
        mix = accs[0]
        h1 = rows[0] + mix * _rms(mix) * vecs[0]
        return [mix, h1, h1 * _rms(h1) * vecs[1]]

    mix, h1, xn2 = _matmul_rows([(merged, wo)], [(h0, d, 0)], [g_post_mix, g_pre_ffn], mix_epi,
                                [("row", d, F32), ("row", d, F32), act], name="out_proj")
    ug = _matmul(xn2, wug, name="up_proj")
    fact = _ffn_conv_fwd(ug, cwf_f, f)

    def loss_epi(accs, rows, vecs, row0):
        ffn = accs[0]
        r = _rms(ffn)
        fh = ffn * r
        h2 = rows[0] + fh * vecs[0]
        rid = row0 + lax.broadcasted_iota(jnp.int32, (ffn.shape[0], 1), 0)
        err = jnp.where((rid >= N_META) & (rid < real), h2 - rows[1], 0.0)
        dy = err * (1.0 / d)
        dffn, dg = _rms_bwd(dy, fh, r, vecs[0])
        loss = jnp.zeros((1, LANES), F32) + 0.5 * jnp.sum(err * err) / d
        return [dffn, dy, dg, loss]

    dffn, dh2, dg_post_ffn, loss_part = _matmul_rows(
        [(fact, wd)], [(h1, d, 0), (target, d, 0)], [g_post_ffn], loss_epi,
        [act, ("row", d, F32), ("acc", 1, d), ("acc", 1, LANES)], name="down_proj_loss")

    gw_down = _matmul_tn(fact, dffn, name="grad_w_down")
    df = _matmul(dffn, wd, nt=True, name="down_proj_bwd")
    du2, dg2, gcw_ffn = _ffn_conv_bwd(ug, df, cwf_f, f)
    dug = jnp.concatenate([du2, dg2], axis=1)
    gw_up = _matmul_tn(xn2, dug, name="grad_w_up_gate")

    def ffn_in_epi(accs, rows, vecs, row0):
        h1v, dyv, mixv = rows
        r3 = _rms(h1v)
        dh1n, dg3 = _rms_bwd(accs[0], h1v * r3, r3, vecs[0])
        dh1 = dyv + dh1n
        r2 = _rms(mixv)
        dmix, dgm = _rms_bwd(dh1, mixv * r2, r2, vecs[1])
        return [dh1, dmix, dg3, dgm]

    dh1, dmix, dg_pre_ffn, dg_post_mix = _matmul_rows(
        [(dug, wug)], [(h1, d, 0), (dh2, d, 0), (mix, d, 0)], [g_pre_ffn, g_post_mix], ffn_in_epi,
        [("row", d, F32), act, ("acc", 1, d), ("acc", 1, d)], nt=True, name="up_proj_bwd")
    gw_out = _matmul_tn(merged, dmix, name="grad_w_out")

    def merge_bwd_epi(accs, rows, vecs, row0):
        dm = accs[0]
        gate_c, gate_a = gates(rows[0], vecs[0])
        dpre_c = dm * rows[1].astype(F32) * gate_c * (1.0 - gate_c)
        dpre_a = dm * rows[2].astype(F32) * gate_a * (1.0 - gate_a)
        dbias = jnp.concatenate([jnp.sum(dpre_c, axis=0, keepdims=True), jnp.sum(dpre_a, axis=0, keepdims=True)], axis=0)
        return [dm * gate_c, dm * gate_a, jnp.concatenate([dpre_c, dpre_a], axis=1), dbias]

    dbconv, dbattn, dgates, gb_gate = _matmul_rows(
        [(dmix, wo)], [(hin, 2 * d, 3), (bconv, d, 0), (battn, d, 0)], [bg_f], merge_bwd_epi,
        [act, act, ("row", 2 * d, BF16), ("acc", 2, d)], nt=True, name="out_proj_bwd")
    gw_pc = _matmul_tn(yconv, dbconv, name="grad_w_proj_conv")
    gw_pa = _matmul_tn(o, dbattn, name="grad_w_proj_attn")
    dyconv = _matmul(dbconv, wpc, nt=True, name="proj_conv_bwd")
    do = _matmul(dbattn, wpa, nt=True, name="proj_attn_bwd")
    db, dc, dhc, gcw_mix = _mixer_conv_bwd(hin, dyconv, cwm_f, d)
    dq, dk, dv = _attention_bwd(hin, att_tot, do, d)
    dhin = jnp.concatenate([db, dc, dhc, dq, dk, dv, dgates], axis=1)
    gw_in = _matmul_tn(xn1, dhin, name="grad_w_in")

    def in_epi(accs, rows, vecs, row0):
        h0v, dh1v = rows
        r1 = _rms(h0v)
        dh0n, dg1 = _rms_bwd(accs[0], h0v * r1, r1, vecs[0])
        return [dh1v + dh0n, dg1]

    dh0, dg_pre_mix = _matmul_rows([(dhin, wi)], [(h0, d, 0), (dh1, d, 0)], [g_pre_mix], in_epi,
                                   [("row", d, F32), ("acc", 1, d)], nt=True, name="in_proj_bwd")
    grad_x = dh0[N_META:real][None]

    big_g = [gw_in, gw_pc, gw_pa, gw_out, gw_up, gw_down]
    shapes = [w.shape for w in big_w]
    landed = _swap_other_halves(big_g, kinds, shapes)
    parts = [_pair_sum(place, g, l, k, s, name=f"pair_sum_{n}") for g, l, k, s, n in zip(big_g, landed, kinds, shapes, big_names)]
    landed = _chip_exchange(parts)
    halves = [_chip_sum(place, p, l, name=f"chip_sum_{n}") for p, l, n in zip(parts, landed, big_names)]
    big_r = _swap_reduced_halves(halves)

    small_g = [dh0[:N_META], dg_pre_mix, gcw_mix, gb_gate, dg_post_mix, dg_pre_ffn, gcw_ffn, dg_post_ffn, loss_part]
    flat = jnp.concatenate([s.reshape(-1) for s in small_g])
    pad = -flat.shape[0] % (8 * LANES)
    summed = _sum_small(jnp.pad(flat, (0, pad)).reshape(-1, LANES)).reshape(-1)
    small_r, pos = [], 0
    for s in small_g:
        small_r.append(summed[pos:pos + s.size].reshape(s.shape))
        pos += s.size
    g_meta, g_g1, g_cwm, g_bg, g_g2, g_g3, g_cwf, g_g4, loss_row = small_r
    loss = loss_row[0, 0]

    def my_cols(full, width):
        return lax.dynamic_slice_in_dim(full, chip * width, width, axis=1)

    grads = {
        "meta_tokens": my_cols(g_meta, d // 4), "g_pre_mix": g_g1, "w_in": big_r[0][None],
        "conv_w_mix": my_cols(g_cwm, d // 4)[None], "w_proj_conv": big_r[1][None], "w_proj_attn": big_r[2][None],
        "b_gate": my_cols(g_bg, d // 4)[None], "w_out": big_r[3][None], "g_post_mix": g_g2, "g_pre_ffn": g_g3,
        "w_up_gate": big_r[4][None], "conv_w_ffn": my_cols(g_cwf, f // 4)[None], "w_down": big_r[5][None], "g_post_ffn": g_g4,
    }
    weights = dict(meta_tokens=meta_tokens, g_pre_mix=g_pre_mix, w_in=w_in, conv_w_mix=conv_w_mix, w_proj_conv=w_proj_conv,
                   w_proj_attn=w_proj_attn, b_gate=b_gate, w_out=w_out, g_post_mix=g_post_mix, g_pre_ffn=g_pre_ffn,
                   w_up_gate=w_up_gate, conv_w_ffn=conv_w_ffn, w_down=w_down, g_post_ffn=g_post_ffn)
    m_in = dict(meta_tokens=m_meta_tokens, g_pre_mix=m_g_pre_mix, w_in=m_w_in, conv_w_mix=m_conv_w_mix, w_proj_conv=m_w_proj_conv,
                w_proj_attn=m_w_proj_attn, b_gate=m_b_gate, w_out=m_w_out, g_post_mix=m_g_post_mix, g_pre_ffn=m_g_pre_ffn,
                w_up_gate=m_w_up_gate, conv_w_ffn=m_conv_w_ffn, w_down=m_w_down, g_post_ffn=m_g_post_ffn)
    v_in = dict(meta_tokens=v_meta_tokens, g_pre_mix=v_g_pre_mix, w_in=v_w_in, conv_w_mix=v_conv_w_mix, w_proj_conv=v_w_proj_conv,
                w_proj_attn=v_w_proj_attn, b_gate=v_b_gate, w_out=v_w_out, g_post_mix=v_g_post_mix, g_pre_ffn=v_g_pre_ffn,
                w_up_gate=v_w_up_gate, conv_w_ffn=v_conv_w_ffn, w_down=v_w_down, g_post_ffn=v_g_post_ffn)
    names = list(weights)
    deltas, new_m, new_v = [], [], []
    for n in names:
        shape = weights[n].shape
        two_d = (-1, shape[-1])
        dl, nm, nv = _adamw(weights[n].reshape(two_d), grads[n].reshape(two_d), m_in[n].reshape(two_d), v_in[n].reshape(two_d),
                            name=f"adamw_{n}")
        deltas.append(dl.reshape(shape))
        new_m.append(nm.reshape(shape))
        new_v.append(nv.reshape(shape))
    return (loss, grad_x, *[grads[n].reshape(weights[n].shape) for n in names], *deltas, *new_m, *new_v)
```

```python
import functools
import math

import jax
import jax.numpy as jnp
from jax import lax
from jax.experimental import pallas as pl
from jax.experimental.pallas import tpu as pltpu

F32 = jnp.float32
BF16 = jnp.bfloat16

N_META = 16
HEAD_DIM = 64
LANES = 128
RMS_EPS = 1e-6
ATT_BLOCK = 256
VMEM_LIMIT = 56 * 1024 * 1024

ADAM_LR = 0.001
ADAM_B1 = 0.9
ADAM_B2 = 0.999
ADAM_EPS = 1e-08
ADAM_WD = 0.01
ADAM_STEP = 10

MESH = pl.DeviceIdType.MESH


def _tile(n, cap, unit=LANES):
    d = (min(cap, n) // unit) * unit
    while d >= unit:
        if n % d == 0:
            return d
        d -= unit
    raise ValueError(f"no tile for {n} under {cap}")


def _params(*sem):
    return pltpu.CompilerParams(dimension_semantics=sem, vmem_limit_bytes=VMEM_LIMIT)


def _pcall(body, **kw):
    return pl.pallas_call(body, **kw)


def _dot(a, b, dims):
    return lax.dot_general(a, b, (dims, ((), ())), preferred_element_type=F32)


NN = ((1,), (0,))
NT = ((1,), (1,))
TN = ((0,), (0,))


def _matmul(a, b, *, nt=False, out_dtype=BF16, name):
    m, kdim = a.shape
    n = b.shape[0] if nt else b.shape[1]
    tm, tn, tk = _tile(m, 640, 16), _tile(n, 1536), _tile(kdim, 1536)
    nk = kdim // tk

    def body(a_ref, b_ref, o_ref, *scratch):
        p = _dot(a_ref[...], b_ref[...], NT if nt else NN)
        if nk == 1:
            o_ref[...] = p.astype(o_ref.dtype)
            return
        acc, k = scratch[0], pl.program_id(2)

        @pl.when(k == 0)
        def _():
            acc[...] = p

        @pl.when(k > 0)
        def _():
            acc[...] += p

        @pl.when(k == nk - 1)
        def _():
            o_ref[...] = acc[...].astype(o_ref.dtype)

    b_spec = pl.BlockSpec((tn, tk), lambda j, i, k: (j, k)) if nt else pl.BlockSpec((tk, tn), lambda j, i, k: (k, j))
    return _pcall(
        body, name=name, grid=(n // tn, m // tm, nk),
        in_specs=[pl.BlockSpec((tm, tk), lambda j, i, k: (i, k)), b_spec],
        out_specs=pl.BlockSpec((tm, tn), lambda j, i, k: (i, j)),
        out_shape=jax.ShapeDtypeStruct((m, n), out_dtype),
        scratch_shapes=[pltpu.VMEM((tm, tn), F32)] if nk > 1 else [],
        compiler_params=_params("arbitrary", "arbitrary", "arbitrary"),
    )(a, b)


def _matmul_tn(a, b, *, name):
    t, ka = a.shape
    nb = b.shape[1]
    ta, tb, tt = _tile(ka, 1408), _tile(nb, 1024), _tile(t, 640, 16)

    def body(a_ref, b_ref, o_ref):
        p = _dot(a_ref[...], b_ref[...], TN)
        k = pl.program_id(2)

        @pl.when(k == 0)
        def _():
            o_ref[...] = p

        @pl.when(k > 0)
        def _():
            o_ref[...] += p

    return _pcall(
        body, name=name, grid=(ka // ta, nb // tb, t // tt),
        in_specs=[pl.BlockSpec((tt, ta), lambda i, j, k: (k, i)), pl.BlockSpec((tt, tb), lambda i, j, k: (k, j))],
        out_specs=pl.BlockSpec((ta, tb), lambda i, j, k: (i, j)),
        out_shape=jax.ShapeDtypeStruct((ka, nb), F32),
        compiler_params=_params("arbitrary", "arbitrary", "arbitrary"),
    )(a, b)


def _matmul_rows(pairs, rows, vecs, epi, outs, *, nt=False, name):
    m, kdim = pairs[0][0].shape
    n = pairs[0][1].shape[0] if nt else pairs[0][1].shape[1]
    tm, tk = _tile(m, 640, 16), _tile(kdim, 1536)
    nk, npair = kdim // tk, len(pairs)

    def body(*refs):
        a_refs, b_refs = refs[0:2 * npair:2], refs[1:2 * npair:2]
        pos = 2 * npair
        row_refs = refs[pos:pos + len(rows)]
        pos += len(rows)
        vec_refs = refs[pos:pos + len(vecs)]
        pos += len(vecs)
        out_refs = refs[pos:pos + len(outs)]
        accs = refs[pos + len(outs):]
        i, k = pl.program_id(0), pl.program_id(1)
        prods = [_dot(a[...], b[...], NT if nt else NN) for a, b in zip(a_refs, b_refs)]
        if nk > 1:
            @pl.when(k == 0)
            def _():
                for acc, p in zip(accs, prods):
                    acc[...] = p

            @pl.when(k > 0)
            def _():
                for acc, p in zip(accs, prods):
                    acc[...] += p

        @pl.when(k == nk - 1)
        def _():
            vals = [acc[...] for acc in accs] if nk > 1 else prods
            res = epi(vals, [r[...] for r in row_refs], [v[...] for v in vec_refs], i * tm)
            for o_ref, spec, val in zip(out_refs, outs, res):
                if spec[0] == "row":
                    o_ref[...] = val.astype(o_ref.dtype)
                else:
                    @pl.when(i == 0)
                    def _():
                        o_ref[...] = val

                    @pl.when(i > 0)
                    def _():
                        o_ref[...] += val

    in_specs, args = [], []
    for a, b in pairs:
        in_specs += [pl.BlockSpec((tm, tk), lambda i, k: (i, k)),
                     pl.BlockSpec((n, tk), lambda i, k: (0, k)) if nt else pl.BlockSpec((tk, n), lambda i, k: (k, 0))]
        args += [a, b]
    for arr, width, cb in rows:
        in_specs.append(pl.BlockSpec((tm, width), functools.partial(lambda i, k, cb: (i, cb), cb=cb)))
        args.append(arr)
    for v in vecs:
        in_specs.append(pl.BlockSpec(v.shape, lambda i, k: (0, 0)))
        args.append(v)
    out_specs, out_shape = [], []
    for spec in outs:
        if spec[0] == "row":
            out_specs.append(pl.BlockSpec((tm, spec[1]), lambda i, k: (i, 0)))
            out_shape.append(jax.ShapeDtypeStruct((m, spec[1]), spec[2]))
        else:
            out_specs.append(pl.BlockSpec((spec[1], spec[2]), lambda i, k: (0, 0)))
            out_shape.append(jax.ShapeDtypeStruct((spec[1], spec[2]), F32))
    return _pcall(
        body, name=name, grid=(m // tm, nk), in_specs=in_specs, out_specs=out_specs, out_shape=out_shape,
        scratch_shapes=[pltpu.VMEM((tm, n), F32) for _ in pairs] if nk > 1 else [],
        compiler_params=_params("arbitrary", "arbitrary"),
    )(*args)


def _rms(v):
    return lax.rsqrt(jnp.mean(v * v, axis=-1, keepdims=True) + RMS_EPS)


def _rms_bwd(dz, vhat, r, g):
    t = dz * g
    dv = r * (t - vhat * jnp.mean(t * vhat, axis=-1, keepdims=True))
    return dv, jnp.sum(dz * vhat, axis=0, keepdims=True)


def _prenorm(h, g, *, name):
    m, d = h.shape
    tm = _tile(m, 640, 16)

    def body(h_ref, g_ref, o_ref):
        v = h_ref[...]
        o_ref[...] = (v * _rms(v) * g_ref[...]).astype(BF16)

    return _pcall(
        body, name=name, grid=(m // tm,),
        in_specs=[pl.BlockSpec((tm, d), lambda i: (i, 0)), pl.BlockSpec((1, d), lambda i: (0, 0))],
        out_specs=pl.BlockSpec((tm, d), lambda i: (i, 0)),
        out_shape=jax.ShapeDtypeStruct((m, d), BF16),
        compiler_params=_params("arbitrary"),
    )(h, g)


def _shift_down(u, k):
    rows = lax.broadcasted_iota(jnp.int32, u.shape, 0)
    return jnp.where(rows >= k, pltpu.roll(u, k, 0), 0.0)


def _shift_up(u, k):
    n = u.shape[0]
    rows = lax.broadcasted_iota(jnp.int32, u.shape, 0)
    return jnp.where(rows < n - k, pltpu.roll(u, n - k, 0), 0.0)


def _conv(u, w):
    return w[2:3] * u + w[1:2] * _shift_down(u, 1) + w[0:1] * _shift_down(u, 2)


def _conv_bwd(dcu, u, w):
    du = w[2:3] * dcu + w[1:2] * _shift_up(dcu, 1) + w[0:1] * _shift_up(dcu, 2)
    dw = [jnp.sum(dcu * _shift_down(u, 2 - k) if k < 2 else dcu * u, axis=0, keepdims=True) for k in range(3)]
    return du, dw


def _strip(arr, t, cb0):
    return pl.BlockSpec((t, LANES), functools.partial(lambda s, cb0: (0, cb0 + s), cb0=cb0))


def _mixer_conv_fwd(hin, w, d):
    t, ns = hin.shape[0], d // LANES

    def body(b_ref, c_ref, h_ref, w_ref, y_ref):
        u = c_ref[...].astype(F32) * h_ref[...].astype(F32)
        y_ref[...] = (b_ref[...].astype(F32) * _conv(u, w_ref[...])).astype(BF16)

    return _pcall(
        body, name="mixer_conv_fwd", grid=(ns,),
        in_specs=[_strip(hin, t, 0), _strip(hin, t, ns), _strip(hin, t, 2 * ns), pl.BlockSpec((3, LANES), lambda s: (0, s))],
        out_specs=pl.BlockSpec((t, LANES), lambda s: (0, s)),
        out_shape=jax.ShapeDtypeStruct((t, d), BF16),
        compiler_params=_params("arbitrary"),
    )(hin, hin, hin, w)


def _mixer_conv_bwd(hin, dy, w, d):
    t, ns = hin.shape[0], d // LANES

    def body(b_ref, c_ref, h_ref, dy_ref, w_ref, db_ref, dc_ref, dh_ref, dw_ref):
        b, c, h, g = (r[...].astype(F32) for r in (b_ref, c_ref, h_ref, dy_ref))
        wv = w_ref[...]
        u = c * h
        db_ref[...] = (g * _conv(u, wv)).astype(BF16)
        du, dw = _conv_bwd(g * b, u, wv)
        dc_ref[...] = (du * h).astype(BF16)
        dh_ref[...] = (du * c).astype(BF16)
        for k in range(3):
            dw_ref[k:k + 1, :] = dw[k]

    col = pl.BlockSpec((t, LANES), lambda s: (0, s))
    act = jax.ShapeDtypeStruct((t, d), BF16)
    return _pcall(
        body, name="mixer_conv_bwd", grid=(ns,),
        in_specs=[_strip(hin, t, 0), _strip(hin, t, ns), _strip(hin, t, 2 * ns), col, pl.BlockSpec((3, LANES), lambda s: (0, s))],
        out_specs=[col, col, col, pl.BlockSpec((3, LANES), lambda s: (0, s))],
        out_shape=[act, act, act, jax.ShapeDtypeStruct((3, d), F32)],
        compiler_params=_params("arbitrary"),
    )(hin, hin, hin, dy, w)


GELU_C = math.sqrt(2.0 / math.pi)
GELU_A = 0.044715


def _gelu_tanh(x):
    return jnp.tanh(GELU_C * (x + GELU_A * x * x * x))


def _ffn_conv_fwd(ug, w, f):
    t, ns = ug.shape[0], f // LANES

    def body(u_ref, g_ref, w_ref, o_ref):
        cu = _conv(u_ref[...].astype(F32), w_ref[...])
        o_ref[...] = (0.5 * cu * (1.0 + _gelu_tanh(cu)) * g_ref[...].astype(F32)).astype(BF16)

    return _pcall(
        body, name="ffn_conv_fwd", grid=(ns,),
        in_specs=[_strip(ug, t, 0), _strip(ug, t, ns), pl.BlockSpec((3, LANES), lambda s: (0, s))],
        out_specs=pl.BlockSpec((t, LANES), lambda s: (0, s)),
        out_shape=jax.ShapeDtypeStruct((t, f), BF16),
        compiler_params=_params("arbitrary"),
    )(ug, ug, w)


def _ffn_conv_bwd(ug, df, w, f):
    t, ns = ug.shape[0], f // LANES

    def body(u_ref, g_ref, df_ref, w_ref, du_ref, dg_ref, dw_ref):
        u, g, d = (r[...].astype(F32) for r in (u_ref, g_ref, df_ref))
        wv = w_ref[...]
        cu = _conv(u, wv)
        th = _gelu_tanh(cu)
        dg_ref[...] = (d * 0.5 * cu * (1.0 + th)).astype(BF16)
        dgelu = 0.5 * (1.0 + th) + 0.5 * cu * (1.0 - th * th) * GELU_C * (1.0 + 3.0 * GELU_A * cu * cu)
        du, dw = _conv_bwd(d * g * dgelu, u, wv)
        du_ref[...] = du.astype(BF16)
        for k in range(3):
            dw_ref[k:k + 1, :] = dw[k]

    col = pl.BlockSpec((t, LANES), lambda s: (0, s))
    act = jax.ShapeDtypeStruct((t, f), BF16)
    return _pcall(
        body, name="ffn_conv_bwd", grid=(ns,),
        in_specs=[_strip(ug, t, 0), _strip(ug, t, ns), col, pl.BlockSpec((3, LANES), lambda s: (0, s))],
        out_specs=[col, col, pl.BlockSpec((3, LANES), lambda s: (0, s))],
        out_shape=[act, act, jax.ShapeDtypeStruct((3, f), F32)],
        compiler_params=_params("arbitrary"),
    )(ug, ug, df, w)


def _log_terms(z):
    minus_abs = lax.bitcast_convert_type(lax.bitcast_convert_type(z, jnp.uint32) | jnp.uint32(0x80000000), F32)
    lb = jnp.minimum(z, 0.0) - jnp.log(1.0 + jnp.exp(minus_abs))
    return lb, lb - z


def _split_cat(v, axis):
    hi = v.astype(BF16)
    return jnp.concatenate([hi, (v - hi.astype(F32)).astype(BF16)], axis=axis)


def _head_masks():
    lane = lax.broadcasted_iota(jnp.int32, (1, LANES), 1)
    return lane < HEAD_DIM, lane >= HEAD_DIM


def _attention_fwd(hin, d):
    t, blk, nd = hin.shape[0], ATT_BLOCK, d // LANES
    nq = t // blk
    scale = 1.0 / math.sqrt(HEAD_DIM)
    tri = jnp.tril(jnp.ones((blk, blk), F32)).astype(BF16)
    tri = jnp.concatenate([tri, tri], axis=0)

    def body(q_ref, k_ref, v_ref, tri_ref, o_ref, tot_ref):
        i = pl.program_id(1)
        masks = _head_masks()
        q = q_ref[...].astype(F32) * scale
        qh = [jnp.where(mh, q, 0.0).astype(BF16) for mh in masks]
        after = lax.broadcasted_iota(jnp.int32, (blk, blk), 0) > lax.broadcasted_iota(jnp.int32, (blk, blk), 1)
        tri_v = tri_ref[...]

        def scores(j):
            j0 = pl.multiple_of(j * blk, blk)
            kj = k_ref[pl.ds(j0, blk), :]
            return tuple(_dot(qh[h], kj, NT) for h in range(2))

        def weights(z, c, diagonal, between=lambda: None):
            halves = [slice(r, r + blk // 2) for r in range(0, blk, blk // 2)]
            other = between()
            sums = []
            for h in range(2):
                for rows in halves:
                    lf = _log_terms(z[h][rows])[1]
                    if diagonal:
                        lf = jnp.where(after[rows], lf, 0.0)
                    sums.append(_dot(_split_cat(lf, 1), tri_v, NN))
            sums = [jnp.concatenate(sums[2 * h:2 * h + 2], axis=0) for h in range(2)]
            a = [jnp.exp(z[h] + sums[h] + c[h]) for h in range(2)]
            if diagonal:
                a = [jnp.where(after, ah, 0.0) for ah in a]
            return tuple(ah.astype(BF16) for ah in a), tuple(c[h] + sums[h][:, 0:1] for h in range(2)), other

        def apply(acc, a, j):
            j0 = pl.multiple_of(j * blk, blk)
            vj = v_ref[pl.ds(j0, blk), :]
            for h in range(2):
                acc = acc + _dot(a[h], jnp.where(masks[h], vj, jnp.zeros_like(vj)), NN)
            return acc

        zero = jnp.zeros((blk, 1), F32)
        a, c, _ = weights(scores(i), (zero, zero), True)

        def step(m, carry):
            z, a, c, acc = carry
            j = i - 1 - m
            a, c, (z_next, acc) = weights(z, c, False, lambda: (scores(jnp.maximum(j - 1, 0)), apply(acc, a, j + 1)))
            return z_next, a, c, acc

        _, a, c, acc = lax.fori_loop(0, i, step, (scores(jnp.maximum(i - 1, 0)), a, c, jnp.zeros((blk, LANES), F32)))
        o_ref[...] = apply(acc, a, 0).astype(BF16)
        tot_ref[...] = jnp.where(masks[0], c[0], c[1])

    def whole(cb0):
        return pl.BlockSpec((t, LANES), functools.partial(lambda p, i, cb0: (0, cb0 + p), cb0=cb0))

    tile = pl.BlockSpec((blk, LANES), lambda p, i: (i, p))
    return _pcall(
        body, name="attention_fwd", grid=(nd, nq),
        in_specs=[pl.BlockSpec((blk, LANES), lambda p, i: (i, 3 * nd + p)), whole(4 * nd), whole(5 * nd),
                  pl.BlockSpec((2 * blk, blk), lambda p, i: (0, 0))],
        out_specs=[tile, tile],
        out_shape=[jax.ShapeDtypeStruct((t, d), BF16), jax.ShapeDtypeStruct((t, d), F32)],
        compiler_params=_params("arbitrary", "arbitrary"),
    )(hin, hin, hin, tri)


def _attention_bwd(hin, tot, do, d):
    t, blk, nd = hin.shape[0], ATT_BLOCK, d // LANES
    nq = t // blk
    scale = 1.0 / math.sqrt(HEAD_DIM)
    upper = jnp.triu(jnp.ones((blk, blk), F32)).astype(BF16)
    upper = jnp.concatenate([upper, upper], axis=1)
    lower = jnp.tril(jnp.ones((blk, blk), F32), -1).astype(BF16)

    def body(q_ref, k_ref, v_ref, tot_ref, do_ref, up_ref, low_ref, dq_ref, dk_ref, dv_ref, dk_acc, dv_acc):
        i = pl.program_id(1)
        masks = _head_masks()

        @pl.when(i == 0)
        def _():
            dk_acc[...] = jnp.zeros_like(dk_acc)
            dv_acc[...] = jnp.zeros_like(dv_acc)

        q = q_ref[...].astype(F32) * scale
        dout = do_ref[...]
        qh = [jnp.where(mh, q, 0.0).astype(BF16) for mh in masks]
        doh = [jnp.where(mh, dout, jnp.zeros_like(dout)) for mh in masks]
        lane = lax.broadcasted_iota(jnp.int32, (8, LANES), 1)
        totals = [lax.dot_general(jnp.where(lane == h * HEAD_DIM, 1.0, 0.0), tot_ref[...], (NT, ((), ())),
                                  precision=lax.Precision.HIGHEST, preferred_element_type=F32)[0:1, :] for h in range(2)]
        after = lax.broadcasted_iota(jnp.int32, (blk, blk), 1) > lax.broadcasted_iota(jnp.int32, (blk, blk), 0)
        up, low = up_ref[...], low_ref[...]

        def products(j):
            j0 = pl.multiple_of(j * blk, blk)
            kj, vj = k_ref[pl.ds(j0, blk), :], v_ref[pl.ds(j0, blk), :]
            return tuple((_dot(kj, qh[h], NT), _dot(vj, doh[h], NT)) for h in range(2))

        def stage_sums(z, diagonal):
            lb, lf = _log_terms(z)
            if diagonal:
                lf = jnp.where(after, lf, 0.0)
            return jnp.exp(lb).astype(BF16), _dot(up, _split_cat(lf, 0), NN)

        def stage_weights(z, da, sums, rest, diagonal):
            rest = rest - sums[0:1, :]
            a = jnp.exp(z + sums + rest)
            if diagonal:
                a = jnp.where(after, a, 0.0)
            g = da * a
            return a.astype(BF16), g, _dot(low, g.astype(BF16), NN), rest

        def stage_dz(g, sig, earlier, before, diagonal):
            dlf = earlier + before
            dz = g - sig.astype(F32) * (g + dlf)
            if diagonal:
                dz = jnp.where(after, dz, 0.0)
            return dz.astype(BF16), dlf[blk - 1:blk, :] + g[blk - 1:blk, :]

        def grads(cur, rest, before, diagonal, between=(lambda: None, lambda: None)):
            sig, sums = zip(*[stage_sums(cur[h][0], diagonal) for h in range(2)])
            first = between[0]()
            a, g, earlier, rest = zip(*[stage_weights(*cur[h], sums[h], rest[h], diagonal) for h in range(2)])
            second = between[1]()
            dz, before = zip(*[stage_dz(g[h], sig[h], earlier[h], before[h], diagonal) for h in range(2)])
            return dz, a, rest, before, first, second

        def flush(j, dz, a, dq):
            j0 = pl.multiple_of(j * blk, blk)
            kj = k_ref[pl.ds(j0, blk), :]
            dk_new = jnp.zeros((blk, LANES), F32)
            dv_new = jnp.zeros((blk, LANES), F32)
            for h in range(2):
                dk_new = dk_new + _dot(dz[h], qh[h], NN)
                dv_new = dv_new + _dot(a[h], doh[h], NN)
                dq = dq + _dot(dz[h], jnp.where(masks[h], kj, jnp.zeros_like(kj)), TN)
            dk_acc[pl.ds(j0, blk), :] += dk_new
            dv_acc[pl.ds(j0, blk), :] += dv_new
            return dq

        def step(j, carry):
            cur, dz, a, rest, before, dq = carry
            dz, a, rest, before, nxt, dq = grads(
                cur, rest, before, False, (lambda: products(j + 1), lambda: flush(jnp.maximum(j - 1, 0), dz, a, dq)))
            return nxt, dz, a, rest, before, dq

        zero = jnp.zeros((1, blk), F32)
        nothing = jnp.zeros((blk, blk), BF16)
        cur, dz, a, rest, before, dq = lax.fori_loop(
            0, i, step, (products(0), (nothing, nothing), (nothing, nothing), tuple(totals), (zero, zero), jnp.zeros((blk, LANES), F32)))
        dq = flush(jnp.maximum(i - 1, 0), dz, a, dq)
        dz, a = grads(cur, rest, before, True)[:2]
        dq_ref[...] = (flush(i, dz, a, dq) * scale).astype(BF16)

        @pl.when(i == nq - 1)
        def _():
            dk_ref[...] = dk_acc[...].astype(BF16)
            dv_ref[...] = dv_acc[...].astype(BF16)

    def whole(cb0):
        return pl.BlockSpec((t, LANES), functools.partial(lambda p, i, cb0: (0, cb0 + p), cb0=cb0))

    tile = pl.BlockSpec((blk, LANES), lambda p, i: (i, p))
    const = pl.BlockSpec((blk, blk), lambda p, i: (0, 0))
    act = jax.ShapeDtypeStruct((t, d), BF16)
    return _pcall(
        body, name="attention_bwd", grid=(nd, nq),
        in_specs=[pl.BlockSpec((blk, LANES), lambda p, i: (i, 3 * nd + p)), whole(4 * nd), whole(5 * nd), tile, tile,
                  pl.BlockSpec((blk, 2 * blk), lambda p, i: (0, 0)), const],
        out_specs=[tile, whole(0), whole(0)],
        out_shape=[act, act, act],
        scratch_shapes=[pltpu.VMEM((t, LANES), F32), pltpu.VMEM((t, LANES), F32)],
        compiler_params=_params("arbitrary", "arbitrary"),
    )(hin, hin, hin, tot, do, upper, lower)


def _adamw(w, g, m, v, *, name):
    r, c = w.shape
    tr = _tile(r, 256, 8) if r % 8 == 0 and r * c * 4 > (1 << 20) else r
    c1, c2 = 1.0 - ADAM_B1 ** ADAM_STEP, 1.0 - ADAM_B2 ** ADAM_STEP

    def body(w_ref, g_ref, m_ref, v_ref, d_ref, nm_ref, nv_ref):
        gv = g_ref[...]
        nm = ADAM_B1 * m_ref[...] + (1.0 - ADAM_B1) * gv
        nv = ADAM_B2 * v_ref[...] + (1.0 - ADAM_B2) * (gv * gv)
        d_ref[...] = -ADAM_LR * ((nm / c1) / (jnp.sqrt(nv / c2) + ADAM_EPS) + ADAM_WD * w_ref[...])
        nm_ref[...] = nm
        nv_ref[...] = nv

    spec = pl.BlockSpec((tr, c), lambda i: (i, 0))
    shape = jax.ShapeDtypeStruct((r, c), F32)
    return _pcall(
        body, name=name, grid=(r // tr,), in_specs=[spec] * 4, out_specs=[spec] * 3, out_shape=[shape] * 3,
        compiler_params=_params("arbitrary"),
    )(w, g, m, v)


HBM = pl.BlockSpec(memory_space=pltpu.HBM)


def _place():
    x, y, c = lax.axis_index("x"), lax.axis_index("y"), lax.axis_index("c")
    return x, y, c, [(1 - x, y), (x, 1 - y), (1 - x, 1 - y)]


def _window(ref, kind, shard_shape, j, half):
    r, w = shard_shape
    if kind == "col":
        return ref.at[pl.ds(half * (r // 2), r // 2), pl.ds(j * w, w)]
    return ref.at[pl.ds(j * r + half * (r // 2), r // 2), :]


def _full_shape(kind, shard_shape):
    r, w = shard_shape
    return (r, 4 * w) if kind == "col" else (4 * r, w)


def _remote(src, dst, send_sem, recv_sem, device):
    return pltpu.make_async_remote_copy(src_ref=src, dst_ref=dst, send_sem=send_sem, recv_sem=recv_sem,
                                        device_id=device, device_id_type=MESH)


def _gather_weights(bigs, kinds, smalls):
    nb, ns = len(bigs), len(smalls)
    shapes = [b.shape for b in bigs]

    def body(*refs):
        big_in, small_in = refs[:nb], refs[nb:nb + ns]
        big_out, small_out = refs[nb + ns:2 * nb + ns], refs[2 * nb + ns:2 * (nb + ns)]
        send_sems, recv_sems, local_sems = refs[2 * (nb + ns):]
        x, y, c, chips = _place()
        me, sibling = 2 * x + y, (x, y, 1 - c)
        local, sent = [], []
        for a in range(nb):
            r, w = shapes[a]
            own = big_out[a].at[:, pl.ds(me * w, w)] if kinds[a] == "col" else big_out[a].at[pl.ds(me * r, r), :]
            local.append(pltpu.make_async_copy(big_in[a], own, local_sems.at[a]))
            for k, chip in enumerate(chips):
                sent.append(_remote(big_in[a].at[pl.ds(c * (r // 2), r // 2), :], _window(big_out[a], kinds[a], shapes[a], me, c),
                                    send_sems.at[a, k], recv_sems.at[a, k], (*chip, c)))
        for s in range(ns):
            local.append(pltpu.make_async_copy(small_in[s], small_out[s].at[me], local_sems.at[nb + s]))
            for k, chip in enumerate(chips):
                sent.append(_remote(small_in[s], small_out[s].at[me], send_sems.at[nb + s, k], recv_sems.at[nb + s, k], (*chip, c)))
        for cp in local + sent:
            cp.start()
        for a in range(nb):
            for k, (px, py) in enumerate(chips):
                win = _window(big_out[a], kinds[a], shapes[a], 2 * px + py, c)
                _remote(win, win, send_sems.at[a, k], recv_sems.at[a, k], (px, py, c)).wait_recv()
                fwd = _remote(win, win, send_sems.at[a, 3 + k], recv_sems.at[a, 3 + k], sibling)
                fwd.start()
                sent.append(fwd)
        for s in range(ns):
            for k, (px, py) in enumerate(chips):
                dst = small_out[s].at[2 * px + py]
                _remote(dst, dst, send_sems.at[nb + s, k], recv_sems.at[nb + s, k], (px, py, c)).wait_recv()
        for a in range(nb):
            for k, (px, py) in enumerate(chips):
                win = _window(big_out[a], kinds[a], shapes[a], 2 * px + py, 1 - c)
                _remote(win, win, send_sems.at[a, 3 + k], recv_sems.at[a, 3 + k], sibling).wait_recv()
        for cp in sent:
            cp.wait_send()
        for cp in local:
            cp.wait()

    out_shape = [jax.ShapeDtypeStruct(_full_shape(k, s), b.dtype) for b, k, s in zip(bigs, kinds, shapes)]
    out_shape += [jax.ShapeDtypeStruct((4, *s.shape), s.dtype) for s in smalls]
    return _pcall(
        body, name="gather_weights", in_specs=[HBM] * (nb + ns), out_specs=[HBM] * (nb + ns), out_shape=out_shape,
        scratch_shapes=[pltpu.SemaphoreType.DMA((nb + ns, 6)), pltpu.SemaphoreType.DMA((nb + ns, 6)), pltpu.SemaphoreType.DMA((nb + ns,))],
    )(*bigs, *smalls)


def _swap_other_halves(grads, kinds, shapes):
    n = len(grads)

    def body(*refs):
        g_in, land = refs[:n], refs[n:2 * n]
        send_sems, recv_sems = refs[2 * n:]
        x, y, c, _ = _place()
        sibling = (x, y, 1 - c)
        sent = []
        for a in range(n):
            for j in range(4):
                cp = _remote(_window(g_in[a], kinds[a], shapes[a], j, 1 - c), land[a].at[j], send_sems.at[a, j], recv_sems.at[a, j], sibling)
                cp.start()
                sent.append(cp)
        for a in range(n):
            for j in range(4):
                _remote(land[a].at[j], land[a].at[j], send_sems.at[a, j], recv_sems.at[a, j], sibling).wait_recv()
        for cp in sent:
            cp.wait_send()

    return _pcall(
        body, name="swap_other_halves", in_specs=[HBM] * n, out_specs=[HBM] * n,
        out_shape=[jax.ShapeDtypeStruct((4, s[0] // 2, s[1]), F32) for s in shapes],
        scratch_shapes=[pltpu.SemaphoreType.DMA((n, 4)), pltpu.SemaphoreType.DMA((n, 4))],
    )(*grads)


def _pair_sum(place, g, land, kind, shard_shape, *, name):
    r, w = shard_shape
    hr = r // 2
    tr = _tile(hr, 256, 16)
    nr = hr // tr

    def body(place_ref, g_ref, l_ref, o_ref, own_ref):
        total = g_ref[...] + l_ref[...]
        o_ref[...] = total.astype(BF16)

        @pl.when(pl.program_id(1) == place_ref[1])
        def _():
            own_ref[...] = total

    if kind == "col":
        g_spec = pl.BlockSpec((tr, w), lambda i, j, p: (p[0] * nr + i, j))
    else:
        g_spec = pl.BlockSpec((tr, w), lambda i, j, p: ((2 * j + p[0]) * nr + i, 0))
    part = pl.BlockSpec((None, tr, w), lambda i, j, p: (j, i, 0))
    return _pcall(
        body, name=name,
        grid_spec=pltpu.PrefetchScalarGridSpec(num_scalar_prefetch=1, grid=(nr, 4), in_specs=[g_spec, part],
                                               out_specs=[part, pl.BlockSpec((tr, w), lambda i, j, p: (i, 0))]),
        out_shape=[jax.ShapeDtypeStruct((4, hr, w), BF16), jax.ShapeDtypeStruct((hr, w), F32)],
        compiler_params=_params("arbitrary", "arbitrary"),
    )(place, g, land)


def _chip_exchange(parts):
    n = len(parts)

    def body(*refs):
        p_in, land = refs[:n], refs[n:2 * n]
        send_sems, recv_sems = refs[2 * n:]
        x, y, c, chips = _place()
        sent = []
        for a in range(n):
            for k, (px, py) in enumerate(chips):
                cp = _remote(p_in[a].at[2 * px + py], land[a].at[k], send_sems.at[a, k], recv_sems.at[a, k], (px, py, c))
                cp.start()
                sent.append(cp)
        for a in range(n):
            for k, (px, py) in enumerate(chips):
                _remote(land[a].at[k], land[a].at[k], send_sems.at[a, k], recv_sems.at[a, k], (px, py, c)).wait_recv()
        for cp in sent:
            cp.wait_send()

    return _pcall(
        body, name="chip_exchange", in_specs=[HBM] * n, out_specs=[HBM] * n,
        out_shape=[jax.ShapeDtypeStruct((3, *p.shape[1:]), p.dtype) for p in parts],
        scratch_shapes=[pltpu.SemaphoreType.DMA((n, 3)), pltpu.SemaphoreType.DMA((n, 3))],
    )(*parts)


def _chip_sum(place, own, land, *, name):
    hr, w = own.shape
    tr = _tile(hr, 256, 16)
    nr = hr // tr

    def body(place_ref, p_ref, l_ref, o_ref):
        o_ref[...] = ((p_ref[...] + l_ref[0].astype(F32)) + l_ref[1].astype(F32)) + l_ref[2].astype(F32)

    return _pcall(
        body, name=name,
        grid_spec=pltpu.PrefetchScalarGridSpec(
            num_scalar_prefetch=1, grid=(nr,),
            in_specs=[pl.BlockSpec((tr, w), lambda i, p: (i, 0)), pl.BlockSpec((3, tr, w), lambda i, p: (0, i, 0))],
            out_specs=pl.BlockSpec((tr, w), lambda i, p: (p[0] * nr + i, 0))),
        out_shape=jax.ShapeDtypeStruct((2 * hr, w), F32),
        compiler_params=_params("arbitrary"),
    )(place, own, land)


def _swap_reduced_halves(halves):
    n = len(halves)

    def body(*refs):
        src, out = refs[:n], refs[n:2 * n]
        send_sems, recv_sems = refs[2 * n:]
        x, y, c, _ = _place()
        sibling = (x, y, 1 - c)
        sent = []
        for a in range(n):
            hr = out[a].shape[0] // 2
            cp = _remote(src[a].at[pl.ds(c * hr, hr), :], out[a].at[pl.ds(c * hr, hr), :], send_sems.at[a], recv_sems.at[a], sibling)
            cp.start()
            sent.append(cp)
        for a in range(n):
            hr = out[a].shape[0] // 2
            other = out[a].at[pl.ds((1 - c) * hr, hr), :]
            _remote(other, other, send_sems.at[a], recv_sems.at[a], sibling).wait_recv()
        for cp in sent:
            cp.wait_send()

    return _pcall(
        body, name="swap_reduced_halves", in_specs=[HBM] * n, out_specs=[HBM] * n,
        out_shape=[jax.ShapeDtypeStruct(h.shape, F32) for h in halves],
        input_output_aliases={a: a for a in range(n)},
        scratch_shapes=[pltpu.SemaphoreType.DMA((n,)), pltpu.SemaphoreType.DMA((n,))],
    )(*halves)


def _sum_small(packed):
    rows = packed.shape[0]

    def body(in_ref, out_ref, land, send_sems, recv_sems, local_sem):
        x, y, c, _ = _place()
        me = 4 * x + 2 * y + c
        own = pltpu.make_async_copy(in_ref, land.at[me], local_sem)
        own.start()
        flips = [(r >> 2, (r >> 1) & 1, r & 1) for r in range(1, 8)]
        sent = []
        for k, (fx, fy, fc) in enumerate(flips):
            cp = _remote(in_ref, land.at[me], send_sems.at[k], recv_sems.at[k], (x ^ fx, y ^ fy, c ^ fc))
            cp.start()
            sent.append(cp)
        for k, (fx, fy, fc) in enumerate(flips):
            src = land.at[4 * (x ^ fx) + 2 * (y ^ fy) + (c ^ fc)]
            _remote(src, src, send_sems.at[k], recv_sems.at[k], (x ^ fx, y ^ fy, c ^ fc)).wait_recv()
        for cp in sent:
            cp.wait_send()
        own.wait()
        total = land[0]
        for dev in range(1, 8):
            total = total + land[dev]
        out_ref[...] = total

    vmem = pl.BlockSpec(memory_space=pltpu.VMEM)
    return _pcall(
        body, name="sum_small", in_specs=[vmem], out_specs=vmem, out_shape=jax.ShapeDtypeStruct(packed.shape, F32),
        scratch_shapes=[pltpu.VMEM((8, rows, LANES), F32), pltpu.SemaphoreType.DMA((7,)), pltpu.SemaphoreType.DMA((7,)), pltpu.SemaphoreType.DMA],
    )(packed)


def _sigmoid(v):
    return 1.0 / (1.0 + jnp.exp(-v))


def kernel(x, meta_tokens, g_pre_mix, w_in, conv_w_mix, w_proj_conv, w_proj_attn, b_gate, w_out, g_post_mix, g_pre_ffn, w_up_gate, conv_w_ffn, w_down, g_post_ffn, loss_target, m_meta_tokens, m_g_pre_mix, m_w_in, m_conv_w_mix, m_w_proj_conv, m_w_proj_attn, m_b_gate, m_w_out, m_g_post_mix, m_g_pre_ffn, m_w_up_gate, m_conv_w_ffn, m_w_down, m_g_post_ffn, v_meta_tokens, v_g_pre_mix, v_w_in, v_conv_w_mix, v_w_proj_conv, v_w_proj_attn, v_b_gate, v_w_out, v_g_post_mix, v_g_pre_ffn, v_w_up_gate, v_conv_w_ffn, v_w_down, v_g_post_ffn):
    seq, d = x.shape[1], x.shape[2]
    f = w_down.shape[1] * 4
    real = N_META + seq
    t = -(-real // ATT_BLOCK) * ATT_BLOCK
    nd = d // LANES
    cx, cy, cc = lax.axis_index("x"), lax.axis_index("y"), lax.axis_index("c")
    chip = 2 * cx + cy
    place = jnp.stack([cc, chip]).astype(jnp.int32)

    big_names = ["w_in", "w_proj_conv", "w_proj_attn", "w_out", "w_up_gate", "w_down"]
    kinds = ["col", "row", "row", "row", "col", "row"]
    big_w = [w_in[0], w_proj_conv[0], w_proj_attn[0], w_out[0], w_up_gate[0], w_down[0]]
    small_w = [meta_tokens, conv_w_mix[0], b_gate[0], conv_w_ffn[0]]
    gathered = _gather_weights([w.astype(BF16) for w in big_w], kinds, small_w)
    wi, wpc, wpa, wo, wug, wd = gathered[:6]
    meta_f, cwm_f, bg_f, cwf_f = (jnp.moveaxis(s, 0, 1).reshape(s.shape[1], -1) for s in gathered[6:])

    h0 = jnp.concatenate([meta_f, x[0], jnp.zeros((t - real, d), F32)], axis=0)
    target = jnp.concatenate([jnp.zeros((N_META, d), F32), loss_target[0], jnp.zeros((t - real, d), F32)], axis=0)
    xn1 = _prenorm(h0, g_pre_mix, name="prenorm_mix")
    hin = _matmul(xn1, wi, name="in_proj")
    yconv = _mixer_conv_fwd(hin, cwm_f, d)
    o, att_tot = _attention_fwd(hin, d)

    def gates(gpre, bias):
        return _sigmoid(gpre[:, :d].astype(F32) + bias[0:1]), _sigmoid(gpre[:, d:].astype(F32) + bias[1:2])

    def merge_epi(accs, rows, vecs, row0):
        gate_c, gate_a = gates(rows[0], vecs[0])
        return [gate_c * accs[0] + gate_a * accs[1], accs[0], accs[1]]

    act = ("row", d, BF16)
    merged, bconv, battn = _matmul_rows([(yconv, wpc), (o, wpa)], [(hin, 2 * d, 3)], [bg_f], merge_epi, [act, act, act], name="branch_proj")

    def mix_epi(accs, rows, vecs, row0):
        mix = accs[0]
        h1 = rows[0] + mix * _rms(mix) * vecs[0]
        return [mix, h1, h1 * _rms(h1) * vecs[1]]

    mix, h1, xn2 = _matmul_rows([(merged, wo)], [(h0, d, 0)], [g_post_mix, g_pre_ffn], mix_epi,
                                [("row", d, F32), ("row", d, F32), act], name="out_proj")
    ug = _matmul(xn2, wug, name="up_proj")
    fact = _ffn_conv_fwd(ug, cwf_f, f)

    def loss_epi(accs, rows, vecs, row0):
        ffn = accs[0]
        r = _rms(ffn)
        fh = ffn * r
        h2 = rows[0] + fh * vecs[0]
        rid = row0 + lax.broadcasted_iota(jnp.int32, (ffn.shape[0], 1), 0)
        err = jnp.where((rid >= N_META) & (rid < real), h2 - rows[1], 0.0)
        dy = err * (1.0 / d)
        dffn, dg = _rms_bwd(dy, fh, r, vecs[0])
        loss = jnp.zeros((1, LANES), F32) + 0.5 * jnp.sum(err * err) / d
        return [dffn, dy, dg, loss]

    dffn, dh2, dg_post_ffn, loss_part = _matmul_rows(
        [(fact, wd)], [(h1, d, 0), (target, d, 0)], [g_post_ffn], loss_epi,
        [act, ("row", d, F32), ("acc", 1, d), ("acc", 1, LANES)], name="down_proj_loss")

    gw_down = _matmul_tn(fact, dffn, name="grad_w_down")
    df = _matmul(dffn, wd, nt=True, name="down_proj_bwd")
    du2, dg2, gcw_ffn = _ffn_conv_bwd(ug, df, cwf_f, f)
    dug = jnp.concatenate([du2, dg2], axis=1)
    gw_up = _matmul_tn(xn2, dug, name="grad_w_up_gate")

    def ffn_in_epi(accs, rows, vecs, row0):
        h1v, dyv, mixv = rows
        r3 = _rms(h1v)
        dh1n, dg3 = _rms_bwd(accs[0], h1v * r3, r3, vecs[0])
        dh1 = dyv + dh1n
        r2 = _rms(mixv)
        dmix, dgm = _rms_bwd(dh1, mixv * r2, r2, vecs[1])
        return [dh1, dmix, dg3, dgm]

    dh1, dmix, dg_pre_ffn, dg_post_mix = _matmul_rows(
        [(dug, wug)], [(h1, d, 0), (dh2, d, 0), (mix, d, 0)], [g_pre_ffn, g_post_mix], ffn_in_epi,
        [("row", d, F32), act, ("acc", 1, d), ("acc", 1, d)], nt=True, name="up_proj_bwd")
    gw_out = _matmul_tn(merged, dmix, name="grad_w_out")

    def merge_bwd_epi(accs, rows, vecs, row0):
        dm = accs[0]
        gate_c, gate_a = gates(rows[0], vecs[0])
        dpre_c = dm * rows[1].astype(F32) * gate_c * (1.0 - gate_c)
        dpre_a = dm * rows[2].astype(F32) * gate_a * (1.0 - gate_a)
        dbias = jnp.concatenate([jnp.sum(dpre_c, axis=0, keepdims=True), jnp.sum(dpre_a, axis=0, keepdims=True)], axis=0)
        return [dm * gate_c, dm * gate_a, jnp.concatenate([dpre_c, dpre_a], axis=1), dbias]

    dbconv, dbattn, dgates, gb_gate = _matmul_rows(
        [(dmix, wo)], [(hin, 2 * d, 3), (bconv, d, 0), (battn, d, 0)], [bg_f], merge_bwd_epi,
        [act, act, ("row", 2 * d, BF16), ("acc", 2, d)], nt=True, name="out_proj_bwd")
    gw_pc = _matmul_tn(yconv, dbconv, name="grad_w_proj_conv")
    gw_pa = _matmul_tn(o, dbattn, name="grad_w_proj_attn")
    dyconv = _matmul(dbconv, wpc, nt=True, name="proj_conv_bwd")
    do = _matmul(dbattn, wpa, nt=True, name="proj_attn_bwd")
    db, dc, dhc, gcw_mix = _mixer_conv_bwd(hin, dyconv, cwm_f, d)
    dq, dk, dv = _attention_bwd(hin, att_tot, do, d)
    dhin = jnp.concatenate([db, dc, dhc, dq, dk, dv, dgates], axis=1)
    gw_in = _matmul_tn(xn1, dhin, name="grad_w_in")

    def in_epi(accs, rows, vecs, row0):
        h0v, dh1v = rows
        r1 = _rms(h0v)
        dh0n, dg1 = _rms_bwd(accs[0], h0v * r1, r1, vecs[0])
        return [dh1v + dh0n, dg1]

    dh0, dg_pre_mix = _matmul_rows([(dhin, wi)], [(h0, d, 0), (dh1, d, 0)], [g_pre_mix], in_epi,
                                   [("row", d, F32), ("acc", 1, d)], nt=True, name="in_proj_bwd")
    grad_x = dh0[N_META:real][None]

    big_g = [gw_in, gw_pc, gw_pa, gw_out, gw_up, gw_down]
    shapes = [w.shape for w in big_w]
    landed = _swap_other_halves(big_g, kinds, shapes)
    parts, own = zip(*[_pair_sum(place, g, l, k, s, name=f"pair_sum_{n}") for g, l, k, s, n in zip(big_g, landed, kinds, shapes, big_names)])
    landed = _chip_exchange(parts)
    halves = [_chip_sum(place, p, l, name=f"chip_sum_{n}") for p, l, n in zip(own, landed, big_names)]
    big_r = _swap_reduced_halves(halves)

    small_g = [dh0[:N_META], dg_pre_mix, gcw_mix, gb_gate, dg_post_mix, dg_pre_ffn, gcw_ffn, dg_post_ffn, loss_part]
    flat = jnp.concatenate([s.reshape(-1) for s in small_g])
    pad = -flat.shape[0] % (8 * LANES)
    summed = _sum_small(jnp.pad(flat, (0, pad)).reshape(-1, LANES)).reshape(-1)
    small_r, pos = [], 0
    for s in small_g:
        small_r.append(summed[pos:pos + s.size].reshape(s.shape))
        pos += s.size
    g_meta, g_g1, g_cwm, g_bg, g_g2, g_g3, g_cwf, g_g4, loss_row = small_r
    loss = loss_row[0, 0]

    def my_cols(full, width):
        return lax.dynamic_slice_in_dim(full, chip * width, width, axis=1)

    grads = {
        "meta_tokens": my_cols(g_meta, d // 4), "g_pre_mix": g_g1, "w_in": big_r[0][None],
        "conv_w_mix": my_cols(g_cwm, d // 4)[None], "w_proj_conv": big_r[1][None], "w_proj_attn": big_r[2][None],
        "b_gate": my_cols(g_bg, d // 4)[None], "w_out": big_r[3][None], "g_post_mix": g_g2, "g_pre_ffn": g_g3,
        "w_up_gate": big_r[4][None], "conv_w_ffn": my_cols(g_cwf, f // 4)[None], "w_down": big_r[5][None], "g_post_ffn": g_g4,
    }
    weights = dict(meta_tokens=meta_tokens, g_pre_mix=g_pre_mix, w_in=w_in, conv_w_mix=conv_w_mix, w_proj_conv=w_proj_conv,
                   w_proj_attn=w_proj_attn, b_gate=b_gate, w_out=w_out, g_post_mix=g_post_mix, g_pre_ffn=g_pre_ffn,
                   w_up_gate=w_up_gate, conv_w_ffn=conv_w_ffn, w_down=w_down, g_post_ffn=g_post_ffn)
    m_in = dict(meta_tokens=m_meta_tokens, g_pre_mix=m_g_pre_mix, w_in=m_w_in, conv_w_mix=m_conv_w_mix, w_proj_conv=m_w_proj_conv,
                w_proj_attn=m_w_proj_attn, b_gate=m_b_gate, w_out=m_w_out, g_post_mix=m_g_post_mix, g_pre_ffn=m_g_pre_ffn,
                w_up_gate=m_w_up_gate, conv_w_ffn=m_conv_w_ffn, w_down=m_w_down, g_post_ffn=m_g_post_ffn)
    v_in = dict(meta_tokens=v_meta_tokens, g_pre_mix=v_g_pre_mix, w_in=v_w_in, conv_w_mix=v_conv_w_mix, w_proj_conv=v_w_proj_conv,
                w_proj_attn=v_w_proj_attn, b_gate=v_b_gate, w_out=v_w_out, g_post_mix=v_g_post_mix, g_pre_ffn=v_g_pre_ffn,
                w_up_gate=v_w_up_gate, conv_w_ffn=v_conv_w_ffn, w_down=v_w_down, g_post_ffn=v_g_post_ffn)
    names = list(weights)
    deltas, new_m, new_v = [], [], []
    for n in names:
        shape = weights[n].shape
        two_d = (-1, shape[-1])
        dl, nm, nv = _adamw(weights[n].reshape(two_d), grads[n].reshape(two_d), m_in[n].reshape(two_d), v_in[n].reshape(two_d),
                            name=f"adamw_{n}")
        deltas.append(dl.reshape(shape))
        new_m.append(nm.reshape(shape))
        new_v.append(nv.reshape(shape))
    return (loss, grad_x, *[grads[n].reshape(weights[n].shape) for n in names], *deltas, *new_m, *new_v)
```

```python
import functools
import math

import jax
import jax.numpy as jnp
from jax import lax
from jax.experimental import pallas as pl
from jax.experimental.pallas import tpu as pltpu

F32 = jnp.float32
BF16 = jnp.bfloat16

N_META = 16
HEAD_DIM = 64
LANES = 128
RMS_EPS = 1e-6
ATT_BLOCK = 256
VMEM_LIMIT = 56 * 1024 * 1024

ADAM_LR = 0.001
ADAM_B1 = 0.9
ADAM_B2 = 0.999
ADAM_EPS = 1e-08
ADAM_WD = 0.01
ADAM_STEP = 10

MESH = pl.DeviceIdType.MESH


def _tile(n, cap, unit=LANES):
    d = (min(cap, n) // unit) * unit
    while d >= unit:
        if n % d == 0:
            return d
        d -= unit
    raise ValueError(f"no tile for {n} under {cap}")


MXU_EDGE = 256


def _mxu_tile(n, cap):
    if n <= cap:
        return n
    return _tile(n, cap, MXU_EDGE) if n % MXU_EDGE == 0 else _tile(n, cap)


def _params(*sem):
    return pltpu.CompilerParams(dimension_semantics=sem, vmem_limit_bytes=VMEM_LIMIT)


def _pcall(body, **kw):
    return pl.pallas_call(body, **kw)


def _dot(a, b, dims):
    return lax.dot_general(a, b, (dims, ((), ())), preferred_element_type=F32)


NN = ((1,), (0,))
NT = ((1,), (1,))
TN = ((0,), (0,))


def _matmul(a, b, *, nt=False, out_dtype=BF16, name):
    m, kdim = a.shape
    n = b.shape[0] if nt else b.shape[1]
    tm, tn, tk = _tile(m, 640, 16), _mxu_tile(n, 2816), _mxu_tile(kdim, 2816)
    nk = kdim // tk

    def body(a_ref, b_ref, o_ref, *scratch):
        p = _dot(a_ref[...], b_ref[...], NT if nt else NN)
        if nk == 1:
            o_ref[...] = p.astype(o_ref.dtype)
            return
        acc, k = scratch[0], pl.program_id(2)

        @pl.when(k == 0)
        def _():
            acc[...] = p

        @pl.when(k > 0)
        def _():
            acc[...] += p

        @pl.when(k == nk - 1)
        def _():
            o_ref[...] = acc[...].astype(o_ref.dtype)

    b_spec = pl.BlockSpec((tn, tk), lambda j, i, k: (j, k)) if nt else pl.BlockSpec((tk, tn), lambda j, i, k: (k, j))
    return _pcall(
        body, name=name, grid=(n // tn, m // tm, nk),
        in_specs=[pl.BlockSpec((tm, tk), lambda j, i, k: (i, k)), b_spec],
        out_specs=pl.BlockSpec((tm, tn), lambda j, i, k: (i, j)),
        out_shape=jax.ShapeDtypeStruct((m, n), out_dtype),
        scratch_shapes=[pltpu.VMEM((tm, tn), F32)] if nk > 1 else [],
        compiler_params=_params("arbitrary", "arbitrary", "arbitrary"),
    )(a, b)


def _matmul_tn(a, b, *, name):
    t, ka = a.shape
    nb = b.shape[1]
    tb = _mxu_tile(nb, 1024)
    budget = VMEM_LIMIT * 5 // 7
    ta = next(c for c in range(ka, 0, -LANES) if ka % c == 0 and 2 * (2 * t * (c + tb) + 4 * c * tb) <= budget)

    def body(a_ref, b_ref, o_ref):
        o_ref[...] = _dot(a_ref[...], b_ref[...], TN)

    return _pcall(
        body, name=name, grid=(ka // ta, nb // tb),
        in_specs=[pl.BlockSpec((t, ta), lambda i, j: (0, i)), pl.BlockSpec((t, tb), lambda i, j: (0, j))],
        out_specs=pl.BlockSpec((ta, tb), lambda i, j: (i, j)),
        out_shape=jax.ShapeDtypeStruct((ka, nb), F32),
        compiler_params=_params("arbitrary", "arbitrary"),
    )(a, b)


def _matmul_rows(pairs, rows, vecs, epi, outs, *, nt=False, name):
    m, kdim = pairs[0][0].shape
    n = pairs[0][1].shape[0] if nt else pairs[0][1].shape[1]
    tm, tk = _tile(m, 640, 16), _mxu_tile(kdim, 2816)
    nk, npair = kdim // tk, len(pairs)

    def body(*refs):
        a_refs, b_refs = refs[0:2 * npair:2], refs[1:2 * npair:2]
        pos = 2 * npair
        row_refs = refs[pos:pos + len(rows)]
        pos += len(rows)
        vec_refs = refs[pos:pos + len(vecs)]
        pos += len(vecs)
        out_refs = refs[pos:pos + len(outs)]
        accs = refs[pos + len(outs):]
        i, k = pl.program_id(0), pl.program_id(1)
        prods = [_dot(a[...], b[...], NT if nt else NN) for a, b in zip(a_refs, b_refs)]
        if nk > 1:
            @pl.when(k == 0)
            def _():
                for acc, p in zip(accs, prods):
                    acc[...] = p

            @pl.when(k > 0)
            def _():
                for acc, p in zip(accs, prods):
                    acc[...] += p

        @pl.when(k == nk - 1)
        def _():
            vals = [acc[...] for acc in accs] if nk > 1 else prods
            res = epi(vals, [r[...] for r in row_refs], [v[...] for v in vec_refs], i * tm)
            for o_ref, spec, val in zip(out_refs, outs, res):
                if spec[0] == "row":
                    o_ref[...] = val.astype(o_ref.dtype)
                else:
                    @pl.when(i == 0)
                    def _():
                        o_ref[...] = val

                    @pl.when(i > 0)
                    def _():
                        o_ref[...] += val

    in_specs, args = [], []
    for a, b in pairs:
        in_specs += [pl.BlockSpec((tm, tk), lambda i, k: (i, k)),
                     pl.BlockSpec((n, tk), lambda i, k: (0, k)) if nt else pl.BlockSpec((tk, n), lambda i, k: (k, 0))]
        args += [a, b]
    for arr, width, cb in rows:
        in_specs.append(pl.BlockSpec((tm, width), functools.partial(lambda i, k, cb: (i, cb), cb=cb)))
        args.append(arr)
    for v in vecs:
        in_specs.append(pl.BlockSpec(v.shape, lambda i, k: (0, 0)))
        args.append(v)
    out_specs, out_shape = [], []
    for spec in outs:
        if spec[0] == "row":
            out_specs.append(pl.BlockSpec((tm, spec[1]), lambda i, k: (i, 0)))
            out_shape.append(jax.ShapeDtypeStruct((m, spec[1]), spec[2]))
        else:
            out_specs.append(pl.BlockSpec((spec[1], spec[2]), lambda i, k: (0, 0)))
            out_shape.append(jax.ShapeDtypeStruct((spec[1], spec[2]), F32))
    return _pcall(
        body, name=name, grid=(m // tm, nk), in_specs=in_specs, out_specs=out_specs, out_shape=out_shape,
        scratch_shapes=[pltpu.VMEM((tm, n), F32) for _ in pairs] if nk > 1 else [],
        compiler_params=_params("arbitrary", "arbitrary"),
    )(*args)


def _matmul_rows_joined(segments, b, rows, vecs, epi, outs, *, tk, name):
    m = segments[0].shape[0]
    n, kdim = b.shape
    tm = _tile(m, 640, 16)
    nk = kdim // tk
    first = [0]
    for s in segments:
        assert s.shape[1] % tk == 0
        first.append(first[-1] + s.shape[1] // tk)
    assert first[-1] == nk

    def body(*refs):
        seg_refs, b_ref = refs[:len(segments)], refs[len(segments)]
        pos = len(segments) + 1
        row_refs = refs[pos:pos + len(rows)]
        pos += len(rows)
        vec_refs = refs[pos:pos + len(vecs)]
        pos += len(vecs)
        out_refs, joined_ref, acc = refs[pos:pos + len(outs)], refs[pos + len(outs)], refs[pos + len(outs) + 1]
        i, k = pl.program_id(0), pl.program_id(1)
        for s, seg in enumerate(seg_refs):
            @pl.when((k >= first[s]) & (k < first[s + 1]))
            def _():
                part = seg[...]
                joined_ref[...] = part
                p = _dot(part, b_ref[...], NT)

                @pl.when(k == 0)
                def _():
                    acc[...] = p

                @pl.when(k > 0)
                def _():
                    acc[...] += p

        @pl.when(k == nk - 1)
        def _():
            res = epi([acc[...]], [r[...] for r in row_refs], [v[...] for v in vec_refs], i * tm)
            for o_ref, spec, val in zip(out_refs, outs, res):
                if spec[0] == "row":
                    o_ref[...] = val.astype(o_ref.dtype)
                else:
                    @pl.when(i == 0)
                    def _():
                        o_ref[...] = val

                    @pl.when(i > 0)
                    def _():
                        o_ref[...] += val

    def seg_spec(s):
        last = first[s + 1] - first[s] - 1
        return pl.BlockSpec((tm, tk), lambda i, k: (i, jnp.clip(k - first[s], 0, last)))

    in_specs = [seg_spec(s) for s in range(len(segments))] + [pl.BlockSpec((n, tk), lambda i, k: (0, k))]
    args = [*segments, b]
    for arr, width, cb in rows:
        in_specs.append(pl.BlockSpec((tm, width), functools.partial(lambda i, k, cb: (i, cb), cb=cb)))
        args.append(arr)
    for v in vecs:
        in_specs.append(pl.BlockSpec(v.shape, lambda i, k: (0, 0)))
        args.append(v)
    out_specs, out_shape = [], []
    for spec in outs:
        if spec[0] == "row":
            out_specs.append(pl.BlockSpec((tm, spec[1]), lambda i, k: (i, 0)))
            out_shape.append(jax.ShapeDtypeStruct((m, spec[1]), spec[2]))
        else:
            out_specs.append(pl.BlockSpec((spec[1], spec[2]), lambda i, k: (0, 0)))
            out_shape.append(jax.ShapeDtypeStruct((spec[1], spec[2]), F32))
    out_specs.append(pl.BlockSpec((tm, tk), lambda i, k: (i, k)))
    out_shape.append(jax.ShapeDtypeStruct((m, kdim), segments[0].dtype))
    return _pcall(
        body, name=name, grid=(m // tm, nk), in_specs=in_specs, out_specs=out_specs, out_shape=out_shape,
        scratch_shapes=[pltpu.VMEM((tm, n), F32)],
        compiler_params=_params("arbitrary", "arbitrary"),
    )(*args)


def _rms(v):
    return lax.rsqrt(jnp.mean(v * v, axis=-1, keepdims=True) + RMS_EPS)


def _rms_bwd(dz, vhat, r, g):
    t = dz * g
    dv = r * (t - vhat * jnp.mean(t * vhat, axis=-1, keepdims=True))
    return dv, jnp.sum(dz * vhat, axis=0, keepdims=True)


def _prenorm(h, g, *, name):
    m, d = h.shape
    tm = _tile(m, 640, 16)

    def body(h_ref, g_ref, o_ref):
        v = h_ref[...]
        o_ref[...] = (v * _rms(v) * g_ref[...]).astype(BF16)

    return _pcall(
        body, name=name, grid=(m // tm,),
        in_specs=[pl.BlockSpec((tm, d), lambda i: (i, 0)), pl.BlockSpec((1, d), lambda i: (0, 0))],
        out_specs=pl.BlockSpec((tm, d), lambda i: (i, 0)),
        out_shape=jax.ShapeDtypeStruct((m, d), BF16),
        compiler_params=_params("arbitrary"),
    )(h, g)


def _shift_down(u, k):
    rows = lax.broadcasted_iota(jnp.int32, u.shape, 0)
    return jnp.where(rows >= k, pltpu.roll(u, k, 0), 0.0)


def _shift_up(u, k):
    n = u.shape[0]
    rows = lax.broadcasted_iota(jnp.int32, u.shape, 0)
    return jnp.where(rows < n - k, pltpu.roll(u, n - k, 0), 0.0)


def _conv(u, w):
    return w[2:3] * u + w[1:2] * _shift_down(u, 1) + w[0:1] * _shift_down(u, 2)


def _conv_bwd(dcu, u, w):
    du = w[2:3] * dcu + w[1:2] * _shift_up(dcu, 1) + w[0:1] * _shift_up(dcu, 2)
    dw = [jnp.sum(dcu * _shift_down(u, 2 - k) if k < 2 else dcu * u, axis=0, keepdims=True) for k in range(3)]
    return du, dw


def _strip(arr, t, cb0):
    return pl.BlockSpec((t, LANES), functools.partial(lambda s, cb0: (0, cb0 + s), cb0=cb0))


def _mixer_conv_fwd(hin, w, d):
    t, ns = hin.shape[0], d // LANES

    def body(b_ref, c_ref, h_ref, w_ref, y_ref):
        u = c_ref[...].astype(F32) * h_ref[...].astype(F32)
        y_ref[...] = (b_ref[...].astype(F32) * _conv(u, w_ref[...])).astype(BF16)

    return _pcall(
        body, name="mixer_conv_fwd", grid=(ns,),
        in_specs=[_strip(hin, t, 0), _strip(hin, t, ns), _strip(hin, t, 2 * ns), pl.BlockSpec((3, LANES), lambda s: (0, s))],
        out_specs=pl.BlockSpec((t, LANES), lambda s: (0, s)),
        out_shape=jax.ShapeDtypeStruct((t, d), BF16),
        compiler_params=_params("arbitrary"),
    )(hin, hin, hin, w)


def _mixer_conv_bwd(hin, dy, w, d):
    t, ns = hin.shape[0], d // LANES

    def body(b_ref, c_ref, h_ref, dy_ref, w_ref, db_ref, dc_ref, dh_ref, dw_ref):
        b, c, h, g = (r[...].astype(F32) for r in (b_ref, c_ref, h_ref, dy_ref))
        wv = w_ref[...]
        u = c * h
        db_ref[...] = (g * _conv(u, wv)).astype(BF16)
        du, dw = _conv_bwd(g * b, u, wv)
        dc_ref[...] = (du * h).astype(BF16)
        dh_ref[...] = (du * c).astype(BF16)
        for k in range(3):
            dw_ref[k:k + 1, :] = dw[k]

    col = pl.BlockSpec((t, LANES), lambda s: (0, s))
    act = jax.ShapeDtypeStruct((t, d), BF16)
    return _pcall(
        body, name="mixer_conv_bwd", grid=(ns,),
        in_specs=[_strip(hin, t, 0), _strip(hin, t, ns), _strip(hin, t, 2 * ns), col, pl.BlockSpec((3, LANES), lambda s: (0, s))],
        out_specs=[col, col, col, pl.BlockSpec((3, LANES), lambda s: (0, s))],
        out_shape=[act, act, act, jax.ShapeDtypeStruct((3, d), F32)],
        compiler_params=_params("arbitrary"),
    )(hin, hin, hin, dy, w)


GELU_C = math.sqrt(2.0 / math.pi)
GELU_A = 0.044715


def _gelu_tanh(x):
    return jnp.tanh(GELU_C * (x + GELU_A * x * x * x))


def _ffn_conv_fwd(ug, w, f):
    t, ns = ug.shape[0], f // LANES

    def body(u_ref, g_ref, w_ref, o_ref):
        cu = _conv(u_ref[...].astype(F32), w_ref[...])
        o_ref[...] = (0.5 * cu * (1.0 + _gelu_tanh(cu)) * g_ref[...].astype(F32)).astype(BF16)

    return _pcall(
        body, name="ffn_conv_fwd", grid=(ns,),
        in_specs=[_strip(ug, t, 0), _strip(ug, t, ns), pl.BlockSpec((3, LANES), lambda s: (0, s))],
        out_specs=pl.BlockSpec((t, LANES), lambda s: (0, s)),
        out_shape=jax.ShapeDtypeStruct((t, f), BF16),
        compiler_params=_params("arbitrary"),
    )(ug, ug, w)


def _ffn_conv_bwd(ug, df, w, f):
    t, ns = ug.shape[0], f // LANES

    def body(u_ref, g_ref, df_ref, w_ref, du_ref, dg_ref, dw_ref):
        u, g, d = (r[...].astype(F32) for r in (u_ref, g_ref, df_ref))
        wv = w_ref[...]
        cu = _conv(u, wv)
        th = _gelu_tanh(cu)
        dg_ref[...] = (d * 0.5 * cu * (1.0 + th)).astype(BF16)
        dgelu = 0.5 * (1.0 + th) + 0.5 * cu * (1.0 - th * th) * GELU_C * (1.0 + 3.0 * GELU_A * cu * cu)
        du, dw = _conv_bwd(d * g * dgelu, u, wv)
        du_ref[...] = du.astype(BF16)
        for k in range(3):
            dw_ref[k:k + 1, :] = dw[k]

    col = pl.BlockSpec((t, LANES), lambda s: (0, s))
    act = jax.ShapeDtypeStruct((t, f), BF16)
    return _pcall(
        body, name="ffn_conv_bwd", grid=(ns,),
        in_specs=[_strip(ug, t, 0), _strip(ug, t, ns), col, pl.BlockSpec((3, LANES), lambda s: (0, s))],
        out_specs=[col, col, pl.BlockSpec((3, LANES), lambda s: (0, s))],
        out_shape=[act, act, jax.ShapeDtypeStruct((3, f), F32)],
        compiler_params=_params("arbitrary"),
    )(ug, ug, df, w)


def _log_terms(z):
    minus_abs = lax.bitcast_convert_type(lax.bitcast_convert_type(z, jnp.uint32) | jnp.uint32(0x80000000), F32)
    lb = jnp.minimum(z, 0.0) - jnp.log(1.0 + jnp.exp(minus_abs))
    return lb, lb - z


SUM_TERMS = 1


def _split_cat(v, axis):
    terms = [v.astype(BF16)]
    for _ in range(SUM_TERMS - 1):
        v = v - terms[-1].astype(F32)
        terms.append(v.astype(BF16))
    return terms[0] if SUM_TERMS == 1 else jnp.concatenate(terms, axis=axis)


def _head_masks():
    lane = lax.broadcasted_iota(jnp.int32, (1, LANES), 1)
    return lane < HEAD_DIM, lane >= HEAD_DIM


def _attention_fwd(hin, d):
    t, blk, nd = hin.shape[0], ATT_BLOCK, d // LANES
    nq = t // blk
    scale = 1.0 / math.sqrt(HEAD_DIM)
    tri = jnp.tril(jnp.ones((blk, blk), F32)).astype(BF16)
    tri = jnp.concatenate([tri] * SUM_TERMS, axis=0)

    def body(q_ref, k_ref, v_ref, tri_ref, o_ref, tot_ref):
        i = pl.program_id(1)
        masks = _head_masks()
        q = q_ref[...].astype(F32) * scale
        qh = [jnp.where(mh, q, 0.0).astype(BF16) for mh in masks]
        after = lax.broadcasted_iota(jnp.int32, (blk, blk), 0) > lax.broadcasted_iota(jnp.int32, (blk, blk), 1)
        tri_v = tri_ref[...]

        def scores(j):
            j0 = pl.multiple_of(j * blk, blk)
            kj = k_ref[pl.ds(j0, blk), :]
            return tuple(_dot(qh[h], kj, NT) for h in range(2))

        def weights(z, c, diagonal, between=lambda: None):
            halves = [slice(r, r + blk // 2) for r in range(0, blk, blk // 2)]
            other = between()
            sums = []
            for h in range(2):
                for rows in halves:
                    lf = _log_terms(z[h][rows])[1]
                    if diagonal:
                        lf = jnp.where(after[rows], lf, 0.0)
                    sums.append(_dot(_split_cat(lf, 1), tri_v, NN))
            sums = [jnp.concatenate(sums[2 * h:2 * h + 2], axis=0) for h in range(2)]
            a = [jnp.exp(z[h] + sums[h] + c[h]) for h in range(2)]
            if diagonal:
                a = [jnp.where(after, ah, 0.0) for ah in a]
            return tuple(ah.astype(BF16) for ah in a), tuple(c[h] + sums[h][:, 0:1] for h in range(2)), other

        def apply(acc, a, j):
            j0 = pl.multiple_of(j * blk, blk)
            vj = v_ref[pl.ds(j0, blk), :]
            for h in range(2):
                acc = acc + _dot(a[h], jnp.where(masks[h], vj, jnp.zeros_like(vj)), NN)
            return acc

        zero = jnp.zeros((blk, 1), F32)
        a, c, _ = weights(scores(i), (zero, zero), True)

        def step(m, carry):
            z, a, c, acc = carry
            j = i - 1 - m
            a, c, (z_next, acc) = weights(z, c, False, lambda: (scores(jnp.maximum(j - 1, 0)), apply(acc, a, j + 1)))
            return z_next, a, c, acc

        _, a, c, acc = lax.fori_loop(0, i, step, (scores(jnp.maximum(i - 1, 0)), a, c, jnp.zeros((blk, LANES), F32)))
        o_ref[...] = apply(acc, a, 0).astype(BF16)
        tot_ref[...] = jnp.where(masks[0], c[0], c[1])

    def whole(cb0):
        return pl.BlockSpec((t, LANES), functools.partial(lambda p, i, cb0: (0, cb0 + p), cb0=cb0))

    tile = pl.BlockSpec((blk, LANES), lambda p, i: (i, p))
    return _pcall(
        body, name="attention_fwd", grid=(nd, nq),
        in_specs=[pl.BlockSpec((blk, LANES), lambda p, i: (i, 3 * nd + p)), whole(4 * nd), whole(5 * nd),
                  pl.BlockSpec((SUM_TERMS * blk, blk), lambda p, i: (0, 0))],
        out_specs=[tile, tile],
        out_shape=[jax.ShapeDtypeStruct((t, d), BF16), jax.ShapeDtypeStruct((t, d), F32)],
        compiler_params=_params("arbitrary", "arbitrary"),
    )(hin, hin, hin, tri)


def _attention_bwd(hin, tot, do, d):
    t, blk, nd = hin.shape[0], ATT_BLOCK, d // LANES
    nq = t // blk
    scale = 1.0 / math.sqrt(HEAD_DIM)
    upper = jnp.triu(jnp.ones((blk, blk), F32)).astype(BF16)
    upper = jnp.concatenate([upper] * SUM_TERMS, axis=1)
    lower = jnp.tril(jnp.ones((blk, blk), F32), -1).astype(BF16)

    def body(q_ref, k_ref, v_ref, tot_ref, do_ref, up_ref, low_ref, dq_ref, dk_ref, dv_ref, dk_acc, dv_acc):
        i = pl.program_id(1)
        masks = _head_masks()

        @pl.when(i == 0)
        def _():
            dk_acc[...] = jnp.zeros_like(dk_acc)
            dv_acc[...] = jnp.zeros_like(dv_acc)

        q = q_ref[...].astype(F32) * scale
        dout = do_ref[...]
        qh = [jnp.where(mh, q, 0.0).astype(BF16) for mh in masks]
        doh = [jnp.where(mh, dout, jnp.zeros_like(dout)) for mh in masks]
        lane = lax.broadcasted_iota(jnp.int32, (8, LANES), 1)
        totals = [lax.dot_general(jnp.where(lane == h * HEAD_DIM, 1.0, 0.0), tot_ref[...], (NT, ((), ())),
                                  precision=lax.Precision.HIGHEST, preferred_element_type=F32)[0:1, :] for h in range(2)]
        after = lax.broadcasted_iota(jnp.int32, (blk, blk), 1) > lax.broadcasted_iota(jnp.int32, (blk, blk), 0)
        up, low = up_ref[...], low_ref[...]

        def products(j):
            j0 = pl.multiple_of(j * blk, blk)
            kj, vj = k_ref[pl.ds(j0, blk), :], v_ref[pl.ds(j0, blk), :]
            return tuple((_dot(kj, qh[h], NT), _dot(vj, doh[h], NT)) for h in range(2))

        def stage_sums(z, diagonal):
            lb, lf = _log_terms(z)
            if diagonal:
                lf = jnp.where(after, lf, 0.0)
            return jnp.exp(lb).astype(BF16), _dot(up, _split_cat(lf, 0), NN)

        def stage_weights(z, da, sums, rest, diagonal):
            rest = rest - sums[0:1, :]
            a = jnp.exp(z + sums + rest)
            if diagonal:
                a = jnp.where(after, a, 0.0)
            g = da * a
            return a.astype(BF16), g, _dot(low, g.astype(BF16), NN), rest

        def stage_dz(g, sig, earlier, before, diagonal):
            dlf = earlier + before
            dz = g - sig.astype(F32) * (g + dlf)
            if diagonal:
                dz = jnp.where(after, dz, 0.0)
            return dz.astype(BF16), dlf[blk - 1:blk, :] + g[blk - 1:blk, :]

        def grads(cur, rest, before, diagonal, between=(lambda: None, lambda: None)):
            sig, sums = zip(*[stage_sums(cur[h][0], diagonal) for h in range(2)])
            first = between[0]()
            a, g, earlier, rest = zip(*[stage_weights(*cur[h], sums[h], rest[h], diagonal) for h in range(2)])
            second = between[1]()
            dz, before = zip(*[stage_dz(g[h], sig[h], earlier[h], before[h], diagonal) for h in range(2)])
            return dz, a, rest, before, first, second

        def flush(j, dz, a, dq):
            j0 = pl.multiple_of(j * blk, blk)
            kj = k_ref[pl.ds(j0, blk), :]
            dk_new = jnp.zeros((blk, LANES), F32)
            dv_new = jnp.zeros((blk, LANES), F32)
            for h in range(2):
                dk_new = dk_new + _dot(dz[h], qh[h], NN)
                dv_new = dv_new + _dot(a[h], doh[h], NN)
                dq = dq + _dot(dz[h], jnp.where(masks[h], kj, jnp.zeros_like(kj)), TN)
            dk_acc[pl.ds(j0, blk), :] += dk_new
            dv_acc[pl.ds(j0, blk), :] += dv_new
            return dq

        def step(j, carry):
            cur, dz, a, rest, before, dq = carry
            dz, a, rest, before, nxt, dq = grads(
                cur, rest, before, False, (lambda: products(j + 1), lambda: flush(jnp.maximum(j - 1, 0), dz, a, dq)))
            return nxt, dz, a, rest, before, dq

        zero = jnp.zeros((1, blk), F32)
        nothing = jnp.zeros((blk, blk), BF16)
        cur, dz, a, rest, before, dq = lax.fori_loop(
            0, i, step, (products(0), (nothing, nothing), (nothing, nothing), tuple(totals), (zero, zero), jnp.zeros((blk, LANES), F32)))
        dq = flush(jnp.maximum(i - 1, 0), dz, a, dq)
        dz, a = grads(cur, rest, before, True)[:2]
        dq_ref[...] = (flush(i, dz, a, dq) * scale).astype(BF16)

        @pl.when(i == nq - 1)
        def _():
            dk_ref[...] = dk_acc[...].astype(BF16)
            dv_ref[...] = dv_acc[...].astype(BF16)

    def whole(cb0):
        return pl.BlockSpec((t, LANES), functools.partial(lambda p, i, cb0: (0, cb0 + p), cb0=cb0))

    tile = pl.BlockSpec((blk, LANES), lambda p, i: (i, p))
    const = pl.BlockSpec((blk, blk), lambda p, i: (0, 0))
    act = jax.ShapeDtypeStruct((t, d), BF16)
    return _pcall(
        body, name="attention_bwd", grid=(nd, nq),
        in_specs=[pl.BlockSpec((blk, LANES), lambda p, i: (i, 3 * nd + p)), whole(4 * nd), whole(5 * nd), tile, tile,
                  pl.BlockSpec((blk, SUM_TERMS * blk), lambda p, i: (0, 0)), const],
        out_specs=[tile, whole(0), whole(0)],
        out_shape=[act, act, act],
        scratch_shapes=[pltpu.VMEM((t, LANES), F32), pltpu.VMEM((t, LANES), F32)],
        compiler_params=_params("arbitrary", "arbitrary"),
    )(hin, hin, hin, tot, do, upper, lower)


def _adamw(w, g, m, v, *, name):
    r, c = w.shape
    tr = _tile(r, 256, 8) if r % 8 == 0 and r * c * 4 > (1 << 20) else r
    c1, c2 = 1.0 - ADAM_B1 ** ADAM_STEP, 1.0 - ADAM_B2 ** ADAM_STEP

    def body(w_ref, g_ref, m_ref, v_ref, d_ref, nm_ref, nv_ref):
        gv = g_ref[...]
        nm = ADAM_B1 * m_ref[...] + (1.0 - ADAM_B1) * gv
        nv = ADAM_B2 * v_ref[...] + (1.0 - ADAM_B2) * (gv * gv)
        d_ref[...] = -ADAM_LR * ((nm / c1) / (jnp.sqrt(nv / c2) + ADAM_EPS) + ADAM_WD * w_ref[...])
        nm_ref[...] = nm
        nv_ref[...] = nv

    spec = pl.BlockSpec((tr, c), lambda i: (i, 0))
    shape = jax.ShapeDtypeStruct((r, c), F32)
    return _pcall(
        body, name=name, grid=(r // tr,), in_specs=[spec] * 4, out_specs=[spec] * 3, out_shape=[shape] * 3,
        compiler_params=_params("arbitrary"),
    )(w, g, m, v)


HBM = pl.BlockSpec(memory_space=pltpu.HBM)


def _place():
    x, y, c = lax.axis_index("x"), lax.axis_index("y"), lax.axis_index("c")
    return x, y, c, [(1 - x, y), (x, 1 - y), (1 - x, 1 - y)]


def _window(ref, kind, shard_shape, j, half):
    r, w = shard_shape
    if kind == "col":
        return ref.at[pl.ds(half * (r // 2), r // 2), pl.ds(j * w, w)]
    return ref.at[pl.ds(j * r + half * (r // 2), r // 2), :]


def _full_shape(kind, shard_shape):
    r, w = shard_shape
    return (r, 4 * w) if kind == "col" else (4 * r, w)


def _remote(src, dst, send_sem, recv_sem, device):
    return pltpu.make_async_remote_copy(src_ref=src, dst_ref=dst, send_sem=send_sem, recv_sem=recv_sem,
                                        device_id=device, device_id_type=MESH)


def _gather_weights(bigs, kinds, smalls):
    nb, ns = len(bigs), len(smalls)
    shapes = [b.shape for b in bigs]

    def body(*refs):
        big_in, small_in = refs[:nb], refs[nb:nb + ns]
        big_out, small_out = refs[nb + ns:2 * nb + ns], refs[2 * nb + ns:2 * (nb + ns)]
        send_sems, recv_sems, local_sems = refs[2 * (nb + ns):]
        x, y, c, chips = _place()
        me, sibling = 2 * x + y, (x, y, 1 - c)
        local, sent = [], []
        for a in range(nb):
            r, w = shapes[a]
            own = big_out[a].at[:, pl.ds(me * w, w)] if kinds[a] == "col" else big_out[a].at[pl.ds(me * r, r), :]
            local.append(pltpu.make_async_copy(big_in[a], own, local_sems.at[a]))
            for k, chip in enumerate(chips[:2]):
                sent.append(_remote(big_in[a].at[pl.ds(c * (r // 2), r // 2), :], _window(big_out[a], kinds[a], shapes[a], me, c),
                                    send_sems.at[a, k], recv_sems.at[a, k], (*chip, c)))
        for s in range(ns):
            local.append(pltpu.make_async_copy(small_in[s], small_out[s].at[me], local_sems.at[nb + s]))
            for k, chip in enumerate(chips):
                sent.append(_remote(small_in[s], small_out[s].at[me], send_sems.at[nb + s, k], recv_sems.at[nb + s, k], (*chip, c)))
        for cp in local + sent:
            cp.start()
        def arrived(a, k, px, py):
            win = _window(big_out[a], kinds[a], shapes[a], 2 * px + py, c)
            _remote(win, win, send_sems.at[a, k], recv_sems.at[a, k], (px, py, c)).wait_recv()
            fwd = _remote(win, win, send_sems.at[a, 3 + k], recv_sems.at[a, 3 + k], sibling)
            fwd.start()
            sent.append(fwd)

        from_chip, to_chip = ((x + c) % 2, (y + 1 - c) % 2), ((x + 1 - c) % 2, (y + c) % 2)
        for a in range(nb):
            for k, (px, py) in enumerate(chips[:2]):
                arrived(a, k, px, py)
            win = _window(big_out[a], kinds[a], shapes[a], 2 * from_chip[0] + from_chip[1], c)
            relay = _remote(win, win, send_sems.at[a, 2], recv_sems.at[a, 2], (*to_chip, c))
            relay.start()
            sent.append(relay)
        for a in range(nb):
            arrived(a, 2, *chips[2])
        for s in range(ns):
            for k, (px, py) in enumerate(chips):
                dst = small_out[s].at[2 * px + py]
                _remote(dst, dst, send_sems.at[nb + s, k], recv_sems.at[nb + s, k], (px, py, c)).wait_recv()
        for a in range(nb):
            for k, (px, py) in enumerate(chips):
                win = _window(big_out[a], kinds[a], shapes[a], 2 * px + py, 1 - c)
                _remote(win, win, send_sems.at[a, 3 + k], recv_sems.at[a, 3 + k], sibling).wait_recv()
        for cp in sent:
            cp.wait_send()
        for cp in local:
            cp.wait()

    out_shape = [jax.ShapeDtypeStruct(_full_shape(k, s), b.dtype) for b, k, s in zip(bigs, kinds, shapes)]
    out_shape += [jax.ShapeDtypeStruct((4, *s.shape), s.dtype) for s in smalls]
    return _pcall(
        body, name="gather_weights", in_specs=[HBM] * (nb + ns), out_specs=[HBM] * (nb + ns), out_shape=out_shape,
        scratch_shapes=[pltpu.SemaphoreType.DMA((nb + ns, 6)), pltpu.SemaphoreType.DMA((nb + ns, 6)), pltpu.SemaphoreType.DMA((nb + ns,))],
    )(*bigs, *smalls)


def _swap_other_halves(grads, kinds, shapes):
    n = len(grads)

    def body(*refs):
        g_in, land = refs[:n], refs[n:2 * n]
        send_sems, recv_sems = refs[2 * n:]
        x, y, c, _ = _place()
        sibling = (x, y, 1 - c)
        sent = []
        for a in range(n):
            for j in range(4):
                cp = _remote(_window(g_in[a], kinds[a], shapes[a], j, 1 - c), land[a].at[j], send_sems.at[a, j], recv_sems.at[a, j], sibling)
                cp.start()
                sent.append(cp)
        for a in range(n):
            for j in range(4):
                _remote(land[a].at[j], land[a].at[j], send_sems.at[a, j], recv_sems.at[a, j], sibling).wait_recv()
        for cp in sent:
            cp.wait_send()

    return _pcall(
        body, name="swap_other_halves", in_specs=[HBM] * n, out_specs=[HBM] * n,
        out_shape=[jax.ShapeDtypeStruct((4, s[0] // 2, s[1]), F32) for s in shapes],
        scratch_shapes=[pltpu.SemaphoreType.DMA((n, 4)), pltpu.SemaphoreType.DMA((n, 4))],
    )(*grads)


def _pair_sum(place, g, land, kind, shard_shape, *, name):
    r, w = shard_shape
    hr = r // 2
    tr = _tile(hr, 256, 16)
    nr = hr // tr

    def body(place_ref, g_ref, l_ref, o_ref, own_ref):
        total = g_ref[...] + l_ref[...]
        o_ref[...] = total.astype(BF16)

        @pl.when(pl.program_id(1) == place_ref[1])
        def _():
            own_ref[...] = total

    if kind == "col":
        g_spec = pl.BlockSpec((tr, w), lambda i, j, p: (p[0] * nr + i, j))
    else:
        g_spec = pl.BlockSpec((tr, w), lambda i, j, p: ((2 * j + p[0]) * nr + i, 0))
    part = pl.BlockSpec((None, tr, w), lambda i, j, p: (j, i, 0))
    return _pcall(
        body, name=name,
        grid_spec=pltpu.PrefetchScalarGridSpec(num_scalar_prefetch=1, grid=(nr, 4), in_specs=[g_spec, part],
                                               out_specs=[part, pl.BlockSpec((tr, w), lambda i, j, p: (i, 0))]),
        out_shape=[jax.ShapeDtypeStruct((4, hr, w), BF16), jax.ShapeDtypeStruct((hr, w), F32)],
        compiler_params=_params("arbitrary", "arbitrary"),
    )(place, g, land)


def _chip_exchange(parts):
    n = len(parts)

    def body(*refs):
        p_in, land = refs[:n], refs[n:2 * n]
        send_sems, recv_sems = refs[2 * n:]
        x, y, c, chips = _place()
        sent = []
        for a in range(n):
            for k, (px, py) in enumerate(chips):
                cp = _remote(p_in[a].at[2 * px + py], land[a].at[k], send_sems.at[a, k], recv_sems.at[a, k], (px, py, c))
                cp.start()
                sent.append(cp)
        for a in range(n):
            for k, (px, py) in enumerate(chips):
                _remote(land[a].at[k], land[a].at[k], send_sems.at[a, k], recv_sems.at[a, k], (px, py, c)).wait_recv()
        for cp in sent:
            cp.wait_send()

    return _pcall(
        body, name="chip_exchange", in_specs=[HBM] * n, out_specs=[HBM] * n,
        out_shape=[jax.ShapeDtypeStruct((3, *p.shape[1:]), p.dtype) for p in parts],
        scratch_shapes=[pltpu.SemaphoreType.DMA((n, 3)), pltpu.SemaphoreType.DMA((n, 3))],
    )(*parts)


def _chip_sum(place, own, land, *, name):
    hr, w = own.shape
    tr = _tile(hr, 256, 16)
    nr = hr // tr

    def body(place_ref, p_ref, l_ref, o_ref):
        o_ref[...] = ((p_ref[...] + l_ref[0].astype(F32)) + l_ref[1].astype(F32)) + l_ref[2].astype(F32)

    return _pcall(
        body, name=name,
        grid_spec=pltpu.PrefetchScalarGridSpec(
            num_scalar_prefetch=1, grid=(nr,),
            in_specs=[pl.BlockSpec((tr, w), lambda i, p: (i, 0)), pl.BlockSpec((3, tr, w), lambda i, p: (0, i, 0))],
            out_specs=pl.BlockSpec((tr, w), lambda i, p: (p[0] * nr + i, 0))),
        out_shape=jax.ShapeDtypeStruct((2 * hr, w), F32),
        compiler_params=_params("arbitrary"),
    )(place, own, land)


def _swap_reduced_halves(halves):
    n = len(halves)

    def body(*refs):
        src, out = refs[:n], refs[n:2 * n]
        send_sems, recv_sems = refs[2 * n:]
        x, y, c, _ = _place()
        sibling = (x, y, 1 - c)
        sent = []
        for a in range(n):
            hr = out[a].shape[0] // 2
            cp = _remote(src[a].at[pl.ds(c * hr, hr), :], out[a].at[pl.ds(c * hr, hr), :], send_sems.at[a], recv_sems.at[a], sibling)
            cp.start()
            sent.append(cp)
        for a in range(n):
            hr = out[a].shape[0] // 2
            other = out[a].at[pl.ds((1 - c) * hr, hr), :]
            _remote(other, other, send_sems.at[a], recv_sems.at[a], sibling).wait_recv()
        for cp in sent:
            cp.wait_send()

    return _pcall(
        body, name="swap_reduced_halves", in_specs=[HBM] * n, out_specs=[HBM] * n,
        out_shape=[jax.ShapeDtypeStruct(h.shape, F32) for h in halves],
        input_output_aliases={a: a for a in range(n)},
        scratch_shapes=[pltpu.SemaphoreType.DMA((n,)), pltpu.SemaphoreType.DMA((n,))],
    )(*halves)


def _sum_small(packed):
    rows = packed.shape[0]

    def body(in_ref, out_ref, land, send_sems, recv_sems, local_sem):
        x, y, c, _ = _place()
        me = 4 * x + 2 * y + c
        own = pltpu.make_async_copy(in_ref, land.at[me], local_sem)
        own.start()
        flips = [(r >> 2, (r >> 1) & 1, r & 1) for r in range(1, 8)]
        sent = []
        for k, (fx, fy, fc) in enumerate(flips):
            cp = _remote(in_ref, land.at[me], send_sems.at[k], recv_sems.at[k], (x ^ fx, y ^ fy, c ^ fc))
            cp.start()
            sent.append(cp)
        for k, (fx, fy, fc) in enumerate(flips):
            src = land.at[4 * (x ^ fx) + 2 * (y ^ fy) + (c ^ fc)]
            _remote(src, src, send_sems.at[k], recv_sems.at[k], (x ^ fx, y ^ fy, c ^ fc)).wait_recv()
        for cp in sent:
            cp.wait_send()
        own.wait()
        total = land[0]
        for dev in range(1, 8):
            total = total + land[dev]
        out_ref[...] = total

    vmem = pl.BlockSpec(memory_space=pltpu.VMEM)
    return _pcall(
        body, name="sum_small", in_specs=[vmem], out_specs=vmem, out_shape=jax.ShapeDtypeStruct(packed.shape, F32),
        scratch_shapes=[pltpu.VMEM((8, rows, LANES), F32), pltpu.SemaphoreType.DMA((7,)), pltpu.SemaphoreType.DMA((7,)), pltpu.SemaphoreType.DMA],
    )(packed)


def _sigmoid(v):
    return 1.0 / (1.0 + jnp.exp(-v))


def kernel(x, meta_tokens, g_pre_mix, w_in, conv_w_mix, w_proj_conv, w_proj_attn, b_gate, w_out, g_post_mix, g_pre_ffn, w_up_gate, conv_w_ffn, w_down, g_post_ffn, loss_target, m_meta_tokens, m_g_pre_mix, m_w_in, m_conv_w_mix, m_w_proj_conv, m_w_proj_attn, m_b_gate, m_w_out, m_g_post_mix, m_g_pre_ffn, m_w_up_gate, m_conv_w_ffn, m_w_down, m_g_post_ffn, v_meta_tokens, v_g_pre_mix, v_w_in, v_conv_w_mix, v_w_proj_conv, v_w_proj_attn, v_b_gate, v_w_out, v_g_post_mix, v_g_pre_ffn, v_w_up_gate, v_conv_w_ffn, v_w_down, v_g_post_ffn):
    seq, d = x.shape[1], x.shape[2]
    f = w_down.shape[1] * 4
    real = N_META + seq
    t = -(-real // ATT_BLOCK) * ATT_BLOCK
    nd = d // LANES
    cx, cy, cc = lax.axis_index("x"), lax.axis_index("y"), lax.axis_index("c")
    chip = 2 * cx + cy
    place = jnp.stack([cc, chip]).astype(jnp.int32)

    big_names = ["w_in", "w_proj_conv", "w_proj_attn", "w_out", "w_up_gate", "w_down"]
    kinds = ["col", "row", "row", "row", "col", "row"]
    big_w = [w_in[0], w_proj_conv[0], w_proj_attn[0], w_out[0], w_up_gate[0], w_down[0]]
    small_w = [meta_tokens, conv_w_mix[0], b_gate[0], conv_w_ffn[0]]
    gathered = _gather_weights([w.astype(BF16) for w in big_w], kinds, small_w)
    wi, wpc, wpa, wo, wug, wd = gathered[:6]
    meta_f, cwm_f, bg_f, cwf_f = (jnp.moveaxis(s, 0, 1).reshape(s.shape[1], -1) for s in gathered[6:])

    h0 = jnp.concatenate([meta_f, x[0], jnp.zeros((t - real, d), F32)], axis=0)
    target = jnp.concatenate([jnp.zeros((N_META, d), F32), loss_target[0], jnp.zeros((t - real, d), F32)], axis=0)
    xn1 = _prenorm(h0, g_pre_mix, name="prenorm_mix")
    hin = _matmul(xn1, wi, name="in_proj")
    yconv = _mixer_conv_fwd(hin, cwm_f, d)
    o, att_tot = _attention_fwd(hin, d)

    def gates(gpre, bias):
        return _sigmoid(gpre[:, :d].astype(F32) + bias[0:1]), _sigmoid(gpre[:, d:].astype(F32) + bias[1:2])

    def merge_epi(accs, rows, vecs, row0):
        gate_c, gate_a = gates(rows[0], vecs[0])
        return [gate_c * accs[0] + gate_a * accs[1], accs[0], accs[1]]

    act = ("row", d, BF16)
    merged, bconv, battn = _matmul_rows([(yconv, wpc), (o, wpa)], [(hin, 2 * d, 3)], [bg_f], merge_epi, [act, act, act], name="branch_proj")

    def mix_epi(accs, rows, vecs, row0):
        mix = accs[0]
        h1 = rows[0] + mix * _rms(mix) * vecs[0]
        return [mix, h1, h1 * _rms(h1) * vecs[1]]

    mix, h1, xn2 = _matmul_rows([(merged, wo)], [(h0, d, 0)], [g_post_mix, g_pre_ffn], mix_epi,
                                [("row", d, F32), ("row", d, F32), act], name="out_proj")
    ug = _matmul(xn2, wug, name="up_proj")
    fact = _ffn_conv_fwd(ug, cwf_f, f)

    def loss_epi(accs, rows, vecs, row0):
        ffn = accs[0]
        r = _rms(ffn)
        fh = ffn * r
        h2 = rows[0] + fh * vecs[0]
        rid = row0 + lax.broadcasted_iota(jnp.int32, (ffn.shape[0], 1), 0)
        err = jnp.where((rid >= N_META) & (rid < real), h2 - rows[1], 0.0)
        dy = err * (1.0 / d)
        dffn, dg = _rms_bwd(dy, fh, r, vecs[0])
        loss = jnp.zeros((1, LANES), F32) + 0.5 * jnp.sum(err * err) / d
        return [dffn, dy, dg, loss]

    dffn, dh2, dg_post_ffn, loss_part = _matmul_rows(
        [(fact, wd)], [(h1, d, 0), (target, d, 0)], [g_post_ffn], loss_epi,
        [act, ("row", d, F32), ("acc", 1, d), ("acc", 1, LANES)], name="down_proj_loss")

    gw_down = _matmul_tn(fact, dffn, name="grad_w_down")
    df = _matmul(dffn, wd, nt=True, name="down_proj_bwd")
    du2, dg2, gcw_ffn = _ffn_conv_bwd(ug, df, cwf_f, f)

    def ffn_in_epi(accs, rows, vecs, row0):
        h1v, dyv, mixv = rows
        r3 = _rms(h1v)
        dh1n, dg3 = _rms_bwd(accs[0], h1v * r3, r3, vecs[0])
        dh1 = dyv + dh1n
        r2 = _rms(mixv)
        dmix, dgm = _rms_bwd(dh1, mixv * r2, r2, vecs[1])
        return [dh1, dmix, dg3, dgm]

    dh1, dmix, dg_pre_ffn, dg_post_mix, dug = _matmul_rows_joined(
        [du2, dg2], wug, [(h1, d, 0), (dh2, d, 0), (mix, d, 0)], [g_pre_ffn, g_post_mix], ffn_in_epi,
        [("row", d, F32), act, ("acc", 1, d), ("acc", 1, d)], tk=f // 2, name="up_proj_bwd")
    gw_up = _matmul_tn(xn2, dug, name="grad_w_up_gate")
    gw_out = _matmul_tn(merged, dmix, name="grad_w_out")

    def merge_bwd_epi(accs, rows, vecs, row0):
        dm = accs[0]
        gate_c, gate_a = gates(rows[0], vecs[0])
        dpre_c = dm * rows[1].astype(F32) * gate_c * (1.0 - gate_c)
        dpre_a = dm * rows[2].astype(F32) * gate_a * (1.0 - gate_a)
        dbias = jnp.concatenate([jnp.sum(dpre_c, axis=0, keepdims=True), jnp.sum(dpre_a, axis=0, keepdims=True)], axis=0)
        return [dm * gate_c, dm * gate_a, jnp.concatenate([dpre_c, dpre_a], axis=1), dbias]

    dbconv, dbattn, dgates, gb_gate = _matmul_rows(
        [(dmix, wo)], [(hin, 2 * d, 3), (bconv, d, 0), (battn, d, 0)], [bg_f], merge_bwd_epi,
        [act, act, ("row", 2 * d, BF16), ("acc", 2, d)], nt=True, name="out_proj_bwd")
    gw_pc = _matmul_tn(yconv, dbconv, name="grad_w_proj_conv")
    gw_pa = _matmul_tn(o, dbattn, name="grad_w_proj_attn")
    dyconv = _matmul(dbconv, wpc, nt=True, name="proj_conv_bwd")
    do = _matmul(dbattn, wpa, nt=True, name="proj_attn_bwd")
    db, dc, dhc, gcw_mix = _mixer_conv_bwd(hin, dyconv, cwm_f, d)
    dq, dk, dv = _attention_bwd(hin, att_tot, do, d)

    def in_epi(accs, rows, vecs, row0):
        h0v, dh1v = rows
        r1 = _rms(h0v)
        dh0n, dg1 = _rms_bwd(accs[0], h0v * r1, r1, vecs[0])
        return [dh1v + dh0n, dg1]

    dh0, dg_pre_mix, dhin = _matmul_rows_joined(
        [db, dc, dhc, dq, dk, dv, dgates], wi, [(h0, d, 0), (dh1, d, 0)], [g_pre_mix], in_epi,
        [("row", d, F32), ("acc", 1, d)], tk=d, name="in_proj_bwd")
    gw_in = _matmul_tn(xn1, dhin, name="grad_w_in")
    grad_x = dh0[N_META:real][None]

    big_g = [gw_in, gw_pc, gw_pa, gw_out, gw_up, gw_down]
    shapes = [w.shape for w in big_w]
    landed = _swap_other_halves(big_g, kinds, shapes)
    parts, own = zip(*[_pair_sum(place, g, l, k, s, name=f"pair_sum_{n}") for g, l, k, s, n in zip(big_g, landed, kinds, shapes, big_names)])
    landed = _chip_exchange(parts)
    halves = [_chip_sum(place, p, l, name=f"chip_sum_{n}") for p, l, n in zip(own, landed, big_names)]
    big_r = _swap_reduced_halves(halves)

    small_g = [dh0[:N_META], dg_pre_mix, gcw_mix, gb_gate, dg_post_mix, dg_pre_ffn, gcw_ffn, dg_post_ffn, loss_part]
    flat = jnp.concatenate([s.reshape(-1) for s in small_g])
    pad = -flat.shape[0] % (8 * LANES)
    summed = _sum_small(jnp.pad(flat, (0, pad)).reshape(-1, LANES)).reshape(-1)
    small_r, pos = [], 0
    for s in small_g:
        small_r.append(summed[pos:pos + s.size].reshape(s.shape))
        pos += s.size
    g_meta, g_g1, g_cwm, g_bg, g_g2, g_g3, g_cwf, g_g4, loss_row = small_r
    loss = loss_row[0, 0]

    def my_cols(full, width):
        return lax.dynamic_slice_in_dim(full, chip * width, width, axis=1)

    grads = {
        "meta_tokens": my_cols(g_meta, d // 4), "g_pre_mix": g_g1, "w_in": big_r[0][None],
        "conv_w_mix": my_cols(g_cwm, d // 4)[None], "w_proj_conv": big_r[1][None], "w_proj_attn": big_r[2][None],
        "b_gate": my_cols(g_bg, d // 4)[None], "w_out": big_r[3][None], "g_post_mix": g_g2, "g_pre_ffn": g_g3,
        "w_up_gate": big_r[4][None], "conv_w_ffn": my_cols(g_cwf, f // 4)[None], "w_down": big_r[5][None], "g_post_ffn": g_g4,
    }
    weights = dict(meta_tokens=meta_tokens, g_pre_mix=g_pre_mix, w_in=w_in, conv_w_mix=conv_w_mix, w_proj_conv=w_proj_conv,
                   w_proj_attn=w_proj_attn, b_gate=b_gate, w_out=w_out, g_post_mix=g_post_mix, g_pre_ffn=g_pre_ffn,
                   w_up_gate=w_up_gate, conv_w_ffn=conv_w_ffn, w_down=w_down, g_post_ffn=g_post_ffn)
    m_in = dict(meta_tokens=m_meta_tokens, g_pre_mix=m_g_pre_mix, w_in=m_w_in, conv_w_mix=m_conv_w_mix, w_proj_conv=m_w_proj_conv,
                w_proj_attn=m_w_proj_attn, b_gate=m_b_gate, w_out=m_w_out, g_post_mix=m_g_post_mix, g_pre_ffn=m_g_pre_ffn,
                w_up_gate=m_w_up_gate, conv_w_ffn=m_conv_w_ffn, w_down=m_w_down, g_post_ffn=m_g_post_ffn)
    v_in = dict(meta_tokens=v_meta_tokens, g_pre_mix=v_g_pre_mix, w_in=v_w_in, conv_w_mix=v_conv_w_mix, w_proj_conv=v_w_proj_conv,
                w_proj_attn=v_w_proj_attn, b_gate=v_b_gate, w_out=v_w_out, g_post_mix=v_g_post_mix, g_pre_ffn=v_g_pre_ffn,
                w_up_gate=v_w_up_gate, conv_w_ffn=v_conv_w_ffn, w_down=v_w_down, g_post_ffn=v_g_post_ffn)
    names = list(weights)
    deltas, new_m, new_v = [], [], []
    for n in names:
        shape = weights[n].shape
        two_d = (-1, shape[-1])
        dl, nm, nv = _adamw(weights[n].reshape(two_d), grads[n].reshape(two_d), m_in[n].reshape(two_d), v_in[n].reshape(two_d),
                            name=f"adamw_{n}")
        deltas.append(dl.reshape(shape))
        new_m.append(nm.reshape(shape))
        new_v.append(nv.reshape(shape))
    return (loss, grad_x, *[grads[n].reshape(weights[n].shape) for n in names], *deltas, *new_m, *new_v)
```

```python
import functools
import math

import jax
import jax.numpy as jnp
from jax import lax
from jax.experimental import pallas as pl
from jax.experimental.pallas import tpu as pltpu

F32 = jnp.float32
BF16 = jnp.bfloat16

N_META = 16
HEAD_DIM = 64
LANES = 128
RMS_EPS = 1e-6
ATT_BLOCK = 256
VMEM_LIMIT = 56 * 1024 * 1024

ADAM_LR = 0.001
ADAM_B1 = 0.9
ADAM_B2 = 0.999
ADAM_EPS = 1e-08
ADAM_WD = 0.01
ADAM_STEP = 10

MESH = pl.DeviceIdType.MESH


def _tile(n, cap, unit=LANES):
    d = (min(cap, n) // unit) * unit
    while d >= unit:
        if n % d == 0:
            return d
        d -= unit
    raise ValueError(f"no tile for {n} under {cap}")


MXU_EDGE = 256


def _mxu_tile(n, cap):
    if n <= cap:
        return n
    return _tile(n, cap, MXU_EDGE) if n % MXU_EDGE == 0 else _tile(n, cap)


def _params(*sem):
    return pltpu.CompilerParams(dimension_semantics=sem, vmem_limit_bytes=VMEM_LIMIT)


def _pcall(body, **kw):
    return pl.pallas_call(body, **kw)


def _dot(a, b, dims):
    return lax.dot_general(a, b, (dims, ((), ())), preferred_element_type=F32)


NN = ((1,), (0,))
NT = ((1,), (1,))
TN = ((0,), (0,))


def _matmul(a, b, *, nt=False, out_dtype=BF16, name):
    m, kdim = a.shape
    n = b.shape[0] if nt else b.shape[1]
    tm, tn, tk = _tile(m, 640, 16), _mxu_tile(n, 2816), _mxu_tile(kdim, 2816)
    nk = kdim // tk

    def body(a_ref, b_ref, o_ref, *scratch):
        p = _dot(a_ref[...], b_ref[...], NT if nt else NN)
        if nk == 1:
            o_ref[...] = p.astype(o_ref.dtype)
            return
        acc, k = scratch[0], pl.program_id(2)

        @pl.when(k == 0)
        def _():
            acc[...] = p

        @pl.when(k > 0)
        def _():
            acc[...] += p

        @pl.when(k == nk - 1)
        def _():
            o_ref[...] = acc[...].astype(o_ref.dtype)

    b_spec = pl.BlockSpec((tn, tk), lambda j, i, k: (j, k)) if nt else pl.BlockSpec((tk, tn), lambda j, i, k: (k, j))
    return _pcall(
        body, name=name, grid=(n // tn, m // tm, nk),
        in_specs=[pl.BlockSpec((tm, tk), lambda j, i, k: (i, k)), b_spec],
        out_specs=pl.BlockSpec((tm, tn), lambda j, i, k: (i, j)),
        out_shape=jax.ShapeDtypeStruct((m, n), out_dtype),
        scratch_shapes=[pltpu.VMEM((tm, tn), F32)] if nk > 1 else [],
        compiler_params=_params("arbitrary", "arbitrary", "arbitrary"),
    )(a, b)


def _matmul_tn(a, b, *, name):
    t, ka = a.shape
    nb = b.shape[1]
    tb = _mxu_tile(nb, 1024)
    budget = VMEM_LIMIT * 5 // 7
    ta = next(c for c in range(ka, 0, -LANES) if ka % c == 0 and 2 * (2 * t * (c + tb) + 4 * c * tb) <= budget)

    def body(a_ref, b_ref, o_ref):
        o_ref[...] = _dot(a_ref[...], b_ref[...], TN)

    return _pcall(
        body, name=name, grid=(ka // ta, nb // tb),
        in_specs=[pl.BlockSpec((t, ta), lambda i, j: (0, i)), pl.BlockSpec((t, tb), lambda i, j: (0, j))],
        out_specs=pl.BlockSpec((ta, tb), lambda i, j: (i, j)),
        out_shape=jax.ShapeDtypeStruct((ka, nb), F32),
        compiler_params=_params("arbitrary", "arbitrary"),
    )(a, b)


def _matmul_rows(pairs, rows, vecs, epi, outs, *, nt=False, name):
    m, kdim = pairs[0][0].shape
    n = pairs[0][1].shape[0] if nt else pairs[0][1].shape[1]
    tm, tk = _tile(m, 640, 16), _mxu_tile(kdim, 2816)
    nk, npair = kdim // tk, len(pairs)

    def body(*refs):
        a_refs, b_refs = refs[0:2 * npair:2], refs[1:2 * npair:2]
        pos = 2 * npair
        row_refs = refs[pos:pos + len(rows)]
        pos += len(rows)
        vec_refs = refs[pos:pos + len(vecs)]
        pos += len(vecs)
        out_refs = refs[pos:pos + len(outs)]
        accs = refs[pos + len(outs):]
        i, k = pl.program_id(0), pl.program_id(1)
        prods = [_dot(a[...], b[...], NT if nt else NN) for a, b in zip(a_refs, b_refs)]
        if nk > 1:
            @pl.when(k == 0)
            def _():
                for acc, p in zip(accs, prods):
                    acc[...] = p

            @pl.when(k > 0)
            def _():
                for acc, p in zip(accs, prods):
                    acc[...] += p

        @pl.when(k == nk - 1)
        def _():
            vals = [acc[...] for acc in accs] if nk > 1 else prods
            res = epi(vals, [r[...] for r in row_refs], [v[...] for v in vec_refs], i * tm)
            for o_ref, spec, val in zip(out_refs, outs, res):
                if spec[0] == "row":
                    o_ref[...] = val.astype(o_ref.dtype)
                else:
                    @pl.when(i == 0)
                    def _():
                        o_ref[...] = val

                    @pl.when(i > 0)
                    def _():
                        o_ref[...] += val

    in_specs, args = [], []
    for a, b in pairs:
        in_specs += [pl.BlockSpec((tm, tk), lambda i, k: (i, k)),
                     pl.BlockSpec((n, tk), lambda i, k: (0, k)) if nt else pl.BlockSpec((tk, n), lambda i, k: (k, 0))]
        args += [a, b]
    for arr, width, cb in rows:
        in_specs.append(pl.BlockSpec((tm, width), functools.partial(lambda i, k, cb: (i, cb), cb=cb)))
        args.append(arr)
    for v in vecs:
        in_specs.append(pl.BlockSpec(v.shape, lambda i, k: (0, 0)))
        args.append(v)
    out_specs, out_shape = [], []
    for spec in outs:
        if spec[0] == "row":
            out_specs.append(pl.BlockSpec((tm, spec[1]), lambda i, k: (i, 0)))
            out_shape.append(jax.ShapeDtypeStruct((m, spec[1]), spec[2]))
        else:
            out_specs.append(pl.BlockSpec((spec[1], spec[2]), lambda i, k: (0, 0)))
            out_shape.append(jax.ShapeDtypeStruct((spec[1], spec[2]), F32))
    return _pcall(
        body, name=name, grid=(m // tm, nk), in_specs=in_specs, out_specs=out_specs, out_shape=out_shape,
        scratch_shapes=[pltpu.VMEM((tm, n), F32) for _ in pairs] if nk > 1 else [],
        compiler_params=_params("arbitrary", "arbitrary"),
    )(*args)


def _matmul_rows_joined(segments, b, rows, vecs, epi, outs, *, tk, name):
    m = segments[0].shape[0]
    n, kdim = b.shape
    tm = _tile(m, 640, 16)
    nk = kdim // tk
    first = [0]
    for s in segments:
        assert s.shape[1] % tk == 0
        first.append(first[-1] + s.shape[1] // tk)
    assert first[-1] == nk

    def body(*refs):
        seg_refs, b_ref = refs[:len(segments)], refs[len(segments)]
        pos = len(segments) + 1
        row_refs = refs[pos:pos + len(rows)]
        pos += len(rows)
        vec_refs = refs[pos:pos + len(vecs)]
        pos += len(vecs)
        out_refs, joined_ref, acc = refs[pos:pos + len(outs)], refs[pos + len(outs)], refs[pos + len(outs) + 1]
        i, k = pl.program_id(0), pl.program_id(1)
        for s, seg in enumerate(seg_refs):
            @pl.when((k >= first[s]) & (k < first[s + 1]))
            def _():
                part = seg[...]
                joined_ref[...] = part
                p = _dot(part, b_ref[...], NT)

                @pl.when(k == 0)
                def _():
                    acc[...] = p

                @pl.when(k > 0)
                def _():
                    acc[...] += p

        @pl.when(k == nk - 1)
        def _():
            res = epi([acc[...]], [r[...] for r in row_refs], [v[...] for v in vec_refs], i * tm)
            for o_ref, spec, val in zip(out_refs, outs, res):
                if spec[0] == "row":
                    o_ref[...] = val.astype(o_ref.dtype)
                else:
                    @pl.when(i == 0)
                    def _():
                        o_ref[...] = val

                    @pl.when(i > 0)
                    def _():
                        o_ref[...] += val

    def seg_spec(s):
        last = first[s + 1] - first[s] - 1
        return pl.BlockSpec((tm, tk), lambda i, k: (i, jnp.clip(k - first[s], 0, last)))

    in_specs = [seg_spec(s) for s in range(len(segments))] + [pl.BlockSpec((n, tk), lambda i, k: (0, k))]
    args = [*segments, b]
    for arr, width, cb in rows:
        in_specs.append(pl.BlockSpec((tm, width), functools.partial(lambda i, k, cb: (i, cb), cb=cb)))
        args.append(arr)
    for v in vecs:
        in_specs.append(pl.BlockSpec(v.shape, lambda i, k: (0, 0)))
        args.append(v)
    out_specs, out_shape = [], []
    for spec in outs:
        if spec[0] == "row":
            out_specs.append(pl.BlockSpec((tm, spec[1]), lambda i, k: (i, 0)))
            out_shape.append(jax.ShapeDtypeStruct((m, spec[1]), spec[2]))
        else:
            out_specs.append(pl.BlockSpec((spec[1], spec[2]), lambda i, k: (0, 0)))
            out_shape.append(jax.ShapeDtypeStruct((spec[1], spec[2]), F32))
    out_specs.append(pl.BlockSpec((tm, tk), lambda i, k: (i, k)))
    out_shape.append(jax.ShapeDtypeStruct((m, kdim), segments[0].dtype))
    return _pcall(
        body, name=name, grid=(m // tm, nk), in_specs=in_specs, out_specs=out_specs, out_shape=out_shape,
        scratch_shapes=[pltpu.VMEM((tm, n), F32)],
        compiler_params=_params("arbitrary", "arbitrary"),
    )(*args)


def _rms(v):
    return lax.rsqrt(jnp.mean(v * v, axis=-1, keepdims=True) + RMS_EPS)


def _rms_bwd(dz, vhat, r, g):
    t = dz * g
    dv = r * (t - vhat * jnp.mean(t * vhat, axis=-1, keepdims=True))
    return dv, jnp.sum(dz * vhat, axis=0, keepdims=True)


def _prenorm(h, g, *, name):
    m, d = h.shape
    tm = _tile(m, 640, 16)

    def body(h_ref, g_ref, o_ref):
        v = h_ref[...]
        o_ref[...] = (v * _rms(v) * g_ref[...]).astype(BF16)

    return _pcall(
        body, name=name, grid=(m // tm,),
        in_specs=[pl.BlockSpec((tm, d), lambda i: (i, 0)), pl.BlockSpec((1, d), lambda i: (0, 0))],
        out_specs=pl.BlockSpec((tm, d), lambda i: (i, 0)),
        out_shape=jax.ShapeDtypeStruct((m, d), BF16),
        compiler_params=_params("arbitrary"),
    )(h, g)


def _shift_down(u, k):
    rows = lax.broadcasted_iota(jnp.int32, u.shape, 0)
    return jnp.where(rows >= k, pltpu.roll(u, k, 0), 0.0)


def _shift_up(u, k):
    n = u.shape[0]
    rows = lax.broadcasted_iota(jnp.int32, u.shape, 0)
    return jnp.where(rows < n - k, pltpu.roll(u, n - k, 0), 0.0)


def _conv(u, w):
    return w[2:3] * u + w[1:2] * _shift_down(u, 1) + w[0:1] * _shift_down(u, 2)


def _conv_bwd(dcu, u, w):
    du = w[2:3] * dcu + w[1:2] * _shift_up(dcu, 1) + w[0:1] * _shift_up(dcu, 2)
    dw = [jnp.sum(dcu * _shift_down(u, 2 - k) if k < 2 else dcu * u, axis=0, keepdims=True) for k in range(3)]
    return du, dw


def _strip(arr, t, cb0):
    return pl.BlockSpec((t, LANES), functools.partial(lambda s, cb0: (0, cb0 + s), cb0=cb0))


def _mixer_conv_fwd(hin, w, d):
    t, ns = hin.shape[0], d // LANES

    def body(b_ref, c_ref, h_ref, w_ref, y_ref):
        u = c_ref[...].astype(F32) * h_ref[...].astype(F32)
        y_ref[...] = (b_ref[...].astype(F32) * _conv(u, w_ref[...])).astype(BF16)

    return _pcall(
        body, name="mixer_conv_fwd", grid=(ns,),
        in_specs=[_strip(hin, t, 0), _strip(hin, t, ns), _strip(hin, t, 2 * ns), pl.BlockSpec((3, LANES), lambda s: (0, s))],
        out_specs=pl.BlockSpec((t, LANES), lambda s: (0, s)),
        out_shape=jax.ShapeDtypeStruct((t, d), BF16),
        compiler_params=_params("arbitrary"),
    )(hin, hin, hin, w)


def _mixer_conv_bwd(hin, dy, w, d):
    t, ns = hin.shape[0], d // LANES

    def body(b_ref, c_ref, h_ref, dy_ref, w_ref, db_ref, dc_ref, dh_ref, dw_ref):
        b, c, h, g = (r[...].astype(F32) for r in (b_ref, c_ref, h_ref, dy_ref))
        wv = w_ref[...]
        u = c * h
        db_ref[...] = (g * _conv(u, wv)).astype(BF16)
        du, dw = _conv_bwd(g * b, u, wv)
        dc_ref[...] = (du * h).astype(BF16)
        dh_ref[...] = (du * c).astype(BF16)
        for k in range(3):
            dw_ref[k:k + 1, :] = dw[k]

    col = pl.BlockSpec((t, LANES), lambda s: (0, s))
    act = jax.ShapeDtypeStruct((t, d), BF16)
    return _pcall(
        body, name="mixer_conv_bwd", grid=(ns,),
        in_specs=[_strip(hin, t, 0), _strip(hin, t, ns), _strip(hin, t, 2 * ns), col, pl.BlockSpec((3, LANES), lambda s: (0, s))],
        out_specs=[col, col, col, pl.BlockSpec((3, LANES), lambda s: (0, s))],
        out_shape=[act, act, act, jax.ShapeDtypeStruct((3, d), F32)],
        compiler_params=_params("arbitrary"),
    )(hin, hin, hin, dy, w)


GELU_C = math.sqrt(2.0 / math.pi)
GELU_A = 0.044715


def _gelu_tanh(x):
    return jnp.tanh(GELU_C * (x + GELU_A * x * x * x))


def _ffn_conv_fwd(ug, w, f):
    t, ns = ug.shape[0], f // LANES

    def body(u_ref, g_ref, w_ref, o_ref):
        cu = _conv(u_ref[...].astype(F32), w_ref[...])
        o_ref[...] = (0.5 * cu * (1.0 + _gelu_tanh(cu)) * g_ref[...].astype(F32)).astype(BF16)

    return _pcall(
        body, name="ffn_conv_fwd", grid=(ns,),
        in_specs=[_strip(ug, t, 0), _strip(ug, t, ns), pl.BlockSpec((3, LANES), lambda s: (0, s))],
        out_specs=pl.BlockSpec((t, LANES), lambda s: (0, s)),
        out_shape=jax.ShapeDtypeStruct((t, f), BF16),
        compiler_params=_params("arbitrary"),
    )(ug, ug, w)


def _ffn_conv_bwd(ug, df, w, f):
    t, ns = ug.shape[0], f // LANES

    def body(u_ref, g_ref, df_ref, w_ref, du_ref, dg_ref, dw_ref):
        u, g, d = (r[...].astype(F32) for r in (u_ref, g_ref, df_ref))
        wv = w_ref[...]
        cu = _conv(u, wv)
        th = _gelu_tanh(cu)
        dg_ref[...] = (d * 0.5 * cu * (1.0 + th)).astype(BF16)
        dgelu = 0.5 * (1.0 + th) + 0.5 * cu * (1.0 - th * th) * GELU_C * (1.0 + 3.0 * GELU_A * cu * cu)
        du, dw = _conv_bwd(d * g * dgelu, u, wv)
        du_ref[...] = du.astype(BF16)
        for k in range(3):
            dw_ref[k:k + 1, :] = dw[k]

    col = pl.BlockSpec((t, LANES), lambda s: (0, s))
    act = jax.ShapeDtypeStruct((t, f), BF16)
    return _pcall(
        body, name="ffn_conv_bwd", grid=(ns,),
        in_specs=[_strip(ug, t, 0), _strip(ug, t, ns), col, pl.BlockSpec((3, LANES), lambda s: (0, s))],
        out_specs=[col, col, pl.BlockSpec((3, LANES), lambda s: (0, s))],
        out_shape=[act, act, jax.ShapeDtypeStruct((3, f), F32)],
        compiler_params=_params("arbitrary"),
    )(ug, ug, df, w)


def _log_terms(z):
    minus_abs = lax.bitcast_convert_type(lax.bitcast_convert_type(z, jnp.uint32) | jnp.uint32(0x80000000), F32)
    lb = jnp.minimum(z, 0.0) - jnp.log(1.0 + jnp.exp(minus_abs))
    return lb, lb - z


SUM_TERMS = 1


def _split_cat(v, axis):
    terms = [v.astype(BF16)]
    for _ in range(SUM_TERMS - 1):
        v = v - terms[-1].astype(F32)
        terms.append(v.astype(BF16))
    return terms[0] if SUM_TERMS == 1 else jnp.concatenate(terms, axis=axis)


def _head_masks():
    lane = lax.broadcasted_iota(jnp.int32, (1, LANES), 1)
    return lane < HEAD_DIM, lane >= HEAD_DIM


def _attention_fwd(hin, d, shards, kinds):
    t, blk, nd = hin.shape[0], ATT_BLOCK, d // LANES
    nq = t // blk
    scale = 1.0 / math.sqrt(HEAD_DIM)
    tri = jnp.tril(jnp.ones((blk, blk), F32)).astype(BF16)
    tri = jnp.concatenate([tri] * SUM_TERMS, axis=0)
    ns = len(shards)
    shapes = [s.shape for s in shards]

    def body(q_ref, k_ref, v_ref, tri_ref, *rest):
        shard_refs, (o_ref, tot_ref), full_refs, sems = rest[:ns], rest[ns:ns + 2], rest[ns + 2:2 * ns + 2], rest[2 * ns + 2:]
        p, i = pl.program_id(0), pl.program_id(1)
        start, forward, finish = _gather_phases(shard_refs, full_refs, kinds, shapes, *sems)
        pl.when((p == 0) & (i == 0))(start)
        pl.when((p == nd // 2) & (i == 0))(forward)
        masks = _head_masks()
        q = q_ref[...].astype(F32) * scale
        qh = [jnp.where(mh, q, 0.0).astype(BF16) for mh in masks]
        after = lax.broadcasted_iota(jnp.int32, (blk, blk), 0) > lax.broadcasted_iota(jnp.int32, (blk, blk), 1)
        tri_v = tri_ref[...]

        def scores(j):
            j0 = pl.multiple_of(j * blk, blk)
            kj = k_ref[pl.ds(j0, blk), :]
            return tuple(_dot(qh[h], kj, NT) for h in range(2))

        def weights(z, c, diagonal, between=lambda: None):
            halves = [slice(r, r + blk // 2) for r in range(0, blk, blk // 2)]
            other = between()
            sums = []
            for h in range(2):
                for rows in halves:
                    lf = _log_terms(z[h][rows])[1]
                    if diagonal:
                        lf = jnp.where(after[rows], lf, 0.0)
                    sums.append(_dot(_split_cat(lf, 1), tri_v, NN))
            sums = [jnp.concatenate(sums[2 * h:2 * h + 2], axis=0) for h in range(2)]
            a = [jnp.exp(z[h] + sums[h] + c[h]) for h in range(2)]
            if diagonal:
                a = [jnp.where(after, ah, 0.0) for ah in a]
            return tuple(ah.astype(BF16) for ah in a), tuple(c[h] + sums[h][:, 0:1] for h in range(2)), other

        def apply(acc, a, j):
            j0 = pl.multiple_of(j * blk, blk)
            vj = v_ref[pl.ds(j0, blk), :]
            for h in range(2):
                acc = acc + _dot(a[h], jnp.where(masks[h], vj, jnp.zeros_like(vj)), NN)
            return acc

        zero = jnp.zeros((blk, 1), F32)
        a, c, _ = weights(scores(i), (zero, zero), True)

        def step(m, carry):
            z, a, c, acc = carry
            j = i - 1 - m
            a, c, (z_next, acc) = weights(z, c, False, lambda: (scores(jnp.maximum(j - 1, 0)), apply(acc, a, j + 1)))
            return z_next, a, c, acc

        _, a, c, acc = lax.fori_loop(0, i, step, (scores(jnp.maximum(i - 1, 0)), a, c, jnp.zeros((blk, LANES), F32)))
        o_ref[...] = apply(acc, a, 0).astype(BF16)
        tot_ref[...] = jnp.where(masks[0], c[0], c[1])
        pl.when((p == nd - 1) & (i == nq - 1))(finish)

    def whole(cb0):
        return pl.BlockSpec((t, LANES), functools.partial(lambda p, i, cb0: (0, cb0 + p), cb0=cb0))

    tile = pl.BlockSpec((blk, LANES), lambda p, i: (i, p))
    res = _pcall(
        body, name="attention_fwd", grid=(nd, nq),
        in_specs=[pl.BlockSpec((blk, LANES), lambda p, i: (i, 3 * nd + p)), whole(4 * nd), whole(5 * nd),
                  pl.BlockSpec((SUM_TERMS * blk, blk), lambda p, i: (0, 0))] + [HBM] * ns,
        out_specs=[tile, tile] + [HBM] * ns,
        out_shape=[jax.ShapeDtypeStruct((t, d), BF16), jax.ShapeDtypeStruct((t, d), F32)]
        + [jax.ShapeDtypeStruct(_full_shape(k, s), BF16) for k, s in zip(kinds, shapes)],
        scratch_shapes=_gather_scratch(ns),
        compiler_params=_params("arbitrary", "arbitrary"),
    )(hin, hin, hin, tri, *shards)
    return res[0], res[1], res[2:]


def _attention_bwd(hin, tot, do, d, parts):
    t, blk, nd = hin.shape[0], ATT_BLOCK, d // LANES
    nq = t // blk
    scale = 1.0 / math.sqrt(HEAD_DIM)
    upper = jnp.triu(jnp.ones((blk, blk), F32)).astype(BF16)
    upper = jnp.concatenate([upper] * SUM_TERMS, axis=1)
    lower = jnp.tril(jnp.ones((blk, blk), F32), -1).astype(BF16)
    npart = len(parts)

    def body(q_ref, k_ref, v_ref, tot_ref, do_ref, up_ref, low_ref, *rest):
        part_refs, (dq_ref, dk_ref, dv_ref) = rest[:npart], rest[npart:npart + 3]
        land_refs, (send_sems, recv_sems, dk_acc, dv_acc) = rest[npart + 3:2 * npart + 3], rest[2 * npart + 3:]
        p, i = pl.program_id(0), pl.program_id(1)
        start, finish = _exchange_phases(part_refs, land_refs, send_sems, recv_sems)
        pl.when((p == 0) & (i == 0))(start)
        masks = _head_masks()

        @pl.when(i == 0)
        def _():
            dk_acc[...] = jnp.zeros_like(dk_acc)
            dv_acc[...] = jnp.zeros_like(dv_acc)

        q = q_ref[...].astype(F32) * scale
        dout = do_ref[...]
        qh = [jnp.where(mh, q, 0.0).astype(BF16) for mh in masks]
        doh = [jnp.where(mh, dout, jnp.zeros_like(dout)) for mh in masks]
        lane = lax.broadcasted_iota(jnp.int32, (8, LANES), 1)
        totals = [lax.dot_general(jnp.where(lane == h * HEAD_DIM, 1.0, 0.0), tot_ref[...], (NT, ((), ())),
                                  precision=lax.Precision.HIGHEST, preferred_element_type=F32)[0:1, :] for h in range(2)]
        after = lax.broadcasted_iota(jnp.int32, (blk, blk), 1) > lax.broadcasted_iota(jnp.int32, (blk, blk), 0)
        up, low = up_ref[...], low_ref[...]

        def products(j):
            j0 = pl.multiple_of(j * blk, blk)
            kj, vj = k_ref[pl.ds(j0, blk), :], v_ref[pl.ds(j0, blk), :]
            return tuple((_dot(kj, qh[h], NT), _dot(vj, doh[h], NT)) for h in range(2))

        def stage_sums(z, diagonal):
            lb, lf = _log_terms(z)
            if diagonal:
                lf = jnp.where(after, lf, 0.0)
            return jnp.exp(lb).astype(BF16), _dot(up, _split_cat(lf, 0), NN)

        def stage_weights(z, da, sums, rest, diagonal):
            rest = rest - sums[0:1, :]
            a = jnp.exp(z + sums + rest)
            if diagonal:
                a = jnp.where(after, a, 0.0)
            g = da * a
            return a.astype(BF16), g, _dot(low, g.astype(BF16), NN), rest

        def stage_dz(g, sig, earlier, before, diagonal):
            dlf = earlier + before
            dz = g - sig.astype(F32) * (g + dlf)
            if diagonal:
                dz = jnp.where(after, dz, 0.0)
            return dz.astype(BF16), dlf[blk - 1:blk, :] + g[blk - 1:blk, :]

        def grads(cur, rest, before, diagonal, between=(lambda: None, lambda: None)):
            sig, sums = zip(*[stage_sums(cur[h][0], diagonal) for h in range(2)])
            first = between[0]()
            a, g, earlier, rest = zip(*[stage_weights(*cur[h], sums[h], rest[h], diagonal) for h in range(2)])
            second = between[1]()
            dz, before = zip(*[stage_dz(g[h], sig[h], earlier[h], before[h], diagonal) for h in range(2)])
            return dz, a, rest, before, first, second

        def flush(j, dz, a, dq):
            j0 = pl.multiple_of(j * blk, blk)
            kj = k_ref[pl.ds(j0, blk), :]
            dk_new = jnp.zeros((blk, LANES), F32)
            dv_new = jnp.zeros((blk, LANES), F32)
            for h in range(2):
                dk_new = dk_new + _dot(dz[h], qh[h], NN)
                dv_new = dv_new + _dot(a[h], doh[h], NN)
                dq = dq + _dot(dz[h], jnp.where(masks[h], kj, jnp.zeros_like(kj)), TN)
            dk_acc[pl.ds(j0, blk), :] += dk_new
            dv_acc[pl.ds(j0, blk), :] += dv_new
            return dq

        def step(j, carry):
            cur, dz, a, rest, before, dq = carry
            dz, a, rest, before, nxt, dq = grads(
                cur, rest, before, False, (lambda: products(j + 1), lambda: flush(jnp.maximum(j - 1, 0), dz, a, dq)))
            return nxt, dz, a, rest, before, dq

        zero = jnp.zeros((1, blk), F32)
        nothing = jnp.zeros((blk, blk), BF16)
        cur, dz, a, rest, before, dq = lax.fori_loop(
            0, i, step, (products(0), (nothing, nothing), (nothing, nothing), tuple(totals), (zero, zero), jnp.zeros((blk, LANES), F32)))
        dq = flush(jnp.maximum(i - 1, 0), dz, a, dq)
        dz, a = grads(cur, rest, before, True)[:2]
        dq_ref[...] = (flush(i, dz, a, dq) * scale).astype(BF16)

        @pl.when(i == nq - 1)
        def _():
            dk_ref[...] = dk_acc[...].astype(BF16)
            dv_ref[...] = dv_acc[...].astype(BF16)

        pl.when((p == nd - 1) & (i == nq - 1))(finish)

    def whole(cb0):
        return pl.BlockSpec((t, LANES), functools.partial(lambda p, i, cb0: (0, cb0 + p), cb0=cb0))

    tile = pl.BlockSpec((blk, LANES), lambda p, i: (i, p))
    const = pl.BlockSpec((blk, blk), lambda p, i: (0, 0))
    act = jax.ShapeDtypeStruct((t, d), BF16)
    res = _pcall(
        body, name="attention_bwd", grid=(nd, nq),
        in_specs=[pl.BlockSpec((blk, LANES), lambda p, i: (i, 3 * nd + p)), whole(4 * nd), whole(5 * nd), tile, tile,
                  pl.BlockSpec((blk, SUM_TERMS * blk), lambda p, i: (0, 0)), const] + [HBM] * npart,
        out_specs=[tile, whole(0), whole(0)] + [HBM] * npart,
        out_shape=[act, act, act] + _exchange_shapes(parts),
        scratch_shapes=[pltpu.SemaphoreType.DMA((npart, 3)), pltpu.SemaphoreType.DMA((npart, 3)),
                        pltpu.VMEM((t, LANES), F32), pltpu.VMEM((t, LANES), F32)],
        compiler_params=_params("arbitrary", "arbitrary"),
    )(hin, hin, hin, tot, do, upper, lower, *parts)
    return res[0], res[1], res[2], res[3:]


def _adamw(w, g, m, v, *, name):
    r, c = w.shape
    tr = _tile(r, 256, 8) if r % 8 == 0 and r * c * 4 > (1 << 20) else r
    c1, c2 = 1.0 - ADAM_B1 ** ADAM_STEP, 1.0 - ADAM_B2 ** ADAM_STEP

    def body(w_ref, g_ref, m_ref, v_ref, d_ref, nm_ref, nv_ref):
        gv = g_ref[...]
        nm = ADAM_B1 * m_ref[...] + (1.0 - ADAM_B1) * gv
        nv = ADAM_B2 * v_ref[...] + (1.0 - ADAM_B2) * (gv * gv)
        d_ref[...] = -ADAM_LR * ((nm / c1) / (jnp.sqrt(nv / c2) + ADAM_EPS) + ADAM_WD * w_ref[...])
        nm_ref[...] = nm
        nv_ref[...] = nv

    spec = pl.BlockSpec((tr, c), lambda i: (i, 0))
    shape = jax.ShapeDtypeStruct((r, c), F32)
    return _pcall(
        body, name=name, grid=(r // tr,), in_specs=[spec] * 4, out_specs=[spec] * 3, out_shape=[shape] * 3,
        compiler_params=_params("arbitrary"),
    )(w, g, m, v)


HBM = pl.BlockSpec(memory_space=pltpu.HBM)


def _place():
    x, y, c = lax.axis_index("x"), lax.axis_index("y"), lax.axis_index("c")
    return x, y, c, [(1 - x, y), (x, 1 - y), (1 - x, 1 - y)]


def _window(ref, kind, shard_shape, j, half):
    r, w = shard_shape
    if kind == "col":
        return ref.at[pl.ds(half * (r // 2), r // 2), pl.ds(j * w, w)]
    return ref.at[pl.ds(j * r + half * (r // 2), r // 2), :]


def _full_shape(kind, shard_shape):
    r, w = shard_shape
    return (r, 4 * w) if kind == "col" else (4 * r, w)


def _remote(src, dst, send_sem, recv_sem, device):
    return pltpu.make_async_remote_copy(src_ref=src, dst_ref=dst, send_sem=send_sem, recv_sem=recv_sem,
                                        device_id=device, device_id_type=MESH)


def _gather_phases(big_in, big_out, kinds, shapes, send_sems, recv_sems, local_sems):
    x, y, c, chips = _place()
    me, sibling = 2 * x + y, (x, y, 1 - c)
    n = len(big_in)

    def local(a):
        r, w = shapes[a]
        own = big_out[a].at[:, pl.ds(me * w, w)] if kinds[a] == "col" else big_out[a].at[pl.ds(me * r, r), :]
        return pltpu.make_async_copy(big_in[a], own, local_sems.at[a])

    def over_ici(a, k):
        r = shapes[a][0]
        return _remote(big_in[a].at[pl.ds(c * (r // 2), r // 2), :], _window(big_out[a], kinds[a], shapes[a], me, c),
                       send_sems.at[a, k], recv_sems.at[a, k], (*chips[k], c))

    def landed(a, k, half, slot):
        win = _window(big_out[a], kinds[a], shapes[a], 2 * chips[k][0] + chips[k][1], half)
        return _remote(win, win, send_sems.at[a, slot], recv_sems.at[a, slot], sibling)

    def start():
        for a in range(n):
            local(a).start()
            for k in range(3):
                over_ici(a, k).start()

    def forward():
        for a in range(n):
            for k in range(3):
                landed(a, k, c, k).wait_recv()
                landed(a, k, c, 3 + k).start()

    def finish():
        for a in range(n):
            for k in range(3):
                landed(a, k, 1 - c, 3 + k).wait_recv()
        for a in range(n):
            for k in range(3):
                over_ici(a, k).wait_send()
                landed(a, k, c, 3 + k).wait_send()
            local(a).wait()

    return start, forward, finish


def _gather_scratch(n):
    return [pltpu.SemaphoreType.DMA((n, 6)), pltpu.SemaphoreType.DMA((n, 6)), pltpu.SemaphoreType.DMA((n,))]


def _gather_weights(bigs, kinds, smalls):
    nb, ns = len(bigs), len(smalls)
    shapes = [b.shape for b in bigs]

    def body(*refs):
        big_in, small_in = refs[:nb], refs[nb:nb + ns]
        big_out, small_out = refs[nb + ns:2 * nb + ns], refs[2 * nb + ns:2 * (nb + ns)]
        send_sems, recv_sems, local_sems, small_send, small_recv, small_local = refs[2 * (nb + ns):]
        x, y, c, chips = _place()
        me = 2 * x + y
        start, forward, finish = _gather_phases(big_in, big_out, kinds, shapes, send_sems, recv_sems, local_sems)
        start()
        local, sent = [], []
        for s in range(ns):
            local.append(pltpu.make_async_copy(small_in[s], small_out[s].at[me], small_local.at[s]))
            for k, chip in enumerate(chips):
                sent.append(_remote(small_in[s], small_out[s].at[me], small_send.at[s, k], small_recv.at[s, k], (*chip, c)))
        for cp in local + sent:
            cp.start()
        forward()
        for s in range(ns):
            for k, (px, py) in enumerate(chips):
                dst = small_out[s].at[2 * px + py]
                _remote(dst, dst, small_send.at[s, k], small_recv.at[s, k], (px, py, c)).wait_recv()
        finish()
        for cp in sent:
            cp.wait_send()
        for cp in local:
            cp.wait()

    out_shape = [jax.ShapeDtypeStruct(_full_shape(k, s), b.dtype) for b, k, s in zip(bigs, kinds, shapes)]
    out_shape += [jax.ShapeDtypeStruct((4, *s.shape), s.dtype) for s in smalls]
    return _pcall(
        body, name="gather_weights", in_specs=[HBM] * (nb + ns), out_specs=[HBM] * (nb + ns), out_shape=out_shape,
        scratch_shapes=_gather_scratch(nb) + [pltpu.SemaphoreType.DMA((ns, 3)), pltpu.SemaphoreType.DMA((ns, 3)), pltpu.SemaphoreType.DMA((ns,))],
    )(*bigs, *smalls)


def _swap_other_halves(grads, kinds, shapes, *, name):
    n = len(grads)

    def body(*refs):
        g_in, land = refs[:n], refs[n:2 * n]
        send_sems, recv_sems = refs[2 * n:]
        x, y, c, _ = _place()
        sibling = (x, y, 1 - c)
        sent = []
        for a in range(n):
            for j in range(4):
                cp = _remote(_window(g_in[a], kinds[a], shapes[a], j, 1 - c), land[a].at[j], send_sems.at[a, j], recv_sems.at[a, j], sibling)
                cp.start()
                sent.append(cp)
        for a in range(n):
            for j in range(4):
                _remote(land[a].at[j], land[a].at[j], send_sems.at[a, j], recv_sems.at[a, j], sibling).wait_recv()
        for cp in sent:
            cp.wait_send()

    return _pcall(
        body, name=name, in_specs=[HBM] * n, out_specs=[HBM] * n,
        out_shape=[jax.ShapeDtypeStruct((4, s[0] // 2, s[1]), F32) for s in shapes],
        scratch_shapes=[pltpu.SemaphoreType.DMA((n, 4)), pltpu.SemaphoreType.DMA((n, 4))],
    )(*grads)


def _pair_sum(place, g, land, kind, shard_shape, *, name):
    r, w = shard_shape
    hr = r // 2
    tr = _tile(hr, 256, 16)
    nr = hr // tr

    def body(place_ref, g_ref, l_ref, o_ref, own_ref):
        total = g_ref[...] + l_ref[...]
        o_ref[...] = total.astype(BF16)

        @pl.when(pl.program_id(1) == place_ref[1])
        def _():
            own_ref[...] = total

    if kind == "col":
        g_spec = pl.BlockSpec((tr, w), lambda i, j, p: (p[0] * nr + i, j))
    else:
        g_spec = pl.BlockSpec((tr, w), lambda i, j, p: ((2 * j + p[0]) * nr + i, 0))
    part = pl.BlockSpec((None, tr, w), lambda i, j, p: (j, i, 0))
    return _pcall(
        body, name=name,
        grid_spec=pltpu.PrefetchScalarGridSpec(num_scalar_prefetch=1, grid=(nr, 4), in_specs=[g_spec, part],
                                               out_specs=[part, pl.BlockSpec((tr, w), lambda i, j, p: (i, 0))]),
        out_shape=[jax.ShapeDtypeStruct((4, hr, w), BF16), jax.ShapeDtypeStruct((hr, w), F32)],
        compiler_params=_params("arbitrary", "arbitrary"),
    )(place, g, land)


def _exchange_phases(p_in, land, send_sems, recv_sems):
    x, y, c, chips = _place()

    def copy(a, k):
        px, py = chips[k]
        return _remote(p_in[a].at[2 * px + py], land[a].at[k], send_sems.at[a, k], recv_sems.at[a, k], (px, py, c))

    def start():
        for a in range(len(p_in)):
            for k in range(3):
                copy(a, k).start()

    def finish():
        for a in range(len(p_in)):
            for k in range(3):
                copy(a, k).wait()

    return start, finish


def _exchange_shapes(parts):
    return [jax.ShapeDtypeStruct((3, *p.shape[1:]), p.dtype) for p in parts]


def _chip_exchange(parts, *, name):
    n = len(parts)

    def body(*refs):
        start, finish = _exchange_phases(refs[:n], refs[n:2 * n], *refs[2 * n:])
        start()
        finish()

    return _pcall(
        body, name=name, in_specs=[HBM] * n, out_specs=[HBM] * n, out_shape=_exchange_shapes(parts),
        scratch_shapes=[pltpu.SemaphoreType.DMA((n, 3)), pltpu.SemaphoreType.DMA((n, 3))],
    )(*parts)


def _chip_sum(place, own, land, *, name):
    hr, w = own.shape
    tr = _tile(hr, 256, 16)
    nr = hr // tr

    def body(place_ref, p_ref, l_ref, o_ref):
        o_ref[...] = ((p_ref[...] + l_ref[0].astype(F32)) + l_ref[1].astype(F32)) + l_ref[2].astype(F32)

    return _pcall(
        body, name=name,
        grid_spec=pltpu.PrefetchScalarGridSpec(
            num_scalar_prefetch=1, grid=(nr,),
            in_specs=[pl.BlockSpec((tr, w), lambda i, p: (i, 0)), pl.BlockSpec((3, tr, w), lambda i, p: (0, i, 0))],
            out_specs=pl.BlockSpec((tr, w), lambda i, p: (p[0] * nr + i, 0))),
        out_shape=jax.ShapeDtypeStruct((2 * hr, w), F32),
        compiler_params=_params("arbitrary"),
    )(place, own, land)


def _swap_reduced_halves(halves):
    n = len(halves)

    def body(*refs):
        src, out = refs[:n], refs[n:2 * n]
        send_sems, recv_sems = refs[2 * n:]
        x, y, c, _ = _place()
        sibling = (x, y, 1 - c)
        sent = []
        for a in range(n):
            hr = out[a].shape[0] // 2
            cp = _remote(src[a].at[pl.ds(c * hr, hr), :], out[a].at[pl.ds(c * hr, hr), :], send_sems.at[a], recv_sems.at[a], sibling)
            cp.start()
            sent.append(cp)
        for a in range(n):
            hr = out[a].shape[0] // 2
            other = out[a].at[pl.ds((1 - c) * hr, hr), :]
            _remote(other, other, send_sems.at[a], recv_sems.at[a], sibling).wait_recv()
        for cp in sent:
            cp.wait_send()

    return _pcall(
        body, name="swap_reduced_halves", in_specs=[HBM] * n, out_specs=[HBM] * n,
        out_shape=[jax.ShapeDtypeStruct(h.shape, F32) for h in halves],
        input_output_aliases={a: a for a in range(n)},
        scratch_shapes=[pltpu.SemaphoreType.DMA((n,)), pltpu.SemaphoreType.DMA((n,))],
    )(*halves)


def _sum_small(packed):
    rows = packed.shape[0]

    def body(in_ref, out_ref, land, send_sems, recv_sems, local_sem):
        x, y, c, _ = _place()
        me = 4 * x + 2 * y + c
        own = pltpu.make_async_copy(in_ref, land.at[me], local_sem)
        own.start()
        flips = [(r >> 2, (r >> 1) & 1, r & 1) for r in range(1, 8)]
        sent = []
        for k, (fx, fy, fc) in enumerate(flips):
            cp = _remote(in_ref, land.at[me], send_sems.at[k], recv_sems.at[k], (x ^ fx, y ^ fy, c ^ fc))
            cp.start()
            sent.append(cp)
        for k, (fx, fy, fc) in enumerate(flips):
            src = land.at[4 * (x ^ fx) + 2 * (y ^ fy) + (c ^ fc)]
            _remote(src, src, send_sems.at[k], recv_sems.at[k], (x ^ fx, y ^ fy, c ^ fc)).wait_recv()
        for cp in sent:
            cp.wait_send()
        own.wait()
        total = land[0]
        for dev in range(1, 8):
            total = total + land[dev]
        out_ref[...] = total

    vmem = pl.BlockSpec(memory_space=pltpu.VMEM)
    return _pcall(
        body, name="sum_small", in_specs=[vmem], out_specs=vmem, out_shape=jax.ShapeDtypeStruct(packed.shape, F32),
        scratch_shapes=[pltpu.VMEM((8, rows, LANES), F32), pltpu.SemaphoreType.DMA((7,)), pltpu.SemaphoreType.DMA((7,)), pltpu.SemaphoreType.DMA],
    )(packed)


def _sigmoid(v):
    return 1.0 / (1.0 + jnp.exp(-v))


def kernel(x, meta_tokens, g_pre_mix, w_in, conv_w_mix, w_proj_conv, w_proj_attn, b_gate, w_out, g_post_mix, g_pre_ffn, w_up_gate, conv_w_ffn, w_down, g_post_ffn, loss_target, m_meta_tokens, m_g_pre_mix, m_w_in, m_conv_w_mix, m_w_proj_conv, m_w_proj_attn, m_b_gate, m_w_out, m_g_post_mix, m_g_pre_ffn, m_w_up_gate, m_conv_w_ffn, m_w_down, m_g_post_ffn, v_meta_tokens, v_g_pre_mix, v_w_in, v_conv_w_mix, v_w_proj_conv, v_w_proj_attn, v_b_gate, v_w_out, v_g_post_mix, v_g_pre_ffn, v_w_up_gate, v_conv_w_ffn, v_w_down, v_g_post_ffn):
    seq, d = x.shape[1], x.shape[2]
    f = w_down.shape[1] * 4
    real = N_META + seq
    t = -(-real // ATT_BLOCK) * ATT_BLOCK
    nd = d // LANES
    cx, cy, cc = lax.axis_index("x"), lax.axis_index("y"), lax.axis_index("c")
    chip = 2 * cx + cy
    place = jnp.stack([cc, chip]).astype(jnp.int32)

    big_names = ["w_in", "w_proj_conv", "w_proj_attn", "w_out", "w_up_gate", "w_down"]
    kinds = ["col", "row", "row", "row", "col", "row"]
    big_w = [w_in[0], w_proj_conv[0], w_proj_attn[0], w_out[0], w_up_gate[0], w_down[0]]
    small_w = [meta_tokens, conv_w_mix[0], b_gate[0], conv_w_ffn[0]]
    shards = [w.astype(BF16) for w in big_w]
    shapes = [w.shape for w in big_w]
    gathered = _gather_weights(shards[:1], kinds[:1], small_w)
    wi = gathered[0]
    meta_f, cwm_f, bg_f, cwf_f = (jnp.moveaxis(s, 0, 1).reshape(s.shape[1], -1) for s in gathered[1:])

    h0 = jnp.concatenate([meta_f, x[0], jnp.zeros((t - real, d), F32)], axis=0)
    target = jnp.concatenate([jnp.zeros((N_META, d), F32), loss_target[0], jnp.zeros((t - real, d), F32)], axis=0)
    xn1 = _prenorm(h0, g_pre_mix, name="prenorm_mix")
    hin = _matmul(xn1, wi, name="in_proj")
    yconv = _mixer_conv_fwd(hin, cwm_f, d)
    o, att_tot, (wpc, wpa, wo, wug, wd) = _attention_fwd(hin, d, shards[1:], kinds[1:])

    def gates(gpre, bias):
        return _sigmoid(gpre[:, :d].astype(F32) + bias[0:1]), _sigmoid(gpre[:, d:].astype(F32) + bias[1:2])

    def merge_epi(accs, rows, vecs, row0):
        gate_c, gate_a = gates(rows[0], vecs[0])
        return [gate_c * accs[0] + gate_a * accs[1], accs[0], accs[1]]

    act = ("row", d, BF16)
    merged, bconv, battn = _matmul_rows([(yconv, wpc), (o, wpa)], [(hin, 2 * d, 3)], [bg_f], merge_epi, [act, act, act], name="branch_proj")

    def mix_epi(accs, rows, vecs, row0):
        mix = accs[0]
        h1 = rows[0] + mix * _rms(mix) * vecs[0]
        return [mix, h1, h1 * _rms(h1) * vecs[1]]

    mix, h1, xn2 = _matmul_rows([(merged, wo)], [(h0, d, 0)], [g_post_mix, g_pre_ffn], mix_epi,
                                [("row", d, F32), ("row", d, F32), act], name="out_proj")
    ug = _matmul(xn2, wug, name="up_proj")
    fact = _ffn_conv_fwd(ug, cwf_f, f)

    def loss_epi(accs, rows, vecs, row0):
        ffn = accs[0]
        r = _rms(ffn)
        fh = ffn * r
        h2 = rows[0] + fh * vecs[0]
        rid = row0 + lax.broadcasted_iota(jnp.int32, (ffn.shape[0], 1), 0)
        err = jnp.where((rid >= N_META) & (rid < real), h2 - rows[1], 0.0)
        dy = err * (1.0 / d)
        dffn, dg = _rms_bwd(dy, fh, r, vecs[0])
        loss = jnp.zeros((1, LANES), F32) + 0.5 * jnp.sum(err * err) / d
        return [dffn, dy, dg, loss]

    dffn, dh2, dg_post_ffn, loss_part = _matmul_rows(
        [(fact, wd)], [(h1, d, 0), (target, d, 0)], [g_post_ffn], loss_epi,
        [act, ("row", d, F32), ("acc", 1, d), ("acc", 1, LANES)], name="down_proj_loss")

    gw_down = _matmul_tn(fact, dffn, name="grad_w_down")
    df = _matmul(dffn, wd, nt=True, name="down_proj_bwd")
    du2, dg2, gcw_ffn = _ffn_conv_bwd(ug, df, cwf_f, f)

    def ffn_in_epi(accs, rows, vecs, row0):
        h1v, dyv, mixv = rows
        r3 = _rms(h1v)
        dh1n, dg3 = _rms_bwd(accs[0], h1v * r3, r3, vecs[0])
        dh1 = dyv + dh1n
        r2 = _rms(mixv)
        dmix, dgm = _rms_bwd(dh1, mixv * r2, r2, vecs[1])
        return [dh1, dmix, dg3, dgm]

    dh1, dmix, dg_pre_ffn, dg_post_mix, dug = _matmul_rows_joined(
        [du2, dg2], wug, [(h1, d, 0), (dh2, d, 0), (mix, d, 0)], [g_pre_ffn, g_post_mix], ffn_in_epi,
        [("row", d, F32), act, ("acc", 1, d), ("acc", 1, d)], tk=f // 2, name="up_proj_bwd")
    gw_up = _matmul_tn(xn2, dug, name="grad_w_up_gate")
    gw_out = _matmul_tn(merged, dmix, name="grad_w_out")

    def merge_bwd_epi(accs, rows, vecs, row0):
        dm = accs[0]
        gate_c, gate_a = gates(rows[0], vecs[0])
        dpre_c = dm * rows[1].astype(F32) * gate_c * (1.0 - gate_c)
        dpre_a = dm * rows[2].astype(F32) * gate_a * (1.0 - gate_a)
        dbias = jnp.concatenate([jnp.sum(dpre_c, axis=0, keepdims=True), jnp.sum(dpre_a, axis=0, keepdims=True)], axis=0)
        return [dm * gate_c, dm * gate_a, jnp.concatenate([dpre_c, dpre_a], axis=1), dbias]

    dbconv, dbattn, dgates, gb_gate = _matmul_rows(
        [(dmix, wo)], [(hin, 2 * d, 3), (bconv, d, 0), (battn, d, 0)], [bg_f], merge_bwd_epi,
        [act, act, ("row", 2 * d, BF16), ("acc", 2, d)], nt=True, name="out_proj_bwd")
    gw_pc = _matmul_tn(yconv, dbconv, name="grad_w_proj_conv")
    gw_pa = _matmul_tn(o, dbattn, name="grad_w_proj_attn")
    dyconv = _matmul(dbconv, wpc, nt=True, name="proj_conv_bwd")
    do = _matmul(dbattn, wpa, nt=True, name="proj_attn_bwd")
    db, dc, dhc, gcw_mix = _mixer_conv_bwd(hin, dyconv, cwm_f, d)

    def pair_sums(grads, first):
        landed = _swap_other_halves(grads, kinds[first:first + len(grads)], shapes[first:first + len(grads)],
                                    name=f"swap_other_halves_{big_names[first]}")
        return zip(*[_pair_sum(place, g, l, kinds[first + a], shapes[first + a], name=f"pair_sum_{big_names[first + a]}")
                     for a, (g, l) in enumerate(zip(grads, landed))])

    parts, own = pair_sums([gw_pc, gw_pa, gw_out, gw_up, gw_down], 1)
    dq, dk, dv, landed = _attention_bwd(hin, att_tot, do, d, parts)

    def in_epi(accs, rows, vecs, row0):
        h0v, dh1v = rows
        r1 = _rms(h0v)
        dh0n, dg1 = _rms_bwd(accs[0], h0v * r1, r1, vecs[0])
        return [dh1v + dh0n, dg1]

    dh0, dg_pre_mix, dhin = _matmul_rows_joined(
        [db, dc, dhc, dq, dk, dv, dgates], wi, [(h0, d, 0), (dh1, d, 0)], [g_pre_mix], in_epi,
        [("row", d, F32), ("acc", 1, d)], tk=d, name="in_proj_bwd")
    gw_in = _matmul_tn(xn1, dhin, name="grad_w_in")
    grad_x = dh0[N_META:real][None]

    parts_in, own_in = pair_sums([gw_in], 0)
    landed = list(_chip_exchange(parts_in, name="chip_exchange_w_in")) + list(landed)
    halves = [_chip_sum(place, p, l, name=f"chip_sum_{n}") for p, l, n in zip(list(own_in) + list(own), landed, big_names)]
    big_r = _swap_reduced_halves(halves)

    small_g = [dh0[:N_META], dg_pre_mix, gcw_mix, gb_gate, dg_post_mix, dg_pre_ffn, gcw_ffn, dg_post_ffn, loss_part]
    flat = jnp.concatenate([s.reshape(-1) for s in small_g])
    pad = -flat.shape[0] % (8 * LANES)
    summed = _sum_small(jnp.pad(flat, (0, pad)).reshape(-1, LANES)).reshape(-1)
    small_r, pos = [], 0
    for s in small_g:
        small_r.append(summed[pos:pos + s.size].reshape(s.shape))
        pos += s.size
    g_meta, g_g1, g_cwm, g_bg, g_g2, g_g3, g_cwf, g_g4, loss_row = small_r
    loss = loss_row[0, 0]

    def my_cols(full, width):
        return lax.dynamic_slice_in_dim(full, chip * width, width, axis=1)

    grads = {
        "meta_tokens": my_cols(g_meta, d // 4), "g_pre_mix": g_g1, "w_in": big_r[0][None],
        "conv_w_mix": my_cols(g_cwm, d // 4)[None], "w_proj_conv": big_r[1][None], "w_proj_attn": big_r[2][None],
        "b_gate": my_cols(g_bg, d // 4)[None], "w_out": big_r[3][None], "g_post_mix": g_g2, "g_pre_ffn": g_g3,
        "w_up_gate": big_r[4][None], "conv_w_ffn": my_cols(g_cwf, f // 4)[None], "w_down": big_r[5][None], "g_post_ffn": g_g4,
    }
    weights = dict(meta_tokens=meta_tokens, g_pre_mix=g_pre_mix, w_in=w_in, conv_w_mix=conv_w_mix, w_proj_conv=w_proj_conv,
                   w_proj_attn=w_proj_attn, b_gate=b_gate, w_out=w_out, g_post_mix=g_post_mix, g_pre_ffn=g_pre_ffn,
                   w_up_gate=w_up_gate, conv_w_ffn=conv_w_ffn, w_down=w_down, g_post_ffn=g_post_ffn)
    m_in = dict(meta_tokens=m_meta_tokens, g_pre_mix=m_g_pre_mix, w_in=m_w_in, conv_w_mix=m_conv_w_mix, w_proj_conv=m_w_proj_conv,
                w_proj_attn=m_w_proj_attn, b_gate=m_b_gate, w_out=m_w_out, g_post_mix=m_g_post_mix, g_pre_ffn=m_g_pre_ffn,
                w_up_gate=m_w_up_gate, conv_w_ffn=m_conv_w_ffn, w_down=m_w_down, g_post_ffn=m_g_post_ffn)
    v_in = dict(meta_tokens=v_meta_tokens, g_pre_mix=v_g_pre_mix, w_in=v_w_in, conv_w_mix=v_conv_w_mix, w_proj_conv=v_w_proj_conv,
                w_proj_attn=v_w_proj_attn, b_gate=v_b_gate, w_out=v_w_out, g_post_mix=v_g_post_mix, g_pre_ffn=v_g_pre_ffn,
                w_up_gate=v_w_up_gate, conv_w_ffn=v_conv_w_ffn, w_down=v_w_down, g_post_ffn=v_g_post_ffn)
    names = list(weights)
    deltas, new_m, new_v = [], [], []
    for n in names:
        shape = weights[n].shape
        two_d = (-1, shape[-1])
        dl, nm, nv = _adamw(weights[n].reshape(two_d), grads[n].reshape(two_d), m_in[n].reshape(two_d), v_in[n].reshape(two_d),
                            name=f"adamw_{n}")
        deltas.append(dl.reshape(shape))
        new_m.append(nm.reshape(shape))
        new_v.append(nv.reshape(shape))
    return (loss, grad_x, *[grads[n].reshape(weights[n].shape) for n in names], *deltas, *new_m, *new_v)
```

```python
import functools
import math

import jax
import jax.numpy as jnp
from jax import lax
from jax.experimental import pallas as pl
from jax.experimental.pallas import tpu as pltpu

F32 = jnp.float32
BF16 = jnp.bfloat16

N_META = 16
HEAD_DIM = 64
LANES = 128
RMS_EPS = 1e-6
ATT_BLOCK = 256
ATT_LANE_BLOCKS = 2
VMEM_LIMIT = 56 * 1024 * 1024

ADAM_LR = 0.001
ADAM_B1 = 0.9
ADAM_B2 = 0.999
ADAM_EPS = 1e-08
ADAM_WD = 0.01
ADAM_STEP = 10

MESH = pl.DeviceIdType.MESH


def _tile(n, cap, unit=LANES):
    d = (min(cap, n) // unit) * unit
    while d >= unit:
        if n % d == 0:
            return d
        d -= unit
    raise ValueError(f"no tile for {n} under {cap}")


MXU_EDGE = 256


def _mxu_tile(n, cap):
    if n <= cap:
        return n
    return _tile(n, cap, MXU_EDGE) if n % MXU_EDGE == 0 else _tile(n, cap)


def _params(*sem):
    return pltpu.CompilerParams(dimension_semantics=sem, vmem_limit_bytes=VMEM_LIMIT)


def _pcall(body, **kw):
    return pl.pallas_call(body, **kw)


def _dot(a, b, dims):
    return lax.dot_general(a, b, (dims, ((), ())), preferred_element_type=F32)


NN = ((1,), (0,))
NT = ((1,), (1,))
TN = ((0,), (0,))


def _matmul(a, b, *, nt=False, out_dtype=BF16, name):
    m, kdim = a.shape
    n = b.shape[0] if nt else b.shape[1]
    tm, tn, tk = _tile(m, 640, 16), _mxu_tile(n, 2816), _mxu_tile(kdim, 2816)
    nk = kdim // tk

    def body(a_ref, b_ref, o_ref, *scratch):
        p = _dot(a_ref[...], b_ref[...], NT if nt else NN)
        if nk == 1:
            o_ref[...] = p.astype(o_ref.dtype)
            return
        acc, k = scratch[0], pl.program_id(2)

        @pl.when(k == 0)
        def _():
            acc[...] = p

        @pl.when(k > 0)
        def _():
            acc[...] += p

        @pl.when(k == nk - 1)
        def _():
            o_ref[...] = acc[...].astype(o_ref.dtype)

    b_spec = pl.BlockSpec((tn, tk), lambda j, i, k: (j, k)) if nt else pl.BlockSpec((tk, tn), lambda j, i, k: (k, j))
    return _pcall(
        body, name=name, grid=(n // tn, m // tm, nk),
        in_specs=[pl.BlockSpec((tm, tk), lambda j, i, k: (i, k)), b_spec],
        out_specs=pl.BlockSpec((tm, tn), lambda j, i, k: (i, j)),
        out_shape=jax.ShapeDtypeStruct((m, n), out_dtype),
        scratch_shapes=[pltpu.VMEM((tm, tn), F32)] if nk > 1 else [],
        compiler_params=_params("arbitrary", "arbitrary", "arbitrary"),
    )(a, b)


def _matmul_tn(a, b, *, name):
    t, ka = a.shape
    nb = b.shape[1]
    tb = _mxu_tile(nb, 1024)
    budget = VMEM_LIMIT * 5 // 7
    ta = next(c for c in range(ka, 0, -LANES) if ka % c == 0 and 2 * (2 * t * (c + tb) + 4 * c * tb) <= budget)

    def body(a_ref, b_ref, o_ref):
        o_ref[...] = _dot(a_ref[...], b_ref[...], TN)

    return _pcall(
        body, name=name, grid=(ka // ta, nb // tb),
        in_specs=[pl.BlockSpec((t, ta), lambda i, j: (0, i)), pl.BlockSpec((t, tb), lambda i, j: (0, j))],
        out_specs=pl.BlockSpec((ta, tb), lambda i, j: (i, j)),
        out_shape=jax.ShapeDtypeStruct((ka, nb), F32),
        compiler_params=_params("arbitrary", "arbitrary"),
    )(a, b)


def _matmul_rows(pairs, rows, vecs, epi, outs, *, nt=False, name):
    m, kdim = pairs[0][0].shape
    n = pairs[0][1].shape[0] if nt else pairs[0][1].shape[1]
    tm, tk = _tile(m, 640, 16), _mxu_tile(kdim, 2816)
    nk, npair = kdim // tk, len(pairs)

    def body(*refs):
        a_refs, b_refs = refs[0:2 * npair:2], refs[1:2 * npair:2]
        pos = 2 * npair
        row_refs = refs[pos:pos + len(rows)]
        pos += len(rows)
        vec_refs = refs[pos:pos + len(vecs)]
        pos += len(vecs)
        out_refs = refs[pos:pos + len(outs)]
        accs = refs[pos + len(outs):]
        i, k = pl.program_id(0), pl.program_id(1)
        prods = [_dot(a[...], b[...], NT if nt else NN) for a, b in zip(a_refs, b_refs)]
        if nk > 1:
            @pl.when(k == 0)
            def _():
                for acc, p in zip(accs, prods):
                    acc[...] = p

            @pl.when(k > 0)
            def _():
                for acc, p in zip(accs, prods):
                    acc[...] += p

        @pl.when(k == nk - 1)
        def _():
            vals = [acc[...] for acc in accs] if nk > 1 else prods
            res = epi(vals, [r[...] for r in row_refs], [v[...] for v in vec_refs], i * tm)
            for o_ref, spec, val in zip(out_refs, outs, res):
                if spec[0] == "row":
                    o_ref[...] = val.astype(o_ref.dtype)
                else:
                    @pl.when(i == 0)
                    def _():
                        o_ref[...] = val

                    @pl.when(i > 0)
                    def _():
                        o_ref[...] += val

    in_specs, args = [], []
    for a, b in pairs:
        in_specs += [pl.BlockSpec((tm, tk), lambda i, k: (i, k)),
                     pl.BlockSpec((n, tk), lambda i, k: (0, k)) if nt else pl.BlockSpec((tk, n), lambda i, k: (k, 0))]
        args += [a, b]
    for arr, width, cb in rows:
        in_specs.append(pl.BlockSpec((tm, width), functools.partial(lambda i, k, cb: (i, cb), cb=cb)))
        args.append(arr)
    for v in vecs:
        in_specs.append(pl.BlockSpec(v.shape, lambda i, k: (0, 0)))
        args.append(v)
    out_specs, out_shape = [], []
    for spec in outs:
        if spec[0] == "row":
            out_specs.append(pl.BlockSpec((tm, spec[1]), lambda i, k: (i, 0)))
            out_shape.append(jax.ShapeDtypeStruct((m, spec[1]), spec[2]))
        else:
            out_specs.append(pl.BlockSpec((spec[1], spec[2]), lambda i, k: (0, 0)))
            out_shape.append(jax.ShapeDtypeStruct((spec[1], spec[2]), F32))
    return _pcall(
        body, name=name, grid=(m // tm, nk), in_specs=in_specs, out_specs=out_specs, out_shape=out_shape,
        scratch_shapes=[pltpu.VMEM((tm, n), F32) for _ in pairs] if nk > 1 else [],
        compiler_params=_params("arbitrary", "arbitrary"),
    )(*args)


def _matmul_rows_joined(segments, b, rows, vecs, epi, outs, *, tk, name):
    m = segments[0].shape[0]
    n, kdim = b.shape
    tm = _tile(m, 640, 16)
    nk = kdim // tk
    first = [0]
    for s in segments:
        assert s.shape[1] % tk == 0
        first.append(first[-1] + s.shape[1] // tk)
    assert first[-1] == nk

    def body(*refs):
        seg_refs, b_ref = refs[:len(segments)], refs[len(segments)]
        pos = len(segments) + 1
        row_refs = refs[pos:pos + len(rows)]
        pos += len(rows)
        vec_refs = refs[pos:pos + len(vecs)]
        pos += len(vecs)
        out_refs, joined_ref, acc = refs[pos:pos + len(outs)], refs[pos + len(outs)], refs[pos + len(outs) + 1]
        i, k = pl.program_id(0), pl.program_id(1)

        @pl.when(k == 0)
        def _():
            acc[...] = jnp.zeros_like(acc)

        for s, seg in enumerate(seg_refs):
            @pl.when((k >= first[s]) & (k < first[s + 1]))
            def _():
                joined_ref[...] = seg[...]
                acc[...] += _dot(seg[...], b_ref[...], NT)

        @pl.when(k == nk - 1)
        def _():
            res = epi([acc[...]], [r[...] for r in row_refs], [v[...] for v in vec_refs], i * tm)
            for o_ref, spec, val in zip(out_refs, outs, res):
                if spec[0] == "row":
                    o_ref[...] = val.astype(o_ref.dtype)
                else:
                    @pl.when(i == 0)
                    def _():
                        o_ref[...] = val

                    @pl.when(i > 0)
                    def _():
                        o_ref[...] += val

    def seg_spec(s):
        last = first[s + 1] - first[s] - 1
        return pl.BlockSpec((tm, tk), lambda i, k: (i, jnp.clip(k - first[s], 0, last)))

    in_specs = [seg_spec(s) for s in range(len(segments))] + [pl.BlockSpec((n, tk), lambda i, k: (0, k))]
    args = [*segments, b]
    for arr, width, cb in rows:
        in_specs.append(pl.BlockSpec((tm, width), functools.partial(lambda i, k, cb: (i, cb), cb=cb)))
        args.append(arr)
    for v in vecs:
        in_specs.append(pl.BlockSpec(v.shape, lambda i, k: (0, 0)))
        args.append(v)
    out_specs, out_shape = [], []
    for spec in outs:
        if spec[0] == "row":
            out_specs.append(pl.BlockSpec((tm, spec[1]), lambda i, k: (i, 0)))
            out_shape.append(jax.ShapeDtypeStruct((m, spec[1]), spec[2]))
        else:
            out_specs.append(pl.BlockSpec((spec[1], spec[2]), lambda i, k: (0, 0)))
            out_shape.append(jax.ShapeDtypeStruct((spec[1], spec[2]), F32))
    out_specs.append(pl.BlockSpec((tm, tk), lambda i, k: (i, k)))
    out_shape.append(jax.ShapeDtypeStruct((m, kdim), segments[0].dtype))
    return _pcall(
        body, name=name, grid=(m // tm, nk), in_specs=in_specs, out_specs=out_specs, out_shape=out_shape,
        scratch_shapes=[pltpu.VMEM((tm, n), F32)],
        compiler_params=_params("arbitrary", "arbitrary"),
    )(*args)


def _rms(v):
    return lax.rsqrt(jnp.mean(v * v, axis=-1, keepdims=True) + RMS_EPS)


def _rms_bwd(dz, vhat, r, g):
    t = dz * g
    dv = r * (t - vhat * jnp.mean(t * vhat, axis=-1, keepdims=True))
    return dv, jnp.sum(dz * vhat, axis=0, keepdims=True)


def _prenorm(h, g, *, name):
    m, d = h.shape
    tm = _tile(m, 640, 16)

    def body(h_ref, g_ref, o_ref):
        v = h_ref[...]
        o_ref[...] = (v * _rms(v) * g_ref[...]).astype(BF16)

    return _pcall(
        body, name=name, grid=(m // tm,),
        in_specs=[pl.BlockSpec((tm, d), lambda i: (i, 0)), pl.BlockSpec((1, d), lambda i: (0, 0))],
        out_specs=pl.BlockSpec((tm, d), lambda i: (i, 0)),
        out_shape=jax.ShapeDtypeStruct((m, d), BF16),
        compiler_params=_params("arbitrary"),
    )(h, g)


def _shift_down(u, k):
    rows = lax.broadcasted_iota(jnp.int32, u.shape, 0)
    return jnp.where(rows >= k, pltpu.roll(u, k, 0), 0.0)


def _shift_up(u, k):
    n = u.shape[0]
    rows = lax.broadcasted_iota(jnp.int32, u.shape, 0)
    return jnp.where(rows < n - k, pltpu.roll(u, n - k, 0), 0.0)


def _conv(u, w):
    return w[2:3] * u + w[1:2] * _shift_down(u, 1) + w[0:1] * _shift_down(u, 2)


def _conv_bwd(dcu, u, w):
    du = w[2:3] * dcu + w[1:2] * _shift_up(dcu, 1) + w[0:1] * _shift_up(dcu, 2)
    dw = [jnp.sum(dcu * _shift_down(u, 2 - k) if k < 2 else dcu * u, axis=0, keepdims=True) for k in range(3)]
    return du, dw


def _strip(arr, t, cb0):
    return pl.BlockSpec((t, LANES), functools.partial(lambda s, cb0: (0, cb0 + s), cb0=cb0))


def _mixer_conv_fwd(hin, w, d):
    t, ns = hin.shape[0], d // LANES

    def body(b_ref, c_ref, h_ref, w_ref, y_ref):
        u = c_ref[...].astype(F32) * h_ref[...].astype(F32)
        y_ref[...] = (b_ref[...].astype(F32) * _conv(u, w_ref[...])).astype(BF16)

    return _pcall(
        body, name="mixer_conv_fwd", grid=(ns,),
        in_specs=[_strip(hin, t, 0), _strip(hin, t, ns), _strip(hin, t, 2 * ns), pl.BlockSpec((3, LANES), lambda s: (0, s))],
        out_specs=pl.BlockSpec((t, LANES), lambda s: (0, s)),
        out_shape=jax.ShapeDtypeStruct((t, d), BF16),
        compiler_params=_params("arbitrary"),
    )(hin, hin, hin, w)


def _mixer_conv_bwd(hin, dy, w, d):
    t, ns = hin.shape[0], d // LANES

    def body(b_ref, c_ref, h_ref, dy_ref, w_ref, db_ref, dc_ref, dh_ref, dw_ref):
        b, c, h, g = (r[...].astype(F32) for r in (b_ref, c_ref, h_ref, dy_ref))
        wv = w_ref[...]
        u = c * h
        db_ref[...] = (g * _conv(u, wv)).astype(BF16)
        du, dw = _conv_bwd(g * b, u, wv)
        dc_ref[...] = (du * h).astype(BF16)
        dh_ref[...] = (du * c).astype(BF16)
        for k in range(3):
            dw_ref[k:k + 1, :] = dw[k]

    col = pl.BlockSpec((t, LANES), lambda s: (0, s))
    act = jax.ShapeDtypeStruct((t, d), BF16)
    return _pcall(
        body, name="mixer_conv_bwd", grid=(ns,),
        in_specs=[_strip(hin, t, 0), _strip(hin, t, ns), _strip(hin, t, 2 * ns), col, pl.BlockSpec((3, LANES), lambda s: (0, s))],
        out_specs=[col, col, col, pl.BlockSpec((3, LANES), lambda s: (0, s))],
        out_shape=[act, act, act, jax.ShapeDtypeStruct((3, d), F32)],
        compiler_params=_params("arbitrary"),
    )(hin, hin, hin, dy, w)


GELU_C = math.sqrt(2.0 / math.pi)
GELU_A = 0.044715


def _gelu_tanh(x):
    return jnp.tanh(GELU_C * (x + GELU_A * x * x * x))


def _ffn_conv_fwd(ug, w, f):
    t, ns = ug.shape[0], f // LANES

    def body(u_ref, g_ref, w_ref, o_ref):
        cu = _conv(u_ref[...].astype(F32), w_ref[...])
        o_ref[...] = (0.5 * cu * (1.0 + _gelu_tanh(cu)) * g_ref[...].astype(F32)).astype(BF16)

    return _pcall(
        body, name="ffn_conv_fwd", grid=(ns,),
        in_specs=[_strip(ug, t, 0), _strip(ug, t, ns), pl.BlockSpec((3, LANES), lambda s: (0, s))],
        out_specs=pl.BlockSpec((t, LANES), lambda s: (0, s)),
        out_shape=jax.ShapeDtypeStruct((t, f), BF16),
        compiler_params=_params("arbitrary"),
    )(ug, ug, w)


def _ffn_conv_bwd(ug, df, w, f):
    t, ns = ug.shape[0], f // LANES

    def body(u_ref, g_ref, df_ref, w_ref, du_ref, dg_ref, dw_ref):
        u, g, d = (r[...].astype(F32) for r in (u_ref, g_ref, df_ref))
        wv = w_ref[...]
        cu = _conv(u, wv)
        th = _gelu_tanh(cu)
        dg_ref[...] = (d * 0.5 * cu * (1.0 + th)).astype(BF16)
        dgelu = 0.5 * (1.0 + th) + 0.5 * cu * (1.0 - th * th) * GELU_C * (1.0 + 3.0 * GELU_A * cu * cu)
        du, dw = _conv_bwd(d * g * dgelu, u, wv)
        du_ref[...] = du.astype(BF16)
        for k in range(3):
            dw_ref[k:k + 1, :] = dw[k]

    col = pl.BlockSpec((t, LANES), lambda s: (0, s))
    act = jax.ShapeDtypeStruct((t, f), BF16)
    return _pcall(
        body, name="ffn_conv_bwd", grid=(ns,),
        in_specs=[_strip(ug, t, 0), _strip(ug, t, ns), col, pl.BlockSpec((3, LANES), lambda s: (0, s))],
        out_specs=[col, col, pl.BlockSpec((3, LANES), lambda s: (0, s))],
        out_shape=[act, act, jax.ShapeDtypeStruct((3, f), F32)],
        compiler_params=_params("arbitrary"),
    )(ug, ug, df, w)


def _log_terms(z):
    minus_abs = lax.bitcast_convert_type(lax.bitcast_convert_type(z, jnp.uint32) | jnp.uint32(0x80000000), F32)
    lb = jnp.minimum(z, 0.0) - jnp.log(1.0 + jnp.exp(minus_abs))
    return lb, lb - z


SUM_TERMS = 1


def _split_cat(v, axis):
    terms = [v.astype(BF16)]
    for _ in range(SUM_TERMS - 1):
        v = v - terms[-1].astype(F32)
        terms.append(v.astype(BF16))
    return terms[0] if SUM_TERMS == 1 else jnp.concatenate(terms, axis=axis)


def _head_masks():
    lane = lax.broadcasted_iota(jnp.int32, (1, LANES), 1)
    return lane < HEAD_DIM, lane >= HEAD_DIM


def _attention_fwd(hin, d, shards, kinds):
    t, blk, nlb = hin.shape[0], ATT_BLOCK, ATT_LANE_BLOCKS
    width = nlb * LANES
    nd = d // width
    nq = t // blk
    scale = 1.0 / math.sqrt(HEAD_DIM)
    tri = jnp.tril(jnp.ones((blk, blk), F32)).astype(BF16)
    tri = jnp.concatenate([tri] * SUM_TERMS, axis=0)
    ns = len(shards)
    shapes = [s.shape for s in shards]
    heads = [(lb, half) for lb in range(nlb) for half in range(2)]

    def body(q_ref, k_ref, v_ref, tri_ref, *rest):
        shard_refs, (o_ref, tot_ref), full_refs, sems = rest[:ns], rest[ns:ns + 2], rest[ns + 2:2 * ns + 2], rest[2 * ns + 2:]
        p, i = pl.program_id(0), pl.program_id(1)
        start, forward, finish = _gather_phases(shard_refs, full_refs, kinds, shapes, *sems)
        pl.when((p == 0) & (i == 0))(start)
        pl.when((p == nd // 2) & (i == 0))(forward)
        masks = _head_masks()
        lanes = [slice(lb * LANES, (lb + 1) * LANES) for lb in range(nlb)]
        q = q_ref[...].astype(F32) * scale
        qh = [jnp.where(masks[half], q[:, lanes[lb]], 0.0).astype(BF16) for lb, half in heads]
        after = lax.broadcasted_iota(jnp.int32, (blk, blk), 0) > lax.broadcasted_iota(jnp.int32, (blk, blk), 1)
        tri_v = tri_ref[...]

        def scores(j, which):
            j0 = pl.multiple_of(j * blk, blk)
            kj = k_ref[pl.ds(j0, blk), :]
            return tuple(_dot(qh[h], kj[:, lanes[heads[h][0]]], NT) for h in which)

        early = tuple(h for h in range(len(heads)) if h % 2 == 0)
        late = tuple(h for h in range(len(heads)) if h % 2 == 1)

        def both(first, second):
            out = [None] * len(heads)
            for h, z in zip(early + late, first + second):
                out[h] = z
            return tuple(out)

        def weights(z, c, diagonal, between=lambda: None):
            halves = [slice(r, r + blk // 2) for r in range(0, blk, blk // 2)]
            other = between()
            sums = []
            for h in range(len(heads)):
                parts = []
                for rows in halves:
                    lf = _log_terms(z[h][rows])[1]
                    if diagonal:
                        lf = jnp.where(after[rows], lf, 0.0)
                    parts.append(_dot(_split_cat(lf, 1), tri_v, NN))
                sums.append(jnp.concatenate(parts, axis=0))
            a = [jnp.exp(z[h] + sums[h] + c[h]) for h in range(len(heads))]
            if diagonal:
                a = [jnp.where(after, ah, 0.0) for ah in a]
            return tuple(ah.astype(BF16) for ah in a), tuple(c[h] + sums[h][:, 0:1] for h in range(len(heads))), other

        def apply(acc, a, j):
            j0 = pl.multiple_of(j * blk, blk)
            vj = v_ref[pl.ds(j0, blk), :]
            acc = list(acc)
            for h, (lb, half) in enumerate(heads):
                vh = vj[:, lanes[lb]]
                acc[lb] = acc[lb] + _dot(a[h], jnp.where(masks[half], vh, jnp.zeros_like(vh)), NN)
            return tuple(acc)

        zero = jnp.zeros((blk, 1), F32)
        a, c, _ = weights(both(scores(i, early), scores(i, late)), (zero,) * len(heads), True)

        def step(m, carry):
            z, a, c, acc = carry
            j = i - 1 - m
            a, c, (z_next, acc) = weights(both(z, scores(j, late)), c, False,
                                          lambda: (scores(jnp.maximum(j - 1, 0), early), apply(acc, a, j + 1)))
            return z_next, a, c, acc

        acc = (jnp.zeros((blk, LANES), F32),) * nlb
        _, a, c, acc = lax.fori_loop(0, i, step, (scores(jnp.maximum(i - 1, 0), early), a, c, acc))
        o_ref[...] = jnp.concatenate(apply(acc, a, 0), axis=1).astype(BF16)
        tot_ref[...] = jnp.concatenate([jnp.where(masks[0], c[2 * lb], c[2 * lb + 1]) for lb in range(nlb)], axis=1)
        pl.when((p == nd - 1) & (i == nq - 1))(finish)

    def whole(cb0):
        return pl.BlockSpec((t, width), functools.partial(lambda p, i, cb0: (0, cb0 + p), cb0=cb0))

    tile = pl.BlockSpec((blk, width), lambda p, i: (i, p))
    res = _pcall(
        body, name="attention_fwd", grid=(nd, nq),
        in_specs=[pl.BlockSpec((blk, width), lambda p, i: (i, 3 * nd + p)), whole(4 * nd), whole(5 * nd),
                  pl.BlockSpec((SUM_TERMS * blk, blk), lambda p, i: (0, 0))] + [HBM] * ns,
        out_specs=[tile, tile] + [HBM] * ns,
        out_shape=[jax.ShapeDtypeStruct((t, d), BF16), jax.ShapeDtypeStruct((t, d), F32)]
        + [jax.ShapeDtypeStruct(_full_shape(k, s), BF16) for k, s in zip(kinds, shapes)],
        scratch_shapes=_gather_scratch(ns),
        compiler_params=_params("arbitrary", "arbitrary"),
    )(hin, hin, hin, tri, *shards)
    return res[0], res[1], res[2:]


def _attention_bwd(hin, tot, do, d, parts):
    t, blk, nlb = hin.shape[0], ATT_BLOCK, ATT_LANE_BLOCKS
    width = nlb * LANES
    nd = d // width
    nq = t // blk
    scale = 1.0 / math.sqrt(HEAD_DIM)
    upper = jnp.triu(jnp.ones((blk, blk), F32)).astype(BF16)
    upper = jnp.concatenate([upper] * SUM_TERMS, axis=1)
    lower = jnp.tril(jnp.ones((blk, blk), F32), -1).astype(BF16)
    npart = len(parts)
    heads = [(lb, half) for lb in range(nlb) for half in range(2)]
    nh = len(heads)

    def body(q_ref, k_ref, v_ref, tot_ref, do_ref, up_ref, low_ref, *rest):
        part_refs, (dq_ref, dk_ref, dv_ref) = rest[:npart], rest[npart:npart + 3]
        land_refs, (send_sems, recv_sems, dk_acc, dv_acc) = rest[npart + 3:2 * npart + 3], rest[2 * npart + 3:]
        p, i = pl.program_id(0), pl.program_id(1)
        start, finish = _exchange_phases(part_refs, land_refs, send_sems, recv_sems)
        pl.when((p == 0) & (i == 0))(start)
        masks = _head_masks()
        lanes = [slice(lb * LANES, (lb + 1) * LANES) for lb in range(nlb)]

        @pl.when(i == 0)
        def _():
            dk_acc[...] = jnp.zeros_like(dk_acc)
            dv_acc[...] = jnp.zeros_like(dv_acc)

        q = q_ref[...].astype(F32) * scale
        dout = do_ref[...]
        qh = [jnp.where(masks[half], q[:, lanes[lb]], 0.0).astype(BF16) for lb, half in heads]
        doh = [jnp.where(masks[half], dout[:, lanes[lb]], jnp.zeros((blk, LANES), BF16)) for lb, half in heads]
        lane = lax.broadcasted_iota(jnp.int32, (8, LANES), 1)
        totals = [lax.dot_general(jnp.where(lane == half * HEAD_DIM, 1.0, 0.0), tot_ref[:, lanes[lb]], (NT, ((), ())),
                                  precision=lax.Precision.HIGHEST, preferred_element_type=F32)[0:1, :] for lb, half in heads]
        after = lax.broadcasted_iota(jnp.int32, (blk, blk), 1) > lax.broadcasted_iota(jnp.int32, (blk, blk), 0)
        up, low = up_ref[...], low_ref[...]

        def products(j, which):
            j0 = pl.multiple_of(j * blk, blk)
            kj, vj = k_ref[pl.ds(j0, blk), :], v_ref[pl.ds(j0, blk), :]
            return tuple((_dot(kj[:, lanes[heads[h][0]]], qh[h], NT), _dot(vj[:, lanes[heads[h][0]]], doh[h], NT)) for h in which)

        early = tuple(h for h in range(nh) if h % 2 == 0)
        late = tuple(h for h in range(nh) if h % 2 == 1)

        def both(first, second):
            out = [None] * nh
            for h, v in zip(early + late, first + second):
                out[h] = v
            return tuple(out)

        def stage_sums(z, diagonal):
            lb, lf = _log_terms(z)
            if diagonal:
                lf = jnp.where(after, lf, 0.0)
            return jnp.exp(lb).astype(BF16), _dot(up, _split_cat(lf, 0), NN)

        def stage_weights(z, da, sums, rest, diagonal):
            rest = rest - sums[0:1, :]
            a = jnp.exp(z + sums + rest)
            if diagonal:
                a = jnp.where(after, a, 0.0)
            g = da * a
            return a.astype(BF16), g, _dot(low, g.astype(BF16), NN), rest

        def stage_dz(g, sig, earlier, before, diagonal):
            dlf = earlier + before
            dz = g - sig.astype(F32) * (g + dlf)
            if diagonal:
                dz = jnp.where(after, dz, 0.0)
            return dz.astype(BF16), dlf[blk - 1:blk, :] + g[blk - 1:blk, :]

        def grads(cur, rest, before, diagonal, between=(lambda: None, lambda: None)):
            sig, sums = zip(*[stage_sums(cur[h][0], diagonal) for h in range(nh)])
            first = between[0]()
            a, g, earlier, rest = zip(*[stage_weights(*cur[h], sums[h], rest[h], diagonal) for h in range(nh)])
            second = between[1]()
            dz, before = zip(*[stage_dz(g[h], sig[h], earlier[h], before[h], diagonal) for h in range(nh)])
            return dz, a, rest, before, first, second

        def flush(j, dz, a, dq):
            j0 = pl.multiple_of(j * blk, blk)
            kj = k_ref[pl.ds(j0, blk), :]
            dq = list(dq)
            dk_new = [jnp.zeros((blk, LANES), F32)] * nlb
            dv_new = [jnp.zeros((blk, LANES), F32)] * nlb
            for h, (lb, half) in enumerate(heads):
                kh = kj[:, lanes[lb]]
                dk_new[lb] = dk_new[lb] + _dot(dz[h], qh[h], NN)
                dv_new[lb] = dv_new[lb] + _dot(a[h], doh[h], NN)
                dq[lb] = dq[lb] + _dot(dz[h], jnp.where(masks[half], kh, jnp.zeros_like(kh)), TN)
            for lb in range(nlb):
                dk_acc[pl.ds(j0, blk), lanes[lb]] += dk_new[lb]
                dv_acc[pl.ds(j0, blk), lanes[lb]] += dv_new[lb]
            return tuple(dq)

        def step(j, carry):
            cur, dz, a, rest, before, dq = carry
            dz, a, rest, before, nxt, dq = grads(
                both(cur, products(j, late)), rest, before, False,
                (lambda: products(j + 1, early), lambda: flush(jnp.maximum(j - 1, 0), dz, a, dq)))
            return nxt, dz, a, rest, before, dq

        zero = jnp.zeros((1, blk), F32)
        nothing = (jnp.zeros((blk, blk), BF16),) * nh
        cur, dz, a, rest, before, dq = lax.fori_loop(
            0, i, step, (products(0, early), nothing, nothing, tuple(totals), (zero,) * nh, (jnp.zeros((blk, LANES), F32),) * nlb))
        dq = flush(jnp.maximum(i - 1, 0), dz, a, dq)
        dz, a = grads(both(cur, products(i, late)), rest, before, True)[:2]
        dq_ref[...] = (jnp.concatenate(flush(i, dz, a, dq), axis=1) * scale).astype(BF16)

        @pl.when(i == nq - 1)
        def _():
            dk_ref[...] = dk_acc[...].astype(BF16)
            dv_ref[...] = dv_acc[...].astype(BF16)

        pl.when((p == nd - 1) & (i == nq - 1))(finish)

    def whole(cb0):
        return pl.BlockSpec((t, width), functools.partial(lambda p, i, cb0: (0, cb0 + p), cb0=cb0))

    tile = pl.BlockSpec((blk, width), lambda p, i: (i, p))
    const = pl.BlockSpec((blk, blk), lambda p, i: (0, 0))
    act = jax.ShapeDtypeStruct((t, d), BF16)
    res = _pcall(
        body, name="attention_bwd", grid=(nd, nq),
        in_specs=[pl.BlockSpec((blk, width), lambda p, i: (i, 3 * nd + p)), whole(4 * nd), whole(5 * nd), tile, tile,
                  pl.BlockSpec((blk, SUM_TERMS * blk), lambda p, i: (0, 0)), const] + [HBM] * npart,
        out_specs=[tile, whole(0), whole(0)] + [HBM] * npart,
        out_shape=[act, act, act] + _exchange_shapes(parts),
        scratch_shapes=[pltpu.SemaphoreType.DMA((npart, 3)), pltpu.SemaphoreType.DMA((npart, 3)),
                        pltpu.VMEM((t, width), F32), pltpu.VMEM((t, width), F32)],
        compiler_params=_params("arbitrary", "arbitrary"),
    )(hin, hin, hin, tot, do, upper, lower, *parts)
    return res[0], res[1], res[2], res[3:]


def _adamw(w, g, m, v, *, name):
    r, c = w.shape
    tr = _tile(r, 256, 8) if r % 8 == 0 and r * c * 4 > (1 << 20) else r
    c1, c2 = 1.0 - ADAM_B1 ** ADAM_STEP, 1.0 - ADAM_B2 ** ADAM_STEP

    def body(w_ref, g_ref, m_ref, v_ref, d_ref, nm_ref, nv_ref):
        gv = g_ref[...]
        nm = ADAM_B1 * m_ref[...] + (1.0 - ADAM_B1) * gv
        nv = ADAM_B2 * v_ref[...] + (1.0 - ADAM_B2) * (gv * gv)
        d_ref[...] = -ADAM_LR * ((nm / c1) / (jnp.sqrt(nv / c2) + ADAM_EPS) + ADAM_WD * w_ref[...])
        nm_ref[...] = nm
        nv_ref[...] = nv

    spec = pl.BlockSpec((tr, c), lambda i: (i, 0))
    shape = jax.ShapeDtypeStruct((r, c), F32)
    return _pcall(
        body, name=name, grid=(r // tr,), in_specs=[spec] * 4, out_specs=[spec] * 3, out_shape=[shape] * 3,
        compiler_params=_params("arbitrary"),
    )(w, g, m, v)


HBM = pl.BlockSpec(memory_space=pltpu.HBM)


def _place():
    x, y, c = lax.axis_index("x"), lax.axis_index("y"), lax.axis_index("c")
    return x, y, c, [(1 - x, y), (x, 1 - y), (1 - x, 1 - y)]


def _window(ref, kind, shard_shape, j, half):
    r, w = shard_shape
    if kind == "col":
        return ref.at[pl.ds(half * (r // 2), r // 2), pl.ds(j * w, w)]
    return ref.at[pl.ds(j * r + half * (r // 2), r // 2), :]


def _full_shape(kind, shard_shape):
    r, w = shard_shape
    return (r, 4 * w) if kind == "col" else (4 * r, w)


def _remote(src, dst, send_sem, recv_sem, device):
    return pltpu.make_async_remote_copy(src_ref=src, dst_ref=dst, send_sem=send_sem, recv_sem=recv_sem,
                                        device_id=device, device_id_type=MESH)


def _gather_phases(big_in, big_out, kinds, shapes, send_sems, recv_sems, local_sems):
    x, y, c, chips = _place()
    me, sibling = 2 * x + y, (x, y, 1 - c)
    n = len(big_in)

    def local(a):
        r, w = shapes[a]
        own = big_out[a].at[:, pl.ds(me * w, w)] if kinds[a] == "col" else big_out[a].at[pl.ds(me * r, r), :]
        return pltpu.make_async_copy(big_in[a], own, local_sems.at[a])

    def over_ici(a, k):
        r = shapes[a][0]
        return _remote(big_in[a].at[pl.ds(c * (r // 2), r // 2), :], _window(big_out[a], kinds[a], shapes[a], me, c),
                       send_sems.at[a, k], recv_sems.at[a, k], (*chips[k], c))

    def landed(a, k, half, slot):
        win = _window(big_out[a], kinds[a], shapes[a], 2 * chips[k][0] + chips[k][1], half)
        return _remote(win, win, send_sems.at[a, slot], recv_sems.at[a, slot], sibling)

    def start():
        for a in range(n):
            local(a).start()
            for k in range(3):
                over_ici(a, k).start()

    def forward():
        for a in range(n):
            for k in range(3):
                landed(a, k, c, k).wait_recv()
                landed(a, k, c, 3 + k).start()

    def finish():
        for a in range(n):
            for k in range(3):
                landed(a, k, 1 - c, 3 + k).wait_recv()
        for a in range(n):
            for k in range(3):
                over_ici(a, k).wait_send()
                landed(a, k, c, 3 + k).wait_send()
            local(a).wait()

    return start, forward, finish


def _gather_scratch(n):
    return [pltpu.SemaphoreType.DMA((n, 6)), pltpu.SemaphoreType.DMA((n, 6)), pltpu.SemaphoreType.DMA((n,))]


def _gather_weights(bigs, kinds, smalls):
    nb, ns = len(bigs), len(smalls)
    shapes = [b.shape for b in bigs]

    def body(*refs):
        big_in, small_in = refs[:nb], refs[nb:nb + ns]
        big_out, small_out = refs[nb + ns:2 * nb + ns], refs[2 * nb + ns:2 * (nb + ns)]
        send_sems, recv_sems, local_sems, small_send, small_recv, small_local = refs[2 * (nb + ns):]
        x, y, c, chips = _place()
        me = 2 * x + y
        start, forward, finish = _gather_phases(big_in, big_out, kinds, shapes, send_sems, recv_sems, local_sems)
        start()
        local, sent = [], []
        for s in range(ns):
            local.append(pltpu.make_async_copy(small_in[s], small_out[s].at[me], small_local.at[s]))
            for k, chip in enumerate(chips):
                sent.append(_remote(small_in[s], small_out[s].at[me], small_send.at[s, k], small_recv.at[s, k], (*chip, c)))
        for cp in local + sent:
            cp.start()
        forward()
        for s in range(ns):
            for k, (px, py) in enumerate(chips):
                dst = small_out[s].at[2 * px + py]
                _remote(dst, dst, small_send.at[s, k], small_recv.at[s, k], (px, py, c)).wait_recv()
        finish()
        for cp in sent:
            cp.wait_send()
        for cp in local:
            cp.wait()

    out_shape = [jax.ShapeDtypeStruct(_full_shape(k, s), b.dtype) for b, k, s in zip(bigs, kinds, shapes)]
    out_shape += [jax.ShapeDtypeStruct((4, *s.shape), s.dtype) for s in smalls]
    return _pcall(
        body, name="gather_weights", in_specs=[HBM] * (nb + ns), out_specs=[HBM] * (nb + ns), out_shape=out_shape,
        scratch_shapes=_gather_scratch(nb) + [pltpu.SemaphoreType.DMA((ns, 3)), pltpu.SemaphoreType.DMA((ns, 3)), pltpu.SemaphoreType.DMA((ns,))],
    )(*bigs, *smalls)


def _swap_other_halves(grads, kinds, shapes, *, name):
    n = len(grads)

    def body(*refs):
        g_in, land = refs[:n], refs[n:2 * n]
        send_sems, recv_sems = refs[2 * n:]
        x, y, c, _ = _place()
        sibling = (x, y, 1 - c)
        sent = []
        for a in range(n):
            for j in range(4):
                cp = _remote(_window(g_in[a], kinds[a], shapes[a], j, 1 - c), land[a].at[j], send_sems.at[a, j], recv_sems.at[a, j], sibling)
                cp.start()
                sent.append(cp)
        for a in range(n):
            for j in range(4):
                _remote(land[a].at[j], land[a].at[j], send_sems.at[a, j], recv_sems.at[a, j], sibling).wait_recv()
        for cp in sent:
            cp.wait_send()

    return _pcall(
        body, name=name, in_specs=[HBM] * n, out_specs=[HBM] * n,
        out_shape=[jax.ShapeDtypeStruct((4, s[0] // 2, s[1]), F32) for s in shapes],
        scratch_shapes=[pltpu.SemaphoreType.DMA((n, 4)), pltpu.SemaphoreType.DMA((n, 4))],
    )(*grads)


def _pair_sum(place, g, land, kind, shard_shape, *, name):
    r, w = shard_shape
    hr = r // 2
    tr = _tile(hr, 256, 16)
    nr = hr // tr

    def body(place_ref, g_ref, l_ref, o_ref, own_ref):
        total = g_ref[...] + l_ref[...]
        o_ref[...] = total.astype(BF16)

        @pl.when(pl.program_id(1) == place_ref[1])
        def _():
            own_ref[...] = total

    if kind == "col":
        g_spec = pl.BlockSpec((tr, w), lambda i, j, p: (p[0] * nr + i, j))
    else:
        g_spec = pl.BlockSpec((tr, w), lambda i, j, p: ((2 * j + p[0]) * nr + i, 0))
    part = pl.BlockSpec((None, tr, w), lambda i, j, p: (j, i, 0))
    return _pcall(
        body, name=name,
        grid_spec=pltpu.PrefetchScalarGridSpec(num_scalar_prefetch=1, grid=(nr, 4), in_specs=[g_spec, part],
                                               out_specs=[part, pl.BlockSpec((tr, w), lambda i, j, p: (i, 0))]),
        out_shape=[jax.ShapeDtypeStruct((4, hr, w), BF16), jax.ShapeDtypeStruct((hr, w), F32)],
        compiler_params=_params("arbitrary", "arbitrary"),
    )(place, g, land)


def _exchange_phases(p_in, land, send_sems, recv_sems):
    x, y, c, chips = _place()

    def copy(a, k):
        px, py = chips[k]
        return _remote(p_in[a].at[2 * px + py], land[a].at[k], send_sems.at[a, k], recv_sems.at[a, k], (px, py, c))

    def start():
        for a in range(len(p_in)):
            for k in range(3):
                copy(a, k).start()

    def finish():
        for a in range(len(p_in)):
            for k in range(3):
                copy(a, k).wait()

    return start, finish


def _exchange_shapes(parts):
    return [jax.ShapeDtypeStruct((3, *p.shape[1:]), p.dtype) for p in parts]


def _chip_exchange(parts, *, name):
    n = len(parts)

    def body(*refs):
        start, finish = _exchange_phases(refs[:n], refs[n:2 * n], *refs[2 * n:])
        start()
        finish()

    return _pcall(
        body, name=name, in_specs=[HBM] * n, out_specs=[HBM] * n, out_shape=_exchange_shapes(parts),
        scratch_shapes=[pltpu.SemaphoreType.DMA((n, 3)), pltpu.SemaphoreType.DMA((n, 3))],
    )(*parts)


def _chip_sum(place, own, land, *, name):
    hr, w = own.shape
    tr = _tile(hr, 256, 16)
    nr = hr // tr

    def body(place_ref, p_ref, l_ref, o_ref):
        o_ref[...] = ((p_ref[...] + l_ref[0].astype(F32)) + l_ref[1].astype(F32)) + l_ref[2].astype(F32)

    return _pcall(
        body, name=name,
        grid_spec=pltpu.PrefetchScalarGridSpec(
            num_scalar_prefetch=1, grid=(nr,),
            in_specs=[pl.BlockSpec((tr, w), lambda i, p: (i, 0)), pl.BlockSpec((3, tr, w), lambda i, p: (0, i, 0))],
            out_specs=pl.BlockSpec((tr, w), lambda i, p: (p[0] * nr + i, 0))),
        out_shape=jax.ShapeDtypeStruct((2 * hr, w), F32),
        compiler_params=_params("arbitrary"),
    )(place, own, land)


def _swap_reduced_halves(halves):
    n = len(halves)

    def body(*refs):
        src, out = refs[:n], refs[n:2 * n]
        send_sems, recv_sems = refs[2 * n:]
        x, y, c, _ = _place()
        sibling = (x, y, 1 - c)
        sent = []
        for a in range(n):
            hr = out[a].shape[0] // 2
            cp = _remote(src[a].at[pl.ds(c * hr, hr), :], out[a].at[pl.ds(c * hr, hr), :], send_sems.at[a], recv_sems.at[a], sibling)
            cp.start()
            sent.append(cp)
        for a in range(n):
            hr = out[a].shape[0] // 2
            other = out[a].at[pl.ds((1 - c) * hr, hr), :]
            _remote(other, other, send_sems.at[a], recv_sems.at[a], sibling).wait_recv()
        for cp in sent:
            cp.wait_send()

    return _pcall(
        body, name="swap_reduced_halves", in_specs=[HBM] * n, out_specs=[HBM] * n,
        out_shape=[jax.ShapeDtypeStruct(h.shape, F32) for h in halves],
        input_output_aliases={a: a for a in range(n)},
        scratch_shapes=[pltpu.SemaphoreType.DMA((n,)), pltpu.SemaphoreType.DMA((n,))],
    )(*halves)


def _sum_small(packed):
    rows = packed.shape[0]

    def body(in_ref, out_ref, land, send_sems, recv_sems, local_sem):
        x, y, c, _ = _place()
        me = 4 * x + 2 * y + c
        own = pltpu.make_async_copy(in_ref, land.at[me], local_sem)
        own.start()
        flips = [(r >> 2, (r >> 1) & 1, r & 1) for r in range(1, 8)]
        sent = []
        for k, (fx, fy, fc) in enumerate(flips):
            cp = _remote(in_ref, land.at[me], send_sems.at[k], recv_sems.at[k], (x ^ fx, y ^ fy, c ^ fc))
            cp.start()
            sent.append(cp)
        for k, (fx, fy, fc) in enumerate(flips):
            src = land.at[4 * (x ^ fx) + 2 * (y ^ fy) + (c ^ fc)]
            _remote(src, src, send_sems.at[k], recv_sems.at[k], (x ^ fx, y ^ fy, c ^ fc)).wait_recv()
        for cp in sent:
            cp.wait_send()
        own.wait()
        total = land[0]
        for dev in range(1, 8):
            total = total + land[dev]
        out_ref[...] = total

    vmem = pl.BlockSpec(memory_space=pltpu.VMEM)
    return _pcall(
        body, name="sum_small", in_specs=[vmem], out_specs=vmem, out_shape=jax.ShapeDtypeStruct(packed.shape, F32),
        scratch_shapes=[pltpu.VMEM((8, rows, LANES), F32), pltpu.SemaphoreType.DMA((7,)), pltpu.SemaphoreType.DMA((7,)), pltpu.SemaphoreType.DMA],
    )(packed)


def _sigmoid(v):
    return 1.0 / (1.0 + jnp.exp(-v))


def kernel(x, meta_tokens, g_pre_mix, w_in, conv_w_mix, w_proj_conv, w_proj_attn, b_gate, w_out, g_post_mix, g_pre_ffn, w_up_gate, conv_w_ffn, w_down, g_post_ffn, loss_target, m_meta_tokens, m_g_pre_mix, m_w_in, m_conv_w_mix, m_w_proj_conv, m_w_proj_attn, m_b_gate, m_w_out, m_g_post_mix, m_g_pre_ffn, m_w_up_gate, m_conv_w_ffn, m_w_down, m_g_post_ffn, v_meta_tokens, v_g_pre_mix, v_w_in, v_conv_w_mix, v_w_proj_conv, v_w_proj_attn, v_b_gate, v_w_out, v_g_post_mix, v_g_pre_ffn, v_w_up_gate, v_conv_w_ffn, v_w_down, v_g_post_ffn):
    seq, d = x.shape[1], x.shape[2]
    f = w_down.shape[1] * 4
    real = N_META + seq
    t = -(-real // ATT_BLOCK) * ATT_BLOCK
    nd = d // LANES
    cx, cy, cc = lax.axis_index("x"), lax.axis_index("y"), lax.axis_index("c")
    chip = 2 * cx + cy
    place = jnp.stack([cc, chip]).astype(jnp.int32)

    big_names = ["w_in", "w_proj_conv", "w_proj_attn", "w_out", "w_up_gate", "w_down"]
    kinds = ["col", "row", "row", "row", "col", "row"]
    big_w = [w_in[0], w_proj_conv[0], w_proj_attn[0], w_out[0], w_up_gate[0], w_down[0]]
    small_w = [meta_tokens, conv_w_mix[0], b_gate[0], conv_w_ffn[0]]
    shards = [w.astype(BF16) for w in big_w]
    shapes = [w.shape for w in big_w]
    gathered = _gather_weights(shards[:1], kinds[:1], small_w)
    wi = gathered[0]
    meta_f, cwm_f, bg_f, cwf_f = (jnp.moveaxis(s, 0, 1).reshape(s.shape[1], -1) for s in gathered[1:])

    h0 = jnp.concatenate([meta_f, x[0], jnp.zeros((t - real, d), F32)], axis=0)
    target = jnp.concatenate([jnp.zeros((N_META, d), F32), loss_target[0], jnp.zeros((t - real, d), F32)], axis=0)
    xn1 = _prenorm(h0, g_pre_mix, name="prenorm_mix")
    hin = _matmul(xn1, wi, name="in_proj")
    yconv = _mixer_conv_fwd(hin, cwm_f, d)
    o, att_tot, (wpc, wpa, wo, wug, wd) = _attention_fwd(hin, d, shards[1:], kinds[1:])

    def gates(gpre, bias):
        return _sigmoid(gpre[:, :d].astype(F32) + bias[0:1]), _sigmoid(gpre[:, d:].astype(F32) + bias[1:2])

    def merge_epi(accs, rows, vecs, row0):
        gate_c, gate_a = gates(rows[0], vecs[0])
        return [gate_c * accs[0] + gate_a * accs[1], accs[0], accs[1]]

    act = ("row", d, BF16)
    merged, bconv, battn = _matmul_rows([(yconv, wpc), (o, wpa)], [(hin, 2 * d, 3)], [bg_f], merge_epi, [act, act, act], name="branch_proj")

    def mix_epi(accs, rows, vecs, row0):
        mix = accs[0]
        h1 = rows[0] + mix * _rms(mix) * vecs[0]
        return [mix, h1, h1 * _rms(h1) * vecs[1]]

    mix, h1, xn2 = _matmul_rows([(merged, wo)], [(h0, d, 0)], [g_post_mix, g_pre_ffn], mix_epi,
                                [("row", d, F32), ("row", d, F32), act], name="out_proj")
    ug = _matmul(xn2, wug, name="up_proj")
    fact = _ffn_conv_fwd(ug, cwf_f, f)

    def loss_epi(accs, rows, vecs, row0):
        ffn = accs[0]
        r = _rms(ffn)
        fh = ffn * r
        h2 = rows[0] + fh * vecs[0]
        rid = row0 + lax.broadcasted_iota(jnp.int32, (ffn.shape[0], 1), 0)
        err = jnp.where((rid >= N_META) & (rid < real), h2 - rows[1], 0.0)
        dy = err * (1.0 / d)
        dffn, dg = _rms_bwd(dy, fh, r, vecs[0])
        loss = jnp.zeros((1, LANES), F32) + 0.5 * jnp.sum(err * err) / d
        return [dffn, dy, dg, loss]

    dffn, dh2, dg_post_ffn, loss_part = _matmul_rows(
        [(fact, wd)], [(h1, d, 0), (target, d, 0)], [g_post_ffn], loss_epi,
        [act, ("row", d, F32), ("acc", 1, d), ("acc", 1, LANES)], name="down_proj_loss")

    gw_down = _matmul_tn(fact, dffn, name="grad_w_down")
    df = _matmul(dffn, wd, nt=True, name="down_proj_bwd")
    du2, dg2, gcw_ffn = _ffn_conv_bwd(ug, df, cwf_f, f)

    def ffn_in_epi(accs, rows, vecs, row0):
        h1v, dyv, mixv = rows
        r3 = _rms(h1v)
        dh1n, dg3 = _rms_bwd(accs[0], h1v * r3, r3, vecs[0])
        dh1 = dyv + dh1n
        r2 = _rms(mixv)
        dmix, dgm = _rms_bwd(dh1, mixv * r2, r2, vecs[1])
        return [dh1, dmix, dg3, dgm]

    dh1, dmix, dg_pre_ffn, dg_post_mix, dug = _matmul_rows_joined(
        [du2, dg2], wug, [(h1, d, 0), (dh2, d, 0), (mix, d, 0)], [g_pre_ffn, g_post_mix], ffn_in_epi,
        [("row", d, F32), act, ("acc", 1, d), ("acc", 1, d)], tk=f // 2, name="up_proj_bwd")
    gw_up = _matmul_tn(xn2, dug, name="grad_w_up_gate")
    gw_out = _matmul_tn(merged, dmix, name="grad_w_out")

    def merge_bwd_epi(accs, rows, vecs, row0):
        dm = accs[0]
        gate_c, gate_a = gates(rows[0], vecs[0])
        dpre_c = dm * rows[1].astype(F32) * gate_c * (1.0 - gate_c)
        dpre_a = dm * rows[2].astype(F32) * gate_a * (1.0 - gate_a)
        dbias = jnp.concatenate([jnp.sum(dpre_c, axis=0, keepdims=True), jnp.sum(dpre_a, axis=0, keepdims=True)], axis=0)
        return [dm * gate_c, dm * gate_a, jnp.concatenate([dpre_c, dpre_a], axis=1), dbias]

    dbconv, dbattn, dgates, gb_gate = _matmul_rows(
        [(dmix, wo)], [(hin, 2 * d, 3), (bconv, d, 0), (battn, d, 0)], [bg_f], merge_bwd_epi,
        [act, act, ("row", 2 * d, BF16), ("acc", 2, d)], nt=True, name="out_proj_bwd")
    gw_pc = _matmul_tn(yconv, dbconv, name="grad_w_proj_conv")
    gw_pa = _matmul_tn(o, dbattn, name="grad_w_proj_attn")
    dyconv = _matmul(dbconv, wpc, nt=True, name="proj_conv_bwd")
    do = _matmul(dbattn, wpa, nt=True, name="proj_attn_bwd")
    db, dc, dhc, gcw_mix = _mixer_conv_bwd(hin, dyconv, cwm_f, d)

    def pair_sums(grads, first):
        landed = _swap_other_halves(grads, kinds[first:first + len(grads)], shapes[first:first + len(grads)],
                                    name=f"swap_other_halves_{big_names[first]}")
        return zip(*[_pair_sum(place, g, l, kinds[first + a], shapes[first + a], name=f"pair_sum_{big_names[first + a]}")
                     for a, (g, l) in enumerate(zip(grads, landed))])

    parts, own = pair_sums([gw_pc, gw_pa, gw_out, gw_up, gw_down], 1)
    dq, dk, dv, landed = _attention_bwd(hin, att_tot, do, d, parts)

    def in_epi(accs, rows, vecs, row0):
        h0v, dh1v = rows
        r1 = _rms(h0v)
        dh0n, dg1 = _rms_bwd(accs[0], h0v * r1, r1, vecs[0])
        return [dh1v + dh0n, dg1]

    dh0, dg_pre_mix, dhin = _matmul_rows_joined(
        [db, dc, dhc, dq, dk, dv, dgates], wi, [(h0, d, 0), (dh1, d, 0)], [g_pre_mix], in_epi,
        [("row", d, F32), ("acc", 1, d)], tk=d, name="in_proj_bwd")
    gw_in = _matmul_tn(xn1, dhin, name="grad_w_in")
    grad_x = dh0[N_META:real][None]

    parts_in, own_in = pair_sums([gw_in], 0)
    landed = list(_chip_exchange(parts_in, name="chip_exchange_w_in")) + list(landed)
    halves = [_chip_sum(place, p, l, name=f"chip_sum_{n}") for p, l, n in zip(list(own_in) + list(own), landed, big_names)]
    big_r = _swap_reduced_halves(halves)

    small_g = [dh0[:N_META], dg_pre_mix, gcw_mix, gb_gate, dg_post_mix, dg_pre_ffn, gcw_ffn, dg_post_ffn, loss_part]
    flat = jnp.concatenate([s.reshape(-1) for s in small_g])
    pad = -flat.shape[0] % (8 * LANES)
    summed = _sum_small(jnp.pad(flat, (0, pad)).reshape(-1, LANES)).reshape(-1)
    small_r, pos = [], 0
    for s in small_g:
        small_r.append(summed[pos:pos + s.size].reshape(s.shape))
        pos += s.size
    g_meta, g_g1, g_cwm, g_bg, g_g2, g_g3, g_cwf, g_g4, loss_row = small_r
    loss = loss_row[0, 0]

    def my_cols(full, width):
        return lax.dynamic_slice_in_dim(full, chip * width, width, axis=1)

    grads = {
        "meta_tokens": my_cols(g_meta, d // 4), "g_pre_mix": g_g1, "w_in": big_r[0][None],
        "conv_w_mix": my_cols(g_cwm, d // 4)[None], "w_proj_conv": big_r[1][None], "w_proj_attn": big_r[2][None],
        "b_gate": my_cols(g_bg, d // 4)[None], "w_out": big_r[3][None], "g_post_mix": g_g2, "g_pre_ffn": g_g3,
        "w_up_gate": big_r[4][None], "conv_w_ffn": my_cols(g_cwf, f // 4)[None], "w_down": big_r[5][None], "g_post_ffn": g_g4,
    }
    weights = dict(meta_tokens=meta_tokens, g_pre_mix=g_pre_mix, w_in=w_in, conv_w_mix=conv_w_mix, w_proj_conv=w_proj_conv,
                   w_proj_attn=w_proj_attn, b_gate=b_gate, w_out=w_out, g_post_mix=g_post_mix, g_pre_ffn=g_pre_ffn,
                   w_up_gate=w_up_gate, conv_w_ffn=conv_w_ffn, w_down=w_down, g_post_ffn=g_post_ffn)
    m_in = dict(meta_tokens=m_meta_tokens, g_pre_mix=m_g_pre_mix, w_in=m_w_in, conv_w_mix=m_conv_w_mix, w_proj_conv=m_w_proj_conv,
                w_proj_attn=m_w_proj_attn, b_gate=m_b_gate, w_out=m_w_out, g_post_mix=m_g_post_mix, g_pre_ffn=m_g_pre_ffn,
                w_up_gate=m_w_up_gate, conv_w_ffn=m_conv_w_ffn, w_down=m_w_down, g_post_ffn=m_g_post_ffn)
    v_in = dict(meta_tokens=v_meta_tokens, g_pre_mix=v_g_pre_mix, w_in=v_w_in, conv_w_mix=v_conv_w_mix, w_proj_conv=v_w_proj_conv,
                w_proj_attn=v_w_proj_attn, b_gate=v_b_gate, w_out=v_w_out, g_post_mix=v_g_post_mix, g_pre_ffn=v_g_pre_ffn,
                w_up_gate=v_w_up_gate, conv_w_ffn=v_conv_w_ffn, w_down=v_w_down, g_post_ffn=v_g_post_ffn)
    names = list(weights)
    deltas, new_m, new_v = [], [], []
    for n in names:
        shape = weights[n].shape
        two_d = (-1, shape[-1])
        dl, nm, nv = _adamw(weights[n].reshape(two_d), grads[n].reshape(two_d), m_in[n].reshape(two_d), v_in[n].reshape(two_d),
                            name=f"adamw_{n}")
        deltas.append(dl.reshape(shape))
        new_m.append(nm.reshape(shape))
        new_v.append(nv.reshape(shape))
    return (loss, grad_x, *[grads[n].reshape(weights[n].shape) for n in names], *deltas, *new_m, *new_v)
```

```python
import functools
import math

import jax
import jax.numpy as jnp
from jax import lax
from jax.experimental import pallas as pl
from jax.experimental.pallas import tpu as pltpu

F32 = jnp.float32
BF16 = jnp.bfloat16

N_META = 16
HEAD_DIM = 64
LANES = 128
RMS_EPS = 1e-6
ATT_BLOCK = 256
ATT_FWD_LANE_BLOCKS = 4
ATT_BWD_LANE_BLOCKS = 2
VMEM_LIMIT = 56 * 1024 * 1024

ADAM_LR = 0.001
ADAM_B1 = 0.9
ADAM_B2 = 0.999
ADAM_EPS = 1e-08
ADAM_WD = 0.01
ADAM_STEP = 10

MESH = pl.DeviceIdType.MESH


def _tile(n, cap, unit=LANES):
    d = (min(cap, n) // unit) * unit
    while d >= unit:
        if n % d == 0:
            return d
        d -= unit
    raise ValueError(f"no tile for {n} under {cap}")


MXU_EDGE = 256


def _mxu_tile(n, cap):
    if n <= cap:
        return n
    return _tile(n, cap, MXU_EDGE) if n % MXU_EDGE == 0 else _tile(n, cap)


def _params(*sem):
    return pltpu.CompilerParams(dimension_semantics=sem, vmem_limit_bytes=VMEM_LIMIT)


def _pcall(body, **kw):
    return pl.pallas_call(body, **kw)


def _dot(a, b, dims):
    return lax.dot_general(a, b, (dims, ((), ())), preferred_element_type=F32)


NN = ((1,), (0,))
NT = ((1,), (1,))
TN = ((0,), (0,))


def _matmul(a, b, *, nt=False, out_dtype=BF16, name):
    m, kdim = a.shape
    n = b.shape[0] if nt else b.shape[1]
    tm, tn, tk = _tile(m, 640, 16), _mxu_tile(n, 2816), _mxu_tile(kdim, 2816)
    nk = kdim // tk

    def body(a_ref, b_ref, o_ref, *scratch):
        p = _dot(a_ref[...], b_ref[...], NT if nt else NN)
        if nk == 1:
            o_ref[...] = p.astype(o_ref.dtype)
            return
        acc, k = scratch[0], pl.program_id(2)

        @pl.when(k == 0)
        def _():
            acc[...] = p

        @pl.when(k > 0)
        def _():
            acc[...] += p

        @pl.when(k == nk - 1)
        def _():
            o_ref[...] = acc[...].astype(o_ref.dtype)

    b_spec = pl.BlockSpec((tn, tk), lambda j, i, k: (j, k)) if nt else pl.BlockSpec((tk, tn), lambda j, i, k: (k, j))
    return _pcall(
        body, name=name, grid=(n // tn, m // tm, nk),
        in_specs=[pl.BlockSpec((tm, tk), lambda j, i, k: (i, k)), b_spec],
        out_specs=pl.BlockSpec((tm, tn), lambda j, i, k: (i, j)),
        out_shape=jax.ShapeDtypeStruct((m, n), out_dtype),
        scratch_shapes=[pltpu.VMEM((tm, tn), F32)] if nk > 1 else [],
        compiler_params=_params("arbitrary", "arbitrary", "arbitrary"),
    )(a, b)


def _matmul_tn(a, b, *, name):
    t, ka = a.shape
    nb = b.shape[1]
    tb = _mxu_tile(nb, 1024)
    budget = VMEM_LIMIT * 5 // 7
    ta = next(c for c in range(ka, 0, -LANES) if ka % c == 0 and 2 * (2 * t * (c + tb) + 4 * c * tb) <= budget)

    def body(a_ref, b_ref, o_ref):
        o_ref[...] = _dot(a_ref[...], b_ref[...], TN)

    return _pcall(
        body, name=name, grid=(ka // ta, nb // tb),
        in_specs=[pl.BlockSpec((t, ta), lambda i, j: (0, i)), pl.BlockSpec((t, tb), lambda i, j: (0, j))],
        out_specs=pl.BlockSpec((ta, tb), lambda i, j: (i, j)),
        out_shape=jax.ShapeDtypeStruct((ka, nb), F32),
        compiler_params=_params("arbitrary", "arbitrary"),
    )(a, b)


def _matmul_rows(pairs, rows, vecs, epi, outs, *, nt=False, name):
    m, kdim = pairs[0][0].shape
    n = pairs[0][1].shape[0] if nt else pairs[0][1].shape[1]
    tm, tk = _tile(m, 640, 16), _mxu_tile(kdim, 2816)
    nk, npair = kdim // tk, len(pairs)

    def body(*refs):
        a_refs, b_refs = refs[0:2 * npair:2], refs[1:2 * npair:2]
        pos = 2 * npair
        row_refs = refs[pos:pos + len(rows)]
        pos += len(rows)
        vec_refs = refs[pos:pos + len(vecs)]
        pos += len(vecs)
        out_refs = refs[pos:pos + len(outs)]
        accs = refs[pos + len(outs):]
        i, k = pl.program_id(0), pl.program_id(1)
        prods = [_dot(a[...], b[...], NT if nt else NN) for a, b in zip(a_refs, b_refs)]
        if nk > 1:
            @pl.when(k == 0)
            def _():
                for acc, p in zip(accs, prods):
                    acc[...] = p

            @pl.when(k > 0)
            def _():
                for acc, p in zip(accs, prods):
                    acc[...] += p

        @pl.when(k == nk - 1)
        def _():
            vals = [acc[...] for acc in accs] if nk > 1 else prods
            res = epi(vals, [r[...] for r in row_refs], [v[...] for v in vec_refs], i * tm)
            for o_ref, spec, val in zip(out_refs, outs, res):
                if spec[0] == "row":
                    o_ref[...] = val.astype(o_ref.dtype)
                else:
                    @pl.when(i == 0)
                    def _():
                        o_ref[...] = val

                    @pl.when(i > 0)
                    def _():
                        o_ref[...] += val

    in_specs, args = [], []
    for a, b in pairs:
        in_specs += [pl.BlockSpec((tm, tk), lambda i, k: (i, k)),
                     pl.BlockSpec((n, tk), lambda i, k: (0, k)) if nt else pl.BlockSpec((tk, n), lambda i, k: (k, 0))]
        args += [a, b]
    for arr, width, cb in rows:
        in_specs.append(pl.BlockSpec((tm, width), functools.partial(lambda i, k, cb: (i, cb), cb=cb)))
        args.append(arr)
    for v in vecs:
        in_specs.append(pl.BlockSpec(v.shape, lambda i, k: (0, 0)))
        args.append(v)
    out_specs, out_shape = [], []
    for spec in outs:
        if spec[0] == "row":
            out_specs.append(pl.BlockSpec((tm, spec[1]), lambda i, k: (i, 0)))
            out_shape.append(jax.ShapeDtypeStruct((m, spec[1]), spec[2]))
        else:
            out_specs.append(pl.BlockSpec((spec[1], spec[2]), lambda i, k: (0, 0)))
            out_shape.append(jax.ShapeDtypeStruct((spec[1], spec[2]), F32))
    return _pcall(
        body, name=name, grid=(m // tm, nk), in_specs=in_specs, out_specs=out_specs, out_shape=out_shape,
        scratch_shapes=[pltpu.VMEM((tm, n), F32) for _ in pairs] if nk > 1 else [],
        compiler_params=_params("arbitrary", "arbitrary"),
    )(*args)


def _matmul_rows_joined(segments, b, rows, vecs, epi, outs, *, tk, name, joined=True, parts=()):
    m = segments[0].shape[0]
    n, kdim = b.shape
    tm = _tile(m, 640, 16)
    nm, nk, npart = m // tm, kdim // tk, len(parts)
    first = [0]
    for s in segments:
        assert s.shape[1] % tk == 0
        first.append(first[-1] + s.shape[1] // tk)
    assert first[-1] == nk

    def body(*refs):
        seg_refs, b_ref = refs[:len(segments)], refs[len(segments)]
        pos = len(segments) + 1
        row_refs = refs[pos:pos + len(rows)]
        pos += len(rows)
        vec_refs = refs[pos:pos + len(vecs)]
        pos += len(vecs)
        part_refs = refs[pos:pos + npart]
        pos += npart
        out_refs = refs[pos:pos + len(outs)]
        pos += len(outs)
        joined_ref = refs[pos] if joined else None
        pos += int(joined)
        land_refs, acc = refs[pos:pos + npart], refs[pos + npart]
        i, k = pl.program_id(0), pl.program_id(1)
        if npart:
            start, finish = _exchange_phases(part_refs, land_refs, *refs[pos + npart + 1:])
            pl.when((i == 0) & (k == 0))(start)

        @pl.when(k == 0)
        def _():
            acc[...] = jnp.zeros_like(acc)

        for s, seg in enumerate(seg_refs):
            @pl.when((k >= first[s]) & (k < first[s + 1]))
            def _():
                if joined:
                    joined_ref[...] = seg[...]
                acc[...] += _dot(seg[...], b_ref[...], NT)

        @pl.when(k == nk - 1)
        def _():
            res = epi([acc[...]], [r[...] for r in row_refs], [v[...] for v in vec_refs], i * tm)
            for o_ref, spec, val in zip(out_refs, outs, res):
                if spec[0] == "row":
                    o_ref[...] = val.astype(o_ref.dtype)
                else:
                    @pl.when(i == 0)
                    def _():
                        o_ref[...] = val

                    @pl.when(i > 0)
                    def _():
                        o_ref[...] += val

        if npart:
            pl.when((i == nm - 1) & (k == nk - 1))(finish)

    def seg_spec(s):
        last = first[s + 1] - first[s] - 1
        return pl.BlockSpec((tm, tk), lambda i, k: (i, jnp.clip(k - first[s], 0, last)))

    in_specs = [seg_spec(s) for s in range(len(segments))] + [pl.BlockSpec((n, tk), lambda i, k: (0, k))]
    args = [*segments, b]
    for arr, width, cb in rows:
        in_specs.append(pl.BlockSpec((tm, width), functools.partial(lambda i, k, cb: (i, cb), cb=cb)))
        args.append(arr)
    for v in vecs:
        in_specs.append(pl.BlockSpec(v.shape, lambda i, k: (0, 0)))
        args.append(v)
    in_specs += [HBM] * npart
    args += list(parts)
    out_specs, out_shape = [], []
    for spec in outs:
        if spec[0] == "row":
            out_specs.append(pl.BlockSpec((tm, spec[1]), lambda i, k: (i, 0)))
            out_shape.append(jax.ShapeDtypeStruct((m, spec[1]), spec[2]))
        else:
            out_specs.append(pl.BlockSpec((spec[1], spec[2]), lambda i, k: (0, 0)))
            out_shape.append(jax.ShapeDtypeStruct((spec[1], spec[2]), F32))
    if joined:
        out_specs.append(pl.BlockSpec((tm, tk), lambda i, k: (i, k)))
        out_shape.append(jax.ShapeDtypeStruct((m, kdim), segments[0].dtype))
    out_specs += [HBM] * npart
    out_shape += _exchange_shapes(parts)
    sems = [pltpu.SemaphoreType.DMA((npart, 3)), pltpu.SemaphoreType.DMA((npart, 3))] if npart else []
    return _pcall(
        body, name=name, grid=(nm, nk), in_specs=in_specs, out_specs=out_specs, out_shape=out_shape,
        scratch_shapes=[pltpu.VMEM((tm, n), F32)] + sems,
        compiler_params=_params("arbitrary", "arbitrary"),
    )(*args)


def _matmul_tn_segments(a, segments, *, name):
    t, ka = a.shape
    tb = MXU_EDGE
    first = [0]
    for s in segments:
        assert s.shape[1] % tb == 0
        first.append(first[-1] + s.shape[1] // tb)
    budget = VMEM_LIMIT * 5 // 7
    ta = next(c for c in range(ka, 0, -LANES) if ka % c == 0 and 2 * (2 * t * (c + len(segments) * tb) + 4 * c * tb) <= budget)

    def body(a_ref, *refs):
        seg_refs, o_ref = refs[:-1], refs[-1]
        j = pl.program_id(1)
        for s, seg in enumerate(seg_refs):
            @pl.when((j >= first[s]) & (j < first[s + 1]))
            def _():
                o_ref[...] = _dot(a_ref[...], seg[...], TN)

    def seg_spec(s):
        last = first[s + 1] - first[s] - 1
        return pl.BlockSpec((t, tb), lambda i, j: (0, jnp.clip(j - first[s], 0, last)))

    return _pcall(
        body, name=name, grid=(ka // ta, first[-1]),
        in_specs=[pl.BlockSpec((t, ta), lambda i, j: (0, i))] + [seg_spec(s) for s in range(len(segments))],
        out_specs=pl.BlockSpec((ta, tb), lambda i, j: (i, j)),
        out_shape=jax.ShapeDtypeStruct((ka, first[-1] * tb), F32),
        compiler_params=_params("arbitrary", "arbitrary"),
    )(a, *segments)


def _rms(v):
    return lax.rsqrt(jnp.mean(v * v, axis=-1, keepdims=True) + RMS_EPS)


def _rms_bwd(dz, vhat, r, g):
    t = dz * g
    dv = r * (t - vhat * jnp.mean(t * vhat, axis=-1, keepdims=True))
    return dv, jnp.sum(dz * vhat, axis=0, keepdims=True)


def _prenorm(h, g, *, name):
    m, d = h.shape
    tm = _tile(m, 640, 16)

    def body(h_ref, g_ref, o_ref):
        v = h_ref[...]
        o_ref[...] = (v * _rms(v) * g_ref[...]).astype(BF16)

    return _pcall(
        body, name=name, grid=(m // tm,),
        in_specs=[pl.BlockSpec((tm, d), lambda i: (i, 0)), pl.BlockSpec((1, d), lambda i: (0, 0))],
        out_specs=pl.BlockSpec((tm, d), lambda i: (i, 0)),
        out_shape=jax.ShapeDtypeStruct((m, d), BF16),
        compiler_params=_params("arbitrary"),
    )(h, g)


SUBLANES = 8


def _shift_down(u, k):
    r = pltpu.roll(u, k, 0)
    rows = lax.broadcasted_iota(jnp.int32, (SUBLANES, u.shape[1]), 0)
    return jnp.concatenate([jnp.where(rows >= k, r[:SUBLANES], 0.0), r[SUBLANES:]], axis=0)


def _shift_up(u, k):
    n = u.shape[0]
    r = pltpu.roll(u, n - k, 0)
    rows = lax.broadcasted_iota(jnp.int32, (SUBLANES, u.shape[1]), 0)
    return jnp.concatenate([r[:n - SUBLANES], jnp.where(rows < SUBLANES - k, r[n - SUBLANES:], 0.0)], axis=0)


def _taps(u):
    return _shift_down(u, 2), _shift_down(u, 1), u


def _conv(taps, w):
    return w[0:1] * taps[0] + w[1:2] * taps[1] + w[2:3] * taps[2]


def _conv_bwd(dcu, taps, w):
    du = w[2:3] * dcu + w[1:2] * _shift_up(dcu, 1) + w[0:1] * _shift_up(dcu, 2)
    return du, [jnp.sum(dcu * tap, axis=0, keepdims=True) for tap in taps]


def _strip(arr, t, cb0):
    return pl.BlockSpec((t, LANES), functools.partial(lambda s, cb0: (0, cb0 + s), cb0=cb0))


def _mixer_conv_fwd(hin, w, d):
    t, ns = hin.shape[0], d // LANES

    def body(b_ref, c_ref, h_ref, w_ref, y_ref):
        u = c_ref[...].astype(F32) * h_ref[...].astype(F32)
        y_ref[...] = (b_ref[...].astype(F32) * _conv(_taps(u), w_ref[...])).astype(BF16)

    return _pcall(
        body, name="mixer_conv_fwd", grid=(ns,),
        in_specs=[_strip(hin, t, 0), _strip(hin, t, ns), _strip(hin, t, 2 * ns), pl.BlockSpec((3, LANES), lambda s: (0, s))],
        out_specs=pl.BlockSpec((t, LANES), lambda s: (0, s)),
        out_shape=jax.ShapeDtypeStruct((t, d), BF16),
        compiler_params=_params("arbitrary"),
    )(hin, hin, hin, w)


def _mixer_conv_bwd(hin, dy, w, d):
    t, ns = hin.shape[0], d // LANES

    def body(b_ref, c_ref, h_ref, dy_ref, w_ref, db_ref, dc_ref, dh_ref, dw_ref):
        b, c, h, g = (r[...].astype(F32) for r in (b_ref, c_ref, h_ref, dy_ref))
        wv = w_ref[...]
        taps = _taps(c * h)
        db_ref[...] = (g * _conv(taps, wv)).astype(BF16)
        du, dw = _conv_bwd(g * b, taps, wv)
        dc_ref[...] = (du * h).astype(BF16)
        dh_ref[...] = (du * c).astype(BF16)
        for k in range(3):
            dw_ref[k:k + 1, :] = dw[k]

    col = pl.BlockSpec((t, LANES), lambda s: (0, s))
    act = jax.ShapeDtypeStruct((t, d), BF16)
    return _pcall(
        body, name="mixer_conv_bwd", grid=(ns,),
        in_specs=[_strip(hin, t, 0), _strip(hin, t, ns), _strip(hin, t, 2 * ns), col, pl.BlockSpec((3, LANES), lambda s: (0, s))],
        out_specs=[col, col, col, pl.BlockSpec((3, LANES), lambda s: (0, s))],
        out_shape=[act, act, act, jax.ShapeDtypeStruct((3, d), F32)],
        compiler_params=_params("arbitrary"),
    )(hin, hin, hin, dy, w)


GELU_C = math.sqrt(2.0 / math.pi)
GELU_A = 0.044715


def _gelu_tanh(x):
    return jnp.tanh(GELU_C * (x + GELU_A * x * x * x))


def _ffn_conv_fwd(ug, w, f):
    t, ns = ug.shape[0], f // LANES

    def body(u_ref, g_ref, w_ref, o_ref):
        cu = _conv(_taps(u_ref[...].astype(F32)), w_ref[...])
        o_ref[...] = (0.5 * cu * (1.0 + _gelu_tanh(cu)) * g_ref[...].astype(F32)).astype(BF16)

    return _pcall(
        body, name="ffn_conv_fwd", grid=(ns,),
        in_specs=[_strip(ug, t, 0), _strip(ug, t, ns), pl.BlockSpec((3, LANES), lambda s: (0, s))],
        out_specs=pl.BlockSpec((t, LANES), lambda s: (0, s)),
        out_shape=jax.ShapeDtypeStruct((t, f), BF16),
        compiler_params=_params("arbitrary"),
    )(ug, ug, w)


def _ffn_conv_bwd(ug, df, w, f):
    t, ns = ug.shape[0], f // LANES

    def body(u_ref, g_ref, df_ref, w_ref, du_ref, dg_ref, dw_ref):
        u, g, d = (r[...].astype(F32) for r in (u_ref, g_ref, df_ref))
        wv = w_ref[...]
        taps = _taps(u)
        cu = _conv(taps, wv)
        th = _gelu_tanh(cu)
        half = 0.5 * (1.0 + th)
        dg_ref[...] = (d * (cu * half)).astype(BF16)
        dgelu = half + (0.5 * GELU_C) * cu * (1.0 - th * th) * (1.0 + (3.0 * GELU_A) * (cu * cu))
        du, dw = _conv_bwd(d * g * dgelu, taps, wv)
        du_ref[...] = du.astype(BF16)
        for k in range(3):
            dw_ref[k:k + 1, :] = dw[k]

    col = pl.BlockSpec((t, LANES), lambda s: (0, s))
    act = jax.ShapeDtypeStruct((t, f), BF16)
    return _pcall(
        body, name="ffn_conv_bwd", grid=(ns,),
        in_specs=[_strip(ug, t, 0), _strip(ug, t, ns), col, pl.BlockSpec((3, LANES), lambda s: (0, s))],
        out_specs=[col, col, pl.BlockSpec((3, LANES), lambda s: (0, s))],
        out_shape=[act, act, jax.ShapeDtypeStruct((3, f), F32)],
        compiler_params=_params("arbitrary"),
    )(ug, ug, df, w)


def _log_terms(z):
    minus_abs = lax.bitcast_convert_type(lax.bitcast_convert_type(z, jnp.uint32) | jnp.uint32(0x80000000), F32)
    lb = jnp.minimum(z, 0.0) - jnp.log(1.0 + jnp.exp(minus_abs))
    return lb, lb - z


SUM_TERMS = 1


def _split_cat(v, axis):
    terms = [v.astype(BF16)]
    for _ in range(SUM_TERMS - 1):
        v = v - terms[-1].astype(F32)
        terms.append(v.astype(BF16))
    return terms[0] if SUM_TERMS == 1 else jnp.concatenate(terms, axis=axis)


def _head_masks():
    lane = lax.broadcasted_iota(jnp.int32, (1, LANES), 1)
    return lane < HEAD_DIM, lane >= HEAD_DIM


def _attention_fwd(hin, d, shards, kinds):
    t, blk, nlb = hin.shape[0], ATT_BLOCK, min(ATT_FWD_LANE_BLOCKS, d // LANES)
    width = nlb * LANES
    nd = d // width
    nq = t // blk
    scale = 1.0 / math.sqrt(HEAD_DIM)
    tri = jnp.tril(jnp.ones((blk, blk), F32)).astype(BF16)
    tri = jnp.concatenate([tri] * SUM_TERMS, axis=0)
    ns = len(shards)
    shapes = [s.shape for s in shards]
    heads = [(lb, half) for lb in range(nlb) for half in range(2)]

    def body(q_ref, k_ref, v_ref, tri_ref, *rest):
        shard_refs, (o_ref, tot_ref), full_refs, sems = rest[:ns], rest[ns:ns + 2], rest[ns + 2:2 * ns + 2], rest[2 * ns + 2:]
        p, i = pl.program_id(0), pl.program_id(1)
        start, forward, finish = _gather_phases(shard_refs, full_refs, kinds, shapes, *sems)
        pl.when((p == 0) & (i == 0))(start)
        pl.when((p == nd // 2) & (i == 0))(forward)
        masks = _head_masks()
        lanes = [slice(lb * LANES, (lb + 1) * LANES) for lb in range(nlb)]
        q = q_ref[...].astype(F32) * scale
        qh = [jnp.where(masks[half], q[:, lanes[lb]], 0.0).astype(BF16) for lb, half in heads]
        after = lax.broadcasted_iota(jnp.int32, (blk, blk), 0) > lax.broadcasted_iota(jnp.int32, (blk, blk), 1)
        tri_v = tri_ref[...]

        def scores(j, which):
            j0 = pl.multiple_of(j * blk, blk)
            kj = k_ref[pl.ds(j0, blk), :]
            return tuple(_dot(qh[h], kj[:, lanes[heads[h][0]]], NT) for h in which)

        early = tuple(h for h in range(len(heads)) if h % 2 == 0)
        late = tuple(h for h in range(len(heads)) if h % 2 == 1)

        def both(first, second):
            out = [None] * len(heads)
            for h, z in zip(early + late, first + second):
                out[h] = z
            return tuple(out)

        def weights(z, c, diagonal, between=lambda: None):
            halves = [slice(r, r + blk // 2) for r in range(0, blk, blk // 2)]
            other = between()
            sums = []
            for h in range(len(heads)):
                parts = []
                for rows in halves:
                    lf = _log_terms(z[h][rows])[1]
                    if diagonal:
                        lf = jnp.where(after[rows], lf, 0.0)
                    parts.append(_dot(_split_cat(lf, 1), tri_v, NN))
                sums.append(jnp.concatenate(parts, axis=0))
            a = [jnp.exp(z[h] + sums[h] + c[h]) for h in range(len(heads))]
            if diagonal:
                a = [jnp.where(after, ah, 0.0) for ah in a]
            return tuple(ah.astype(BF16) for ah in a), tuple(c[h] + sums[h][:, 0:1] for h in range(len(heads))), other

        def apply(acc, a, j):
            j0 = pl.multiple_of(j * blk, blk)
            vj = v_ref[pl.ds(j0, blk), :]
            acc = list(acc)
            for h, (lb, half) in enumerate(heads):
                vh = vj[:, lanes[lb]]
                acc[lb] = acc[lb] + _dot(a[h], jnp.where(masks[half], vh, jnp.zeros_like(vh)), NN)
            return tuple(acc)

        zero = jnp.zeros((blk, 1), F32)
        a, c, _ = weights(both(scores(i, early), scores(i, late)), (zero,) * len(heads), True)

        def step(m, carry):
            z, a, c, acc = carry
            j = i - 1 - m
            a, c, (z_next, acc) = weights(both(z, scores(j, late)), c, False,
                                          lambda: (scores(jnp.maximum(j - 1, 0), early), apply(acc, a, j + 1)))
            return z_next, a, c, acc

        acc = (jnp.zeros((blk, LANES), F32),) * nlb
        _, a, c, acc = lax.fori_loop(0, i, step, (scores(jnp.maximum(i - 1, 0), early), a, c, acc))
        o_ref[...] = jnp.concatenate(apply(acc, a, 0), axis=1).astype(BF16)
        tot_ref[...] = jnp.concatenate([jnp.where(masks[0], c[2 * lb], c[2 * lb + 1]) for lb in range(nlb)], axis=1)
        pl.when((p == nd - 1) & (i == nq - 1))(finish)

    def whole(cb0):
        return pl.BlockSpec((t, width), functools.partial(lambda p, i, cb0: (0, cb0 + p), cb0=cb0))

    tile = pl.BlockSpec((blk, width), lambda p, i: (i, p))
    res = _pcall(
        body, name="attention_fwd", grid=(nd, nq),
        in_specs=[pl.BlockSpec((blk, width), lambda p, i: (i, 3 * nd + p)), whole(4 * nd), whole(5 * nd),
                  pl.BlockSpec((SUM_TERMS * blk, blk), lambda p, i: (0, 0))] + [HBM] * ns,
        out_specs=[tile, tile] + [HBM] * ns,
        out_shape=[jax.ShapeDtypeStruct((t, d), BF16), jax.ShapeDtypeStruct((t, d), F32)]
        + [jax.ShapeDtypeStruct(_full_shape(k, s), BF16) for k, s in zip(kinds, shapes)],
        scratch_shapes=_gather_scratch(ns),
        compiler_params=_params("arbitrary", "arbitrary"),
    )(hin, hin, hin, tri, *shards)
    return res[0], res[1], res[2:]


def _attention_bwd(hin, tot, do, d, parts):
    t, blk, nlb = hin.shape[0], ATT_BLOCK, min(ATT_BWD_LANE_BLOCKS, d // LANES)
    width = nlb * LANES
    nd = d // width
    nq = t // blk
    scale = 1.0 / math.sqrt(HEAD_DIM)
    upper = jnp.triu(jnp.ones((blk, blk), F32)).astype(BF16)
    upper = jnp.concatenate([upper] * SUM_TERMS, axis=1)
    lower = jnp.tril(jnp.ones((blk, blk), F32), -1).astype(BF16)
    npart = len(parts)
    heads = [(lb, half) for lb in range(nlb) for half in range(2)]
    nh = len(heads)

    def body(q_ref, k_ref, v_ref, tot_ref, do_ref, up_ref, low_ref, *rest):
        part_refs, (dq_ref, dk_ref, dv_ref) = rest[:npart], rest[npart:npart + 3]
        land_refs, (send_sems, recv_sems, dk_acc, dv_acc) = rest[npart + 3:2 * npart + 3], rest[2 * npart + 3:]
        p, i = pl.program_id(0), pl.program_id(1)
        start, finish = _exchange_phases(part_refs, land_refs, send_sems, recv_sems)
        pl.when((p == 0) & (i == 0))(start)
        masks = _head_masks()
        lanes = [slice(lb * LANES, (lb + 1) * LANES) for lb in range(nlb)]

        @pl.when(i == 0)
        def _():
            dk_acc[...] = jnp.zeros_like(dk_acc)
            dv_acc[...] = jnp.zeros_like(dv_acc)

        q = q_ref[...].astype(F32) * scale
        dout = do_ref[...]
        qh = [jnp.where(masks[half], q[:, lanes[lb]], 0.0).astype(BF16) for lb, half in heads]
        doh = [jnp.where(masks[half], dout[:, lanes[lb]], jnp.zeros((blk, LANES), BF16)) for lb, half in heads]
        lane = lax.broadcasted_iota(jnp.int32, (8, LANES), 1)
        totals = [lax.dot_general(jnp.where(lane == half * HEAD_DIM, 1.0, 0.0), tot_ref[:, lanes[lb]], (NT, ((), ())),
                                  precision=lax.Precision.HIGHEST, preferred_element_type=F32)[0:1, :] for lb, half in heads]
        after = lax.broadcasted_iota(jnp.int32, (blk, blk), 1) > lax.broadcasted_iota(jnp.int32, (blk, blk), 0)
        up, low = up_ref[...], low_ref[...]

        def products(j, which):
            j0 = pl.multiple_of(j * blk, blk)
            kj, vj = k_ref[pl.ds(j0, blk), :], v_ref[pl.ds(j0, blk), :]
            return tuple((_dot(kj[:, lanes[heads[h][0]]], qh[h], NT), _dot(vj[:, lanes[heads[h][0]]], doh[h], NT)) for h in which)

        early = tuple(h for h in range(nh) if h % 2 == 0)
        late = tuple(h for h in range(nh) if h % 2 == 1)

        def both(first, second):
            out = [None] * nh
            for h, v in zip(early + late, first + second):
                out[h] = v
            return tuple(out)

        def stage_sums(z, diagonal):
            lb, lf = _log_terms(z)
            if diagonal:
                lf = jnp.where(after, lf, 0.0)
            return jnp.exp(lb).astype(BF16), _dot(up, _split_cat(lf, 0), NN)

        def stage_weights(z, da, sums, rest, diagonal):
            rest = rest - sums[0:1, :]
            a = jnp.exp(z + sums + rest)
            if diagonal:
                a = jnp.where(after, a, 0.0)
            g = da * a
            return a.astype(BF16), g, _dot(low, g.astype(BF16), NN), rest

        def stage_dz(g, sig, earlier, before, diagonal):
            dlf = earlier + before
            dz = g - sig.astype(F32) * (g + dlf)
            if diagonal:
                dz = jnp.where(after, dz, 0.0)
            return dz.astype(BF16), dlf[blk - 1:blk, :] + g[blk - 1:blk, :]

        def grads(cur, rest, before, diagonal, between=(lambda: None, lambda: None)):
            sig, sums = zip(*[stage_sums(cur[h][0], diagonal) for h in range(nh)])
            first = between[0]()
            a, g, earlier, rest = zip(*[stage_weights(*cur[h], sums[h], rest[h], diagonal) for h in range(nh)])
            second = between[1]()
            dz, before = zip(*[stage_dz(g[h], sig[h], earlier[h], before[h], diagonal) for h in range(nh)])
            return dz, a, rest, before, first, second

        def flush(j, dz, a, dq):
            j0 = pl.multiple_of(j * blk, blk)
            kj = k_ref[pl.ds(j0, blk), :]
            dq = list(dq)
            dk_new = [jnp.zeros((blk, LANES), F32)] * nlb
            dv_new = [jnp.zeros((blk, LANES), F32)] * nlb
            for h, (lb, half) in enumerate(heads):
                kh = kj[:, lanes[lb]]
                dk_new[lb] = dk_new[lb] + _dot(dz[h], qh[h], NN)
                dv_new[lb] = dv_new[lb] + _dot(a[h], doh[h], NN)
                dq[lb] = dq[lb] + _dot(dz[h], jnp.where(masks[half], kh, jnp.zeros_like(kh)), TN)
            for lb in range(nlb):
                dk_acc[pl.ds(j0, blk), lanes[lb]] += dk_new[lb]
                dv_acc[pl.ds(j0, blk), lanes[lb]] += dv_new[lb]
            return tuple(dq)

        def step(j, carry):
            cur, dz, a, rest, before, dq = carry
            dz, a, rest, before, nxt, dq = grads(
                both(cur, products(j, late)), rest, before, False,
                (lambda: products(j + 1, early), lambda: flush(jnp.maximum(j - 1, 0), dz, a, dq)))
            return nxt, dz, a, rest, before, dq

        zero = jnp.zeros((1, blk), F32)
        nothing = (jnp.zeros((blk, blk), BF16),) * nh
        cur, dz, a, rest, before, dq = lax.fori_loop(
            0, i, step, (products(0, early), nothing, nothing, tuple(totals), (zero,) * nh, (jnp.zeros((blk, LANES), F32),) * nlb))
        dq = flush(jnp.maximum(i - 1, 0), dz, a, dq)
        dz, a = grads(both(cur, products(i, late)), rest, before, True)[:2]
        dq_ref[...] = (jnp.concatenate(flush(i, dz, a, dq), axis=1) * scale).astype(BF16)

        @pl.when(i == nq - 1)
        def _():
            dk_ref[...] = dk_acc[...].astype(BF16)
            dv_ref[...] = dv_acc[...].astype(BF16)

        pl.when((p == nd - 1) & (i == nq - 1))(finish)

    def whole(cb0):
        return pl.BlockSpec((t, width), functools.partial(lambda p, i, cb0: (0, cb0 + p), cb0=cb0))

    tile = pl.BlockSpec((blk, width), lambda p, i: (i, p))
    const = pl.BlockSpec((blk, blk), lambda p, i: (0, 0))
    act = jax.ShapeDtypeStruct((t, d), BF16)
    res = _pcall(
        body, name="attention_bwd", grid=(nd, nq),
        in_specs=[pl.BlockSpec((blk, width), lambda p, i: (i, 3 * nd + p)), whole(4 * nd), whole(5 * nd), tile, tile,
                  pl.BlockSpec((blk, SUM_TERMS * blk), lambda p, i: (0, 0)), const] + [HBM] * npart,
        out_specs=[tile, whole(0), whole(0)] + [HBM] * npart,
        out_shape=[act, act, act] + _exchange_shapes(parts),
        scratch_shapes=[pltpu.SemaphoreType.DMA((npart, 3)), pltpu.SemaphoreType.DMA((npart, 3)),
                        pltpu.VMEM((t, width), F32), pltpu.VMEM((t, width), F32)],
        compiler_params=_params("arbitrary", "arbitrary"),
    )(hin, hin, hin, tot, do, upper, lower, *parts)
    return res[0], res[1], res[2], res[3:]


def _adamw(w, g, m, v, *, name):
    r, c = w.shape
    tr = _tile(r, 256, 8) if r % 8 == 0 and r * c * 4 > (1 << 20) else r
    c1, c2 = 1.0 - ADAM_B1 ** ADAM_STEP, 1.0 - ADAM_B2 ** ADAM_STEP

    def body(w_ref, g_ref, m_ref, v_ref, d_ref, nm_ref, nv_ref):
        gv = g_ref[...]
        nm = ADAM_B1 * m_ref[...] + (1.0 - ADAM_B1) * gv
        nv = ADAM_B2 * v_ref[...] + (1.0 - ADAM_B2) * (gv * gv)
        d_ref[...] = -ADAM_LR * ((nm / c1) / (jnp.sqrt(nv / c2) + ADAM_EPS) + ADAM_WD * w_ref[...])
        nm_ref[...] = nm
        nv_ref[...] = nv

    spec = pl.BlockSpec((tr, c), lambda i: (i, 0))
    shape = jax.ShapeDtypeStruct((r, c), F32)
    return _pcall(
        body, name=name, grid=(r // tr,), in_specs=[spec] * 4, out_specs=[spec] * 3, out_shape=[shape] * 3,
        compiler_params=_params("arbitrary"),
    )(w, g, m, v)


HBM = pl.BlockSpec(memory_space=pltpu.HBM)


def _place():
    x, y, c = lax.axis_index("x"), lax.axis_index("y"), lax.axis_index("c")
    return x, y, c, [(1 - x, y), (x, 1 - y), (1 - x, 1 - y)]


def _window(ref, kind, shard_shape, j, half):
    r, w = shard_shape
    if kind == "col":
        return ref.at[pl.ds(half * (r // 2), r // 2), pl.ds(j * w, w)]
    return ref.at[pl.ds(j * r + half * (r // 2), r // 2), :]


def _full_shape(kind, shard_shape):
    r, w = shard_shape
    return (r, 4 * w) if kind == "col" else (4 * r, w)


def _remote(src, dst, send_sem, recv_sem, device):
    return pltpu.make_async_remote_copy(src_ref=src, dst_ref=dst, send_sem=send_sem, recv_sem=recv_sem,
                                        device_id=device, device_id_type=MESH)


def _gather_phases(big_in, big_out, kinds, shapes, send_sems, recv_sems, local_sems):
    x, y, c, chips = _place()
    me, sibling = 2 * x + y, (x, y, 1 - c)
    n = len(big_in)

    def local(a):
        r, w = shapes[a]
        own = big_out[a].at[:, pl.ds(me * w, w)] if kinds[a] == "col" else big_out[a].at[pl.ds(me * r, r), :]
        return pltpu.make_async_copy(big_in[a], own, local_sems.at[a])

    def over_ici(a, k):
        r = shapes[a][0]
        return _remote(big_in[a].at[pl.ds(c * (r // 2), r // 2), :], _window(big_out[a], kinds[a], shapes[a], me, c),
                       send_sems.at[a, k], recv_sems.at[a, k], (*chips[k], c))

    def landed(a, k, half, slot):
        win = _window(big_out[a], kinds[a], shapes[a], 2 * chips[k][0] + chips[k][1], half)
        return _remote(win, win, send_sems.at[a, slot], recv_sems.at[a, slot], sibling)

    def start():
        for a in range(n):
            local(a).start()
            for k in range(3):
                over_ici(a, k).start()

    def forward():
        for a in range(n):
            for k in range(3):
                landed(a, k, c, k).wait_recv()
                landed(a, k, c, 3 + k).start()

    def finish():
        for a in range(n):
            for k in range(3):
                landed(a, k, 1 - c, 3 + k).wait_recv()
        for a in range(n):
            for k in range(3):
                over_ici(a, k).wait_send()
                landed(a, k, c, 3 + k).wait_send()
            local(a).wait()

    return start, forward, finish


def _gather_scratch(n):
    return [pltpu.SemaphoreType.DMA((n, 6)), pltpu.SemaphoreType.DMA((n, 6)), pltpu.SemaphoreType.DMA((n,))]


def _gather_weights(bigs, kinds, smalls):
    nb, ns = len(bigs), len(smalls)
    shapes = [b.shape for b in bigs]

    def body(*refs):
        big_in, small_in = refs[:nb], refs[nb:nb + ns]
        big_out, small_out = refs[nb + ns:2 * nb + ns], refs[2 * nb + ns:2 * (nb + ns)]
        send_sems, recv_sems, local_sems, small_send, small_recv, small_local = refs[2 * (nb + ns):]
        x, y, c, chips = _place()
        me = 2 * x + y
        start, forward, finish = _gather_phases(big_in, big_out, kinds, shapes, send_sems, recv_sems, local_sems)
        start()
        local, sent = [], []
        for s in range(ns):
            local.append(pltpu.make_async_copy(small_in[s], small_out[s].at[me], small_local.at[s]))
            for k, chip in enumerate(chips):
                sent.append(_remote(small_in[s], small_out[s].at[me], small_send.at[s, k], small_recv.at[s, k], (*chip, c)))
        for cp in local + sent:
            cp.start()
        forward()
        for s in range(ns):
            for k, (px, py) in enumerate(chips):
                dst = small_out[s].at[2 * px + py]
                _remote(dst, dst, small_send.at[s, k], small_recv.at[s, k], (px, py, c)).wait_recv()
        finish()
        for cp in sent:
            cp.wait_send()
        for cp in local:
            cp.wait()

    out_shape = [jax.ShapeDtypeStruct(_full_shape(k, s), b.dtype) for b, k, s in zip(bigs, kinds, shapes)]
    out_shape += [jax.ShapeDtypeStruct((4, *s.shape), s.dtype) for s in smalls]
    return _pcall(
        body, name="gather_weights", in_specs=[HBM] * (nb + ns), out_specs=[HBM] * (nb + ns), out_shape=out_shape,
        scratch_shapes=_gather_scratch(nb) + [pltpu.SemaphoreType.DMA((ns, 3)), pltpu.SemaphoreType.DMA((ns, 3)), pltpu.SemaphoreType.DMA((ns,))],
    )(*bigs, *smalls)


def _swap_other_halves(grads, kinds, shapes, *, name):
    n = len(grads)

    def body(*refs):
        g_in, land = refs[:n], refs[n:2 * n]
        send_sems, recv_sems = refs[2 * n:]
        x, y, c, _ = _place()
        sibling = (x, y, 1 - c)
        sent = []
        for a in range(n):
            for j in range(4):
                cp = _remote(_window(g_in[a], kinds[a], shapes[a], j, 1 - c), land[a].at[j], send_sems.at[a, j], recv_sems.at[a, j], sibling)
                cp.start()
                sent.append(cp)
        for a in range(n):
            for j in range(4):
                _remote(land[a].at[j], land[a].at[j], send_sems.at[a, j], recv_sems.at[a, j], sibling).wait_recv()
        for cp in sent:
            cp.wait_send()

    return _pcall(
        body, name=name, in_specs=[HBM] * n, out_specs=[HBM] * n,
        out_shape=[jax.ShapeDtypeStruct((4, s[0] // 2, s[1]), F32) for s in shapes],
        scratch_shapes=[pltpu.SemaphoreType.DMA((n, 4)), pltpu.SemaphoreType.DMA((n, 4))],
    )(*grads)


def _pair_sum(place, g, land, kind, shard_shape, *, name):
    r, w = shard_shape
    hr = r // 2
    tr = _tile(hr, 256, 16)
    nr = hr // tr

    def body(place_ref, g_ref, l_ref, o_ref, own_ref):
        total = g_ref[...] + l_ref[...]
        o_ref[...] = total.astype(BF16)

        @pl.when(pl.program_id(1) == place_ref[1])
        def _():
            own_ref[...] = total

    if kind == "col":
        g_spec = pl.BlockSpec((tr, w), lambda i, j, p: (p[0] * nr + i, j))
    else:
        g_spec = pl.BlockSpec((tr, w), lambda i, j, p: ((2 * j + p[0]) * nr + i, 0))
    part = pl.BlockSpec((None, tr, w), lambda i, j, p: (j, i, 0))
    return _pcall(
        body, name=name,
        grid_spec=pltpu.PrefetchScalarGridSpec(num_scalar_prefetch=1, grid=(nr, 4), in_specs=[g_spec, part],
                                               out_specs=[part, pl.BlockSpec((tr, w), lambda i, j, p: (i, 0))]),
        out_shape=[jax.ShapeDtypeStruct((4, hr, w), BF16), jax.ShapeDtypeStruct((hr, w), F32)],
        compiler_params=_params("arbitrary", "arbitrary"),
    )(place, g, land)


def _exchange_phases(p_in, land, send_sems, recv_sems):
    x, y, c, chips = _place()

    def copy(a, k):
        px, py = chips[k]
        return _remote(p_in[a].at[2 * px + py], land[a].at[k], send_sems.at[a, k], recv_sems.at[a, k], (px, py, c))

    def start():
        for a in range(len(p_in)):
            for k in range(3):
                copy(a, k).start()

    def finish():
        for a in range(len(p_in)):
            for k in range(3):
                copy(a, k).wait()

    return start, finish


def _exchange_shapes(parts):
    return [jax.ShapeDtypeStruct((3, *p.shape[1:]), p.dtype) for p in parts]


def _chip_sum(place, own, land, *, name):
    hr, w = own.shape
    tr = _tile(hr, 256, 16)
    nr = hr // tr

    def body(place_ref, p_ref, l_ref, o_ref):
        o_ref[...] = ((p_ref[...] + l_ref[0].astype(F32)) + l_ref[1].astype(F32)) + l_ref[2].astype(F32)

    return _pcall(
        body, name=name,
        grid_spec=pltpu.PrefetchScalarGridSpec(
            num_scalar_prefetch=1, grid=(nr,),
            in_specs=[pl.BlockSpec((tr, w), lambda i, p: (i, 0)), pl.BlockSpec((3, tr, w), lambda i, p: (0, i, 0))],
            out_specs=pl.BlockSpec((tr, w), lambda i, p: (p[0] * nr + i, 0))),
        out_shape=jax.ShapeDtypeStruct((2 * hr, w), F32),
        compiler_params=_params("arbitrary"),
    )(place, own, land)


def _swap_reduced_halves(halves):
    n = len(halves)

    def body(*refs):
        src, out = refs[:n], refs[n:2 * n]
        send_sems, recv_sems = refs[2 * n:]
        x, y, c, _ = _place()
        sibling = (x, y, 1 - c)
        sent = []
        for a in range(n):
            hr = out[a].shape[0] // 2
            cp = _remote(src[a].at[pl.ds(c * hr, hr), :], out[a].at[pl.ds(c * hr, hr), :], send_sems.at[a], recv_sems.at[a], sibling)
            cp.start()
            sent.append(cp)
        for a in range(n):
            hr = out[a].shape[0] // 2
            other = out[a].at[pl.ds((1 - c) * hr, hr), :]
            _remote(other, other, send_sems.at[a], recv_sems.at[a], sibling).wait_recv()
        for cp in sent:
            cp.wait_send()

    return _pcall(
        body, name="swap_reduced_halves", in_specs=[HBM] * n, out_specs=[HBM] * n,
        out_shape=[jax.ShapeDtypeStruct(h.shape, F32) for h in halves],
        input_output_aliases={a: a for a in range(n)},
        scratch_shapes=[pltpu.SemaphoreType.DMA((n,)), pltpu.SemaphoreType.DMA((n,))],
    )(*halves)


def _sum_small(packed):
    rows = packed.shape[0]

    def body(in_ref, out_ref, land, send_sems, recv_sems, local_sem):
        x, y, c, _ = _place()
        me = 4 * x + 2 * y + c
        own = pltpu.make_async_copy(in_ref, land.at[me], local_sem)
        own.start()
        flips = [(r >> 2, (r >> 1) & 1, r & 1) for r in range(1, 8)]
        sent = []
        for k, (fx, fy, fc) in enumerate(flips):
            cp = _remote(in_ref, land.at[me], send_sems.at[k], recv_sems.at[k], (x ^ fx, y ^ fy, c ^ fc))
            cp.start()
            sent.append(cp)
        for k, (fx, fy, fc) in enumerate(flips):
            src = land.at[4 * (x ^ fx) + 2 * (y ^ fy) + (c ^ fc)]
            _remote(src, src, send_sems.at[k], recv_sems.at[k], (x ^ fx, y ^ fy, c ^ fc)).wait_recv()
        for cp in sent:
            cp.wait_send()
        own.wait()
        total = land[0]
        for dev in range(1, 8):
            total = total + land[dev]
        out_ref[...] = total

    vmem = pl.BlockSpec(memory_space=pltpu.VMEM)
    return _pcall(
        body, name="sum_small", in_specs=[vmem], out_specs=vmem, out_shape=jax.ShapeDtypeStruct(packed.shape, F32),
        scratch_shapes=[pltpu.VMEM((8, rows, LANES), F32), pltpu.SemaphoreType.DMA((7,)), pltpu.SemaphoreType.DMA((7,)), pltpu.SemaphoreType.DMA],
    )(packed)


def _sigmoid(v):
    return 1.0 / (1.0 + jnp.exp(-v))


def kernel(x, meta_tokens, g_pre_mix, w_in, conv_w_mix, w_proj_conv, w_proj_attn, b_gate, w_out, g_post_mix, g_pre_ffn, w_up_gate, conv_w_ffn, w_down, g_post_ffn, loss_target, m_meta_tokens, m_g_pre_mix, m_w_in, m_conv_w_mix, m_w_proj_conv, m_w_proj_attn, m_b_gate, m_w_out, m_g_post_mix, m_g_pre_ffn, m_w_up_gate, m_conv_w_ffn, m_w_down, m_g_post_ffn, v_meta_tokens, v_g_pre_mix, v_w_in, v_conv_w_mix, v_w_proj_conv, v_w_proj_attn, v_b_gate, v_w_out, v_g_post_mix, v_g_pre_ffn, v_w_up_gate, v_conv_w_ffn, v_w_down, v_g_post_ffn):
    seq, d = x.shape[1], x.shape[2]
    f = w_down.shape[1] * 4
    real = N_META + seq
    t = -(-real // ATT_BLOCK) * ATT_BLOCK
    nd = d // LANES
    cx, cy, cc = lax.axis_index("x"), lax.axis_index("y"), lax.axis_index("c")
    chip = 2 * cx + cy
    place = jnp.stack([cc, chip]).astype(jnp.int32)

    big_names = ["w_in", "w_proj_conv", "w_proj_attn", "w_out", "w_up_gate", "w_down"]
    kinds = ["col", "row", "row", "row", "col", "row"]
    big_w = [w_in[0], w_proj_conv[0], w_proj_attn[0], w_out[0], w_up_gate[0], w_down[0]]
    small_w = [meta_tokens, conv_w_mix[0], b_gate[0], conv_w_ffn[0]]
    shards = [w.astype(BF16) for w in big_w]
    shapes = [w.shape for w in big_w]
    gathered = _gather_weights(shards[:1], kinds[:1], small_w)
    wi = gathered[0]
    meta_f, cwm_f, bg_f, cwf_f = (jnp.moveaxis(s, 0, 1).reshape(s.shape[1], -1) for s in gathered[1:])

    h0 = jnp.concatenate([meta_f, x[0], jnp.zeros((t - real, d), F32)], axis=0)
    target = jnp.concatenate([jnp.zeros((N_META, d), F32), loss_target[0], jnp.zeros((t - real, d), F32)], axis=0)
    xn1 = _prenorm(h0, g_pre_mix, name="prenorm_mix")
    hin = _matmul(xn1, wi, name="in_proj")
    yconv = _mixer_conv_fwd(hin, cwm_f, d)
    o, att_tot, (wpc, wpa, wo, wug, wd) = _attention_fwd(hin, d, shards[1:], kinds[1:])

    def gates(gpre, bias):
        return _sigmoid(gpre[:, :d].astype(F32) + bias[0:1]), _sigmoid(gpre[:, d:].astype(F32) + bias[1:2])

    def merge_epi(accs, rows, vecs, row0):
        gate_c, gate_a = gates(rows[0], vecs[0])
        return [gate_c * accs[0] + gate_a * accs[1], accs[0], accs[1]]

    act = ("row", d, BF16)
    merged, bconv, battn = _matmul_rows([(yconv, wpc), (o, wpa)], [(hin, 2 * d, 3)], [bg_f], merge_epi, [act, act, act], name="branch_proj")

    def mix_epi(accs, rows, vecs, row0):
        mix = accs[0]
        h1 = rows[0] + mix * _rms(mix) * vecs[0]
        return [mix, h1, h1 * _rms(h1) * vecs[1]]

    mix, h1, xn2 = _matmul_rows([(merged, wo)], [(h0, d, 0)], [g_post_mix, g_pre_ffn], mix_epi,
                                [("row", d, F32), ("row", d, F32), act], name="out_proj")
    ug = _matmul(xn2, wug, name="up_proj")
    fact = _ffn_conv_fwd(ug, cwf_f, f)

    def loss_epi(accs, rows, vecs, row0):
        ffn = accs[0]
        r = _rms(ffn)
        fh = ffn * r
        h2 = rows[0] + fh * vecs[0]
        rid = row0 + lax.broadcasted_iota(jnp.int32, (ffn.shape[0], 1), 0)
        err = jnp.where((rid >= N_META) & (rid < real), h2 - rows[1], 0.0)
        dy = err * (1.0 / d)
        dffn, dg = _rms_bwd(dy, fh, r, vecs[0])
        loss = jnp.zeros((1, LANES), F32) + 0.5 * jnp.sum(err * err) / d
        return [dffn, dy, dg, loss]

    dffn, dh2, dg_post_ffn, loss_part = _matmul_rows(
        [(fact, wd)], [(h1, d, 0), (target, d, 0)], [g_post_ffn], loss_epi,
        [act, ("row", d, F32), ("acc", 1, d), ("acc", 1, LANES)], name="down_proj_loss")

    gw_down = _matmul_tn(fact, dffn, name="grad_w_down")
    df = _matmul(dffn, wd, nt=True, name="down_proj_bwd")
    du2, dg2, gcw_ffn = _ffn_conv_bwd(ug, df, cwf_f, f)

    def ffn_in_epi(accs, rows, vecs, row0):
        h1v, dyv, mixv = rows
        r3 = _rms(h1v)
        dh1n, dg3 = _rms_bwd(accs[0], h1v * r3, r3, vecs[0])
        dh1 = dyv + dh1n
        r2 = _rms(mixv)
        dmix, dgm = _rms_bwd(dh1, mixv * r2, r2, vecs[1])
        return [dh1, dmix, dg3, dgm]

    dh1, dmix, dg_pre_ffn, dg_post_mix, dug = _matmul_rows_joined(
        [du2, dg2], wug, [(h1, d, 0), (dh2, d, 0), (mix, d, 0)], [g_pre_ffn, g_post_mix], ffn_in_epi,
        [("row", d, F32), act, ("acc", 1, d), ("acc", 1, d)], tk=f // 2, name="up_proj_bwd")
    gw_up = _matmul_tn(xn2, dug, name="grad_w_up_gate")
    gw_out = _matmul_tn(merged, dmix, name="grad_w_out")

    def merge_bwd_epi(accs, rows, vecs, row0):
        dm = accs[0]
        gate_c, gate_a = gates(rows[0], vecs[0])
        dpre_c = dm * rows[1].astype(F32) * gate_c * (1.0 - gate_c)
        dpre_a = dm * rows[2].astype(F32) * gate_a * (1.0 - gate_a)
        dbias = jnp.concatenate([jnp.sum(dpre_c, axis=0, keepdims=True), jnp.sum(dpre_a, axis=0, keepdims=True)], axis=0)
        return [dm * gate_c, dm * gate_a, jnp.concatenate([dpre_c, dpre_a], axis=1), dbias]

    dbconv, dbattn, dgates, gb_gate = _matmul_rows(
        [(dmix, wo)], [(hin, 2 * d, 3), (bconv, d, 0), (battn, d, 0)], [bg_f], merge_bwd_epi,
        [act, act, ("row", 2 * d, BF16), ("acc", 2, d)], nt=True, name="out_proj_bwd")
    gw_pc = _matmul_tn(yconv, dbconv, name="grad_w_proj_conv")
    gw_pa = _matmul_tn(o, dbattn, name="grad_w_proj_attn")
    dyconv = _matmul(dbconv, wpc, nt=True, name="proj_conv_bwd")
    do = _matmul(dbattn, wpa, nt=True, name="proj_attn_bwd")
    db, dc, dhc, gcw_mix = _mixer_conv_bwd(hin, dyconv, cwm_f, d)

    def pair_sums(grads, first):
        landed = _swap_other_halves(grads, kinds[first:first + len(grads)], shapes[first:first + len(grads)],
                                    name=f"swap_other_halves_{big_names[first]}")
        return zip(*[_pair_sum(place, g, l, kinds[first + a], shapes[first + a], name=f"pair_sum_{big_names[first + a]}")
                     for a, (g, l) in enumerate(zip(grads, landed))])

    parts, own = pair_sums([gw_pc, gw_pa, gw_out, gw_up, gw_down], 1)
    dq, dk, dv, landed = _attention_bwd(hin, att_tot, do, d, parts)

    def in_epi(accs, rows, vecs, row0):
        h0v, dh1v = rows
        r1 = _rms(h0v)
        dh0n, dg1 = _rms_bwd(accs[0], h0v * r1, r1, vecs[0])
        return [dh1v + dh0n, dg1]

    dhin = [db, dc, dhc, dq, dk, dv, dgates]
    gw_in = _matmul_tn_segments(xn1, dhin, name="grad_w_in")
    parts_in, own_in = pair_sums([gw_in], 0)
    dh0, dg_pre_mix, landed_in = _matmul_rows_joined(
        dhin, wi, [(h0, d, 0), (dh1, d, 0)], [g_pre_mix], in_epi,
        [("row", d, F32), ("acc", 1, d)], tk=d, name="in_proj_bwd", joined=False, parts=parts_in)
    grad_x = dh0[N_META:real][None]

    landed = [landed_in] + list(landed)
    halves = [_chip_sum(place, p, l, name=f"chip_sum_{n}") for p, l, n in zip(list(own_in) + list(own), landed, big_names)]
    big_r = _swap_reduced_halves(halves)

    small_g = [dh0[:N_META], dg_pre_mix, gcw_mix, gb_gate, dg_post_mix, dg_pre_ffn, gcw_ffn, dg_post_ffn, loss_part]
    flat = jnp.concatenate([s.reshape(-1) for s in small_g])
    pad = -flat.shape[0] % (8 * LANES)
    summed = _sum_small(jnp.pad(flat, (0, pad)).reshape(-1, LANES)).reshape(-1)
    small_r, pos = [], 0
    for s in small_g:
        small_r.append(summed[pos:pos + s.size].reshape(s.shape))
        pos += s.size
    g_meta, g_g1, g_cwm, g_bg, g_g2, g_g3, g_cwf, g_g4, loss_row = small_r
    loss = loss_row[0, 0]

    def my_cols(full, width):
        return lax.dynamic_slice_in_dim(full, chip * width, width, axis=1)

    grads = {
        "meta_tokens": my_cols(g_meta, d // 4), "g_pre_mix": g_g1, "w_in": big_r[0][None],
        "conv_w_mix": my_cols(g_cwm, d // 4)[None], "w_proj_conv": big_r[1][None], "w_proj_attn": big_r[2][None],
        "b_gate": my_cols(g_bg, d // 4)[None], "w_out": big_r[3][None], "g_post_mix": g_g2, "g_pre_ffn": g_g3,
        "w_up_gate": big_r[4][None], "conv_w_ffn": my_cols(g_cwf, f // 4)[None], "w_down": big_r[5][None], "g_post_ffn": g_g4,
    }
    weights = dict(meta_tokens=meta_tokens, g_pre_mix=g_pre_mix, w_in=w_in, conv_w_mix=conv_w_mix, w_proj_conv=w_proj_conv,
                   w_proj_attn=w_proj_attn, b_gate=b_gate, w_out=w_out, g_post_mix=g_post_mix, g_pre_ffn=g_pre_ffn,
                   w_up_gate=w_up_gate, conv_w_ffn=conv_w_ffn, w_down=w_down, g_post_ffn=g_post_ffn)
    m_in = dict(meta_tokens=m_meta_tokens, g_pre_mix=m_g_pre_mix, w_in=m_w_in, conv_w_mix=m_conv_w_mix, w_proj_conv=m_w_proj_conv,
                w_proj_attn=m_w_proj_attn, b_gate=m_b_gate, w_out=m_w_out, g_post_mix=m_g_post_mix, g_pre_ffn=m_g_pre_ffn,
                w_up_gate=m_w_up_gate, conv_w_ffn=m_conv_w_ffn, w_down=m_w_down, g_post_ffn=m_g_post_ffn)
    v_in = dict(meta_tokens=v_meta_tokens, g_pre_mix=v_g_pre_mix, w_in=v_w_in, conv_w_mix=v_conv_w_mix, w_proj_conv=v_w_proj_conv,
                w_proj_attn=v_w_proj_attn, b_gate=v_b_gate, w_out=v_w_out, g_post_mix=v_g_post_mix, g_pre_ffn=v_g_pre_ffn,
                w_up_gate=v_w_up_gate, conv_w_ffn=v_conv_w_ffn, w_down=v_w_down, g_post_ffn=v_g_post_ffn)
    names = list(weights)
    deltas, new_m, new_v = [], [], []
    for n in names:
        shape = weights[n].shape
        two_d = (-1, shape[-1])
        dl, nm, nv = _adamw(weights[n].reshape(two_d), grads[n].reshape(two_d), m_in[n].reshape(two_d), v_in[n].reshape(two_d),
                            name=f"adamw_{n}")
        deltas.append(dl.reshape(shape))
        new_m.append(nm.reshape(shape))
        new_v.append(nv.reshape(shape))
    return (loss, grad_x, *[grads[n].reshape(weights[n].shape) for n in names], *deltas, *new_m, *new_v)
```

```python
import functools
import math

import jax
import jax.numpy as jnp
from jax import lax
from jax.experimental import pallas as pl
from jax.experimental.pallas import tpu as pltpu

F32 = jnp.float32
BF16 = jnp.bfloat16

N_META = 16
HEAD_DIM = 64
LANES = 128
RMS_EPS = 1e-6
ATT_BLOCK = 256
ATT_FWD_LANE_BLOCKS = 4
ATT_BWD_LANE_BLOCKS = 2
VMEM_LIMIT = 56 * 1024 * 1024

ADAM_LR = 0.001
ADAM_B1 = 0.9
ADAM_B2 = 0.999
ADAM_EPS = 1e-08
ADAM_WD = 0.01
ADAM_STEP = 10

MESH = pl.DeviceIdType.MESH


def _tile(n, cap, unit=LANES):
    d = (min(cap, n) // unit) * unit
    while d >= unit:
        if n % d == 0:
            return d
        d -= unit
    raise ValueError(f"no tile for {n} under {cap}")


MXU_EDGE = 256


def _mxu_tile(n, cap):
    if n <= cap:
        return n
    return _tile(n, cap, MXU_EDGE) if n % MXU_EDGE == 0 else _tile(n, cap)


def _params(*sem):
    return pltpu.CompilerParams(dimension_semantics=sem, vmem_limit_bytes=VMEM_LIMIT)


def _pcall(body, **kw):
    return pl.pallas_call(body, **kw)


def _dot(a, b, dims):
    return lax.dot_general(a, b, (dims, ((), ())), preferred_element_type=F32)


NN = ((1,), (0,))
NT = ((1,), (1,))
TN = ((0,), (0,))


def _matmul(a, b, *, nt=False, out_dtype=BF16, name):
    m, kdim = a.shape
    n = b.shape[0] if nt else b.shape[1]
    tm, tn, tk = _tile(m, 640, 16), _mxu_tile(n, 2816), _mxu_tile(kdim, 2816)
    nk = kdim // tk

    def body(a_ref, b_ref, o_ref, *scratch):
        p = _dot(a_ref[...], b_ref[...], NT if nt else NN)
        if nk == 1:
            o_ref[...] = p.astype(o_ref.dtype)
            return
        acc, k = scratch[0], pl.program_id(2)

        @pl.when(k == 0)
        def _():
            acc[...] = p

        @pl.when(k > 0)
        def _():
            acc[...] += p

        @pl.when(k == nk - 1)
        def _():
            o_ref[...] = acc[...].astype(o_ref.dtype)

    b_spec = pl.BlockSpec((tn, tk), lambda j, i, k: (j, k)) if nt else pl.BlockSpec((tk, tn), lambda j, i, k: (k, j))
    return _pcall(
        body, name=name, grid=(n // tn, m // tm, nk),
        in_specs=[pl.BlockSpec((tm, tk), lambda j, i, k: (i, k)), b_spec],
        out_specs=pl.BlockSpec((tm, tn), lambda j, i, k: (i, j)),
        out_shape=jax.ShapeDtypeStruct((m, n), out_dtype),
        scratch_shapes=[pltpu.VMEM((tm, tn), F32)] if nk > 1 else [],
        compiler_params=_params("arbitrary", "arbitrary", "arbitrary"),
    )(a, b)


def _matmul_tn(a, b, *, name):
    t, ka = a.shape
    nb = b.shape[1]
    tb = _mxu_tile(nb, 1024)
    budget = VMEM_LIMIT * 5 // 7
    ta = next(c for c in range(ka, 0, -LANES) if ka % c == 0 and 2 * (2 * t * (c + tb) + 4 * c * tb) <= budget)

    def body(a_ref, b_ref, o_ref):
        o_ref[...] = _dot(a_ref[...], b_ref[...], TN)

    return _pcall(
        body, name=name, grid=(ka // ta, nb // tb),
        in_specs=[pl.BlockSpec((t, ta), lambda i, j: (0, i)), pl.BlockSpec((t, tb), lambda i, j: (0, j))],
        out_specs=pl.BlockSpec((ta, tb), lambda i, j: (i, j)),
        out_shape=jax.ShapeDtypeStruct((ka, nb), F32),
        compiler_params=_params("arbitrary", "arbitrary"),
    )(a, b)


def _matmul_rows(pairs, rows, vecs, epi, outs, *, nt=False, name):
    m, kdim = pairs[0][0].shape
    n = pairs[0][1].shape[0] if nt else pairs[0][1].shape[1]
    tm, tk = _tile(m, 640, 16), _mxu_tile(kdim, 2816)
    nk, npair = kdim // tk, len(pairs)

    def body(*refs):
        a_refs, b_refs = refs[0:2 * npair:2], refs[1:2 * npair:2]
        pos = 2 * npair
        row_refs = refs[pos:pos + len(rows)]
        pos += len(rows)
        vec_refs = refs[pos:pos + len(vecs)]
        pos += len(vecs)
        out_refs = refs[pos:pos + len(outs)]
        accs = refs[pos + len(outs):]
        i, k = pl.program_id(0), pl.program_id(1)
        prods = [_dot(a[...], b[...], NT if nt else NN) for a, b in zip(a_refs, b_refs)]
        if nk > 1:
            @pl.when(k == 0)
            def _():
                for acc, p in zip(accs, prods):
                    acc[...] = p

            @pl.when(k > 0)
            def _():
                for acc, p in zip(accs, prods):
                    acc[...] += p

        @pl.when(k == nk - 1)
        def _():
            vals = [acc[...] for acc in accs] if nk > 1 else prods
            res = epi(vals, [r[...] for r in row_refs], [v[...] for v in vec_refs], i * tm)
            for o_ref, spec, val in zip(out_refs, outs, res):
                if spec[0] == "row":
                    o_ref[...] = val.astype(o_ref.dtype)
                else:
                    @pl.when(i == 0)
                    def _():
                        o_ref[...] = val

                    @pl.when(i > 0)
                    def _():
                        o_ref[...] += val

    in_specs, args = [], []
    for a, b in pairs:
        in_specs += [pl.BlockSpec((tm, tk), lambda i, k: (i, k)),
                     pl.BlockSpec((n, tk), lambda i, k: (0, k)) if nt else pl.BlockSpec((tk, n), lambda i, k: (k, 0))]
        args += [a, b]
    for arr, width, cb in rows:
        in_specs.append(pl.BlockSpec((tm, width), functools.partial(lambda i, k, cb: (i, cb), cb=cb)))
        args.append(arr)
    for v in vecs:
        in_specs.append(pl.BlockSpec(v.shape, lambda i, k: (0, 0)))
        args.append(v)
    out_specs, out_shape = [], []
    for spec in outs:
        if spec[0] == "row":
            out_specs.append(pl.BlockSpec((tm, spec[1]), lambda i, k: (i, 0)))
            out_shape.append(jax.ShapeDtypeStruct((m, spec[1]), spec[2]))
        else:
            out_specs.append(pl.BlockSpec((spec[1], spec[2]), lambda i, k: (0, 0)))
            out_shape.append(jax.ShapeDtypeStruct((spec[1], spec[2]), F32))
    return _pcall(
        body, name=name, grid=(m // tm, nk), in_specs=in_specs, out_specs=out_specs, out_shape=out_shape,
        scratch_shapes=[pltpu.VMEM((tm, n), F32) for _ in pairs] if nk > 1 else [],
        compiler_params=_params("arbitrary", "arbitrary"),
    )(*args)


def _matmul_rows_joined(segments, b, rows, vecs, epi, outs, *, tk, name, joined=True, parts=()):
    m = segments[0].shape[0]
    n, kdim = b.shape
    tm = _tile(m, 640, 16)
    nm, nk, npart = m // tm, kdim // tk, len(parts)
    first = [0]
    for s in segments:
        assert s.shape[1] % tk == 0
        first.append(first[-1] + s.shape[1] // tk)
    assert first[-1] == nk

    def body(*refs):
        seg_refs, b_ref = refs[:len(segments)], refs[len(segments)]
        pos = len(segments) + 1
        row_refs = refs[pos:pos + len(rows)]
        pos += len(rows)
        vec_refs = refs[pos:pos + len(vecs)]
        pos += len(vecs)
        part_refs = refs[pos:pos + npart]
        pos += npart
        out_refs = refs[pos:pos + len(outs)]
        pos += len(outs)
        joined_ref = refs[pos] if joined else None
        pos += int(joined)
        land_refs, acc = refs[pos:pos + npart], refs[pos + npart]
        i, k = pl.program_id(0), pl.program_id(1)
        if npart:
            start, finish = _exchange_phases(part_refs, land_refs, *refs[pos + npart + 1:])
            pl.when((i == 0) & (k == 0))(start)

        @pl.when(k == 0)
        def _():
            acc[...] = jnp.zeros_like(acc)

        for s, seg in enumerate(seg_refs):
            @pl.when((k >= first[s]) & (k < first[s + 1]))
            def _():
                if joined:
                    joined_ref[...] = seg[...]
                acc[...] += _dot(seg[...], b_ref[...], NT)

        @pl.when(k == nk - 1)
        def _():
            res = epi([acc[...]], [r[...] for r in row_refs], [v[...] for v in vec_refs], i * tm)
            for o_ref, spec, val in zip(out_refs, outs, res):
                if spec[0] == "row":
                    o_ref[...] = val.astype(o_ref.dtype)
                else:
                    @pl.when(i == 0)
                    def _():
                        o_ref[...] = val

                    @pl.when(i > 0)
                    def _():
                        o_ref[...] += val

        if npart:
            pl.when((i == nm - 1) & (k == nk - 1))(finish)

    def seg_spec(s):
        last = first[s + 1] - first[s] - 1
        return pl.BlockSpec((tm, tk), lambda i, k: (i, jnp.clip(k - first[s], 0, last)))

    in_specs = [seg_spec(s) for s in range(len(segments))] + [pl.BlockSpec((n, tk), lambda i, k: (0, k))]
    args = [*segments, b]
    for arr, width, cb in rows:
        in_specs.append(pl.BlockSpec((tm, width), functools.partial(lambda i, k, cb: (i, cb), cb=cb)))
        args.append(arr)
    for v in vecs:
        in_specs.append(pl.BlockSpec(v.shape, lambda i, k: (0, 0)))
        args.append(v)
    in_specs += [HBM] * npart
    args += list(parts)
    out_specs, out_shape = [], []
    for spec in outs:
        if spec[0] == "row":
            out_specs.append(pl.BlockSpec((tm, spec[1]), lambda i, k: (i, 0)))
            out_shape.append(jax.ShapeDtypeStruct((m, spec[1]), spec[2]))
        else:
            out_specs.append(pl.BlockSpec((spec[1], spec[2]), lambda i, k: (0, 0)))
            out_shape.append(jax.ShapeDtypeStruct((spec[1], spec[2]), F32))
    if joined:
        out_specs.append(pl.BlockSpec((tm, tk), lambda i, k: (i, k)))
        out_shape.append(jax.ShapeDtypeStruct((m, kdim), segments[0].dtype))
    out_specs += [HBM] * npart
    out_shape += _exchange_shapes(parts)
    sems = [pltpu.SemaphoreType.DMA((npart, 3)), pltpu.SemaphoreType.DMA((npart, 3))] if npart else []
    return _pcall(
        body, name=name, grid=(nm, nk), in_specs=in_specs, out_specs=out_specs, out_shape=out_shape,
        scratch_shapes=[pltpu.VMEM((tm, n), F32)] + sems,
        compiler_params=_params("arbitrary", "arbitrary"),
    )(*args)


def _matmul_tn_segments(a, segments, *, name):
    t, ka = a.shape
    tb = MXU_EDGE
    first = [0]
    for s in segments:
        assert s.shape[1] % tb == 0
        first.append(first[-1] + s.shape[1] // tb)
    budget = VMEM_LIMIT * 13 // 14
    ta = next(c for c in range(ka, 0, -LANES) if ka % c == 0 and 2 * (2 * t * (c + len(segments) * tb) + 4 * c * tb) <= budget)

    def body(a_ref, *refs):
        seg_refs, o_ref = refs[:-1], refs[-1]
        j = pl.program_id(1)
        for s, seg in enumerate(seg_refs):
            @pl.when((j >= first[s]) & (j < first[s + 1]))
            def _():
                o_ref[...] = _dot(a_ref[...], seg[...], TN)

    def seg_spec(s):
        last = first[s + 1] - first[s] - 1
        return pl.BlockSpec((t, tb), lambda i, j: (0, jnp.clip(j - first[s], 0, last)))

    return _pcall(
        body, name=name, grid=(ka // ta, first[-1]),
        in_specs=[pl.BlockSpec((t, ta), lambda i, j: (0, i))] + [seg_spec(s) for s in range(len(segments))],
        out_specs=pl.BlockSpec((ta, tb), lambda i, j: (i, j)),
        out_shape=jax.ShapeDtypeStruct((ka, first[-1] * tb), F32),
        compiler_params=_params("arbitrary", "arbitrary"),
    )(a, *segments)


def _rms(v):
    return lax.rsqrt(jnp.mean(v * v, axis=-1, keepdims=True) + RMS_EPS)


def _rms_bwd(dz, vhat, r, g):
    t = dz * g
    dv = r * (t - vhat * jnp.mean(t * vhat, axis=-1, keepdims=True))
    return dv, jnp.sum(dz * vhat, axis=0, keepdims=True)


def _prenorm(h, g, *, name):
    m, d = h.shape
    tm = _tile(m, 640, 16)

    def body(h_ref, g_ref, o_ref):
        v = h_ref[...]
        o_ref[...] = (v * _rms(v) * g_ref[...]).astype(BF16)

    return _pcall(
        body, name=name, grid=(m // tm,),
        in_specs=[pl.BlockSpec((tm, d), lambda i: (i, 0)), pl.BlockSpec((1, d), lambda i: (0, 0))],
        out_specs=pl.BlockSpec((tm, d), lambda i: (i, 0)),
        out_shape=jax.ShapeDtypeStruct((m, d), BF16),
        compiler_params=_params("arbitrary"),
    )(h, g)


SUBLANES = 8


def _shift_down(u, k):
    r = pltpu.roll(u, k, 0)
    rows = lax.broadcasted_iota(jnp.int32, (SUBLANES, u.shape[1]), 0)
    return jnp.concatenate([jnp.where(rows >= k, r[:SUBLANES], 0.0), r[SUBLANES:]], axis=0)


def _shift_up(u, k):
    n = u.shape[0]
    r = pltpu.roll(u, n - k, 0)
    rows = lax.broadcasted_iota(jnp.int32, (SUBLANES, u.shape[1]), 0)
    return jnp.concatenate([r[:n - SUBLANES], jnp.where(rows < SUBLANES - k, r[n - SUBLANES:], 0.0)], axis=0)


def _taps(u):
    return _shift_down(u, 2), _shift_down(u, 1), u


def _conv(taps, w):
    return w[0:1] * taps[0] + w[1:2] * taps[1] + w[2:3] * taps[2]


def _conv_bwd(dcu, taps, w):
    du = w[2:3] * dcu + w[1:2] * _shift_up(dcu, 1) + w[0:1] * _shift_up(dcu, 2)
    return du, [jnp.sum(dcu * tap, axis=0, keepdims=True) for tap in taps]


def _strip(arr, t, cb0):
    return pl.BlockSpec((t, LANES), functools.partial(lambda s, cb0: (0, cb0 + s), cb0=cb0))


def _mixer_conv_fwd(hin, w, d):
    t, ns = hin.shape[0], d // LANES

    def body(b_ref, c_ref, h_ref, w_ref, y_ref):
        u = c_ref[...].astype(F32) * h_ref[...].astype(F32)
        y_ref[...] = (b_ref[...].astype(F32) * _conv(_taps(u), w_ref[...])).astype(BF16)

    return _pcall(
        body, name="mixer_conv_fwd", grid=(ns,),
        in_specs=[_strip(hin, t, 0), _strip(hin, t, ns), _strip(hin, t, 2 * ns), pl.BlockSpec((3, LANES), lambda s: (0, s))],
        out_specs=pl.BlockSpec((t, LANES), lambda s: (0, s)),
        out_shape=jax.ShapeDtypeStruct((t, d), BF16),
        compiler_params=_params("arbitrary"),
    )(hin, hin, hin, w)


def _mixer_conv_bwd(hin, dy, w, d):
    t, ns = hin.shape[0], d // LANES

    def body(b_ref, c_ref, h_ref, dy_ref, w_ref, db_ref, dc_ref, dh_ref, dw_ref):
        b, c, h, g = (r[...].astype(F32) for r in (b_ref, c_ref, h_ref, dy_ref))
        wv = w_ref[...]
        taps = _taps(c * h)
        db_ref[...] = (g * _conv(taps, wv)).astype(BF16)
        du, dw = _conv_bwd(g * b, taps, wv)
        dc_ref[...] = (du * h).astype(BF16)
        dh_ref[...] = (du * c).astype(BF16)
        for k in range(3):
            dw_ref[k:k + 1, :] = dw[k]

    col = pl.BlockSpec((t, LANES), lambda s: (0, s))
    act = jax.ShapeDtypeStruct((t, d), BF16)
    return _pcall(
        body, name="mixer_conv_bwd", grid=(ns,),
        in_specs=[_strip(hin, t, 0), _strip(hin, t, ns), _strip(hin, t, 2 * ns), col, pl.BlockSpec((3, LANES), lambda s: (0, s))],
        out_specs=[col, col, col, pl.BlockSpec((3, LANES), lambda s: (0, s))],
        out_shape=[act, act, act, jax.ShapeDtypeStruct((3, d), F32)],
        compiler_params=_params("arbitrary"),
    )(hin, hin, hin, dy, w)


GELU_C = math.sqrt(2.0 / math.pi)
GELU_A = 0.044715


def _gelu_tanh(x):
    return jnp.tanh(GELU_C * (x + GELU_A * x * x * x))


def _ffn_conv_fwd(ug, w, f):
    t, ns = ug.shape[0], f // LANES

    def body(u_ref, g_ref, w_ref, o_ref):
        cu = _conv(_taps(u_ref[...].astype(F32)), w_ref[...])
        o_ref[...] = (0.5 * cu * (1.0 + _gelu_tanh(cu)) * g_ref[...].astype(F32)).astype(BF16)

    return _pcall(
        body, name="ffn_conv_fwd", grid=(ns,),
        in_specs=[_strip(ug, t, 0), _strip(ug, t, ns), pl.BlockSpec((3, LANES), lambda s: (0, s))],
        out_specs=pl.BlockSpec((t, LANES), lambda s: (0, s)),
        out_shape=jax.ShapeDtypeStruct((t, f), BF16),
        compiler_params=_params("arbitrary"),
    )(ug, ug, w)


def _ffn_conv_bwd(ug, df, w, f):
    t, ns = ug.shape[0], f // LANES

    def body(u_ref, g_ref, df_ref, w_ref, du_ref, dg_ref, dw_ref):
        u, g, d = (r[...].astype(F32) for r in (u_ref, g_ref, df_ref))
        wv = w_ref[...]
        taps = _taps(u)
        cu = _conv(taps, wv)
        th = _gelu_tanh(cu)
        half = 0.5 * (1.0 + th)
        dg_ref[...] = (d * (cu * half)).astype(BF16)
        dgelu = half + (0.5 * GELU_C) * cu * (1.0 - th * th) * (1.0 + (3.0 * GELU_A) * (cu * cu))
        du, dw = _conv_bwd(d * g * dgelu, taps, wv)
        du_ref[...] = du.astype(BF16)
        for k in range(3):
            dw_ref[k:k + 1, :] = dw[k]

    col = pl.BlockSpec((t, LANES), lambda s: (0, s))
    act = jax.ShapeDtypeStruct((t, f), BF16)
    return _pcall(
        body, name="ffn_conv_bwd", grid=(ns,),
        in_specs=[_strip(ug, t, 0), _strip(ug, t, ns), col, pl.BlockSpec((3, LANES), lambda s: (0, s))],
        out_specs=[col, col, pl.BlockSpec((3, LANES), lambda s: (0, s))],
        out_shape=[act, act, jax.ShapeDtypeStruct((3, f), F32)],
        compiler_params=_params("arbitrary"),
    )(ug, ug, df, w)


def _log_terms(z):
    minus_abs = lax.bitcast_convert_type(lax.bitcast_convert_type(z, jnp.uint32) | jnp.uint32(0x80000000), F32)
    lb = jnp.minimum(z, 0.0) - jnp.log(1.0 + jnp.exp(minus_abs))
    return lb, lb - z


SUM_TERMS = 1


def _split_cat(v, axis):
    terms = [v.astype(BF16)]
    for _ in range(SUM_TERMS - 1):
        v = v - terms[-1].astype(F32)
        terms.append(v.astype(BF16))
    return terms[0] if SUM_TERMS == 1 else jnp.concatenate(terms, axis=axis)


def _head_masks():
    lane = lax.broadcasted_iota(jnp.int32, (1, LANES), 1)
    return lane < HEAD_DIM, lane >= HEAD_DIM


def _attention_fwd(hin, d, shards, kinds):
    t, blk, nlb = hin.shape[0], ATT_BLOCK, min(ATT_FWD_LANE_BLOCKS, d // LANES)
    width = nlb * LANES
    nd = d // width
    nq = t // blk
    scale = 1.0 / math.sqrt(HEAD_DIM)
    tri = jnp.tril(jnp.ones((blk, blk), F32)).astype(BF16)
    tri = jnp.concatenate([tri] * SUM_TERMS, axis=0)
    ns = len(shards)
    shapes = [s.shape for s in shards]
    heads = [(lb, half) for lb in range(nlb) for half in range(2)]

    def body(q_ref, k_ref, v_ref, tri_ref, *rest):
        shard_refs, (o_ref, tot_ref), full_refs, sems = rest[:ns], rest[ns:ns + 2], rest[ns + 2:2 * ns + 2], rest[2 * ns + 2:]
        p, i = pl.program_id(0), pl.program_id(1)
        start, forward, finish = _gather_phases(shard_refs, full_refs, kinds, shapes, *sems)
        pl.when((p == 0) & (i == 0))(start)
        pl.when((p == nd // 2) & (i == 0))(forward)
        masks = _head_masks()
        lanes = [slice(lb * LANES, (lb + 1) * LANES) for lb in range(nlb)]
        q = q_ref[...].astype(F32) * scale
        qh = [jnp.where(masks[half], q[:, lanes[lb]], 0.0).astype(BF16) for lb, half in heads]
        after = lax.broadcasted_iota(jnp.int32, (blk, blk), 0) > lax.broadcasted_iota(jnp.int32, (blk, blk), 1)
        tri_v = tri_ref[...]

        def scores(j, which):
            j0 = pl.multiple_of(j * blk, blk)
            kj = k_ref[pl.ds(j0, blk), :]
            return tuple(_dot(qh[h], kj[:, lanes[heads[h][0]]], NT) for h in which)

        early = tuple(h for h in range(len(heads)) if h % 2 == 0)
        late = tuple(h for h in range(len(heads)) if h % 2 == 1)

        def both(first, second):
            out = [None] * len(heads)
            for h, z in zip(early + late, first + second):
                out[h] = z
            return tuple(out)

        def weights(z, c, diagonal, between=lambda: None):
            halves = [slice(r, r + blk // 2) for r in range(0, blk, blk // 2)]
            other = between()
            sums, a = {}, {}

            def stage_sums(h):
                parts = []
                for rows in halves:
                    lf = _log_terms(z[h][rows])[1]
                    if diagonal:
                        lf = jnp.where(after[rows], lf, 0.0)
                    parts.append(_dot(_split_cat(lf, 1), tri_v, NN))
                sums[h] = jnp.concatenate(parts, axis=0)

            def stage_weights(h):
                ah = jnp.exp(z[h] + sums[h] + c[h])
                a[h] = (jnp.where(after, ah, 0.0) if diagonal else ah).astype(BF16)

            order = range(len(heads))
            for h in order:
                stage_sums(h)
            for h in order:
                stage_weights(h)
            return tuple(a[h] for h in order), tuple(c[h] + sums[h][:, 0:1] for h in order), other

        def apply(acc, a, j):
            j0 = pl.multiple_of(j * blk, blk)
            vj = v_ref[pl.ds(j0, blk), :]
            acc = list(acc)
            for h, (lb, half) in enumerate(heads):
                vh = vj[:, lanes[lb]]
                acc[lb] = acc[lb] + _dot(a[h], jnp.where(masks[half], vh, jnp.zeros_like(vh)), NN)
            return tuple(acc)

        zero = jnp.zeros((blk, 1), F32)
        a, c, _ = weights(both(scores(i, early), scores(i, late)), (zero,) * len(heads), True)

        def step(m, carry):
            z, a, c, acc = carry
            j = i - 1 - m
            a, c, (z_next, acc) = weights(both(z, scores(j, late)), c, False,
                                          lambda: (scores(jnp.maximum(j - 1, 0), early), apply(acc, a, j + 1)))
            return z_next, a, c, acc

        acc = (jnp.zeros((blk, LANES), F32),) * nlb
        _, a, c, acc = lax.fori_loop(0, i, step, (scores(jnp.maximum(i - 1, 0), early), a, c, acc))
        o_ref[...] = jnp.concatenate(apply(acc, a, 0), axis=1).astype(BF16)
        tot_ref[...] = jnp.concatenate([jnp.where(masks[0], c[2 * lb], c[2 * lb + 1]) for lb in range(nlb)], axis=1)
        pl.when((p == nd - 1) & (i == nq - 1))(finish)

    def whole(cb0):
        return pl.BlockSpec((t, width), functools.partial(lambda p, i, cb0: (0, cb0 + p), cb0=cb0))

    tile = pl.BlockSpec((blk, width), lambda p, i: (i, p))
    res = _pcall(
        body, name="attention_fwd", grid=(nd, nq),
        in_specs=[pl.BlockSpec((blk, width), lambda p, i: (i, 3 * nd + p)), whole(4 * nd), whole(5 * nd),
                  pl.BlockSpec((SUM_TERMS * blk, blk), lambda p, i: (0, 0))] + [HBM] * ns,
        out_specs=[tile, tile] + [HBM] * ns,
        out_shape=[jax.ShapeDtypeStruct((t, d), BF16), jax.ShapeDtypeStruct((t, d), F32)]
        + [jax.ShapeDtypeStruct(_full_shape(k, s), BF16) for k, s in zip(kinds, shapes)],
        scratch_shapes=_gather_scratch(ns),
        compiler_params=_params("arbitrary", "arbitrary"),
    )(hin, hin, hin, tri, *shards)
    return res[0], res[1], res[2:]


def _attention_bwd(hin, tot, do, d, parts):
    t, blk, nlb = hin.shape[0], ATT_BLOCK, min(ATT_BWD_LANE_BLOCKS, d // LANES)
    width = nlb * LANES
    nd = d // width
    nq = t // blk
    scale = 1.0 / math.sqrt(HEAD_DIM)
    upper = jnp.triu(jnp.ones((blk, blk), F32)).astype(BF16)
    upper = jnp.concatenate([upper] * SUM_TERMS, axis=1)
    lower = jnp.tril(jnp.ones((blk, blk), F32), -1).astype(BF16)
    npart = len(parts)
    heads = [(lb, half) for lb in range(nlb) for half in range(2)]
    nh = len(heads)

    def body(q_ref, k_ref, v_ref, tot_ref, do_ref, up_ref, low_ref, *rest):
        part_refs, (dq_ref, dk_ref, dv_ref) = rest[:npart], rest[npart:npart + 3]
        land_refs, (send_sems, recv_sems, dk_acc, dv_acc) = rest[npart + 3:2 * npart + 3], rest[2 * npart + 3:]
        p, i = pl.program_id(0), pl.program_id(1)
        start, finish = _exchange_phases(part_refs, land_refs, send_sems, recv_sems)
        pl.when((p == 0) & (i == 0))(start)
        masks = _head_masks()
        lanes = [slice(lb * LANES, (lb + 1) * LANES) for lb in range(nlb)]

        @pl.when(i == 0)
        def _():
            dk_acc[...] = jnp.zeros_like(dk_acc)
            dv_acc[...] = jnp.zeros_like(dv_acc)

        q = q_ref[...].astype(F32) * scale
        dout = do_ref[...]
        qh = [jnp.where(masks[half], q[:, lanes[lb]], 0.0).astype(BF16) for lb, half in heads]
        doh = [jnp.where(masks[half], dout[:, lanes[lb]], jnp.zeros((blk, LANES), BF16)) for lb, half in heads]
        lane = lax.broadcasted_iota(jnp.int32, (8, LANES), 1)
        totals = [lax.dot_general(jnp.where(lane == half * HEAD_DIM, 1.0, 0.0), tot_ref[:, lanes[lb]], (NT, ((), ())),
                                  precision=lax.Precision.HIGHEST, preferred_element_type=F32)[0:1, :] for lb, half in heads]
        after = lax.broadcasted_iota(jnp.int32, (blk, blk), 1) > lax.broadcasted_iota(jnp.int32, (blk, blk), 0)
        up, low = up_ref[...], low_ref[...]

        def products(j, which):
            j0 = pl.multiple_of(j * blk, blk)
            kj, vj = k_ref[pl.ds(j0, blk), :], v_ref[pl.ds(j0, blk), :]
            return tuple((_dot(kj[:, lanes[heads[h][0]]], qh[h], NT), _dot(vj[:, lanes[heads[h][0]]], doh[h], NT)) for h in which)

        early = tuple(h for h in range(nh) if h < 1)
        late = tuple(h for h in range(nh) if h >= 1)

        def both(first, second):
            out = [None] * nh
            for h, v in zip(early + late, first + second):
                out[h] = v
            return tuple(out)

        def stage_sums(z, diagonal):
            lb, lf = _log_terms(z)
            if diagonal:
                lf = jnp.where(after, lf, 0.0)
            return jnp.exp(lb).astype(BF16), _dot(up, _split_cat(lf, 0), NN)

        def stage_weights(z, da, sums, rest, diagonal):
            rest = rest - sums[0:1, :]
            a = jnp.exp(z + sums + rest)
            if diagonal:
                a = jnp.where(after, a, 0.0)
            g = da * a
            return a.astype(BF16), g, _dot(low, g.astype(BF16), NN), rest

        def stage_dz(g, sig, earlier, before, diagonal):
            dlf = earlier + before
            dz = g - sig.astype(F32) * (g + dlf)
            if diagonal:
                dz = jnp.where(after, dz, 0.0)
            return dz.astype(BF16), dlf[blk - 1:blk, :] + g[blk - 1:blk, :]

        def grads(cur, rest, before, diagonal, between=(lambda: None, lambda: None)):
            s1, s2, s3 = {}, {}, {}
            for h in early:
                s1[h] = stage_sums(cur[h][0], diagonal)
            first = between[0]()
            for h in early:
                s2[h] = stage_weights(*cur[h], s1[h][1], rest[h], diagonal)
            for h in late:
                s1[h] = stage_sums(cur[h][0], diagonal)
            second = between[1]()
            for h in early:
                s3[h] = stage_dz(s2[h][1], s1[h][0], s2[h][2], before[h], diagonal)
            for h in late:
                s2[h] = stage_weights(*cur[h], s1[h][1], rest[h], diagonal)
            for h in late:
                s3[h] = stage_dz(s2[h][1], s1[h][0], s2[h][2], before[h], diagonal)
            order = range(nh)
            return (tuple(s3[h][0] for h in order), tuple(s2[h][0] for h in order), tuple(s2[h][3] for h in order),
                    tuple(s3[h][1] for h in order), first, second)

        def flush(j, dz, a, dq):
            j0 = pl.multiple_of(j * blk, blk)
            kj = k_ref[pl.ds(j0, blk), :]
            dq = list(dq)
            dk_new = [jnp.zeros((blk, LANES), F32)] * nlb
            dv_new = [jnp.zeros((blk, LANES), F32)] * nlb
            for h, (lb, half) in enumerate(heads):
                kh = kj[:, lanes[lb]]
                dk_new[lb] = dk_new[lb] + _dot(dz[h], qh[h], NN)
                dv_new[lb] = dv_new[lb] + _dot(a[h], doh[h], NN)
                dq[lb] = dq[lb] + _dot(dz[h], jnp.where(masks[half], kh, jnp.zeros_like(kh)), TN)
            for lb in range(nlb):
                dk_acc[pl.ds(j0, blk), lanes[lb]] += dk_new[lb]
                dv_acc[pl.ds(j0, blk), lanes[lb]] += dv_new[lb]
            return tuple(dq)

        def step(j, carry):
            cur, dz, a, rest, before, dq = carry
            dz, a, rest, before, nxt, dq = grads(
                both(cur, products(j, late)), rest, before, False,
                (lambda: products(j + 1, early), lambda: flush(jnp.maximum(j - 1, 0), dz, a, dq)))
            return nxt, dz, a, rest, before, dq

        zero = jnp.zeros((1, blk), F32)
        nothing = (jnp.zeros((blk, blk), BF16),) * nh
        cur, dz, a, rest, before, dq = lax.fori_loop(
            0, i, step, (products(0, early), nothing, nothing, tuple(totals), (zero,) * nh, (jnp.zeros((blk, LANES), F32),) * nlb))
        dq = flush(jnp.maximum(i - 1, 0), dz, a, dq)
        dz, a = grads(both(cur, products(i, late)), rest, before, True)[:2]
        dq_ref[...] = (jnp.concatenate(flush(i, dz, a, dq), axis=1) * scale).astype(BF16)

        @pl.when(i == nq - 1)
        def _():
            dk_ref[...] = dk_acc[...].astype(BF16)
            dv_ref[...] = dv_acc[...].astype(BF16)

        pl.when((p == nd - 1) & (i == nq - 1))(finish)

    def whole(cb0):
        return pl.BlockSpec((t, width), functools.partial(lambda p, i, cb0: (0, cb0 + p), cb0=cb0))

    tile = pl.BlockSpec((blk, width), lambda p, i: (i, p))
    const = pl.BlockSpec((blk, blk), lambda p, i: (0, 0))
    act = jax.ShapeDtypeStruct((t, d), BF16)
    res = _pcall(
        body, name="attention_bwd", grid=(nd, nq),
        in_specs=[pl.BlockSpec((blk, width), lambda p, i: (i, 3 * nd + p)), whole(4 * nd), whole(5 * nd), tile, tile,
                  pl.BlockSpec((blk, SUM_TERMS * blk), lambda p, i: (0, 0)), const] + [HBM] * npart,
        out_specs=[tile, whole(0), whole(0)] + [HBM] * npart,
        out_shape=[act, act, act] + _exchange_shapes(parts),
        scratch_shapes=[pltpu.SemaphoreType.DMA((npart, 3)), pltpu.SemaphoreType.DMA((npart, 3)),
                        pltpu.VMEM((t, width), F32), pltpu.VMEM((t, width), F32)],
        compiler_params=_params("arbitrary", "arbitrary"),
    )(hin, hin, hin, tot, do, upper, lower, *parts)
    return res[0], res[1], res[2], res[3:]


def _adamw(w, g, m, v, *, name):
    r, c = w.shape
    tr = _tile(r, 256, 8) if r % 8 == 0 and r * c * 4 > (1 << 20) else r
    c1, c2 = 1.0 - ADAM_B1 ** ADAM_STEP, 1.0 - ADAM_B2 ** ADAM_STEP

    def body(w_ref, g_ref, m_ref, v_ref, d_ref, nm_ref, nv_ref):
        gv = g_ref[...]
        nm = ADAM_B1 * m_ref[...] + (1.0 - ADAM_B1) * gv
        nv = ADAM_B2 * v_ref[...] + (1.0 - ADAM_B2) * (gv * gv)
        d_ref[...] = -ADAM_LR * ((nm / c1) / (jnp.sqrt(nv / c2) + ADAM_EPS) + ADAM_WD * w_ref[...])
        nm_ref[...] = nm
        nv_ref[...] = nv

    spec = pl.BlockSpec((tr, c), lambda i: (i, 0))
    shape = jax.ShapeDtypeStruct((r, c), F32)
    return _pcall(
        body, name=name, grid=(r // tr,), in_specs=[spec] * 4, out_specs=[spec] * 3, out_shape=[shape] * 3,
        compiler_params=_params("arbitrary"),
    )(w, g, m, v)


HBM = pl.BlockSpec(memory_space=pltpu.HBM)


def _place():
    x, y, c = lax.axis_index("x"), lax.axis_index("y"), lax.axis_index("c")
    return x, y, c, [(1 - x, y), (x, 1 - y), (1 - x, 1 - y)]


def _window(ref, kind, shard_shape, j, half):
    r, w = shard_shape
    if kind == "col":
        return ref.at[pl.ds(half * (r // 2), r // 2), pl.ds(j * w, w)]
    return ref.at[pl.ds(j * r + half * (r // 2), r // 2), :]


def _full_shape(kind, shard_shape):
    r, w = shard_shape
    return (r, 4 * w) if kind == "col" else (4 * r, w)


def _remote(src, dst, send_sem, recv_sem, device):
    return pltpu.make_async_remote_copy(src_ref=src, dst_ref=dst, send_sem=send_sem, recv_sem=recv_sem,
                                        device_id=device, device_id_type=MESH)


def _gather_phases(big_in, big_out, kinds, shapes, send_sems, recv_sems, local_sems):
    x, y, c, chips = _place()
    me, sibling = 2 * x + y, (x, y, 1 - c)
    n = len(big_in)

    def local(a):
        r, w = shapes[a]
        own = big_out[a].at[:, pl.ds(me * w, w)] if kinds[a] == "col" else big_out[a].at[pl.ds(me * r, r), :]
        return pltpu.make_async_copy(big_in[a], own, local_sems.at[a])

    def over_ici(a, k):
        r = shapes[a][0]
        return _remote(big_in[a].at[pl.ds(c * (r // 2), r // 2), :], _window(big_out[a], kinds[a], shapes[a], me, c),
                       send_sems.at[a, k], recv_sems.at[a, k], (*chips[k], c))

    def landed(a, k, half, slot):
        win = _window(big_out[a], kinds[a], shapes[a], 2 * chips[k][0] + chips[k][1], half)
        return _remote(win, win, send_sems.at[a, slot], recv_sems.at[a, slot], sibling)

    def start():
        for a in range(n):
            local(a).start()
            for k in range(3):
                over_ici(a, k).start()

    def forward():
        for a in range(n):
            for k in range(3):
                landed(a, k, c, k).wait_recv()
                landed(a, k, c, 3 + k).start()

    def finish():
        for a in range(n):
            for k in range(3):
                landed(a, k, 1 - c, 3 + k).wait_recv()
        for a in range(n):
            for k in range(3):
                over_ici(a, k).wait_send()
                landed(a, k, c, 3 + k).wait_send()
            local(a).wait()

    return start, forward, finish


def _gather_scratch(n):
    return [pltpu.SemaphoreType.DMA((n, 6)), pltpu.SemaphoreType.DMA((n, 6)), pltpu.SemaphoreType.DMA((n,))]


def _gather_weights(bigs, kinds, smalls, meta, x2d, target2d, t):
    nb, ns = len(bigs), len(smalls)
    shapes = [b.shape for b in bigs]
    seq, d = x2d.shape
    real = N_META + seq
    assert t - real >= N_META
    zeros = jnp.zeros((t - real, d), F32)
    mw = meta.shape[1]

    def body(*refs):
        big_in, small_in = refs[:nb], refs[nb:nb + ns]
        meta_ref, x_ref, tgt_ref, zero_ref = refs[nb + ns:nb + ns + 4]
        outs = refs[nb + ns + 4:]
        big_out, small_out, (h0_ref, target_ref) = outs[:nb], outs[nb:nb + ns], outs[nb + ns:nb + ns + 2]
        send_sems, recv_sems, local_sems, small_send, small_recv, small_local, layout_sems = outs[nb + ns + 2:]
        x, y, c, chips = _place()
        me = 2 * x + y
        start, forward, finish = _gather_phases(big_in, big_out, kinds, shapes, send_sems, recv_sems, local_sems)
        start()

        def meta_cols(j):
            return h0_ref.at[pl.ds(0, N_META), pl.ds(j * mw, mw)]

        local = [
            pltpu.make_async_copy(x_ref, h0_ref.at[pl.ds(N_META, seq), :], layout_sems.at[0]),
            pltpu.make_async_copy(zero_ref, h0_ref.at[pl.ds(real, t - real), :], layout_sems.at[1]),
            pltpu.make_async_copy(tgt_ref, target_ref.at[pl.ds(N_META, seq), :], layout_sems.at[2]),
            pltpu.make_async_copy(zero_ref, target_ref.at[pl.ds(real, t - real), :], layout_sems.at[3]),
            pltpu.make_async_copy(zero_ref.at[pl.ds(0, N_META), :], target_ref.at[pl.ds(0, N_META), :], layout_sems.at[4]),
            pltpu.make_async_copy(meta_ref, meta_cols(me), layout_sems.at[5]),
        ]
        sent = []
        for s in range(ns):
            local.append(pltpu.make_async_copy(small_in[s], small_out[s].at[me], small_local.at[s]))
            for k, chip in enumerate(chips):
                sent.append(_remote(small_in[s], small_out[s].at[me], small_send.at[s, k], small_recv.at[s, k], (*chip, c)))
        for k, chip in enumerate(chips):
            sent.append(_remote(meta_ref, meta_cols(me), small_send.at[ns, k], small_recv.at[ns, k], (*chip, c)))
        for cp in local + sent:
            cp.start()
        forward()
        for k, (px, py) in enumerate(chips):
            for s in range(ns):
                dst = small_out[s].at[2 * px + py]
                _remote(dst, dst, small_send.at[s, k], small_recv.at[s, k], (px, py, c)).wait_recv()
            dst = meta_cols(2 * px + py)
            _remote(dst, dst, small_send.at[ns, k], small_recv.at[ns, k], (px, py, c)).wait_recv()
        finish()
        for cp in sent:
            cp.wait_send()
        for cp in local:
            cp.wait()

    out_shape = [jax.ShapeDtypeStruct(_full_shape(k, s), b.dtype) for b, k, s in zip(bigs, kinds, shapes)]
    out_shape += [jax.ShapeDtypeStruct((4, *s.shape), s.dtype) for s in smalls]
    out_shape += [jax.ShapeDtypeStruct((t, d), F32)] * 2
    n_in = nb + ns + 4
    return _pcall(
        body, name="gather_weights", in_specs=[HBM] * n_in, out_specs=[HBM] * len(out_shape), out_shape=out_shape,
        scratch_shapes=_gather_scratch(nb) + [pltpu.SemaphoreType.DMA((ns + 1, 3)), pltpu.SemaphoreType.DMA((ns + 1, 3)),
                                              pltpu.SemaphoreType.DMA((ns,)), pltpu.SemaphoreType.DMA((6,))],
    )(*bigs, *smalls, meta, x2d, target2d, zeros)


def _swap_other_halves(grads, kinds, shapes, *, name):
    n = len(grads)

    def body(*refs):
        g_in, land = refs[:n], refs[n:2 * n]
        send_sems, recv_sems = refs[2 * n:]
        x, y, c, _ = _place()
        sibling = (x, y, 1 - c)
        sent = []
        for a in range(n):
            for j in range(4):
                cp = _remote(_window(g_in[a], kinds[a], shapes[a], j, 1 - c), land[a].at[j], send_sems.at[a, j], recv_sems.at[a, j], sibling)
                cp.start()
                sent.append(cp)
        for a in range(n):
            for j in range(4):
                _remote(land[a].at[j], land[a].at[j], send_sems.at[a, j], recv_sems.at[a, j], sibling).wait_recv()
        for cp in sent:
            cp.wait_send()

    return _pcall(
        body, name=name, in_specs=[HBM] * n, out_specs=[HBM] * n,
        out_shape=[jax.ShapeDtypeStruct((4, s[0] // 2, s[1]), F32) for s in shapes],
        scratch_shapes=[pltpu.SemaphoreType.DMA((n, 4)), pltpu.SemaphoreType.DMA((n, 4))],
    )(*grads)


def _pair_sum(place, g, land, kind, shard_shape, *, name):
    r, w = shard_shape
    hr = r // 2
    tr = _tile(hr, 256, 16)
    nr = hr // tr

    def body(place_ref, g_ref, l_ref, o_ref, own_ref):
        total = g_ref[...] + l_ref[...]
        o_ref[...] = total.astype(BF16)

        @pl.when(pl.program_id(1) == place_ref[1])
        def _():
            own_ref[...] = total

    if kind == "col":
        g_spec = pl.BlockSpec((tr, w), lambda i, j, p: (p[0] * nr + i, j))
    else:
        g_spec = pl.BlockSpec((tr, w), lambda i, j, p: ((2 * j + p[0]) * nr + i, 0))
    part = pl.BlockSpec((None, tr, w), lambda i, j, p: (j, i, 0))
    return _pcall(
        body, name=name,
        grid_spec=pltpu.PrefetchScalarGridSpec(num_scalar_prefetch=1, grid=(nr, 4), in_specs=[g_spec, part],
                                               out_specs=[part, pl.BlockSpec((tr, w), lambda i, j, p: (i, 0))]),
        out_shape=[jax.ShapeDtypeStruct((4, hr, w), BF16), jax.ShapeDtypeStruct((hr, w), F32)],
        compiler_params=_params("arbitrary", "arbitrary"),
    )(place, g, land)


def _exchange_phases(p_in, land, send_sems, recv_sems):
    x, y, c, chips = _place()

    def copy(a, k):
        px, py = chips[k]
        return _remote(p_in[a].at[2 * px + py], land[a].at[k], send_sems.at[a, k], recv_sems.at[a, k], (px, py, c))

    def start():
        for a in range(len(p_in)):
            for k in range(3):
                copy(a, k).start()

    def finish():
        for a in range(len(p_in)):
            for k in range(3):
                copy(a, k).wait()

    return start, finish


def _exchange_shapes(parts):
    return [jax.ShapeDtypeStruct((3, *p.shape[1:]), p.dtype) for p in parts]


def _chip_sum(place, own, land, *, name):
    hr, w = own.shape
    tr = _tile(hr, 256, 16)
    nr = hr // tr

    def body(place_ref, p_ref, l_ref, o_ref):
        o_ref[...] = ((p_ref[...] + l_ref[0].astype(F32)) + l_ref[1].astype(F32)) + l_ref[2].astype(F32)

    return _pcall(
        body, name=name,
        grid_spec=pltpu.PrefetchScalarGridSpec(
            num_scalar_prefetch=1, grid=(nr,),
            in_specs=[pl.BlockSpec((tr, w), lambda i, p: (i, 0)), pl.BlockSpec((3, tr, w), lambda i, p: (0, i, 0))],
            out_specs=pl.BlockSpec((tr, w), lambda i, p: (p[0] * nr + i, 0))),
        out_shape=jax.ShapeDtypeStruct((2 * hr, w), F32),
        compiler_params=_params("arbitrary"),
    )(place, own, land)


def _swap_reduced_halves(halves):
    n = len(halves)

    def body(*refs):
        src, out = refs[:n], refs[n:2 * n]
        send_sems, recv_sems = refs[2 * n:]
        x, y, c, _ = _place()
        sibling = (x, y, 1 - c)
        sent = []
        for a in range(n):
            hr = out[a].shape[0] // 2
            cp = _remote(src[a].at[pl.ds(c * hr, hr), :], out[a].at[pl.ds(c * hr, hr), :], send_sems.at[a], recv_sems.at[a], sibling)
            cp.start()
            sent.append(cp)
        for a in range(n):
            hr = out[a].shape[0] // 2
            other = out[a].at[pl.ds((1 - c) * hr, hr), :]
            _remote(other, other, send_sems.at[a], recv_sems.at[a], sibling).wait_recv()
        for cp in sent:
            cp.wait_send()

    return _pcall(
        body, name="swap_reduced_halves", in_specs=[HBM] * n, out_specs=[HBM] * n,
        out_shape=[jax.ShapeDtypeStruct(h.shape, F32) for h in halves],
        input_output_aliases={a: a for a in range(n)},
        scratch_shapes=[pltpu.SemaphoreType.DMA((n,)), pltpu.SemaphoreType.DMA((n,))],
    )(*halves)


def _sum_small(packed):
    rows = packed.shape[0]

    def body(in_ref, out_ref, land, send_sems, recv_sems, local_sem):
        x, y, c, _ = _place()
        me = 4 * x + 2 * y + c
        own = pltpu.make_async_copy(in_ref, land.at[me], local_sem)
        own.start()
        flips = [(r >> 2, (r >> 1) & 1, r & 1) for r in range(1, 8)]
        sent = []
        for k, (fx, fy, fc) in enumerate(flips):
            cp = _remote(in_ref, land.at[me], send_sems.at[k], recv_sems.at[k], (x ^ fx, y ^ fy, c ^ fc))
            cp.start()
            sent.append(cp)
        for k, (fx, fy, fc) in enumerate(flips):
            src = land.at[4 * (x ^ fx) + 2 * (y ^ fy) + (c ^ fc)]
            _remote(src, src, send_sems.at[k], recv_sems.at[k], (x ^ fx, y ^ fy, c ^ fc)).wait_recv()
        for cp in sent:
            cp.wait_send()
        own.wait()
        total = land[0]
        for dev in range(1, 8):
            total = total + land[dev]
        out_ref[...] = total

    vmem = pl.BlockSpec(memory_space=pltpu.VMEM)
    return _pcall(
        body, name="sum_small", in_specs=[vmem], out_specs=vmem, out_shape=jax.ShapeDtypeStruct(packed.shape, F32),
        scratch_shapes=[pltpu.VMEM((8, rows, LANES), F32), pltpu.SemaphoreType.DMA((7,)), pltpu.SemaphoreType.DMA((7,)), pltpu.SemaphoreType.DMA],
    )(packed)


def _sigmoid(v):
    return 1.0 / (1.0 + jnp.exp(-v))


def kernel(x, meta_tokens, g_pre_mix, w_in, conv_w_mix, w_proj_conv, w_proj_attn, b_gate, w_out, g_post_mix, g_pre_ffn, w_up_gate, conv_w_ffn, w_down, g_post_ffn, loss_target, m_meta_tokens, m_g_pre_mix, m_w_in, m_conv_w_mix, m_w_proj_conv, m_w_proj_attn, m_b_gate, m_w_out, m_g_post_mix, m_g_pre_ffn, m_w_up_gate, m_conv_w_ffn, m_w_down, m_g_post_ffn, v_meta_tokens, v_g_pre_mix, v_w_in, v_conv_w_mix, v_w_proj_conv, v_w_proj_attn, v_b_gate, v_w_out, v_g_post_mix, v_g_pre_ffn, v_w_up_gate, v_conv_w_ffn, v_w_down, v_g_post_ffn):
    seq, d = x.shape[1], x.shape[2]
    f = w_down.shape[1] * 4
    real = N_META + seq
    t = -(-real // ATT_BLOCK) * ATT_BLOCK
    nd = d // LANES
    cx, cy, cc = lax.axis_index("x"), lax.axis_index("y"), lax.axis_index("c")
    chip = 2 * cx + cy
    place = jnp.stack([cc, chip]).astype(jnp.int32)

    big_names = ["w_in", "w_proj_conv", "w_proj_attn", "w_out", "w_up_gate", "w_down"]
    kinds = ["col", "row", "row", "row", "col", "row"]
    big_w = [w_in[0], w_proj_conv[0], w_proj_attn[0], w_out[0], w_up_gate[0], w_down[0]]
    small_w = [conv_w_mix[0], b_gate[0], conv_w_ffn[0]]
    shards = [w.astype(BF16) for w in big_w]
    shapes = [w.shape for w in big_w]
    gathered = _gather_weights(shards[:1], kinds[:1], small_w, meta_tokens, x[0], loss_target[0], t)
    wi, h0, target = gathered[0], gathered[-2], gathered[-1]
    cwm_f, bg_f, cwf_f = (jnp.moveaxis(s, 0, 1).reshape(s.shape[1], -1) for s in gathered[1:-2])

    xn1 = _prenorm(h0, g_pre_mix, name="prenorm_mix")
    hin = _matmul(xn1, wi, name="in_proj")
    yconv = _mixer_conv_fwd(hin, cwm_f, d)
    o, att_tot, (wpc, wpa, wo, wug, wd) = _attention_fwd(hin, d, shards[1:], kinds[1:])

    def gates(gpre, bias):
        return _sigmoid(gpre[:, :d].astype(F32) + bias[0:1]), _sigmoid(gpre[:, d:].astype(F32) + bias[1:2])

    def merge_epi(accs, rows, vecs, row0):
        gate_c, gate_a = gates(rows[0], vecs[0])
        return [gate_c * accs[0] + gate_a * accs[1], accs[0], accs[1]]

    act = ("row", d, BF16)
    merged, bconv, battn = _matmul_rows([(yconv, wpc), (o, wpa)], [(hin, 2 * d, 3)], [bg_f], merge_epi, [act, act, act], name="branch_proj")

    def mix_epi(accs, rows, vecs, row0):
        mix = accs[0]
        h1 = rows[0] + mix * _rms(mix) * vecs[0]
        return [mix, h1, h1 * _rms(h1) * vecs[1]]

    mix, h1, xn2 = _matmul_rows([(merged, wo)], [(h0, d, 0)], [g_post_mix, g_pre_ffn], mix_epi,
                                [("row", d, F32), ("row", d, F32), act], name="out_proj")
    ug = _matmul(xn2, wug, name="up_proj")
    fact = _ffn_conv_fwd(ug, cwf_f, f)

    def loss_epi(accs, rows, vecs, row0):
        ffn = accs[0]
        r = _rms(ffn)
        fh = ffn * r
        h2 = rows[0] + fh * vecs[0]
        rid = row0 + lax.broadcasted_iota(jnp.int32, (ffn.shape[0], 1), 0)
        err = jnp.where((rid >= N_META) & (rid < real), h2 - rows[1], 0.0)
        dy = err * (1.0 / d)
        dffn, dg = _rms_bwd(dy, fh, r, vecs[0])
        loss = jnp.zeros((1, LANES), F32) + 0.5 * jnp.sum(err * err) / d
        return [dffn, dy, dg, loss]

    dffn, dh2, dg_post_ffn, loss_part = _matmul_rows(
        [(fact, wd)], [(h1, d, 0), (target, d, 0)], [g_post_ffn], loss_epi,
        [act, ("row", d, F32), ("acc", 1, d), ("acc", 1, LANES)], name="down_proj_loss")

    gw_down = _matmul_tn(fact, dffn, name="grad_w_down")
    df = _matmul(dffn, wd, nt=True, name="down_proj_bwd")
    du2, dg2, gcw_ffn = _ffn_conv_bwd(ug, df, cwf_f, f)

    def ffn_in_epi(accs, rows, vecs, row0):
        h1v, dyv, mixv = rows
        r3 = _rms(h1v)
        dh1n, dg3 = _rms_bwd(accs[0], h1v * r3, r3, vecs[0])
        dh1 = dyv + dh1n
        r2 = _rms(mixv)
        dmix, dgm = _rms_bwd(dh1, mixv * r2, r2, vecs[1])
        return [dh1, dmix, dg3, dgm]

    dh1, dmix, dg_pre_ffn, dg_post_mix, dug = _matmul_rows_joined(
        [du2, dg2], wug, [(h1, d, 0), (dh2, d, 0), (mix, d, 0)], [g_pre_ffn, g_post_mix], ffn_in_epi,
        [("row", d, F32), act, ("acc", 1, d), ("acc", 1, d)], tk=f // 2, name="up_proj_bwd")
    gw_up = _matmul_tn(xn2, dug, name="grad_w_up_gate")
    gw_out = _matmul_tn(merged, dmix, name="grad_w_out")

    def merge_bwd_epi(accs, rows, vecs, row0):
        dm = accs[0]
        gate_c, gate_a = gates(rows[0], vecs[0])
        dpre_c = dm * rows[1].astype(F32) * gate_c * (1.0 - gate_c)
        dpre_a = dm * rows[2].astype(F32) * gate_a * (1.0 - gate_a)
        dbias = jnp.concatenate([jnp.sum(dpre_c, axis=0, keepdims=True), jnp.sum(dpre_a, axis=0, keepdims=True)], axis=0)
        return [dm * gate_c, dm * gate_a, jnp.concatenate([dpre_c, dpre_a], axis=1), dbias]

    dbconv, dbattn, dgates, gb_gate = _matmul_rows(
        [(dmix, wo)], [(hin, 2 * d, 3), (bconv, d, 0), (battn, d, 0)], [bg_f], merge_bwd_epi,
        [act, act, ("row", 2 * d, BF16), ("acc", 2, d)], nt=True, name="out_proj_bwd")
    gw_pc = _matmul_tn(yconv, dbconv, name="grad_w_proj_conv")
    gw_pa = _matmul_tn(o, dbattn, name="grad_w_proj_attn")
    dyconv = _matmul(dbconv, wpc, nt=True, name="proj_conv_bwd")
    do = _matmul(dbattn, wpa, nt=True, name="proj_attn_bwd")
    db, dc, dhc, gcw_mix = _mixer_conv_bwd(hin, dyconv, cwm_f, d)

    def pair_sums(grads, first):
        landed = _swap_other_halves(grads, kinds[first:first + len(grads)], shapes[first:first + len(grads)],
                                    name=f"swap_other_halves_{big_names[first]}")
        return zip(*[_pair_sum(place, g, l, kinds[first + a], shapes[first + a], name=f"pair_sum_{big_names[first + a]}")
                     for a, (g, l) in enumerate(zip(grads, landed))])

    parts, own = pair_sums([gw_pc, gw_pa, gw_out, gw_up, gw_down], 1)
    dq, dk, dv, landed = _attention_bwd(hin, att_tot, do, d, parts)

    def in_epi(accs, rows, vecs, row0):
        h0v, dh1v = rows
        r1 = _rms(h0v)
        dh0n, dg1 = _rms_bwd(accs[0], h0v * r1, r1, vecs[0])
        return [dh1v + dh0n, dg1]

    dhin = [db, dc, dhc, dq, dk, dv, dgates]
    gw_in = _matmul_tn_segments(xn1, dhin, name="grad_w_in")
    parts_in, own_in = pair_sums([gw_in], 0)
    dh0, dg_pre_mix, landed_in = _matmul_rows_joined(
        dhin, wi, [(h0, d, 0), (dh1, d, 0)], [g_pre_mix], in_epi,
        [("row", d, F32), ("acc", 1, d)], tk=d, name="in_proj_bwd", joined=False, parts=parts_in)
    grad_x = dh0[N_META:real][None]

    landed = [landed_in] + list(landed)
    halves = [_chip_sum(place, p, l, name=f"chip_sum_{n}") for p, l, n in zip(list(own_in) + list(own), landed, big_names)]
    big_r = _swap_reduced_halves(halves)

    small_g = [dh0[:N_META], dg_pre_mix, gcw_mix, gb_gate, dg_post_mix, dg_pre_ffn, gcw_ffn, dg_post_ffn, loss_part]
    flat = jnp.concatenate([s.reshape(-1) for s in small_g])
    pad = -flat.shape[0] % (8 * LANES)
    summed = _sum_small(jnp.pad(flat, (0, pad)).reshape(-1, LANES)).reshape(-1)
    small_r, pos = [], 0
    for s in small_g:
        small_r.append(summed[pos:pos + s.size].reshape(s.shape))
        pos += s.size
    g_meta, g_g1, g_cwm, g_bg, g_g2, g_g3, g_cwf, g_g4, loss_row = small_r
    loss = loss_row[0, 0]

    def my_cols(full, width):
        return lax.dynamic_slice_in_dim(full, chip * width, width, axis=1)

    grads = {
        "meta_tokens": my_cols(g_meta, d // 4), "g_pre_mix": g_g1, "w_in": big_r[0][None],
        "conv_w_mix": my_cols(g_cwm, d // 4)[None], "w_proj_conv": big_r[1][None], "w_proj_attn": big_r[2][None],
        "b_gate": my_cols(g_bg, d // 4)[None], "w_out": big_r[3][None], "g_post_mix": g_g2, "g_pre_ffn": g_g3,
        "w_up_gate": big_r[4][None], "conv_w_ffn": my_cols(g_cwf, f // 4)[None], "w_down": big_r[5][None], "g_post_ffn": g_g4,
    }
    weights = dict(meta_tokens=meta_tokens, g_pre_mix=g_pre_mix, w_in=w_in, conv_w_mix=conv_w_mix, w_proj_conv=w_proj_conv,
                   w_proj_attn=w_proj_attn, b_gate=b_gate, w_out=w_out, g_post_mix=g_post_mix, g_pre_ffn=g_pre_ffn,
                   w_up_gate=w_up_gate, conv_w_ffn=conv_w_ffn, w_down=w_down, g_post_ffn=g_post_ffn)
    m_in = dict(meta_tokens=m_meta_tokens, g_pre_mix=m_g_pre_mix, w_in=m_w_in, conv_w_mix=m_conv_w_mix, w_proj_conv=m_w_proj_conv,
                w_proj_attn=m_w_proj_attn, b_gate=m_b_gate, w_out=m_w_out, g_post_mix=m_g_post_mix, g_pre_ffn=m_g_pre_ffn,
                w_up_gate=m_w_up_gate, conv_w_ffn=m_conv_w_ffn, w_down=m_w_down, g_post_ffn=m_g_post_ffn)
    v_in = dict(meta_tokens=v_meta_tokens, g_pre_mix=v_g_pre_mix, w_in=v_w_in, conv_w_mix=v_conv_w_mix, w_proj_conv=v_w_proj_conv,
                w_proj_attn=v_w_proj_attn, b_gate=v_b_gate, w_out=v_w_out, g_post_mix=v_g_post_mix, g_pre_ffn=v_g_pre_ffn,
                w_up_gate=v_w_up_gate, conv_w_ffn=v_conv_w_ffn, w_down=v_w_down, g_post_ffn=v_g_post_ffn)
    names = list(weights)
    deltas, new_m, new_v = [], [], []
    for n in names:
        shape = weights[n].shape
        two_d = (-1, shape[-1])
        dl, nm, nv = _adamw(weights[n].reshape(two_d), grads[n].reshape(two_d), m_in[n].reshape(two_d), v_in[n].reshape(two_d),
                            name=f"adamw_{n}")
        deltas.append(dl.reshape(shape))
        new_m.append(nm.reshape(shape))
        new_v.append(nv.reshape(shape))
    return (loss, grad_x, *[grads[n].reshape(weights[n].shape) for n in names], *deltas, *new_m, *new_v)
```

```python
import functools
import math

import jax
import jax.numpy as jnp
from jax import lax
from jax.experimental import pallas as pl
from jax.experimental.pallas import tpu as pltpu

F32 = jnp.float32
BF16 = jnp.bfloat16

N_META = 16
HEAD_DIM = 64
LANES = 128
RMS_EPS = 1e-6
ATT_BLOCK = 256
ATT_FWD_LANE_BLOCKS = 4
ATT_BWD_LANE_BLOCKS = 2
VMEM_LIMIT = 56 * 1024 * 1024

ADAM_LR = 0.001
ADAM_B1 = 0.9
ADAM_B2 = 0.999
ADAM_EPS = 1e-08
ADAM_WD = 0.01
ADAM_STEP = 10

MESH = pl.DeviceIdType.MESH


def _tile(n, cap, unit=LANES):
    d = (min(cap, n) // unit) * unit
    while d >= unit:
        if n % d == 0:
            return d
        d -= unit
    raise ValueError(f"no tile for {n} under {cap}")


MXU_EDGE = 256


def _mxu_tile(n, cap):
    if n <= cap:
        return n
    return _tile(n, cap, MXU_EDGE) if n % MXU_EDGE == 0 else _tile(n, cap)


def _params(*sem):
    return pltpu.CompilerParams(dimension_semantics=sem, vmem_limit_bytes=VMEM_LIMIT)


def _pcall(body, **kw):
    return pl.pallas_call(body, **kw)


def _dot(a, b, dims):
    return lax.dot_general(a, b, (dims, ((), ())), preferred_element_type=F32)


NN = ((1,), (0,))
NT = ((1,), (1,))
TN = ((0,), (0,))


def _matmul(a, b, *, nt=False, out_dtype=BF16, name):
    m, kdim = a.shape
    n = b.shape[0] if nt else b.shape[1]
    tm, tn, tk = _tile(m, 640, 16), _mxu_tile(n, 2816), _mxu_tile(kdim, 2816)
    nk = kdim // tk

    def body(a_ref, b_ref, o_ref, *scratch):
        p = _dot(a_ref[...], b_ref[...], NT if nt else NN)
        if nk == 1:
            o_ref[...] = p.astype(o_ref.dtype)
            return
        acc, k = scratch[0], pl.program_id(2)

        @pl.when(k == 0)
        def _():
            acc[...] = p

        @pl.when(k > 0)
        def _():
            acc[...] += p

        @pl.when(k == nk - 1)
        def _():
            o_ref[...] = acc[...].astype(o_ref.dtype)

    b_spec = pl.BlockSpec((tn, tk), lambda j, i, k: (j, k)) if nt else pl.BlockSpec((tk, tn), lambda j, i, k: (k, j))
    return _pcall(
        body, name=name, grid=(n // tn, m // tm, nk),
        in_specs=[pl.BlockSpec((tm, tk), lambda j, i, k: (i, k)), b_spec],
        out_specs=pl.BlockSpec((tm, tn), lambda j, i, k: (i, j)),
        out_shape=jax.ShapeDtypeStruct((m, n), out_dtype),
        scratch_shapes=[pltpu.VMEM((tm, tn), F32)] if nk > 1 else [],
        compiler_params=_params("arbitrary", "arbitrary", "arbitrary"),
    )(a, b)


def _matmul_tn(a, b, *, name):
    t, ka = a.shape
    nb = b.shape[1]
    tb = _mxu_tile(nb, 1024)
    budget = VMEM_LIMIT * 5 // 7
    ta = next(c for c in range(ka, 0, -LANES) if ka % c == 0 and 2 * (2 * t * (c + tb) + 4 * c * tb) <= budget)

    def body(a_ref, b_ref, o_ref):
        o_ref[...] = _dot(a_ref[...], b_ref[...], TN)

    return _pcall(
        body, name=name, grid=(ka // ta, nb // tb),
        in_specs=[pl.BlockSpec((t, ta), lambda i, j: (0, i)), pl.BlockSpec((t, tb), lambda i, j: (0, j))],
        out_specs=pl.BlockSpec((ta, tb), lambda i, j: (i, j)),
        out_shape=jax.ShapeDtypeStruct((ka, nb), F32),
        compiler_params=_params("arbitrary", "arbitrary"),
    )(a, b)


def _matmul_rows(pairs, rows, vecs, epi, outs, *, nt=False, name):
    m, kdim = pairs[0][0].shape
    n = pairs[0][1].shape[0] if nt else pairs[0][1].shape[1]
    tm, tk = _tile(m, 640, 16), _mxu_tile(kdim, 2816)
    nk, npair = kdim // tk, len(pairs)

    def body(*refs):
        a_refs, b_refs = refs[0:2 * npair:2], refs[1:2 * npair:2]
        pos = 2 * npair
        row_refs = refs[pos:pos + len(rows)]
        pos += len(rows)
        vec_refs = refs[pos:pos + len(vecs)]
        pos += len(vecs)
        out_refs = refs[pos:pos + len(outs)]
        accs = refs[pos + len(outs):]
        i, k = pl.program_id(0), pl.program_id(1)
        prods = [_dot(a[...], b[...], NT if nt else NN) for a, b in zip(a_refs, b_refs)]
        if nk > 1:
            @pl.when(k == 0)
            def _():
                for acc, p in zip(accs, prods):
                    acc[...] = p

            @pl.when(k > 0)
            def _():
                for acc, p in zip(accs, prods):
                    acc[...] += p

        @pl.when(k == nk - 1)
        def _():
            vals = [acc[...] for acc in accs] if nk > 1 else prods
            res = epi(vals, [r[...] for r in row_refs], [v[...] for v in vec_refs], i * tm)
            for o_ref, spec, val in zip(out_refs, outs, res):
                if spec[0] == "row":
                    o_ref[...] = val.astype(o_ref.dtype)
                else:
                    @pl.when(i == 0)
                    def _():
                        o_ref[...] = val

                    @pl.when(i > 0)
                    def _():
                        o_ref[...] += val

    in_specs, args = [], []
    for a, b in pairs:
        in_specs += [pl.BlockSpec((tm, tk), lambda i, k: (i, k)),
                     pl.BlockSpec((n, tk), lambda i, k: (0, k)) if nt else pl.BlockSpec((tk, n), lambda i, k: (k, 0))]
        args += [a, b]
    for arr, width, cb in rows:
        in_specs.append(pl.BlockSpec((tm, width), functools.partial(lambda i, k, cb: (i, cb), cb=cb)))
        args.append(arr)
    for v in vecs:
        in_specs.append(pl.BlockSpec(v.shape, lambda i, k: (0, 0)))
        args.append(v)
    out_specs, out_shape = [], []
    for spec in outs:
        if spec[0] == "row":
            out_specs.append(pl.BlockSpec((tm, spec[1]), lambda i, k: (i, 0)))
            out_shape.append(jax.ShapeDtypeStruct((m, spec[1]), spec[2]))
        else:
            out_specs.append(pl.BlockSpec((spec[1], spec[2]), lambda i, k: (0, 0)))
            out_shape.append(jax.ShapeDtypeStruct((spec[1], spec[2]), F32))
    return _pcall(
        body, name=name, grid=(m // tm, nk), in_specs=in_specs, out_specs=out_specs, out_shape=out_shape,
        scratch_shapes=[pltpu.VMEM((tm, n), F32) for _ in pairs] if nk > 1 else [],
        compiler_params=_params("arbitrary", "arbitrary"),
    )(*args)


def _matmul_rows_joined(segments, b, rows, vecs, epi, outs, *, tk, name, joined=True, parts=()):
    m = segments[0].shape[0]
    n, kdim = b.shape
    tm = _tile(m, 640, 16)
    nm, nk, npart = m // tm, kdim // tk, len(parts)
    first = [0]
    for s in segments:
        assert s.shape[1] % tk == 0
        first.append(first[-1] + s.shape[1] // tk)
    assert first[-1] == nk

    def body(*refs):
        seg_refs, b_ref = refs[:len(segments)], refs[len(segments)]
        pos = len(segments) + 1
        row_refs = refs[pos:pos + len(rows)]
        pos += len(rows)
        vec_refs = refs[pos:pos + len(vecs)]
        pos += len(vecs)
        part_refs = refs[pos:pos + npart]
        pos += npart
        out_refs = refs[pos:pos + len(outs)]
        pos += len(outs)
        joined_ref = refs[pos] if joined else None
        pos += int(joined)
        land_refs, acc = refs[pos:pos + npart], refs[pos + npart]
        i, k = pl.program_id(0), pl.program_id(1)
        if npart:
            start, finish = _exchange_phases(part_refs, land_refs, *refs[pos + npart + 1:])
            pl.when((i == 0) & (k == 0))(start)

        @pl.when(k == 0)
        def _():
            acc[...] = jnp.zeros_like(acc)

        for s, seg in enumerate(seg_refs):
            @pl.when((k >= first[s]) & (k < first[s + 1]))
            def _():
                if joined:
                    joined_ref[...] = seg[...]
                acc[...] += _dot(seg[...], b_ref[...], NT)

        @pl.when(k == nk - 1)
        def _():
            res = epi([acc[...]], [r[...] for r in row_refs], [v[...] for v in vec_refs], i * tm)
            for o_ref, spec, val in zip(out_refs, outs, res):
                if spec[0] == "row":
                    o_ref[...] = val.astype(o_ref.dtype)
                else:
                    @pl.when(i == 0)
                    def _():
                        o_ref[...] = val

                    @pl.when(i > 0)
                    def _():
                        o_ref[...] += val

        if npart:
            pl.when((i == nm - 1) & (k == nk - 1))(finish)

    def seg_spec(s):
        last = first[s + 1] - first[s] - 1
        return pl.BlockSpec((tm, tk), lambda i, k: (i, jnp.clip(k - first[s], 0, last)))

    in_specs = [seg_spec(s) for s in range(len(segments))] + [pl.BlockSpec((n, tk), lambda i, k: (0, k))]
    args = [*segments, b]
    for arr, width, cb in rows:
        in_specs.append(pl.BlockSpec((tm, width), functools.partial(lambda i, k, cb: (i, cb), cb=cb)))
        args.append(arr)
    for v in vecs:
        in_specs.append(pl.BlockSpec(v.shape, lambda i, k: (0, 0)))
        args.append(v)
    in_specs += [HBM] * npart
    args += list(parts)
    out_specs, out_shape = [], []
    for spec in outs:
        if spec[0] == "row":
            out_specs.append(pl.BlockSpec((tm, spec[1]), lambda i, k: (i, 0)))
            out_shape.append(jax.ShapeDtypeStruct((m, spec[1]), spec[2]))
        else:
            out_specs.append(pl.BlockSpec((spec[1], spec[2]), lambda i, k: (0, 0)))
            out_shape.append(jax.ShapeDtypeStruct((spec[1], spec[2]), F32))
    if joined:
        out_specs.append(pl.BlockSpec((tm, tk), lambda i, k: (i, k)))
        out_shape.append(jax.ShapeDtypeStruct((m, kdim), segments[0].dtype))
    out_specs += [HBM] * npart
    out_shape += _exchange_shapes(parts)
    sems = [pltpu.SemaphoreType.DMA((npart, 3)), pltpu.SemaphoreType.DMA((npart, 3))] if npart else []
    return _pcall(
        body, name=name, grid=(nm, nk), in_specs=in_specs, out_specs=out_specs, out_shape=out_shape,
        scratch_shapes=[pltpu.VMEM((tm, n), F32)] + sems,
        compiler_params=_params("arbitrary", "arbitrary"),
    )(*args)


def _matmul_tn_segments(a, segments, *, name):
    t, ka = a.shape
    tb = MXU_EDGE
    first = [0]
    for s in segments:
        assert s.shape[1] % tb == 0
        first.append(first[-1] + s.shape[1] // tb)
    budget = VMEM_LIMIT * 13 // 14
    ta = next(c for c in range(ka, 0, -LANES) if ka % c == 0 and 2 * (2 * t * (c + len(segments) * tb) + 4 * c * tb) <= budget)

    def body(a_ref, *refs):
        seg_refs, o_ref = refs[:-1], refs[-1]
        j = pl.program_id(1)
        for s, seg in enumerate(seg_refs):
            @pl.when((j >= first[s]) & (j < first[s + 1]))
            def _():
                o_ref[...] = _dot(a_ref[...], seg[...], TN)

    def seg_spec(s):
        last = first[s + 1] - first[s] - 1
        return pl.BlockSpec((t, tb), lambda i, j: (0, jnp.clip(j - first[s], 0, last)))

    return _pcall(
        body, name=name, grid=(ka // ta, first[-1]),
        in_specs=[pl.BlockSpec((t, ta), lambda i, j: (0, i))] + [seg_spec(s) for s in range(len(segments))],
        out_specs=pl.BlockSpec((ta, tb), lambda i, j: (i, j)),
        out_shape=jax.ShapeDtypeStruct((ka, first[-1] * tb), F32),
        compiler_params=_params("arbitrary", "arbitrary"),
    )(a, *segments)


def _rms(v):
    return lax.rsqrt(jnp.mean(v * v, axis=-1, keepdims=True) + RMS_EPS)


def _rms_bwd(dz, vhat, r, g):
    t = dz * g
    dv = r * (t - vhat * jnp.mean(t * vhat, axis=-1, keepdims=True))
    return dv, jnp.sum(dz * vhat, axis=0, keepdims=True)


def _prenorm(h, g, *, name):
    m, d = h.shape
    tm = _tile(m, 640, 16)

    def body(h_ref, g_ref, o_ref):
        v = h_ref[...]
        o_ref[...] = (v * _rms(v) * g_ref[...]).astype(BF16)

    return _pcall(
        body, name=name, grid=(m // tm,),
        in_specs=[pl.BlockSpec((tm, d), lambda i: (i, 0)), pl.BlockSpec((1, d), lambda i: (0, 0))],
        out_specs=pl.BlockSpec((tm, d), lambda i: (i, 0)),
        out_shape=jax.ShapeDtypeStruct((m, d), BF16),
        compiler_params=_params("arbitrary"),
    )(h, g)


SUBLANES = 8


def _shift_down(u, k):
    r = pltpu.roll(u, k, 0)
    rows = lax.broadcasted_iota(jnp.int32, (SUBLANES, u.shape[1]), 0)
    return jnp.concatenate([jnp.where(rows >= k, r[:SUBLANES], 0.0), r[SUBLANES:]], axis=0)


def _shift_up(u, k):
    n = u.shape[0]
    r = pltpu.roll(u, n - k, 0)
    rows = lax.broadcasted_iota(jnp.int32, (SUBLANES, u.shape[1]), 0)
    return jnp.concatenate([r[:n - SUBLANES], jnp.where(rows < SUBLANES - k, r[n - SUBLANES:], 0.0)], axis=0)


def _taps(u):
    return _shift_down(u, 2), _shift_down(u, 1), u


def _conv(taps, w):
    return w[0:1] * taps[0] + w[1:2] * taps[1] + w[2:3] * taps[2]


def _conv_bwd(dcu, taps, w):
    du = w[2:3] * dcu + w[1:2] * _shift_up(dcu, 1) + w[0:1] * _shift_up(dcu, 2)
    return du, [jnp.sum(dcu * tap, axis=0, keepdims=True) for tap in taps]


def _strip(arr, t, cb0):
    return pl.BlockSpec((t, LANES), functools.partial(lambda s, cb0: (0, cb0 + s), cb0=cb0))


def _mixer_conv_fwd(hin, w, d):
    t, ns = hin.shape[0], d // LANES

    def body(b_ref, c_ref, h_ref, w_ref, y_ref):
        u = c_ref[...].astype(F32) * h_ref[...].astype(F32)
        y_ref[...] = (b_ref[...].astype(F32) * _conv(_taps(u), w_ref[...])).astype(BF16)

    return _pcall(
        body, name="mixer_conv_fwd", grid=(ns,),
        in_specs=[_strip(hin, t, 0), _strip(hin, t, ns), _strip(hin, t, 2 * ns), pl.BlockSpec((3, LANES), lambda s: (0, s))],
        out_specs=pl.BlockSpec((t, LANES), lambda s: (0, s)),
        out_shape=jax.ShapeDtypeStruct((t, d), BF16),
        compiler_params=_params("arbitrary"),
    )(hin, hin, hin, w)


def _mixer_conv_bwd(hin, dy, w, d):
    t, ns = hin.shape[0], d // LANES

    def body(b_ref, c_ref, h_ref, dy_ref, w_ref, db_ref, dc_ref, dh_ref, dw_ref):
        b, c, h, g = (r[...].astype(F32) for r in (b_ref, c_ref, h_ref, dy_ref))
        wv = w_ref[...]
        taps = _taps(c * h)
        db_ref[...] = (g * _conv(taps, wv)).astype(BF16)
        du, dw = _conv_bwd(g * b, taps, wv)
        dc_ref[...] = (du * h).astype(BF16)
        dh_ref[...] = (du * c).astype(BF16)
        for k in range(3):
            dw_ref[k:k + 1, :] = dw[k]

    col = pl.BlockSpec((t, LANES), lambda s: (0, s))
    act = jax.ShapeDtypeStruct((t, d), BF16)
    return _pcall(
        body, name="mixer_conv_bwd", grid=(ns,),
        in_specs=[_strip(hin, t, 0), _strip(hin, t, ns), _strip(hin, t, 2 * ns), col, pl.BlockSpec((3, LANES), lambda s: (0, s))],
        out_specs=[col, col, col, pl.BlockSpec((3, LANES), lambda s: (0, s))],
        out_shape=[act, act, act, jax.ShapeDtypeStruct((3, d), F32)],
        compiler_params=_params("arbitrary"),
    )(hin, hin, hin, dy, w)


GELU_C = math.sqrt(2.0 / math.pi)
GELU_A = 0.044715


def _gelu_tanh(x):
    return jnp.tanh(GELU_C * (x + GELU_A * x * x * x))


def _ffn_conv_fwd(ug, w, f):
    t, ns = ug.shape[0], f // LANES

    def body(u_ref, g_ref, w_ref, o_ref):
        cu = _conv(_taps(u_ref[...].astype(F32)), w_ref[...])
        o_ref[...] = (0.5 * cu * (1.0 + _gelu_tanh(cu)) * g_ref[...].astype(F32)).astype(BF16)

    return _pcall(
        body, name="ffn_conv_fwd", grid=(ns,),
        in_specs=[_strip(ug, t, 0), _strip(ug, t, ns), pl.BlockSpec((3, LANES), lambda s: (0, s))],
        out_specs=pl.BlockSpec((t, LANES), lambda s: (0, s)),
        out_shape=jax.ShapeDtypeStruct((t, f), BF16),
        compiler_params=_params("arbitrary"),
    )(ug, ug, w)


def _ffn_conv_bwd(ug, df, w, f):
    t, ns = ug.shape[0], f // LANES

    def body(u_ref, g_ref, df_ref, w_ref, du_ref, dg_ref, dw_ref):
        u, g, d = (r[...].astype(F32) for r in (u_ref, g_ref, df_ref))
        wv = w_ref[...]
        taps = _taps(u)
        cu = _conv(taps, wv)
        th = _gelu_tanh(cu)
        half = 0.5 * (1.0 + th)
        dg_ref[...] = (d * (cu * half)).astype(BF16)
        dgelu = half + (0.5 * GELU_C) * cu * (1.0 - th * th) * (1.0 + (3.0 * GELU_A) * (cu * cu))
        du, dw = _conv_bwd(d * g * dgelu, taps, wv)
        du_ref[...] = du.astype(BF16)
        for k in range(3):
            dw_ref[k:k + 1, :] = dw[k]

    col = pl.BlockSpec((t, LANES), lambda s: (0, s))
    act = jax.ShapeDtypeStruct((t, f), BF16)
    return _pcall(
        body, name="ffn_conv_bwd", grid=(ns,),
        in_specs=[_strip(ug, t, 0), _strip(ug, t, ns), col, pl.BlockSpec((3, LANES), lambda s: (0, s))],
        out_specs=[col, col, pl.BlockSpec((3, LANES), lambda s: (0, s))],
        out_shape=[act, act, jax.ShapeDtypeStruct((3, f), F32)],
        compiler_params=_params("arbitrary"),
    )(ug, ug, df, w)


def _log_terms(z):
    minus_abs = lax.bitcast_convert_type(lax.bitcast_convert_type(z, jnp.uint32) | jnp.uint32(0x80000000), F32)
    lb = jnp.minimum(z, 0.0) - jnp.log(1.0 + jnp.exp(minus_abs))
    return lb, lb - z


SUM_TERMS = 1


def _split_cat(v, axis):
    terms = [v.astype(BF16)]
    for _ in range(SUM_TERMS - 1):
        v = v - terms[-1].astype(F32)
        terms.append(v.astype(BF16))
    return terms[0] if SUM_TERMS == 1 else jnp.concatenate(terms, axis=axis)


def _head_masks():
    lane = lax.broadcasted_iota(jnp.int32, (1, LANES), 1)
    return lane < HEAD_DIM, lane >= HEAD_DIM


def _attention_fwd(hin, d, shards, kinds):
    t, blk, nlb = hin.shape[0], ATT_BLOCK, min(ATT_FWD_LANE_BLOCKS, d // LANES)
    width = nlb * LANES
    nd = d // width
    nq = t // blk
    scale = 1.0 / math.sqrt(HEAD_DIM)
    tri = jnp.tril(jnp.ones((blk, blk), F32)).astype(BF16)
    tri = jnp.concatenate([tri] * SUM_TERMS, axis=0)
    ns = len(shards)
    shapes = [s.shape for s in shards]
    heads = [(lb, half) for lb in range(nlb) for half in range(2)]

    def body(q_ref, k_ref, v_ref, tri_ref, *rest):
        shard_refs, (o_ref, tot_ref), full_refs, sems = rest[:ns], rest[ns:ns + 2], rest[ns + 2:2 * ns + 2], rest[2 * ns + 2:]
        p, i = pl.program_id(0), pl.program_id(1)
        start, forward, finish = _gather_phases(shard_refs, full_refs, kinds, shapes, *sems)
        pl.when((p == 0) & (i == 0))(start)
        pl.when((p == nd // 2) & (i == 0))(forward)
        masks = _head_masks()
        lanes = [slice(lb * LANES, (lb + 1) * LANES) for lb in range(nlb)]
        q = q_ref[...].astype(F32) * scale
        qh = [jnp.where(masks[half], q[:, lanes[lb]], 0.0).astype(BF16) for lb, half in heads]
        after = lax.broadcasted_iota(jnp.int32, (blk, blk), 0) > lax.broadcasted_iota(jnp.int32, (blk, blk), 1)
        tri_v = tri_ref[...]

        def scores(j, which):
            j0 = pl.multiple_of(j * blk, blk)
            kj = k_ref[pl.ds(j0, blk), :]
            return tuple(_dot(qh[h], kj[:, lanes[heads[h][0]]], NT) for h in which)

        early = tuple(h for h in range(len(heads)) if h % 2 == 0)
        late = tuple(h for h in range(len(heads)) if h % 2 == 1)

        def both(first, second):
            out = [None] * len(heads)
            for h, z in zip(early + late, first + second):
                out[h] = z
            return tuple(out)

        def weights(z, c, diagonal, between=lambda: None):
            halves = [slice(r, r + blk // 2) for r in range(0, blk, blk // 2)]
            other = between()
            sums, a = {}, {}

            def stage_sums(h):
                parts = []
                for rows in halves:
                    lf = _log_terms(z[h][rows])[1]
                    if diagonal:
                        lf = jnp.where(after[rows], lf, 0.0)
                    parts.append(_dot(_split_cat(lf, 1), tri_v, NN))
                sums[h] = jnp.concatenate(parts, axis=0)

            def stage_weights(h):
                ah = jnp.exp(z[h] + sums[h] + c[h])
                a[h] = (jnp.where(after, ah, 0.0) if diagonal else ah).astype(BF16)

            order = range(len(heads))
            for h in order:
                stage_sums(h)
            for h in order:
                stage_weights(h)
            return tuple(a[h] for h in order), tuple(c[h] + sums[h][:, 0:1] for h in order), other

        def apply(acc, a, j):
            j0 = pl.multiple_of(j * blk, blk)
            vj = v_ref[pl.ds(j0, blk), :]
            acc = list(acc)
            for h, (lb, half) in enumerate(heads):
                vh = vj[:, lanes[lb]]
                acc[lb] = acc[lb] + _dot(a[h], jnp.where(masks[half], vh, jnp.zeros_like(vh)), NN)
            return tuple(acc)

        zero = jnp.zeros((blk, 1), F32)
        a, c, _ = weights(both(scores(i, early), scores(i, late)), (zero,) * len(heads), True)

        def step(m, carry):
            z, a, c, acc = carry
            j = i - 1 - m
            a, c, (z_next, acc) = weights(both(z, scores(j, late)), c, False,
                                          lambda: (scores(jnp.maximum(j - 1, 0), early), apply(acc, a, j + 1)))
            return z_next, a, c, acc

        acc = (jnp.zeros((blk, LANES), F32),) * nlb
        _, a, c, acc = lax.fori_loop(0, i, step, (scores(jnp.maximum(i - 1, 0), early), a, c, acc))
        o_ref[...] = jnp.concatenate(apply(acc, a, 0), axis=1).astype(BF16)
        tot_ref[...] = jnp.concatenate([jnp.where(masks[0], c[2 * lb], c[2 * lb + 1]) for lb in range(nlb)], axis=1)
        pl.when((p == nd - 1) & (i == nq - 1))(finish)

    def whole(cb0):
        return pl.BlockSpec((t, width), functools.partial(lambda p, i, cb0: (0, cb0 + p), cb0=cb0))

    tile = pl.BlockSpec((blk, width), lambda p, i: (i, p))
    res = _pcall(
        body, name="attention_fwd", grid=(nd, nq),
        in_specs=[pl.BlockSpec((blk, width), lambda p, i: (i, 3 * nd + p)), whole(4 * nd), whole(5 * nd),
                  pl.BlockSpec((SUM_TERMS * blk, blk), lambda p, i: (0, 0))] + [HBM] * ns,
        out_specs=[tile, tile] + [HBM] * ns,
        out_shape=[jax.ShapeDtypeStruct((t, d), BF16), jax.ShapeDtypeStruct((t, d), F32)]
        + [jax.ShapeDtypeStruct(_full_shape(k, s), BF16) for k, s in zip(kinds, shapes)],
        scratch_shapes=_gather_scratch(ns),
        compiler_params=_params("arbitrary", "arbitrary"),
    )(hin, hin, hin, tri, *shards)
    return res[0], res[1], res[2:]


def _attention_bwd(hin, tot, do, d, parts):
    t, blk, nlb = hin.shape[0], ATT_BLOCK, min(ATT_BWD_LANE_BLOCKS, d // LANES)
    width = nlb * LANES
    nd = d // width
    nq = t // blk
    scale = 1.0 / math.sqrt(HEAD_DIM)
    upper = jnp.triu(jnp.ones((blk, blk), F32)).astype(BF16)
    upper = jnp.concatenate([upper] * SUM_TERMS, axis=1)
    lower = jnp.tril(jnp.ones((blk, blk), F32), -1).astype(BF16)
    npart = len(parts)
    heads = [(lb, half) for lb in range(nlb) for half in range(2)]
    nh = len(heads)

    def body(q_ref, k_ref, v_ref, tot_ref, do_ref, up_ref, low_ref, *rest):
        part_refs, (dq_ref, dk_ref, dv_ref) = rest[:npart], rest[npart:npart + 3]
        land_refs, (send_sems, recv_sems, dk_acc, dv_acc) = rest[npart + 3:2 * npart + 3], rest[2 * npart + 3:]
        p, i = pl.program_id(0), pl.program_id(1)
        start, finish = _exchange_phases(part_refs, land_refs, send_sems, recv_sems)
        pl.when((p == 0) & (i == 0))(start)
        masks = _head_masks()
        lanes = [slice(lb * LANES, (lb + 1) * LANES) for lb in range(nlb)]

        @pl.when(i == 0)
        def _():
            dk_acc[...] = jnp.zeros_like(dk_acc)
            dv_acc[...] = jnp.zeros_like(dv_acc)

        q = q_ref[...].astype(F32) * scale
        dout = do_ref[...]
        qh = [jnp.where(masks[half], q[:, lanes[lb]], 0.0).astype(BF16) for lb, half in heads]
        doh = [jnp.where(masks[half], dout[:, lanes[lb]], jnp.zeros((blk, LANES), BF16)) for lb, half in heads]
        lane = lax.broadcasted_iota(jnp.int32, (8, LANES), 1)
        totals = [lax.dot_general(jnp.where(lane == half * HEAD_DIM, 1.0, 0.0), tot_ref[:, lanes[lb]], (NT, ((), ())),
                                  precision=lax.Precision.HIGHEST, preferred_element_type=F32)[0:1, :] for lb, half in heads]
        after = lax.broadcasted_iota(jnp.int32, (blk, blk), 1) > lax.broadcasted_iota(jnp.int32, (blk, blk), 0)
        up, low = up_ref[...], low_ref[...]

        def products(j, which):
            j0 = pl.multiple_of(j * blk, blk)
            kj, vj = k_ref[pl.ds(j0, blk), :], v_ref[pl.ds(j0, blk), :]
            return tuple((_dot(kj[:, lanes[heads[h][0]]], qh[h], NT), _dot(vj[:, lanes[heads[h][0]]], doh[h], NT)) for h in which)

        early = tuple(h for h in range(nh) if h < 1)
        late = tuple(h for h in range(nh) if h >= 1)

        def both(first, second):
            out = [None] * nh
            for h, v in zip(early + late, first + second):
                out[h] = v
            return tuple(out)

        def stage_sums(z, diagonal):
            lb, lf = _log_terms(z)
            if diagonal:
                lf = jnp.where(after, lf, 0.0)
            return jnp.exp(lb).astype(BF16), _dot(up, _split_cat(lf, 0), NN)

        def stage_weights(z, da, sums, rest, diagonal):
            rest = rest - sums[0:1, :]
            a = jnp.exp(z + sums + rest)
            if diagonal:
                a = jnp.where(after, a, 0.0)
            g = da * a
            return a.astype(BF16), g, _dot(low, g.astype(BF16), NN), rest

        def stage_dz(g, sig, earlier, before, diagonal):
            dlf = earlier + before
            dz = g - sig.astype(F32) * (g + dlf)
            if diagonal:
                dz = jnp.where(after, dz, 0.0)
            return dz.astype(BF16), dlf[blk - 1:blk, :] + g[blk - 1:blk, :]

        def grads(cur, rest, before, diagonal, between=(lambda: None, lambda: None)):
            s1, s2, s3 = {}, {}, {}
            for h in early:
                s1[h] = stage_sums(cur[h][0], diagonal)
            first = between[0]()
            for h in early:
                s2[h] = stage_weights(*cur[h], s1[h][1], rest[h], diagonal)
            for h in late:
                s1[h] = stage_sums(cur[h][0], diagonal)
            second = between[1]()
            for h in early:
                s3[h] = stage_dz(s2[h][1], s1[h][0], s2[h][2], before[h], diagonal)
            for h in late:
                s2[h] = stage_weights(*cur[h], s1[h][1], rest[h], diagonal)
            for h in late:
                s3[h] = stage_dz(s2[h][1], s1[h][0], s2[h][2], before[h], diagonal)
            order = range(nh)
            return (tuple(s3[h][0] for h in order), tuple(s2[h][0] for h in order), tuple(s2[h][3] for h in order),
                    tuple(s3[h][1] for h in order), first, second)

        def flush(j, dz, a, dq):
            j0 = pl.multiple_of(j * blk, blk)
            kj = k_ref[pl.ds(j0, blk), :]
            dq = list(dq)
            dk_new = [jnp.zeros((blk, LANES), F32)] * nlb
            dv_new = [jnp.zeros((blk, LANES), F32)] * nlb
            for h, (lb, half) in enumerate(heads):
                kh = kj[:, lanes[lb]]
                dk_new[lb] = dk_new[lb] + _dot(dz[h], qh[h], NN)
                dv_new[lb] = dv_new[lb] + _dot(a[h], doh[h], NN)
                dq[lb] = dq[lb] + _dot(dz[h], jnp.where(masks[half], kh, jnp.zeros_like(kh)), TN)
            for lb in range(nlb):
                dk_acc[pl.ds(j0, blk), lanes[lb]] += dk_new[lb]
                dv_acc[pl.ds(j0, blk), lanes[lb]] += dv_new[lb]
            return tuple(dq)

        def step(j, carry):
            cur, dz, a, rest, before, dq = carry
            dz, a, rest, before, nxt, dq = grads(
                both(cur, products(j, late)), rest, before, False,
                (lambda: products(j + 1, early), lambda: flush(jnp.maximum(j - 1, 0), dz, a, dq)))
            return nxt, dz, a, rest, before, dq

        zero = jnp.zeros((1, blk), F32)
        nothing = (jnp.zeros((blk, blk), BF16),) * nh
        cur, dz, a, rest, before, dq = lax.fori_loop(
            0, i, step, (products(0, early), nothing, nothing, tuple(totals), (zero,) * nh, (jnp.zeros((blk, LANES), F32),) * nlb))
        dq = flush(jnp.maximum(i - 1, 0), dz, a, dq)
        dz, a = grads(both(cur, products(i, late)), rest, before, True)[:2]
        dq_ref[...] = (jnp.concatenate(flush(i, dz, a, dq), axis=1) * scale).astype(BF16)

        @pl.when(i == nq - 1)
        def _():
            dk_ref[...] = dk_acc[...].astype(BF16)
            dv_ref[...] = dv_acc[...].astype(BF16)

        pl.when((p == nd - 1) & (i == nq - 1))(finish)

    def whole(cb0):
        return pl.BlockSpec((t, width), functools.partial(lambda p, i, cb0: (0, cb0 + p), cb0=cb0))

    tile = pl.BlockSpec((blk, width), lambda p, i: (i, p))
    const = pl.BlockSpec((blk, blk), lambda p, i: (0, 0))
    act = jax.ShapeDtypeStruct((t, d), BF16)
    res = _pcall(
        body, name="attention_bwd", grid=(nd, nq),
        in_specs=[pl.BlockSpec((blk, width), lambda p, i: (i, 3 * nd + p)), whole(4 * nd), whole(5 * nd), tile, tile,
                  pl.BlockSpec((blk, SUM_TERMS * blk), lambda p, i: (0, 0)), const] + [HBM] * npart,
        out_specs=[tile, whole(0), whole(0)] + [HBM] * npart,
        out_shape=[act, act, act] + _exchange_shapes(parts),
        scratch_shapes=[pltpu.SemaphoreType.DMA((npart, 3)), pltpu.SemaphoreType.DMA((npart, 3)),
                        pltpu.VMEM((t, width), F32), pltpu.VMEM((t, width), F32)],
        compiler_params=_params("arbitrary", "arbitrary"),
    )(hin, hin, hin, tot, do, upper, lower, *parts)
    return res[0], res[1], res[2], res[3:]


def _adamw(w, g, m, v, *, name):
    r, c = w.shape
    tr = _tile(r, 256, 8) if r % 8 == 0 and r * c * 4 > (1 << 20) else r
    c1, c2 = 1.0 - ADAM_B1 ** ADAM_STEP, 1.0 - ADAM_B2 ** ADAM_STEP

    def body(w_ref, g_ref, m_ref, v_ref, d_ref, nm_ref, nv_ref):
        gv = g_ref[...]
        nm = ADAM_B1 * m_ref[...] + (1.0 - ADAM_B1) * gv
        nv = ADAM_B2 * v_ref[...] + (1.0 - ADAM_B2) * (gv * gv)
        d_ref[...] = -ADAM_LR * ((nm / c1) / (jnp.sqrt(nv / c2) + ADAM_EPS) + ADAM_WD * w_ref[...])
        nm_ref[...] = nm
        nv_ref[...] = nv

    spec = pl.BlockSpec((tr, c), lambda i: (i, 0))
    shape = jax.ShapeDtypeStruct((r, c), F32)
    return _pcall(
        body, name=name, grid=(r // tr,), in_specs=[spec] * 4, out_specs=[spec] * 3, out_shape=[shape] * 3,
        compiler_params=_params("arbitrary"),
    )(w, g, m, v)


HBM = pl.BlockSpec(memory_space=pltpu.HBM)


def _place():
    x, y, c = lax.axis_index("x"), lax.axis_index("y"), lax.axis_index("c")
    return x, y, c, [(1 - x, y), (x, 1 - y), (1 - x, 1 - y)]


def _window(ref, kind, shard_shape, j, half):
    r, w = shard_shape
    if kind == "col":
        return ref.at[pl.ds(half * (r // 2), r // 2), pl.ds(j * w, w)]
    return ref.at[pl.ds(j * r + half * (r // 2), r // 2), :]


def _full_shape(kind, shard_shape):
    r, w = shard_shape
    return (r, 4 * w) if kind == "col" else (4 * r, w)


def _remote(src, dst, send_sem, recv_sem, device):
    return pltpu.make_async_remote_copy(src_ref=src, dst_ref=dst, send_sem=send_sem, recv_sem=recv_sem,
                                        device_id=device, device_id_type=MESH)


def _gather_phases(big_in, big_out, kinds, shapes, send_sems, recv_sems, local_sems):
    x, y, c, chips = _place()
    me, sibling = 2 * x + y, (x, y, 1 - c)
    n = len(big_in)

    def local(a):
        r, w = shapes[a]
        own = big_out[a].at[:, pl.ds(me * w, w)] if kinds[a] == "col" else big_out[a].at[pl.ds(me * r, r), :]
        return pltpu.make_async_copy(big_in[a], own, local_sems.at[a])

    def over_ici(a, k):
        r = shapes[a][0]
        return _remote(big_in[a].at[pl.ds(c * (r // 2), r // 2), :], _window(big_out[a], kinds[a], shapes[a], me, c),
                       send_sems.at[a, k], recv_sems.at[a, k], (*chips[k], c))

    def landed(a, k, half, slot):
        win = _window(big_out[a], kinds[a], shapes[a], 2 * chips[k][0] + chips[k][1], half)
        return _remote(win, win, send_sems.at[a, slot], recv_sems.at[a, slot], sibling)

    def start():
        for a in range(n):
            local(a).start()
            for k in range(3):
                over_ici(a, k).start()

    def forward():
        for a in range(n):
            for k in range(3):
                landed(a, k, c, k).wait_recv()
                landed(a, k, c, 3 + k).start()

    def finish():
        for a in range(n):
            for k in range(3):
                landed(a, k, 1 - c, 3 + k).wait_recv()
        for a in range(n):
            for k in range(3):
                over_ici(a, k).wait_send()
                landed(a, k, c, 3 + k).wait_send()
            local(a).wait()

    return start, forward, finish


def _gather_scratch(n):
    return [pltpu.SemaphoreType.DMA((n, 6)), pltpu.SemaphoreType.DMA((n, 6)), pltpu.SemaphoreType.DMA((n,))]


def _gather_weights(bigs, kinds, smalls):
    nb, ns = len(bigs), len(smalls)
    shapes = [b.shape for b in bigs]

    def body(*refs):
        big_in, small_in = refs[:nb], refs[nb:nb + ns]
        big_out, small_out = refs[nb + ns:2 * nb + ns], refs[2 * nb + ns:2 * (nb + ns)]
        send_sems, recv_sems, local_sems, small_send, small_recv, small_local = refs[2 * (nb + ns):]
        x, y, c, chips = _place()
        me = 2 * x + y
        start, forward, finish = _gather_phases(big_in, big_out, kinds, shapes, send_sems, recv_sems, local_sems)
        start()
        local, sent = [], []
        for s in range(ns):
            local.append(pltpu.make_async_copy(small_in[s], small_out[s].at[me], small_local.at[s]))
            for k, chip in enumerate(chips):
                sent.append(_remote(small_in[s], small_out[s].at[me], small_send.at[s, k], small_recv.at[s, k], (*chip, c)))
        for cp in local + sent:
            cp.start()
        forward()
        for s in range(ns):
            for k, (px, py) in enumerate(chips):
                dst = small_out[s].at[2 * px + py]
                _remote(dst, dst, small_send.at[s, k], small_recv.at[s, k], (px, py, c)).wait_recv()
        finish()
        for cp in sent:
            cp.wait_send()
        for cp in local:
            cp.wait()

    out_shape = [jax.ShapeDtypeStruct(_full_shape(k, s), b.dtype) for b, k, s in zip(bigs, kinds, shapes)]
    out_shape += [jax.ShapeDtypeStruct((4, *s.shape), s.dtype) for s in smalls]
    return _pcall(
        body, name="gather_weights", in_specs=[HBM] * (nb + ns), out_specs=[HBM] * (nb + ns), out_shape=out_shape,
        scratch_shapes=_gather_scratch(nb) + [pltpu.SemaphoreType.DMA((ns, 3)), pltpu.SemaphoreType.DMA((ns, 3)), pltpu.SemaphoreType.DMA((ns,))],
    )(*bigs, *smalls)


def _swap_other_halves(grads, kinds, shapes, *, name):
    n = len(grads)

    def body(*refs):
        g_in, land = refs[:n], refs[n:2 * n]
        send_sems, recv_sems = refs[2 * n:]
        x, y, c, _ = _place()
        sibling = (x, y, 1 - c)
        sent = []
        for a in range(n):
            for j in range(4):
                cp = _remote(_window(g_in[a], kinds[a], shapes[a], j, 1 - c), land[a].at[j], send_sems.at[a, j], recv_sems.at[a, j], sibling)
                cp.start()
                sent.append(cp)
        for a in range(n):
            for j in range(4):
                _remote(land[a].at[j], land[a].at[j], send_sems.at[a, j], recv_sems.at[a, j], sibling).wait_recv()
        for cp in sent:
            cp.wait_send()

    return _pcall(
        body, name=name, in_specs=[HBM] * n, out_specs=[HBM] * n,
        out_shape=[jax.ShapeDtypeStruct((4, s[0] // 2, s[1]), F32) for s in shapes],
        scratch_shapes=[pltpu.SemaphoreType.DMA((n, 4)), pltpu.SemaphoreType.DMA((n, 4))],
    )(*grads)


def _pair_sum(place, g, land, kind, shard_shape, *, name):
    r, w = shard_shape
    hr = r // 2
    tr = _tile(hr, 256, 16)
    nr = hr // tr

    def body(place_ref, g_ref, l_ref, o_ref, own_ref):
        total = g_ref[...] + l_ref[...]
        o_ref[...] = total.astype(BF16)

        @pl.when(pl.program_id(1) == place_ref[1])
        def _():
            own_ref[...] = total

    if kind == "col":
        g_spec = pl.BlockSpec((tr, w), lambda i, j, p: (p[0] * nr + i, j))
    else:
        g_spec = pl.BlockSpec((tr, w), lambda i, j, p: ((2 * j + p[0]) * nr + i, 0))
    part = pl.BlockSpec((None, tr, w), lambda i, j, p: (j, i, 0))
    return _pcall(
        body, name=name,
        grid_spec=pltpu.PrefetchScalarGridSpec(num_scalar_prefetch=1, grid=(nr, 4), in_specs=[g_spec, part],
                                               out_specs=[part, pl.BlockSpec((tr, w), lambda i, j, p: (i, 0))]),
        out_shape=[jax.ShapeDtypeStruct((4, hr, w), BF16), jax.ShapeDtypeStruct((hr, w), F32)],
        compiler_params=_params("arbitrary", "arbitrary"),
    )(place, g, land)


def _exchange_phases(p_in, land, send_sems, recv_sems):
    x, y, c, chips = _place()

    def copy(a, k):
        px, py = chips[k]
        return _remote(p_in[a].at[2 * px + py], land[a].at[k], send_sems.at[a, k], recv_sems.at[a, k], (px, py, c))

    def start():
        for a in range(len(p_in)):
            for k in range(3):
                copy(a, k).start()

    def finish():
        for a in range(len(p_in)):
            for k in range(3):
                copy(a, k).wait()

    return start, finish


def _exchange_shapes(parts):
    return [jax.ShapeDtypeStruct((3, *p.shape[1:]), p.dtype) for p in parts]


def _chip_sum(place, own, land, *, name):
    hr, w = own.shape
    tr = _tile(hr, 256, 16)
    nr = hr // tr

    def body(place_ref, p_ref, l_ref, o_ref):
        o_ref[...] = ((p_ref[...] + l_ref[0].astype(F32)) + l_ref[1].astype(F32)) + l_ref[2].astype(F32)

    return _pcall(
        body, name=name,
        grid_spec=pltpu.PrefetchScalarGridSpec(
            num_scalar_prefetch=1, grid=(nr,),
            in_specs=[pl.BlockSpec((tr, w), lambda i, p: (i, 0)), pl.BlockSpec((3, tr, w), lambda i, p: (0, i, 0))],
            out_specs=pl.BlockSpec((tr, w), lambda i, p: (p[0] * nr + i, 0))),
        out_shape=jax.ShapeDtypeStruct((2 * hr, w), F32),
        compiler_params=_params("arbitrary"),
    )(place, own, land)


def _swap_reduced_halves(halves):
    n = len(halves)

    def body(*refs):
        src, out = refs[:n], refs[n:2 * n]
        send_sems, recv_sems = refs[2 * n:]
        x, y, c, _ = _place()
        sibling = (x, y, 1 - c)
        sent = []
        for a in range(n):
            hr = out[a].shape[0] // 2
            cp = _remote(src[a].at[pl.ds(c * hr, hr), :], out[a].at[pl.ds(c * hr, hr), :], send_sems.at[a], recv_sems.at[a], sibling)
            cp.start()
            sent.append(cp)
        for a in range(n):
            hr = out[a].shape[0] // 2
            other = out[a].at[pl.ds((1 - c) * hr, hr), :]
            _remote(other, other, send_sems.at[a], recv_sems.at[a], sibling).wait_recv()
        for cp in sent:
            cp.wait_send()

    return _pcall(
        body, name="swap_reduced_halves", in_specs=[HBM] * n, out_specs=[HBM] * n,
        out_shape=[jax.ShapeDtypeStruct(h.shape, F32) for h in halves],
        input_output_aliases={a: a for a in range(n)},
        scratch_shapes=[pltpu.SemaphoreType.DMA((n,)), pltpu.SemaphoreType.DMA((n,))],
    )(*halves)


def _sum_small(packed):
    rows = packed.shape[0]

    def body(in_ref, out_ref, land, send_sems, recv_sems, local_sem):
        x, y, c, _ = _place()
        me = 4 * x + 2 * y + c
        own = pltpu.make_async_copy(in_ref, land.at[me], local_sem)
        own.start()
        flips = [(r >> 2, (r >> 1) & 1, r & 1) for r in range(1, 8)]
        sent = []
        for k, (fx, fy, fc) in enumerate(flips):
            cp = _remote(in_ref, land.at[me], send_sems.at[k], recv_sems.at[k], (x ^ fx, y ^ fy, c ^ fc))
            cp.start()
            sent.append(cp)
        for k, (fx, fy, fc) in enumerate(flips):
            src = land.at[4 * (x ^ fx) + 2 * (y ^ fy) + (c ^ fc)]
            _remote(src, src, send_sems.at[k], recv_sems.at[k], (x ^ fx, y ^ fy, c ^ fc)).wait_recv()
        for cp in sent:
            cp.wait_send()
        own.wait()
        total = land[0]
        for dev in range(1, 8):
            total = total + land[dev]
        out_ref[...] = total

    vmem = pl.BlockSpec(memory_space=pltpu.VMEM)
    return _pcall(
        body, name="sum_small", in_specs=[vmem], out_specs=vmem, out_shape=jax.ShapeDtypeStruct(packed.shape, F32),
        scratch_shapes=[pltpu.VMEM((8, rows, LANES), F32), pltpu.SemaphoreType.DMA((7,)), pltpu.SemaphoreType.DMA((7,)), pltpu.SemaphoreType.DMA],
    )(packed)


def _sigmoid(v):
    return 1.0 / (1.0 + jnp.exp(-v))


def kernel(x, meta_tokens, g_pre_mix, w_in, conv_w_mix, w_proj_conv, w_proj_attn, b_gate, w_out, g_post_mix, g_pre_ffn, w_up_gate, conv_w_ffn, w_down, g_post_ffn, loss_target, m_meta_tokens, m_g_pre_mix, m_w_in, m_conv_w_mix, m_w_proj_conv, m_w_proj_attn, m_b_gate, m_w_out, m_g_post_mix, m_g_pre_ffn, m_w_up_gate, m_conv_w_ffn, m_w_down, m_g_post_ffn, v_meta_tokens, v_g_pre_mix, v_w_in, v_conv_w_mix, v_w_proj_conv, v_w_proj_attn, v_b_gate, v_w_out, v_g_post_mix, v_g_pre_ffn, v_w_up_gate, v_conv_w_ffn, v_w_down, v_g_post_ffn):
    seq, d = x.shape[1], x.shape[2]
    f = w_down.shape[1] * 4
    real = N_META + seq
    t = -(-real // ATT_BLOCK) * ATT_BLOCK
    nd = d // LANES
    cx, cy, cc = lax.axis_index("x"), lax.axis_index("y"), lax.axis_index("c")
    chip = 2 * cx + cy
    place = jnp.stack([cc, chip]).astype(jnp.int32)

    big_names = ["w_in", "w_proj_conv", "w_proj_attn", "w_out", "w_up_gate", "w_down"]
    kinds = ["col", "row", "row", "row", "col", "row"]
    big_w = [w_in[0], w_proj_conv[0], w_proj_attn[0], w_out[0], w_up_gate[0], w_down[0]]
    small_w = [meta_tokens, conv_w_mix[0], b_gate[0], conv_w_ffn[0]]
    shards = [w.astype(BF16) for w in big_w]
    shapes = [w.shape for w in big_w]
    gathered = _gather_weights(shards[:1], kinds[:1], small_w)
    wi = gathered[0]
    meta_f, cwm_f, bg_f, cwf_f = (jnp.moveaxis(s, 0, 1).reshape(s.shape[1], -1) for s in gathered[1:])

    h0 = jnp.concatenate([meta_f, x[0], jnp.zeros((t - real, d), F32)], axis=0)
    target = jnp.concatenate([jnp.zeros((N_META, d), F32), loss_target[0], jnp.zeros((t - real, d), F32)], axis=0)
    xn1 = _prenorm(h0, g_pre_mix, name="prenorm_mix")
    hin = _matmul(xn1, wi, name="in_proj")
    yconv = _mixer_conv_fwd(hin, cwm_f, d)
    o, att_tot, (wpc, wpa, wo, wug, wd) = _attention_fwd(hin, d, shards[1:], kinds[1:])

    def gates(gpre, bias):
        return _sigmoid(gpre[:, :d].astype(F32) + bias[0:1]), _sigmoid(gpre[:, d:].astype(F32) + bias[1:2])

    def merge_epi(accs, rows, vecs, row0):
        gate_c, gate_a = gates(rows[0], vecs[0])
        return [gate_c * accs[0] + gate_a * accs[1], accs[0], accs[1]]

    act = ("row", d, BF16)
    merged, bconv, battn = _matmul_rows([(yconv, wpc), (o, wpa)], [(hin, 2 * d, 3)], [bg_f], merge_epi, [act, act, act], name="branch_proj")

    def mix_epi(accs, rows, vecs, row0):
        mix = accs[0]
        h1 = rows[0] + mix * _rms(mix) * vecs[0]
        return [mix, h1, h1 * _rms(h1) * vecs[1]]

    mix, h1, xn2 = _matmul_rows([(merged, wo)], [(h0, d, 0)], [g_post_mix, g_pre_ffn], mix_epi,
                                [("row", d, F32), ("row", d, F32), act], name="out_proj")
    ug = _matmul(xn2, wug, name="up_proj")
    fact = _ffn_conv_fwd(ug, cwf_f, f)

    def loss_epi(accs, rows, vecs, row0):
        ffn = accs[0]
        r = _rms(ffn)
        fh = ffn * r
        h2 = rows[0] + fh * vecs[0]
        rid = row0 + lax.broadcasted_iota(jnp.int32, (ffn.shape[0], 1), 0)
        err = jnp.where((rid >= N_META) & (rid < real), h2 - rows[1], 0.0)
        dy = err * (1.0 / d)
        dffn, dg = _rms_bwd(dy, fh, r, vecs[0])
        loss = jnp.zeros((1, LANES), F32) + 0.5 * jnp.sum(err * err) / d
        return [dffn, dy, dg, loss]

    dffn, dh2, dg_post_ffn, loss_part = _matmul_rows(
        [(fact, wd)], [(h1, d, 0), (target, d, 0)], [g_post_ffn], loss_epi,
        [act, ("row", d, F32), ("acc", 1, d), ("acc", 1, LANES)], name="down_proj_loss")

    gw_down = _matmul_tn(fact, dffn, name="grad_w_down")
    df = _matmul(dffn, wd, nt=True, name="down_proj_bwd")
    du2, dg2, gcw_ffn = _ffn_conv_bwd(ug, df, cwf_f, f)

    def ffn_in_epi(accs, rows, vecs, row0):
        h1v, dyv, mixv = rows
        r3 = _rms(h1v)
        dh1n, dg3 = _rms_bwd(accs[0], h1v * r3, r3, vecs[0])
        dh1 = dyv + dh1n
        r2 = _rms(mixv)
        dmix, dgm = _rms_bwd(dh1, mixv * r2, r2, vecs[1])
        return [dh1, dmix, dg3, dgm]

    dh1, dmix, dg_pre_ffn, dg_post_mix, dug = _matmul_rows_joined(
        [du2, dg2], wug, [(h1, d, 0), (dh2, d, 0), (mix, d, 0)], [g_pre_ffn, g_post_mix], ffn_in_epi,
        [("row", d, F32), act, ("acc", 1, d), ("acc", 1, d)], tk=f // 2, name="up_proj_bwd")
    gw_up = _matmul_tn(xn2, dug, name="grad_w_up_gate")
    gw_out = _matmul_tn(merged, dmix, name="grad_w_out")

    def merge_bwd_epi(accs, rows, vecs, row0):
        dm = accs[0]
        gate_c, gate_a = gates(rows[0], vecs[0])
        dpre_c = dm * rows[1].astype(F32) * gate_c * (1.0 - gate_c)
        dpre_a = dm * rows[2].astype(F32) * gate_a * (1.0 - gate_a)
        dbias = jnp.concatenate([jnp.sum(dpre_c, axis=0, keepdims=True), jnp.sum(dpre_a, axis=0, keepdims=True)], axis=0)
        return [dm * gate_c, dm * gate_a, jnp.concatenate([dpre_c, dpre_a], axis=1), dbias]

    dbconv, dbattn, dgates, gb_gate = _matmul_rows(
        [(dmix, wo)], [(hin, 2 * d, 3), (bconv, d, 0), (battn, d, 0)], [bg_f], merge_bwd_epi,
        [act, act, ("row", 2 * d, BF16), ("acc", 2, d)], nt=True, name="out_proj_bwd")
    gw_pc = _matmul_tn(yconv, dbconv, name="grad_w_proj_conv")
    gw_pa = _matmul_tn(o, dbattn, name="grad_w_proj_attn")
    dyconv = _matmul(dbconv, wpc, nt=True, name="proj_conv_bwd")
    do = _matmul(dbattn, wpa, nt=True, name="proj_attn_bwd")
    db, dc, dhc, gcw_mix = _mixer_conv_bwd(hin, dyconv, cwm_f, d)

    def pair_sums(grads, first):
        landed = _swap_other_halves(grads, kinds[first:first + len(grads)], shapes[first:first + len(grads)],
                                    name=f"swap_other_halves_{big_names[first]}")
        return zip(*[_pair_sum(place, g, l, kinds[first + a], shapes[first + a], name=f"pair_sum_{big_names[first + a]}")
                     for a, (g, l) in enumerate(zip(grads, landed))])

    parts, own = pair_sums([gw_pc, gw_pa, gw_out, gw_up, gw_down], 1)
    dq, dk, dv, landed = _attention_bwd(hin, att_tot, do, d, parts)

    def in_epi(accs, rows, vecs, row0):
        h0v, dh1v = rows
        r1 = _rms(h0v)
        dh0n, dg1 = _rms_bwd(accs[0], h0v * r1, r1, vecs[0])
        return [dh1v + dh0n, dg1]

    dhin = [db, dc, dhc, dq, dk, dv, dgates]
    gw_in = _matmul_tn_segments(xn1, dhin, name="grad_w_in")
    parts_in, own_in = pair_sums([gw_in], 0)
    dh0, dg_pre_mix, landed_in = _matmul_rows_joined(
        dhin, wi, [(h0, d, 0), (dh1, d, 0)], [g_pre_mix], in_epi,
        [("row", d, F32), ("acc", 1, d)], tk=d, name="in_proj_bwd", joined=False, parts=parts_in)
    grad_x = dh0[N_META:real][None]

    landed = [landed_in] + list(landed)
    halves = [_chip_sum(place, p, l, name=f"chip_sum_{n}") for p, l, n in zip(list(own_in) + list(own), landed, big_names)]
    big_r = _swap_reduced_halves(halves)

    small_g = [dh0[:N_META], dg_pre_mix, gcw_mix, gb_gate, dg_post_mix, dg_pre_ffn, gcw_ffn, dg_post_ffn, loss_part]
    flat = jnp.concatenate([s.reshape(-1) for s in small_g])
    pad = -flat.shape[0] % (8 * LANES)
    summed = _sum_small(jnp.pad(flat, (0, pad)).reshape(-1, LANES)).reshape(-1)
    small_r, pos = [], 0
    for s in small_g:
        small_r.append(summed[pos:pos + s.size].reshape(s.shape))
        pos += s.size
    g_meta, g_g1, g_cwm, g_bg, g_g2, g_g3, g_cwf, g_g4, loss_row = small_r
    loss = loss_row[0, 0]

    def my_cols(full, width):
        return lax.dynamic_slice_in_dim(full, chip * width, width, axis=1)

    grads = {
        "meta_tokens": my_cols(g_meta, d // 4), "g_pre_mix": g_g1, "w_in": big_r[0][None],
        "conv_w_mix": my_cols(g_cwm, d // 4)[None], "w_proj_conv": big_r[1][None], "w_proj_attn": big_r[2][None],
        "b_gate": my_cols(g_bg, d // 4)[None], "w_out": big_r[3][None], "g_post_mix": g_g2, "g_pre_ffn": g_g3,
        "w_up_gate": big_r[4][None], "conv_w_ffn": my_cols(g_cwf, f // 4)[None], "w_down": big_r[5][None], "g_post_ffn": g_g4,
    }
    weights = dict(meta_tokens=meta_tokens, g_pre_mix=g_pre_mix, w_in=w_in, conv_w_mix=conv_w_mix, w_proj_conv=w_proj_conv,
                   w_proj_attn=w_proj_attn, b_gate=b_gate, w_out=w_out, g_post_mix=g_post_mix, g_pre_ffn=g_pre_ffn,
                   w_up_gate=w_up_gate, conv_w_ffn=conv_w_ffn, w_down=w_down, g_post_ffn=g_post_ffn)
    m_in = dict(meta_tokens=m_meta_tokens, g_pre_mix=m_g_pre_mix, w_in=m_w_in, conv_w_mix=m_conv_w_mix, w_proj_conv=m_w_proj_conv,
                w_proj_attn=m_w_proj_attn, b_gate=m_b_gate, w_out=m_w_out, g_post_mix=m_g_post_mix, g_pre_ffn=m_g_pre_ffn,
                w_up_gate=m_w_up_gate, conv_w_ffn=m_conv_w_ffn, w_down=m_w_down, g_post_ffn=m_g_post_ffn)
    v_in = dict(meta_tokens=v_meta_tokens, g_pre_mix=v_g_pre_mix, w_in=v_w_in, conv_w_mix=v_conv_w_mix, w_proj_conv=v_w_proj_conv,
                w_proj_attn=v_w_proj_attn, b_gate=v_b_gate, w_out=v_w_out, g_post_mix=v_g_post_mix, g_pre_ffn=v_g_pre_ffn,
                w_up_gate=v_w_up_gate, conv_w_ffn=v_conv_w_ffn, w_down=v_w_down, g_post_ffn=v_g_post_ffn)
    names = list(weights)
    deltas, new_m, new_v = [], [], []
    for n in names:
        shape = weights[n].shape
        two_d = (-1, shape[-1])
        dl, nm, nv = _adamw(weights[n].reshape(two_d), grads[n].reshape(two_d), m_in[n].reshape(two_d), v_in[n].reshape(two_d),
                            name=f"adamw_{n}")
        deltas.append(dl.reshape(shape))
        new_m.append(nm.reshape(shape))
        new_v.append(nv.reshape(shape))
    return (loss, grad_x, *[grads[n].reshape(weights[n].shape) for n in names], *deltas, *new_m, *new_v)
```

```python
import functools
import math

import jax
import jax.numpy as jnp
from jax import lax
from jax.experimental import pallas as pl
from jax.experimental.pallas import tpu as pltpu

F32 = jnp.float32
BF16 = jnp.bfloat16

N_META = 16
HEAD_DIM = 64
LANES = 128
RMS_EPS = 1e-6
ATT_BLOCK = 256
ATT_FWD_LANE_BLOCKS = 4
ATT_BWD_LANE_BLOCKS = 2
VMEM_LIMIT = 56 * 1024 * 1024

ADAM_LR = 0.001
ADAM_B1 = 0.9
ADAM_B2 = 0.999
ADAM_EPS = 1e-08
ADAM_WD = 0.01
ADAM_STEP = 10

MESH = pl.DeviceIdType.MESH


def _tile(n, cap, unit=LANES):
    d = (min(cap, n) // unit) * unit
    while d >= unit:
        if n % d == 0:
            return d
        d -= unit
    raise ValueError(f"no tile for {n} under {cap}")


MXU_EDGE = 256


def _mxu_tile(n, cap):
    if n <= cap:
        return n
    return _tile(n, cap, MXU_EDGE) if n % MXU_EDGE == 0 else _tile(n, cap)


def _params(*sem):
    return pltpu.CompilerParams(dimension_semantics=sem, vmem_limit_bytes=VMEM_LIMIT)


def _pcall(body, **kw):
    return pl.pallas_call(body, **kw)


def _dot(a, b, dims):
    return lax.dot_general(a, b, (dims, ((), ())), preferred_element_type=F32)


NN = ((1,), (0,))
NT = ((1,), (1,))
TN = ((0,), (0,))


def _matmul(a, b, *, nt=False, out_dtype=BF16, name):
    m, kdim = a.shape
    n = b.shape[0] if nt else b.shape[1]
    tm, tn, tk = _tile(m, 640, 16), _mxu_tile(n, 2816), _mxu_tile(kdim, 2816)
    nk = kdim // tk

    def body(a_ref, b_ref, o_ref, *scratch):
        p = _dot(a_ref[...], b_ref[...], NT if nt else NN)
        if nk == 1:
            o_ref[...] = p.astype(o_ref.dtype)
            return
        acc, k = scratch[0], pl.program_id(2)

        @pl.when(k == 0)
        def _():
            acc[...] = p

        @pl.when(k > 0)
        def _():
            acc[...] += p

        @pl.when(k == nk - 1)
        def _():
            o_ref[...] = acc[...].astype(o_ref.dtype)

    b_spec = pl.BlockSpec((tn, tk), lambda j, i, k: (j, k)) if nt else pl.BlockSpec((tk, tn), lambda j, i, k: (k, j))
    return _pcall(
        body, name=name, grid=(n // tn, m // tm, nk),
        in_specs=[pl.BlockSpec((tm, tk), lambda j, i, k: (i, k)), b_spec],
        out_specs=pl.BlockSpec((tm, tn), lambda j, i, k: (i, j)),
        out_shape=jax.ShapeDtypeStruct((m, n), out_dtype),
        scratch_shapes=[pltpu.VMEM((tm, tn), F32)] if nk > 1 else [],
        compiler_params=_params("arbitrary", "arbitrary", "arbitrary"),
    )(a, b)


def _matmul_tn(a, b, *, name):
    t, ka = a.shape
    nb = b.shape[1]
    tb = _mxu_tile(nb, 1024)
    budget = VMEM_LIMIT * 5 // 7
    ta = next(c for c in range(ka, 0, -LANES) if ka % c == 0 and 2 * (2 * t * (c + tb) + 4 * c * tb) <= budget)

    def body(a_ref, b_ref, o_ref):
        o_ref[...] = _dot(a_ref[...], b_ref[...], TN)

    return _pcall(
        body, name=name, grid=(ka // ta, nb // tb),
        in_specs=[pl.BlockSpec((t, ta), lambda i, j: (0, i)), pl.BlockSpec((t, tb), lambda i, j: (0, j))],
        out_specs=pl.BlockSpec((ta, tb), lambda i, j: (i, j)),
        out_shape=jax.ShapeDtypeStruct((ka, nb), F32),
        compiler_params=_params("arbitrary", "arbitrary"),
    )(a, b)


def _matmul_rows(pairs, rows, vecs, epi, outs, *, nt=False, name):
    m, kdim = pairs[0][0].shape
    n = pairs[0][1].shape[0] if nt else pairs[0][1].shape[1]
    tm, tk = _tile(m, 640, 16), _mxu_tile(kdim, 2816)
    nk, npair = kdim // tk, len(pairs)

    def body(*refs):
        a_refs, b_refs = refs[0:2 * npair:2], refs[1:2 * npair:2]
        pos = 2 * npair
        row_refs = refs[pos:pos + len(rows)]
        pos += len(rows)
        vec_refs = refs[pos:pos + len(vecs)]
        pos += len(vecs)
        out_refs = refs[pos:pos + len(outs)]
        accs = refs[pos + len(outs):]
        i, k = pl.program_id(0), pl.program_id(1)
        prods = [_dot(a[...], b[...], NT if nt else NN) for a, b in zip(a_refs, b_refs)]
        if nk > 1:
            @pl.when(k == 0)
            def _():
                for acc, p in zip(accs, prods):
                    acc[...] = p

            @pl.when(k > 0)
            def _():
                for acc, p in zip(accs, prods):
                    acc[...] += p

        @pl.when(k == nk - 1)
        def _():
            vals = [acc[...] for acc in accs] if nk > 1 else prods
            res = epi(vals, [r[...] for r in row_refs], [v[...] for v in vec_refs], i * tm)
            for o_ref, spec, val in zip(out_refs, outs, res):
                if spec[0] == "row":
                    o_ref[...] = val.astype(o_ref.dtype)
                else:
                    @pl.when(i == 0)
                    def _():
                        o_ref[...] = val

                    @pl.when(i > 0)
                    def _():
                        o_ref[...] += val

    in_specs, args = [], []
    for a, b in pairs:
        in_specs += [pl.BlockSpec((tm, tk), lambda i, k: (i, k)),
                     pl.BlockSpec((n, tk), lambda i, k: (0, k)) if nt else pl.BlockSpec((tk, n), lambda i, k: (k, 0))]
        args += [a, b]
    for arr, width, cb in rows:
        in_specs.append(pl.BlockSpec((tm, width), functools.partial(lambda i, k, cb: (i, cb), cb=cb)))
        args.append(arr)
    for v in vecs:
        in_specs.append(pl.BlockSpec(v.shape, lambda i, k: (0, 0)))
        args.append(v)
    out_specs, out_shape = [], []
    for spec in outs:
        if spec[0] == "row":
            out_specs.append(pl.BlockSpec((tm, spec[1]), lambda i, k: (i, 0)))
            out_shape.append(jax.ShapeDtypeStruct((m, spec[1]), spec[2]))
        else:
            out_specs.append(pl.BlockSpec((spec[1], spec[2]), lambda i, k: (0, 0)))
            out_shape.append(jax.ShapeDtypeStruct((spec[1], spec[2]), F32))
    return _pcall(
        body, name=name, grid=(m // tm, nk), in_specs=in_specs, out_specs=out_specs, out_shape=out_shape,
        scratch_shapes=[pltpu.VMEM((tm, n), F32) for _ in pairs] if nk > 1 else [],
        compiler_params=_params("arbitrary", "arbitrary"),
    )(*args)


def _matmul_rows_joined(segments, b, rows, vecs, epi, outs, *, tk, name, joined=True, parts=()):
    m = segments[0].shape[0]
    n, kdim = b.shape
    tm = _tile(m, 640, 16)
    nm, nk, npart = m // tm, kdim // tk, len(parts)
    first = [0]
    for s in segments:
        assert s.shape[1] % tk == 0
        first.append(first[-1] + s.shape[1] // tk)
    assert first[-1] == nk

    def body(*refs):
        seg_refs, b_ref = refs[:len(segments)], refs[len(segments)]
        pos = len(segments) + 1
        row_refs = refs[pos:pos + len(rows)]
        pos += len(rows)
        vec_refs = refs[pos:pos + len(vecs)]
        pos += len(vecs)
        part_refs = refs[pos:pos + npart]
        pos += npart
        out_refs = refs[pos:pos + len(outs)]
        pos += len(outs)
        joined_ref = refs[pos] if joined else None
        pos += int(joined)
        land_refs, acc = refs[pos:pos + npart], refs[pos + npart]
        i, k = pl.program_id(0), pl.program_id(1)
        if npart:
            start, finish = _exchange_phases(part_refs, land_refs, *refs[pos + npart + 1:])
            pl.when((i == 0) & (k == 0))(start)

        @pl.when(k == 0)
        def _():
            acc[...] = jnp.zeros_like(acc)

        for s, seg in enumerate(seg_refs):
            @pl.when((k >= first[s]) & (k < first[s + 1]))
            def _():
                if joined:
                    joined_ref[...] = seg[...]
                acc[...] += _dot(seg[...], b_ref[...], NT)

        @pl.when(k == nk - 1)
        def _():
            res = epi([acc[...]], [r[...] for r in row_refs], [v[...] for v in vec_refs], i * tm)
            for o_ref, spec, val in zip(out_refs, outs, res):
                if spec[0] == "row":
                    o_ref[...] = val.astype(o_ref.dtype)
                else:
                    @pl.when(i == 0)
                    def _():
                        o_ref[...] = val

                    @pl.when(i > 0)
                    def _():
                        o_ref[...] += val

        if npart:
            pl.when((i == nm - 1) & (k == nk - 1))(finish)

    def seg_spec(s):
        last = first[s + 1] - first[s] - 1
        return pl.BlockSpec((tm, tk), lambda i, k: (i, jnp.clip(k - first[s], 0, last)))

    in_specs = [seg_spec(s) for s in range(len(segments))] + [pl.BlockSpec((n, tk), lambda i, k: (0, k))]
    args = [*segments, b]
    for arr, width, cb in rows:
        in_specs.append(pl.BlockSpec((tm, width), functools.partial(lambda i, k, cb: (i, cb), cb=cb)))
        args.append(arr)
    for v in vecs:
        in_specs.append(pl.BlockSpec(v.shape, lambda i, k: (0, 0)))
        args.append(v)
    in_specs += [HBM] * npart
    args += list(parts)
    out_specs, out_shape = [], []
    for spec in outs:
        if spec[0] == "row":
            out_specs.append(pl.BlockSpec((tm, spec[1]), lambda i, k: (i, 0)))
            out_shape.append(jax.ShapeDtypeStruct((m, spec[1]), spec[2]))
        else:
            out_specs.append(pl.BlockSpec((spec[1], spec[2]), lambda i, k: (0, 0)))
            out_shape.append(jax.ShapeDtypeStruct((spec[1], spec[2]), F32))
    if joined:
        out_specs.append(pl.BlockSpec((tm, tk), lambda i, k: (i, k)))
        out_shape.append(jax.ShapeDtypeStruct((m, kdim), segments[0].dtype))
    out_specs += [HBM] * npart
    out_shape += _exchange_shapes(parts)
    sems = [pltpu.SemaphoreType.DMA((npart, 3)), pltpu.SemaphoreType.DMA((npart, 3))] if npart else []
    return _pcall(
        body, name=name, grid=(nm, nk), in_specs=in_specs, out_specs=out_specs, out_shape=out_shape,
        scratch_shapes=[pltpu.VMEM((tm, n), F32)] + sems,
        compiler_params=_params("arbitrary", "arbitrary"),
    )(*args)


def _matmul_tn_segments(a, segments, *, name):
    t, ka = a.shape
    tb = MXU_EDGE
    first = [0]
    for s in segments:
        assert s.shape[1] % tb == 0
        first.append(first[-1] + s.shape[1] // tb)
    budget = VMEM_LIMIT * 13 // 14
    ta = next(c for c in range(ka, 0, -LANES) if ka % c == 0 and 2 * (2 * t * (c + len(segments) * tb) + 4 * c * tb) <= budget)

    def body(a_ref, *refs):
        seg_refs, o_ref = refs[:-1], refs[-1]
        j = pl.program_id(1)
        for s, seg in enumerate(seg_refs):
            @pl.when((j >= first[s]) & (j < first[s + 1]))
            def _():
                o_ref[...] = _dot(a_ref[...], seg[...], TN)

    def seg_spec(s):
        last = first[s + 1] - first[s] - 1
        return pl.BlockSpec((t, tb), lambda i, j: (0, jnp.clip(j - first[s], 0, last)))

    return _pcall(
        body, name=name, grid=(ka // ta, first[-1]),
        in_specs=[pl.BlockSpec((t, ta), lambda i, j: (0, i))] + [seg_spec(s) for s in range(len(segments))],
        out_specs=pl.BlockSpec((ta, tb), lambda i, j: (i, j)),
        out_shape=jax.ShapeDtypeStruct((ka, first[-1] * tb), F32),
        compiler_params=_params("arbitrary", "arbitrary"),
    )(a, *segments)


def _rms(v):
    return lax.rsqrt(jnp.mean(v * v, axis=-1, keepdims=True) + RMS_EPS)


def _rms_bwd(dz, vhat, r, g):
    t = dz * g
    dv = r * (t - vhat * jnp.mean(t * vhat, axis=-1, keepdims=True))
    return dv, jnp.sum(dz * vhat, axis=0, keepdims=True)


def _prenorm(h, g, *, name):
    m, d = h.shape
    tm = _tile(m, 640, 16)

    def body(h_ref, g_ref, o_ref):
        v = h_ref[...]
        o_ref[...] = (v * _rms(v) * g_ref[...]).astype(BF16)

    return _pcall(
        body, name=name, grid=(m // tm,),
        in_specs=[pl.BlockSpec((tm, d), lambda i: (i, 0)), pl.BlockSpec((1, d), lambda i: (0, 0))],
        out_specs=pl.BlockSpec((tm, d), lambda i: (i, 0)),
        out_shape=jax.ShapeDtypeStruct((m, d), BF16),
        compiler_params=_params("arbitrary"),
    )(h, g)


SUBLANES = 8


def _shift_down(u, k):
    r = pltpu.roll(u, k, 0)
    rows = lax.broadcasted_iota(jnp.int32, (SUBLANES, u.shape[1]), 0)
    return jnp.concatenate([jnp.where(rows >= k, r[:SUBLANES], 0.0), r[SUBLANES:]], axis=0)


def _shift_up(u, k):
    n = u.shape[0]
    r = pltpu.roll(u, n - k, 0)
    rows = lax.broadcasted_iota(jnp.int32, (SUBLANES, u.shape[1]), 0)
    return jnp.concatenate([r[:n - SUBLANES], jnp.where(rows < SUBLANES - k, r[n - SUBLANES:], 0.0)], axis=0)


def _taps(u):
    return _shift_down(u, 2), _shift_down(u, 1), u


def _conv(taps, w):
    return w[0:1] * taps[0] + w[1:2] * taps[1] + w[2:3] * taps[2]


def _conv_bwd(dcu, taps, w):
    du = w[2:3] * dcu + w[1:2] * _shift_up(dcu, 1) + w[0:1] * _shift_up(dcu, 2)
    return du, [jnp.sum(dcu * tap, axis=0, keepdims=True) for tap in taps]


def _strip(arr, t, cb0):
    return pl.BlockSpec((t, LANES), functools.partial(lambda s, cb0: (0, cb0 + s), cb0=cb0))


def _mixer_conv_fwd(hin, w, d):
    t, ns = hin.shape[0], d // LANES

    def body(b_ref, c_ref, h_ref, w_ref, y_ref):
        u = c_ref[...].astype(F32) * h_ref[...].astype(F32)
        y_ref[...] = (b_ref[...].astype(F32) * _conv(_taps(u), w_ref[...])).astype(BF16)

    return _pcall(
        body, name="mixer_conv_fwd", grid=(ns,),
        in_specs=[_strip(hin, t, 0), _strip(hin, t, ns), _strip(hin, t, 2 * ns), pl.BlockSpec((3, LANES), lambda s: (0, s))],
        out_specs=pl.BlockSpec((t, LANES), lambda s: (0, s)),
        out_shape=jax.ShapeDtypeStruct((t, d), BF16),
        compiler_params=_params("arbitrary"),
    )(hin, hin, hin, w)


def _mixer_conv_bwd(hin, dy, w, d):
    t, ns = hin.shape[0], d // LANES

    def body(b_ref, c_ref, h_ref, dy_ref, w_ref, db_ref, dc_ref, dh_ref, dw_ref):
        b, c, h, g = (r[...].astype(F32) for r in (b_ref, c_ref, h_ref, dy_ref))
        wv = w_ref[...]
        taps = _taps(c * h)
        db_ref[...] = (g * _conv(taps, wv)).astype(BF16)
        du, dw = _conv_bwd(g * b, taps, wv)
        dc_ref[...] = (du * h).astype(BF16)
        dh_ref[...] = (du * c).astype(BF16)
        for k in range(3):
            dw_ref[k:k + 1, :] = dw[k]

    col = pl.BlockSpec((t, LANES), lambda s: (0, s))
    act = jax.ShapeDtypeStruct((t, d), BF16)
    return _pcall(
        body, name="mixer_conv_bwd", grid=(ns,),
        in_specs=[_strip(hin, t, 0), _strip(hin, t, ns), _strip(hin, t, 2 * ns), col, pl.BlockSpec((3, LANES), lambda s: (0, s))],
        out_specs=[col, col, col, pl.BlockSpec((3, LANES), lambda s: (0, s))],
        out_shape=[act, act, act, jax.ShapeDtypeStruct((3, d), F32)],
        compiler_params=_params("arbitrary"),
    )(hin, hin, hin, dy, w)


GELU_C = math.sqrt(2.0 / math.pi)
GELU_A = 0.044715


def _gelu_tanh(x):
    return jnp.tanh(GELU_C * (x + GELU_A * x * x * x))


def _ffn_conv_fwd(ug, w, f):
    t, ns = ug.shape[0], f // LANES

    def body(u_ref, g_ref, w_ref, o_ref):
        cu = _conv(_taps(u_ref[...].astype(F32)), w_ref[...])
        o_ref[...] = (0.5 * cu * (1.0 + _gelu_tanh(cu)) * g_ref[...].astype(F32)).astype(BF16)

    return _pcall(
        body, name="ffn_conv_fwd", grid=(ns,),
        in_specs=[_strip(ug, t, 0), _strip(ug, t, ns), pl.BlockSpec((3, LANES), lambda s: (0, s))],
        out_specs=pl.BlockSpec((t, LANES), lambda s: (0, s)),
        out_shape=jax.ShapeDtypeStruct((t, f), BF16),
        compiler_params=_params("arbitrary"),
    )(ug, ug, w)


def _ffn_conv_bwd(ug, df, w, f):
    t, ns = ug.shape[0], f // LANES

    def body(u_ref, g_ref, df_ref, w_ref, du_ref, dg_ref, dw_ref):
        u, g, d = (r[...].astype(F32) for r in (u_ref, g_ref, df_ref))
        wv = w_ref[...]
        taps = _taps(u)
        cu = _conv(taps, wv)
        th = _gelu_tanh(cu)
        half = 0.5 * (1.0 + th)
        dg_ref[...] = (d * (cu * half)).astype(BF16)
        dgelu = half + (0.5 * GELU_C) * cu * (1.0 - th * th) * (1.0 + (3.0 * GELU_A) * (cu * cu))
        du, dw = _conv_bwd(d * g * dgelu, taps, wv)
        du_ref[...] = du.astype(BF16)
        for k in range(3):
            dw_ref[k:k + 1, :] = dw[k]

    col = pl.BlockSpec((t, LANES), lambda s: (0, s))
    act = jax.ShapeDtypeStruct((t, f), BF16)
    return _pcall(
        body, name="ffn_conv_bwd", grid=(ns,),
        in_specs=[_strip(ug, t, 0), _strip(ug, t, ns), col, pl.BlockSpec((3, LANES), lambda s: (0, s))],
        out_specs=[col, col, pl.BlockSpec((3, LANES), lambda s: (0, s))],
        out_shape=[act, act, jax.ShapeDtypeStruct((3, f), F32)],
        compiler_params=_params("arbitrary"),
    )(ug, ug, df, w)


def _log_terms(z):
    minus_abs = lax.bitcast_convert_type(lax.bitcast_convert_type(z, jnp.uint32) | jnp.uint32(0x80000000), F32)
    lb = jnp.minimum(z, 0.0) - jnp.log(1.0 + jnp.exp(minus_abs))
    return lb, lb - z


SUM_TERMS = 1


def _split_cat(v, axis):
    terms = [v.astype(BF16)]
    for _ in range(SUM_TERMS - 1):
        v = v - terms[-1].astype(F32)
        terms.append(v.astype(BF16))
    return terms[0] if SUM_TERMS == 1 else jnp.concatenate(terms, axis=axis)


def _head_masks():
    lane = lax.broadcasted_iota(jnp.int32, (1, LANES), 1)
    return lane < HEAD_DIM, lane >= HEAD_DIM


def _attention_fwd(hin, d, shards, kinds):
    t, blk, nlb = hin.shape[0], ATT_BLOCK, min(ATT_FWD_LANE_BLOCKS, d // LANES)
    width = nlb * LANES
    nd = d // width
    nq = t // blk
    scale = 1.0 / math.sqrt(HEAD_DIM)
    tri = jnp.tril(jnp.ones((blk, blk), F32)).astype(BF16)
    tri = jnp.concatenate([tri] * SUM_TERMS, axis=0)
    ns = len(shards)
    shapes = [s.shape for s in shards]
    heads = [(lb, half) for lb in range(nlb) for half in range(2)]

    def body(q_ref, k_ref, v_ref, tri_ref, *rest):
        shard_refs, (o_ref, tot_ref), full_refs, sems = rest[:ns], rest[ns:ns + 2], rest[ns + 2:2 * ns + 2], rest[2 * ns + 2:]
        p, i = pl.program_id(0), pl.program_id(1)
        start, forward, finish = _gather_phases(shard_refs, full_refs, kinds, shapes, *sems)
        pl.when((p == 0) & (i == 0))(start)
        pl.when((p == nd // 2) & (i == 0))(forward)
        masks = _head_masks()
        lanes = [slice(lb * LANES, (lb + 1) * LANES) for lb in range(nlb)]
        q = q_ref[...].astype(F32) * scale
        qh = [jnp.where(masks[half], q[:, lanes[lb]], 0.0).astype(BF16) for lb, half in heads]
        after = lax.broadcasted_iota(jnp.int32, (blk, blk), 0) > lax.broadcasted_iota(jnp.int32, (blk, blk), 1)
        tri_v = tri_ref[...]

        def scores(j, which):
            j0 = pl.multiple_of(j * blk, blk)
            kj = k_ref[pl.ds(j0, blk), :]
            return tuple(_dot(qh[h], kj[:, lanes[heads[h][0]]], NT) for h in which)

        early = tuple(h for h in range(len(heads)) if h < 0)
        late = tuple(h for h in range(len(heads)) if h >= 0)

        def both(first, second):
            out = [None] * len(heads)
            for h, z in zip(early + late, first + second):
                out[h] = z
            return tuple(out)

        def weights(z, c, diagonal, between=lambda: None):
            halves = [slice(r, r + blk // 2) for r in range(0, blk, blk // 2)]
            other = between()
            sums, a = {}, {}

            def stage_sums(h):
                parts = []
                for rows in halves:
                    lf = _log_terms(z[h][rows])[1]
                    if diagonal:
                        lf = jnp.where(after[rows], lf, 0.0)
                    parts.append(_dot(_split_cat(lf, 1), tri_v, NN))
                sums[h] = jnp.concatenate(parts, axis=0)

            def stage_weights(h):
                ah = jnp.exp(z[h] + sums[h] + c[h])
                a[h] = (jnp.where(after, ah, 0.0) if diagonal else ah).astype(BF16)

            order = range(len(heads))
            for h in order:
                stage_sums(h)
            for h in order:
                stage_weights(h)
            return tuple(a[h] for h in order), tuple(c[h] + sums[h][:, 0:1] for h in order), other

        def apply(acc, a, j):
            j0 = pl.multiple_of(j * blk, blk)
            vj = v_ref[pl.ds(j0, blk), :]
            acc = list(acc)
            for h, (lb, half) in enumerate(heads):
                vh = vj[:, lanes[lb]]
                acc[lb] = acc[lb] + _dot(a[h], jnp.where(masks[half], vh, jnp.zeros_like(vh)), NN)
            return tuple(acc)

        zero = jnp.zeros((blk, 1), F32)
        a, c, _ = weights(both(scores(i, early), scores(i, late)), (zero,) * len(heads), True)

        def step(m, carry):
            z, a, c, acc = carry
            j = i - 1 - m
            a, c, (z_next, acc) = weights(both(z, scores(j, late)), c, False,
                                          lambda: (scores(jnp.maximum(j - 1, 0), early), apply(acc, a, j + 1)))
            return z_next, a, c, acc

        acc = (jnp.zeros((blk, LANES), F32),) * nlb
        _, a, c, acc = lax.fori_loop(0, i, step, (scores(jnp.maximum(i - 1, 0), early), a, c, acc))
        o_ref[...] = jnp.concatenate(apply(acc, a, 0), axis=1).astype(BF16)
        tot_ref[...] = jnp.concatenate([jnp.where(masks[0], c[2 * lb], c[2 * lb + 1]) for lb in range(nlb)], axis=1)
        pl.when((p == nd - 1) & (i == nq - 1))(finish)

    def whole(cb0):
        return pl.BlockSpec((t, width), functools.partial(lambda p, i, cb0: (0, cb0 + p), cb0=cb0))

    tile = pl.BlockSpec((blk, width), lambda p, i: (i, p))
    res = _pcall(
        body, name="attention_fwd", grid=(nd, nq),
        in_specs=[pl.BlockSpec((blk, width), lambda p, i: (i, 3 * nd + p)), whole(4 * nd), whole(5 * nd),
                  pl.BlockSpec((SUM_TERMS * blk, blk), lambda p, i: (0, 0))] + [HBM] * ns,
        out_specs=[tile, tile] + [HBM] * ns,
        out_shape=[jax.ShapeDtypeStruct((t, d), BF16), jax.ShapeDtypeStruct((t, d), F32)]
        + [jax.ShapeDtypeStruct(_full_shape(k, s), BF16) for k, s in zip(kinds, shapes)],
        scratch_shapes=_gather_scratch(ns),
        compiler_params=_params("arbitrary", "arbitrary"),
    )(hin, hin, hin, tri, *shards)
    return res[0], res[1], res[2:]


def _attention_bwd(hin, tot, do, d, parts):
    t, blk, nlb = hin.shape[0], ATT_BLOCK, min(ATT_BWD_LANE_BLOCKS, d // LANES)
    width = nlb * LANES
    nd = d // width
    nq = t // blk
    scale = 1.0 / math.sqrt(HEAD_DIM)
    upper = jnp.triu(jnp.ones((blk, blk), F32)).astype(BF16)
    upper = jnp.concatenate([upper] * SUM_TERMS, axis=1)
    lower = jnp.tril(jnp.ones((blk, blk), F32), -1).astype(BF16)
    npart = len(parts)
    heads = [(lb, half) for lb in range(nlb) for half in range(2)]
    nh = len(heads)

    def body(q_ref, k_ref, v_ref, tot_ref, do_ref, up_ref, low_ref, *rest):
        part_refs, (dq_ref, dk_ref, dv_ref) = rest[:npart], rest[npart:npart + 3]
        land_refs, (send_sems, recv_sems, dk_acc, dv_acc) = rest[npart + 3:2 * npart + 3], rest[2 * npart + 3:]
        p, i = pl.program_id(0), pl.program_id(1)
        start, finish = _exchange_phases(part_refs, land_refs, send_sems, recv_sems)
        pl.when((p == 0) & (i == 0))(start)
        masks = _head_masks()
        lanes = [slice(lb * LANES, (lb + 1) * LANES) for lb in range(nlb)]

        @pl.when(i == 0)
        def _():
            dk_acc[...] = jnp.zeros_like(dk_acc)
            dv_acc[...] = jnp.zeros_like(dv_acc)

        q = q_ref[...].astype(F32) * scale
        dout = do_ref[...]
        qh = [jnp.where(masks[half], q[:, lanes[lb]], 0.0).astype(BF16) for lb, half in heads]
        doh = [jnp.where(masks[half], dout[:, lanes[lb]], jnp.zeros((blk, LANES), BF16)) for lb, half in heads]
        lane = lax.broadcasted_iota(jnp.int32, (8, LANES), 1)
        totals = [lax.dot_general(jnp.where(lane == half * HEAD_DIM, 1.0, 0.0), tot_ref[:, lanes[lb]], (NT, ((), ())),
                                  precision=lax.Precision.HIGHEST, preferred_element_type=F32)[0:1, :] for lb, half in heads]
        after = lax.broadcasted_iota(jnp.int32, (blk, blk), 1) > lax.broadcasted_iota(jnp.int32, (blk, blk), 0)
        up, low = up_ref[...], low_ref[...]

        def products(j, which):
            j0 = pl.multiple_of(j * blk, blk)
            kj, vj = k_ref[pl.ds(j0, blk), :], v_ref[pl.ds(j0, blk), :]
            return tuple((_dot(kj[:, lanes[heads[h][0]]], qh[h], NT), _dot(vj[:, lanes[heads[h][0]]], doh[h], NT)) for h in which)

        early = tuple(h for h in range(nh) if h < 0)
        late = tuple(h for h in range(nh) if h >= 0)

        def both(first, second):
            out = [None] * nh
            for h, v in zip(early + late, first + second):
                out[h] = v
            return tuple(out)

        def stage_sums(z, diagonal):
            lb, lf = _log_terms(z)
            if diagonal:
                lf = jnp.where(after, lf, 0.0)
            return jnp.exp(lb).astype(BF16), _dot(up, _split_cat(lf, 0), NN)

        def stage_weights(z, da, sums, rest, diagonal):
            rest = rest - sums[0:1, :]
            a = jnp.exp(z + sums + rest)
            if diagonal:
                a = jnp.where(after, a, 0.0)
            g = da * a
            return a.astype(BF16), g, _dot(low, g.astype(BF16), NN), rest

        def stage_dz(g, sig, earlier, before, diagonal):
            dlf = earlier + before
            dz = g - sig.astype(F32) * (g + dlf)
            if diagonal:
                dz = jnp.where(after, dz, 0.0)
            return dz.astype(BF16), dlf[blk - 1:blk, :] + g[blk - 1:blk, :]

        def grads(cur, rest, before, diagonal, between=(lambda: None, lambda: None)):
            s1, s2, s3 = {}, {}, {}
            for h in early:
                s1[h] = stage_sums(cur[h][0], diagonal)
            first = between[0]()
            for h in early:
                s2[h] = stage_weights(*cur[h], s1[h][1], rest[h], diagonal)
            for h in late:
                s1[h] = stage_sums(cur[h][0], diagonal)
            second = between[1]()
            for h in early:
                s3[h] = stage_dz(s2[h][1], s1[h][0], s2[h][2], before[h], diagonal)
            for h in late:
                s2[h] = stage_weights(*cur[h], s1[h][1], rest[h], diagonal)
            for h in late:
                s3[h] = stage_dz(s2[h][1], s1[h][0], s2[h][2], before[h], diagonal)
            order = range(nh)
            return (tuple(s3[h][0] for h in order), tuple(s2[h][0] for h in order), tuple(s2[h][3] for h in order),
                    tuple(s3[h][1] for h in order), first, second)

        def flush(j, dz, a, dq):
            j0 = pl.multiple_of(j * blk, blk)
            kj = k_ref[pl.ds(j0, blk), :]
            dq = list(dq)
            dk_new = [jnp.zeros((blk, LANES), F32)] * nlb
            dv_new = [jnp.zeros((blk, LANES), F32)] * nlb
            for h, (lb, half) in enumerate(heads):
                kh = kj[:, lanes[lb]]
                dk_new[lb] = dk_new[lb] + _dot(dz[h], qh[h], NN)
                dv_new[lb] = dv_new[lb] + _dot(a[h], doh[h], NN)
                dq[lb] = dq[lb] + _dot(dz[h], jnp.where(masks[half], kh, jnp.zeros_like(kh)), TN)
            for lb in range(nlb):
                dk_acc[pl.ds(j0, blk), lanes[lb]] += dk_new[lb]
                dv_acc[pl.ds(j0, blk), lanes[lb]] += dv_new[lb]
            return tuple(dq)

        def step(j, carry):
            cur, dz, a, rest, before, dq = carry
            dz, a, rest, before, nxt, dq = grads(
                both(cur, products(j, late)), rest, before, False,
                (lambda: products(j + 1, early), lambda: flush(jnp.maximum(j - 1, 0), dz, a, dq)))
            return nxt, dz, a, rest, before, dq

        zero = jnp.zeros((1, blk), F32)
        nothing = (jnp.zeros((blk, blk), BF16),) * nh
        cur, dz, a, rest, before, dq = lax.fori_loop(
            0, i, step, (products(0, early), nothing, nothing, tuple(totals), (zero,) * nh, (jnp.zeros((blk, LANES), F32),) * nlb))
        dq = flush(jnp.maximum(i - 1, 0), dz, a, dq)
        dz, a = grads(both(cur, products(i, late)), rest, before, True)[:2]
        dq_ref[...] = (jnp.concatenate(flush(i, dz, a, dq), axis=1) * scale).astype(BF16)

        @pl.when(i == nq - 1)
        def _():
            dk_ref[...] = dk_acc[...].astype(BF16)
            dv_ref[...] = dv_acc[...].astype(BF16)

        pl.when((p == nd - 1) & (i == nq - 1))(finish)

    def whole(cb0):
        return pl.BlockSpec((t, width), functools.partial(lambda p, i, cb0: (0, cb0 + p), cb0=cb0))

    tile = pl.BlockSpec((blk, width), lambda p, i: (i, p))
    const = pl.BlockSpec((blk, blk), lambda p, i: (0, 0))
    act = jax.ShapeDtypeStruct((t, d), BF16)
    res = _pcall(
        body, name="attention_bwd", grid=(nd, nq),
        in_specs=[pl.BlockSpec((blk, width), lambda p, i: (i, 3 * nd + p)), whole(4 * nd), whole(5 * nd), tile, tile,
                  pl.BlockSpec((blk, SUM_TERMS * blk), lambda p, i: (0, 0)), const] + [HBM] * npart,
        out_specs=[tile, whole(0), whole(0)] + [HBM] * npart,
        out_shape=[act, act, act] + _exchange_shapes(parts),
        scratch_shapes=[pltpu.SemaphoreType.DMA((npart, 3)), pltpu.SemaphoreType.DMA((npart, 3)),
                        pltpu.VMEM((t, width), F32), pltpu.VMEM((t, width), F32)],
        compiler_params=_params("arbitrary", "arbitrary"),
    )(hin, hin, hin, tot, do, upper, lower, *parts)
    return res[0], res[1], res[2], res[3:]


def _adamw(w, g, m, v, *, name):
    r, c = w.shape
    tr = _tile(r, 256, 8) if r % 8 == 0 and r * c * 4 > (1 << 20) else r
    c1, c2 = 1.0 - ADAM_B1 ** ADAM_STEP, 1.0 - ADAM_B2 ** ADAM_STEP

    def body(w_ref, g_ref, m_ref, v_ref, d_ref, nm_ref, nv_ref):
        gv = g_ref[...]
        nm = ADAM_B1 * m_ref[...] + (1.0 - ADAM_B1) * gv
        nv = ADAM_B2 * v_ref[...] + (1.0 - ADAM_B2) * (gv * gv)
        d_ref[...] = -ADAM_LR * ((nm / c1) / (jnp.sqrt(nv / c2) + ADAM_EPS) + ADAM_WD * w_ref[...])
        nm_ref[...] = nm
        nv_ref[...] = nv

    spec = pl.BlockSpec((tr, c), lambda i: (i, 0))
    shape = jax.ShapeDtypeStruct((r, c), F32)
    return _pcall(
        body, name=name, grid=(r // tr,), in_specs=[spec] * 4, out_specs=[spec] * 3, out_shape=[shape] * 3,
        compiler_params=_params("arbitrary"),
    )(w, g, m, v)


HBM = pl.BlockSpec(memory_space=pltpu.HBM)


def _place():
    x, y, c = lax.axis_index("x"), lax.axis_index("y"), lax.axis_index("c")
    return x, y, c, [(1 - x, y), (x, 1 - y), (1 - x, 1 - y)]


def _window(ref, kind, shard_shape, j, half):
    r, w = shard_shape
    if kind == "col":
        return ref.at[pl.ds(half * (r // 2), r // 2), pl.ds(j * w, w)]
    return ref.at[pl.ds(j * r + half * (r // 2), r // 2), :]


def _full_shape(kind, shard_shape):
    r, w = shard_shape
    return (r, 4 * w) if kind == "col" else (4 * r, w)


def _remote(src, dst, send_sem, recv_sem, device):
    return pltpu.make_async_remote_copy(src_ref=src, dst_ref=dst, send_sem=send_sem, recv_sem=recv_sem,
                                        device_id=device, device_id_type=MESH)


def _gather_phases(big_in, big_out, kinds, shapes, send_sems, recv_sems, local_sems):
    x, y, c, chips = _place()
    me, sibling = 2 * x + y, (x, y, 1 - c)
    n = len(big_in)

    def local(a):
        r, w = shapes[a]
        own = big_out[a].at[:, pl.ds(me * w, w)] if kinds[a] == "col" else big_out[a].at[pl.ds(me * r, r), :]
        return pltpu.make_async_copy(big_in[a], own, local_sems.at[a])

    def over_ici(a, k):
        r = shapes[a][0]
        return _remote(big_in[a].at[pl.ds(c * (r // 2), r // 2), :], _window(big_out[a], kinds[a], shapes[a], me, c),
                       send_sems.at[a, k], recv_sems.at[a, k], (*chips[k], c))

    def landed(a, k, half, slot):
        win = _window(big_out[a], kinds[a], shapes[a], 2 * chips[k][0] + chips[k][1], half)
        return _remote(win, win, send_sems.at[a, slot], recv_sems.at[a, slot], sibling)

    def start():
        for a in range(n):
            local(a).start()
            for k in range(3):
                over_ici(a, k).start()

    def forward():
        for a in range(n):
            for k in range(3):
                landed(a, k, c, k).wait_recv()
                landed(a, k, c, 3 + k).start()

    def finish():
        for a in range(n):
            for k in range(3):
                landed(a, k, 1 - c, 3 + k).wait_recv()
        for a in range(n):
            for k in range(3):
                over_ici(a, k).wait_send()
                landed(a, k, c, 3 + k).wait_send()
            local(a).wait()

    return start, forward, finish


def _gather_scratch(n):
    return [pltpu.SemaphoreType.DMA((n, 6)), pltpu.SemaphoreType.DMA((n, 6)), pltpu.SemaphoreType.DMA((n,))]


def _gather_weights(bigs, kinds, smalls):
    nb, ns = len(bigs), len(smalls)
    shapes = [b.shape for b in bigs]

    def body(*refs):
        big_in, small_in = refs[:nb], refs[nb:nb + ns]
        big_out, small_out = refs[nb + ns:2 * nb + ns], refs[2 * nb + ns:2 * (nb + ns)]
        send_sems, recv_sems, local_sems, small_send, small_recv, small_local = refs[2 * (nb + ns):]
        x, y, c, chips = _place()
        me = 2 * x + y
        start, forward, finish = _gather_phases(big_in, big_out, kinds, shapes, send_sems, recv_sems, local_sems)
        start()
        local, sent = [], []
        for s in range(ns):
            local.append(pltpu.make_async_copy(small_in[s], small_out[s].at[me], small_local.at[s]))
            for k, chip in enumerate(chips):
                sent.append(_remote(small_in[s], small_out[s].at[me], small_send.at[s, k], small_recv.at[s, k], (*chip, c)))
        for cp in local + sent:
            cp.start()
        forward()
        for s in range(ns):
            for k, (px, py) in enumerate(chips):
                dst = small_out[s].at[2 * px + py]
                _remote(dst, dst, small_send.at[s, k], small_recv.at[s, k], (px, py, c)).wait_recv()
        finish()
        for cp in sent:
            cp.wait_send()
        for cp in local:
            cp.wait()

    out_shape = [jax.ShapeDtypeStruct(_full_shape(k, s), b.dtype) for b, k, s in zip(bigs, kinds, shapes)]
    out_shape += [jax.ShapeDtypeStruct((4, *s.shape), s.dtype) for s in smalls]
    return _pcall(
        body, name="gather_weights", in_specs=[HBM] * (nb + ns), out_specs=[HBM] * (nb + ns), out_shape=out_shape,
        scratch_shapes=_gather_scratch(nb) + [pltpu.SemaphoreType.DMA((ns, 3)), pltpu.SemaphoreType.DMA((ns, 3)), pltpu.SemaphoreType.DMA((ns,))],
    )(*bigs, *smalls)


def _swap_other_halves(grads, kinds, shapes, *, name):
    n = len(grads)

    def body(*refs):
        g_in, land = refs[:n], refs[n:2 * n]
        send_sems, recv_sems = refs[2 * n:]
        x, y, c, _ = _place()
        sibling = (x, y, 1 - c)
        sent = []
        for a in range(n):
            for j in range(4):
                cp = _remote(_window(g_in[a], kinds[a], shapes[a], j, 1 - c), land[a].at[j], send_sems.at[a, j], recv_sems.at[a, j], sibling)
                cp.start()
                sent.append(cp)
        for a in range(n):
            for j in range(4):
                _remote(land[a].at[j], land[a].at[j], send_sems.at[a, j], recv_sems.at[a, j], sibling).wait_recv()
        for cp in sent:
            cp.wait_send()

    return _pcall(
        body, name=name, in_specs=[HBM] * n, out_specs=[HBM] * n,
        out_shape=[jax.ShapeDtypeStruct((4, s[0] // 2, s[1]), F32) for s in shapes],
        scratch_shapes=[pltpu.SemaphoreType.DMA((n, 4)), pltpu.SemaphoreType.DMA((n, 4))],
    )(*grads)


def _pair_sum(place, g, land, kind, shard_shape, *, name):
    r, w = shard_shape
    hr = r // 2
    tr = _tile(hr, 256, 16)
    nr = hr // tr

    def body(place_ref, g_ref, l_ref, o_ref, own_ref):
        total = g_ref[...] + l_ref[...]
        o_ref[...] = total.astype(BF16)

        @pl.when(pl.program_id(1) == place_ref[1])
        def _():
            own_ref[...] = total

    if kind == "col":
        g_spec = pl.BlockSpec((tr, w), lambda i, j, p: (p[0] * nr + i, j))
    else:
        g_spec = pl.BlockSpec((tr, w), lambda i, j, p: ((2 * j + p[0]) * nr + i, 0))
    part = pl.BlockSpec((None, tr, w), lambda i, j, p: (j, i, 0))
    return _pcall(
        body, name=name,
        grid_spec=pltpu.PrefetchScalarGridSpec(num_scalar_prefetch=1, grid=(nr, 4), in_specs=[g_spec, part],
                                               out_specs=[part, pl.BlockSpec((tr, w), lambda i, j, p: (i, 0))]),
        out_shape=[jax.ShapeDtypeStruct((4, hr, w), BF16), jax.ShapeDtypeStruct((hr, w), F32)],
        compiler_params=_params("arbitrary", "arbitrary"),
    )(place, g, land)


def _exchange_phases(p_in, land, send_sems, recv_sems):
    x, y, c, chips = _place()

    def copy(a, k):
        px, py = chips[k]
        return _remote(p_in[a].at[2 * px + py], land[a].at[k], send_sems.at[a, k], recv_sems.at[a, k], (px, py, c))

    def start():
        for a in range(len(p_in)):
            for k in range(3):
                copy(a, k).start()

    def finish():
        for a in range(len(p_in)):
            for k in range(3):
                copy(a, k).wait()

    return start, finish


def _exchange_shapes(parts):
    return [jax.ShapeDtypeStruct((3, *p.shape[1:]), p.dtype) for p in parts]


def _chip_sum(place, own, land, *, name):
    hr, w = own.shape
    tr = _tile(hr, 256, 16)
    nr = hr // tr

    def body(place_ref, p_ref, l_ref, o_ref):
        o_ref[...] = ((p_ref[...] + l_ref[0].astype(F32)) + l_ref[1].astype(F32)) + l_ref[2].astype(F32)

    return _pcall(
        body, name=name,
        grid_spec=pltpu.PrefetchScalarGridSpec(
            num_scalar_prefetch=1, grid=(nr,),
            in_specs=[pl.BlockSpec((tr, w), lambda i, p: (i, 0)), pl.BlockSpec((3, tr, w), lambda i, p: (0, i, 0))],
            out_specs=pl.BlockSpec((tr, w), lambda i, p: (p[0] * nr + i, 0))),
        out_shape=jax.ShapeDtypeStruct((2 * hr, w), F32),
        compiler_params=_params("arbitrary"),
    )(place, own, land)


def _swap_reduced_halves(halves):
    n = len(halves)

    def body(*refs):
        src, out = refs[:n], refs[n:2 * n]
        send_sems, recv_sems = refs[2 * n:]
        x, y, c, _ = _place()
        sibling = (x, y, 1 - c)
        sent = []
        for a in range(n):
            hr = out[a].shape[0] // 2
            cp = _remote(src[a].at[pl.ds(c * hr, hr), :], out[a].at[pl.ds(c * hr, hr), :], send_sems.at[a], recv_sems.at[a], sibling)
            cp.start()
            sent.append(cp)
        for a in range(n):
            hr = out[a].shape[0] // 2
            other = out[a].at[pl.ds((1 - c) * hr, hr), :]
            _remote(other, other, send_sems.at[a], recv_sems.at[a], sibling).wait_recv()
        for cp in sent:
            cp.wait_send()

    return _pcall(
        body, name="swap_reduced_halves", in_specs=[HBM] * n, out_specs=[HBM] * n,
        out_shape=[jax.ShapeDtypeStruct(h.shape, F32) for h in halves],
        input_output_aliases={a: a for a in range(n)},
        scratch_shapes=[pltpu.SemaphoreType.DMA((n,)), pltpu.SemaphoreType.DMA((n,))],
    )(*halves)


def _sum_small(packed):
    rows = packed.shape[0]

    def body(in_ref, out_ref, land, send_sems, recv_sems, local_sem):
        x, y, c, _ = _place()
        me = 4 * x + 2 * y + c
        own = pltpu.make_async_copy(in_ref, land.at[me], local_sem)
        own.start()
        flips = [(r >> 2, (r >> 1) & 1, r & 1) for r in range(1, 8)]
        sent = []
        for k, (fx, fy, fc) in enumerate(flips):
            cp = _remote(in_ref, land.at[me], send_sems.at[k], recv_sems.at[k], (x ^ fx, y ^ fy, c ^ fc))
            cp.start()
            sent.append(cp)
        for k, (fx, fy, fc) in enumerate(flips):
            src = land.at[4 * (x ^ fx) + 2 * (y ^ fy) + (c ^ fc)]
            _remote(src, src, send_sems.at[k], recv_sems.at[k], (x ^ fx, y ^ fy, c ^ fc)).wait_recv()
        for cp in sent:
            cp.wait_send()
        own.wait()
        total = land[0]
        for dev in range(1, 8):
            total = total + land[dev]
        out_ref[...] = total

    vmem = pl.BlockSpec(memory_space=pltpu.VMEM)
    return _pcall(
        body, name="sum_small", in_specs=[vmem], out_specs=vmem, out_shape=jax.ShapeDtypeStruct(packed.shape, F32),
        scratch_shapes=[pltpu.VMEM((8, rows, LANES), F32), pltpu.SemaphoreType.DMA((7,)), pltpu.SemaphoreType.DMA((7,)), pltpu.SemaphoreType.DMA],
    )(packed)


def _sigmoid(v):
    return 1.0 / (1.0 + jnp.exp(-v))


def kernel(x, meta_tokens, g_pre_mix, w_in, conv_w_mix, w_proj_conv, w_proj_attn, b_gate, w_out, g_post_mix, g_pre_ffn, w_up_gate, conv_w_ffn, w_down, g_post_ffn, loss_target, m_meta_tokens, m_g_pre_mix, m_w_in, m_conv_w_mix, m_w_proj_conv, m_w_proj_attn, m_b_gate, m_w_out, m_g_post_mix, m_g_pre_ffn, m_w_up_gate, m_conv_w_ffn, m_w_down, m_g_post_ffn, v_meta_tokens, v_g_pre_mix, v_w_in, v_conv_w_mix, v_w_proj_conv, v_w_proj_attn, v_b_gate, v_w_out, v_g_post_mix, v_g_pre_ffn, v_w_up_gate, v_conv_w_ffn, v_w_down, v_g_post_ffn):
    seq, d = x.shape[1], x.shape[2]
    f = w_down.shape[1] * 4
    real = N_META + seq
    t = -(-real // ATT_BLOCK) * ATT_BLOCK
    nd = d // LANES
    cx, cy, cc = lax.axis_index("x"), lax.axis_index("y"), lax.axis_index("c")
    chip = 2 * cx + cy
    place = jnp.stack([cc, chip]).astype(jnp.int32)

    big_names = ["w_in", "w_proj_conv", "w_proj_attn", "w_out", "w_up_gate", "w_down"]
    kinds = ["col", "row", "row", "row", "col", "row"]
    big_w = [w_in[0], w_proj_conv[0], w_proj_attn[0], w_out[0], w_up_gate[0], w_down[0]]
    small_w = [meta_tokens, conv_w_mix[0], b_gate[0], conv_w_ffn[0]]
    shards = [w.astype(BF16) for w in big_w]
    shapes = [w.shape for w in big_w]
    gathered = _gather_weights(shards[:1], kinds[:1], small_w)
    wi = gathered[0]
    meta_f, cwm_f, bg_f, cwf_f = (jnp.moveaxis(s, 0, 1).reshape(s.shape[1], -1) for s in gathered[1:])

    h0 = jnp.concatenate([meta_f, x[0], jnp.zeros((t - real, d), F32)], axis=0)
    target = jnp.concatenate([jnp.zeros((N_META, d), F32), loss_target[0], jnp.zeros((t - real, d), F32)], axis=0)
    xn1 = _prenorm(h0, g_pre_mix, name="prenorm_mix")
    hin = _matmul(xn1, wi, name="in_proj")
    yconv = _mixer_conv_fwd(hin, cwm_f, d)
    o, att_tot, (wpc, wpa, wo, wug, wd) = _attention_fwd(hin, d, shards[1:], kinds[1:])

    def gates(gpre, bias):
        return _sigmoid(gpre[:, :d].astype(F32) + bias[0:1]), _sigmoid(gpre[:, d:].astype(F32) + bias[1:2])

    def merge_epi(accs, rows, vecs, row0):
        gate_c, gate_a = gates(rows[0], vecs[0])
        return [gate_c * accs[0] + gate_a * accs[1], accs[0], accs[1]]

    act = ("row", d, BF16)
    merged, bconv, battn = _matmul_rows([(yconv, wpc), (o, wpa)], [(hin, 2 * d, 3)], [bg_f], merge_epi, [act, act, act], name="branch_proj")

    def mix_epi(accs, rows, vecs, row0):
        mix = accs[0]
        h1 = rows[0] + mix * _rms(mix) * vecs[0]
        return [mix, h1, h1 * _rms(h1) * vecs[1]]

    mix, h1, xn2 = _matmul_rows([(merged, wo)], [(h0, d, 0)], [g_post_mix, g_pre_ffn], mix_epi,
                                [("row", d, F32), ("row", d, F32), act], name="out_proj")
    ug = _matmul(xn2, wug, name="up_proj")
    fact = _ffn_conv_fwd(ug, cwf_f, f)

    def loss_epi(accs, rows, vecs, row0):
        ffn = accs[0]
        r = _rms(ffn)
        fh = ffn * r
        h2 = rows[0] + fh * vecs[0]
        rid = row0 + lax.broadcasted_iota(jnp.int32, (ffn.shape[0], 1), 0)
        err = jnp.where((rid >= N_META) & (rid < real), h2 - rows[1], 0.0)
        dy = err * (1.0 / d)
        dffn, dg = _rms_bwd(dy, fh, r, vecs[0])
        loss = jnp.zeros((1, LANES), F32) + 0.5 * jnp.sum(err * err) / d
        return [dffn, dy, dg, loss]

    dffn, dh2, dg_post_ffn, loss_part = _matmul_rows(
        [(fact, wd)], [(h1, d, 0), (target, d, 0)], [g_post_ffn], loss_epi,
        [act, ("row", d, F32), ("acc", 1, d), ("acc", 1, LANES)], name="down_proj_loss")

    gw_down = _matmul_tn(fact, dffn, name="grad_w_down")
    df = _matmul(dffn, wd, nt=True, name="down_proj_bwd")
    du2, dg2, gcw_ffn = _ffn_conv_bwd(ug, df, cwf_f, f)

    def ffn_in_epi(accs, rows, vecs, row0):
        h1v, dyv, mixv = rows
        r3 = _rms(h1v)
        dh1n, dg3 = _rms_bwd(accs[0], h1v * r3, r3, vecs[0])
        dh1 = dyv + dh1n
        r2 = _rms(mixv)
        dmix, dgm = _rms_bwd(dh1, mixv * r2, r2, vecs[1])
        return [dh1, dmix, dg3, dgm]

    dh1, dmix, dg_pre_ffn, dg_post_mix, dug = _matmul_rows_joined(
        [du2, dg2], wug, [(h1, d, 0), (dh2, d, 0), (mix, d, 0)], [g_pre_ffn, g_post_mix], ffn_in_epi,
        [("row", d, F32), act, ("acc", 1, d), ("acc", 1, d)], tk=f // 2, name="up_proj_bwd")
    gw_up = _matmul_tn(xn2, dug, name="grad_w_up_gate")
    gw_out = _matmul_tn(merged, dmix, name="grad_w_out")

    def merge_bwd_epi(accs, rows, vecs, row0):
        dm = accs[0]
        gate_c, gate_a = gates(rows[0], vecs[0])
        dpre_c = dm * rows[1].astype(F32) * gate_c * (1.0 - gate_c)
        dpre_a = dm * rows[2].astype(F32) * gate_a * (1.0 - gate_a)
        dbias = jnp.concatenate([jnp.sum(dpre_c, axis=0, keepdims=True), jnp.sum(dpre_a, axis=0, keepdims=True)], axis=0)
        return [dm * gate_c, dm * gate_a, jnp.concatenate([dpre_c, dpre_a], axis=1), dbias]

    dbconv, dbattn, dgates, gb_gate = _matmul_rows(
        [(dmix, wo)], [(hin, 2 * d, 3), (bconv, d, 0), (battn, d, 0)], [bg_f], merge_bwd_epi,
        [act, act, ("row", 2 * d, BF16), ("acc", 2, d)], nt=True, name="out_proj_bwd")
    gw_pc = _matmul_tn(yconv, dbconv, name="grad_w_proj_conv")
    gw_pa = _matmul_tn(o, dbattn, name="grad_w_proj_attn")
    dyconv = _matmul(dbconv, wpc, nt=True, name="proj_conv_bwd")
    do = _matmul(dbattn, wpa, nt=True, name="proj_attn_bwd")
    db, dc, dhc, gcw_mix = _mixer_conv_bwd(hin, dyconv, cwm_f, d)

    def pair_sums(grads, first):
        landed = _swap_other_halves(grads, kinds[first:first + len(grads)], shapes[first:first + len(grads)],
                                    name=f"swap_other_halves_{big_names[first]}")
        return zip(*[_pair_sum(place, g, l, kinds[first + a], shapes[first + a], name=f"pair_sum_{big_names[first + a]}")
                     for a, (g, l) in enumerate(zip(grads, landed))])

    parts, own = pair_sums([gw_pc, gw_pa, gw_out, gw_up, gw_down], 1)
    dq, dk, dv, landed = _attention_bwd(hin, att_tot, do, d, parts)

    def in_epi(accs, rows, vecs, row0):
        h0v, dh1v = rows
        r1 = _rms(h0v)
        dh0n, dg1 = _rms_bwd(accs[0], h0v * r1, r1, vecs[0])
        return [dh1v + dh0n, dg1]

    dhin = [db, dc, dhc, dq, dk, dv, dgates]
    gw_in = _matmul_tn_segments(xn1, dhin, name="grad_w_in")
    parts_in, own_in = pair_sums([gw_in], 0)
    dh0, dg_pre_mix, landed_in = _matmul_rows_joined(
        dhin, wi, [(h0, d, 0), (dh1, d, 0)], [g_pre_mix], in_epi,
        [("row", d, F32), ("acc", 1, d)], tk=d, name="in_proj_bwd", joined=False, parts=parts_in)
    grad_x = dh0[N_META:real][None]

    landed = [landed_in] + list(landed)
    halves = [_chip_sum(place, p, l, name=f"chip_sum_{n}") for p, l, n in zip(list(own_in) + list(own), landed, big_names)]
    big_r = _swap_reduced_halves(halves)

    small_g = [dh0[:N_META], dg_pre_mix, gcw_mix, gb_gate, dg_post_mix, dg_pre_ffn, gcw_ffn, dg_post_ffn, loss_part]
    flat = jnp.concatenate([s.reshape(-1) for s in small_g])
    pad = -flat.shape[0] % (8 * LANES)
    summed = _sum_small(jnp.pad(flat, (0, pad)).reshape(-1, LANES)).reshape(-1)
    small_r, pos = [], 0
    for s in small_g:
        small_r.append(summed[pos:pos + s.size].reshape(s.shape))
        pos += s.size
    g_meta, g_g1, g_cwm, g_bg, g_g2, g_g3, g_cwf, g_g4, loss_row = small_r
    loss = loss_row[0, 0]

    def my_cols(full, width):
        return lax.dynamic_slice_in_dim(full, chip * width, width, axis=1)

    grads = {
        "meta_tokens": my_cols(g_meta, d // 4), "g_pre_mix": g_g1, "w_in": big_r[0][None],
        "conv_w_mix": my_cols(g_cwm, d // 4)[None], "w_proj_conv": big_r[1][None], "w_proj_attn": big_r[2][None],
        "b_gate": my_cols(g_bg, d // 4)[None], "w_out": big_r[3][None], "g_post_mix": g_g2, "g_pre_ffn": g_g3,
        "w_up_gate": big_r[4][None], "conv_w_ffn": my_cols(g_cwf, f // 4)[None], "w_down": big_r[5][None], "g_post_ffn": g_g4,
    }
    weights = dict(meta_tokens=meta_tokens, g_pre_mix=g_pre_mix, w_in=w_in, conv_w_mix=conv_w_mix, w_proj_conv=w_proj_conv,
                   w_proj_attn=w_proj_attn, b_gate=b_gate, w_out=w_out, g_post_mix=g_post_mix, g_pre_ffn=g_pre_ffn,
                   w_up_gate=w_up_gate, conv_w_ffn=conv_w_ffn, w_down=w_down, g_post_ffn=g_post_ffn)
    m_in = dict(meta_tokens=m_meta_tokens, g_pre_mix=m_g_pre_mix, w_in=m_w_in, conv_w_mix=m_conv_w_mix, w_proj_conv=m_w_proj_conv,
                w_proj_attn=m_w_proj_attn, b_gate=m_b_gate, w_out=m_w_out, g_post_mix=m_g_post_mix, g_pre_ffn=m_g_pre_ffn,
                w_up_gate=m_w_up_gate, conv_w_ffn=m_conv_w_ffn, w_down=m_w_down, g_post_ffn=m_g_post_ffn)
    v_in = dict(meta_tokens=v_meta_tokens, g_pre_mix=v_g_pre_mix, w_in=v_w_in, conv_w_mix=v_conv_w_mix, w_proj_conv=v_w_proj_conv,
                w_proj_attn=v_w_proj_attn, b_gate=v_b_gate, w_out=v_w_out, g_post_mix=v_g_post_mix, g_pre_ffn=v_g_pre_ffn,
                w_up_gate=v_w_up_gate, conv_w_ffn=v_conv_w_ffn, w_down=v_w_down, g_post_ffn=v_g_post_ffn)
    names = list(weights)
    deltas, new_m, new_v = [], [], []
    for n in names:
        shape = weights[n].shape
        two_d = (-1, shape[-1])
        dl, nm, nv = _adamw(weights[n].reshape(two_d), grads[n].reshape(two_d), m_in[n].reshape(two_d), v_in[n].reshape(two_d),
                            name=f"adamw_{n}")
        deltas.append(dl.reshape(shape))
        new_m.append(nm.reshape(shape))
        new_v.append(nv.reshape(shape))
    return (loss, grad_x, *[grads[n].reshape(weights[n].shape) for n in names], *deltas, *new_m, *new_v)
```

```python
import functools
import math

import jax
import jax.numpy as jnp
from jax import lax
from jax.experimental import pallas as pl
from jax.experimental.pallas import tpu as pltpu

F32 = jnp.float32
BF16 = jnp.bfloat16

N_META = 16
HEAD_DIM = 64
LANES = 128
RMS_EPS = 1e-6
ATT_BLOCK = 256
ATT_FWD_LANE_BLOCKS = 4
ATT_BWD_LANE_BLOCKS = 2
VMEM_LIMIT = 56 * 1024 * 1024

ADAM_LR = 0.001
ADAM_B1 = 0.9
ADAM_B2 = 0.999
ADAM_EPS = 1e-08
ADAM_WD = 0.01
ADAM_STEP = 10

MESH = pl.DeviceIdType.MESH


def _tile(n, cap, unit=LANES):
    d = (min(cap, n) // unit) * unit
    while d >= unit:
        if n % d == 0:
            return d
        d -= unit
    raise ValueError(f"no tile for {n} under {cap}")


MXU_EDGE = 256


def _mxu_tile(n, cap):
    if n <= cap:
        return n
    return _tile(n, cap, MXU_EDGE) if n % MXU_EDGE == 0 else _tile(n, cap)


def _params(*sem):
    return pltpu.CompilerParams(dimension_semantics=sem, vmem_limit_bytes=VMEM_LIMIT)


def _pcall(body, **kw):
    return pl.pallas_call(body, **kw)


def _dot(a, b, dims):
    return lax.dot_general(a, b, (dims, ((), ())), preferred_element_type=F32)


NN = ((1,), (0,))
NT = ((1,), (1,))
TN = ((0,), (0,))


def _matmul(a, b, *, nt=False, out_dtype=BF16, name):
    m, kdim = a.shape
    n = b.shape[0] if nt else b.shape[1]
    tm, tn, tk = _tile(m, 640, 16), _mxu_tile(n, 2816), _mxu_tile(kdim, 2816)
    nk = kdim // tk

    def body(a_ref, b_ref, o_ref, *scratch):
        p = _dot(a_ref[...], b_ref[...], NT if nt else NN)
        if nk == 1:
            o_ref[...] = p.astype(o_ref.dtype)
            return
        acc, k = scratch[0], pl.program_id(2)

        @pl.when(k == 0)
        def _():
            acc[...] = p

        @pl.when(k > 0)
        def _():
            acc[...] += p

        @pl.when(k == nk - 1)
        def _():
            o_ref[...] = acc[...].astype(o_ref.dtype)

    b_spec = pl.BlockSpec((tn, tk), lambda j, i, k: (j, k)) if nt else pl.BlockSpec((tk, tn), lambda j, i, k: (k, j))
    return _pcall(
        body, name=name, grid=(n // tn, m // tm, nk),
        in_specs=[pl.BlockSpec((tm, tk), lambda j, i, k: (i, k)), b_spec],
        out_specs=pl.BlockSpec((tm, tn), lambda j, i, k: (i, j)),
        out_shape=jax.ShapeDtypeStruct((m, n), out_dtype),
        scratch_shapes=[pltpu.VMEM((tm, tn), F32)] if nk > 1 else [],
        compiler_params=_params("arbitrary", "arbitrary", "arbitrary"),
    )(a, b)


def _matmul_tn(a, b, *, name):
    t, ka = a.shape
    nb = b.shape[1]
    tb = _mxu_tile(nb, 1024)
    budget = VMEM_LIMIT * 5 // 7
    ta = next(c for c in range(ka, 0, -LANES) if ka % c == 0 and 2 * (2 * t * (c + tb) + 4 * c * tb) <= budget)

    def body(a_ref, b_ref, o_ref):
        o_ref[...] = _dot(a_ref[...], b_ref[...], TN)

    return _pcall(
        body, name=name, grid=(ka // ta, nb // tb),
        in_specs=[pl.BlockSpec((t, ta), lambda i, j: (0, i)), pl.BlockSpec((t, tb), lambda i, j: (0, j))],
        out_specs=pl.BlockSpec((ta, tb), lambda i, j: (i, j)),
        out_shape=jax.ShapeDtypeStruct((ka, nb), F32),
        compiler_params=_params("arbitrary", "arbitrary"),
    )(a, b)


def _matmul_rows(pairs, rows, vecs, epi, outs, *, nt=False, name):
    m, kdim = pairs[0][0].shape
    n = pairs[0][1].shape[0] if nt else pairs[0][1].shape[1]
    tm, tk = _tile(m, 640, 16), _mxu_tile(kdim, 2816)
    nk, npair = kdim // tk, len(pairs)

    def body(*refs):
        a_refs, b_refs = refs[0:2 * npair:2], refs[1:2 * npair:2]
        pos = 2 * npair
        row_refs = refs[pos:pos + len(rows)]
        pos += len(rows)
        vec_refs = refs[pos:pos + len(vecs)]
        pos += len(vecs)
        out_refs = refs[pos:pos + len(outs)]
        accs = refs[pos + len(outs):]
        i, k = pl.program_id(0), pl.program_id(1)
        prods = [_dot(a[...], b[...], NT if nt else NN) for a, b in zip(a_refs, b_refs)]
        if nk > 1:
            @pl.when(k == 0)
            def _():
                for acc, p in zip(accs, prods):
                    acc[...] = p

            @pl.when(k > 0)
            def _():
                for acc, p in zip(accs, prods):
                    acc[...] += p

        @pl.when(k == nk - 1)
        def _():
            vals = [acc[...] for acc in accs] if nk > 1 else prods
            res = epi(vals, [r[...] for r in row_refs], [v[...] for v in vec_refs], i * tm)
            for o_ref, spec, val in zip(out_refs, outs, res):
                if spec[0] == "row":
                    o_ref[...] = val.astype(o_ref.dtype)
                else:
                    @pl.when(i == 0)
                    def _():
                        o_ref[...] = val

                    @pl.when(i > 0)
                    def _():
                        o_ref[...] += val

    in_specs, args = [], []
    for a, b in pairs:
        in_specs += [pl.BlockSpec((tm, tk), lambda i, k: (i, k)),
                     pl.BlockSpec((n, tk), lambda i, k: (0, k)) if nt else pl.BlockSpec((tk, n), lambda i, k: (k, 0))]
        args += [a, b]
    for arr, width, cb in rows:
        in_specs.append(pl.BlockSpec((tm, width), functools.partial(lambda i, k, cb: (i, cb), cb=cb)))
        args.append(arr)
    for v in vecs:
        in_specs.append(pl.BlockSpec(v.shape, lambda i, k: (0, 0)))
        args.append(v)
    out_specs, out_shape = [], []
    for spec in outs:
        if spec[0] == "row":
            out_specs.append(pl.BlockSpec((tm, spec[1]), lambda i, k: (i, 0)))
            out_shape.append(jax.ShapeDtypeStruct((m, spec[1]), spec[2]))
        else:
            out_specs.append(pl.BlockSpec((spec[1], spec[2]), lambda i, k: (0, 0)))
            out_shape.append(jax.ShapeDtypeStruct((spec[1], spec[2]), F32))
    return _pcall(
        body, name=name, grid=(m // tm, nk), in_specs=in_specs, out_specs=out_specs, out_shape=out_shape,
        scratch_shapes=[pltpu.VMEM((tm, n), F32) for _ in pairs] if nk > 1 else [],
        compiler_params=_params("arbitrary", "arbitrary"),
    )(*args)


def _matmul_rows_joined(segments, b, rows, vecs, epi, outs, *, tk, name, joined=True, parts=()):
    m = segments[0].shape[0]
    n, kdim = b.shape
    tm = _tile(m, 640, 16)
    nm, nk, npart = m // tm, kdim // tk, len(parts)
    first = [0]
    for s in segments:
        assert s.shape[1] % tk == 0
        first.append(first[-1] + s.shape[1] // tk)
    assert first[-1] == nk

    def body(*refs):
        seg_refs, b_ref = refs[:len(segments)], refs[len(segments)]
        pos = len(segments) + 1
        row_refs = refs[pos:pos + len(rows)]
        pos += len(rows)
        vec_refs = refs[pos:pos + len(vecs)]
        pos += len(vecs)
        part_refs = refs[pos:pos + npart]
        pos += npart
        out_refs = refs[pos:pos + len(outs)]
        pos += len(outs)
        joined_ref = refs[pos] if joined else None
        pos += int(joined)
        land_refs, acc = refs[pos:pos + npart], refs[pos + npart]
        i, k = pl.program_id(0), pl.program_id(1)
        if npart:
            start, finish = _exchange_phases(part_refs, land_refs, *refs[pos + npart + 1:])
            pl.when((i == 0) & (k == 0))(start)

        @pl.when(k == 0)
        def _():
            acc[...] = jnp.zeros_like(acc)

        for s, seg in enumerate(seg_refs):
            @pl.when((k >= first[s]) & (k < first[s + 1]))
            def _():
                if joined:
                    joined_ref[...] = seg[...]
                acc[...] += _dot(seg[...], b_ref[...], NT)

        @pl.when(k == nk - 1)
        def _():
            res = epi([acc[...]], [r[...] for r in row_refs], [v[...] for v in vec_refs], i * tm)
            for o_ref, spec, val in zip(out_refs, outs, res):
                if spec[0] == "row":
                    o_ref[...] = val.astype(o_ref.dtype)
                else:
                    @pl.when(i == 0)
                    def _():
                        o_ref[...] = val

                    @pl.when(i > 0)
                    def _():
                        o_ref[...] += val

        if npart:
            pl.when((i == nm - 1) & (k == nk - 1))(finish)

    def seg_spec(s):
        last = first[s + 1] - first[s] - 1
        return pl.BlockSpec((tm, tk), lambda i, k: (i, jnp.clip(k - first[s], 0, last)))

    in_specs = [seg_spec(s) for s in range(len(segments))] + [pl.BlockSpec((n, tk), lambda i, k: (0, k))]
    args = [*segments, b]
    for arr, width, cb in rows:
        in_specs.append(pl.BlockSpec((tm, width), functools.partial(lambda i, k, cb: (i, cb), cb=cb)))
        args.append(arr)
    for v in vecs:
        in_specs.append(pl.BlockSpec(v.shape, lambda i, k: (0, 0)))
        args.append(v)
    in_specs += [HBM] * npart
    args += list(parts)
    out_specs, out_shape = [], []
    for spec in outs:
        if spec[0] == "row":
            out_specs.append(pl.BlockSpec((tm, spec[1]), lambda i, k: (i, 0)))
            out_shape.append(jax.ShapeDtypeStruct((m, spec[1]), spec[2]))
        else:
            out_specs.append(pl.BlockSpec((spec[1], spec[2]), lambda i, k: (0, 0)))
            out_shape.append(jax.ShapeDtypeStruct((spec[1], spec[2]), F32))
    if joined:
        out_specs.append(pl.BlockSpec((tm, tk), lambda i, k: (i, k)))
        out_shape.append(jax.ShapeDtypeStruct((m, kdim), segments[0].dtype))
    out_specs += [HBM] * npart
    out_shape += _exchange_shapes(parts)
    sems = [pltpu.SemaphoreType.DMA((npart, 3)), pltpu.SemaphoreType.DMA((npart, 3))] if npart else []
    return _pcall(
        body, name=name, grid=(nm, nk), in_specs=in_specs, out_specs=out_specs, out_shape=out_shape,
        scratch_shapes=[pltpu.VMEM((tm, n), F32)] + sems,
        compiler_params=_params("arbitrary", "arbitrary"),
    )(*args)


def _matmul_tn_segments(a, segments, *, name):
    t, ka = a.shape
    tb = MXU_EDGE
    first = [0]
    for s in segments:
        assert s.shape[1] % tb == 0
        first.append(first[-1] + s.shape[1] // tb)
    budget = VMEM_LIMIT * 13 // 14
    ta = next(c for c in range(ka, 0, -LANES) if ka % c == 0 and 2 * (2 * t * (c + len(segments) * tb) + 4 * c * tb) <= budget)

    def body(a_ref, *refs):
        seg_refs, o_ref = refs[:-1], refs[-1]
        j = pl.program_id(1)
        for s, seg in enumerate(seg_refs):
            @pl.when((j >= first[s]) & (j < first[s + 1]))
            def _():
                o_ref[...] = _dot(a_ref[...], seg[...], TN)

    def seg_spec(s):
        last = first[s + 1] - first[s] - 1
        return pl.BlockSpec((t, tb), lambda i, j: (0, jnp.clip(j - first[s], 0, last)))

    return _pcall(
        body, name=name, grid=(ka // ta, first[-1]),
        in_specs=[pl.BlockSpec((t, ta), lambda i, j: (0, i))] + [seg_spec(s) for s in range(len(segments))],
        out_specs=pl.BlockSpec((ta, tb), lambda i, j: (i, j)),
        out_shape=jax.ShapeDtypeStruct((ka, first[-1] * tb), F32),
        compiler_params=_params("arbitrary", "arbitrary"),
    )(a, *segments)


def _rms(v):
    return lax.rsqrt(jnp.mean(v * v, axis=-1, keepdims=True) + RMS_EPS)


def _rms_bwd(dz, vhat, r, g):
    t = dz * g
    dv = r * (t - vhat * jnp.mean(t * vhat, axis=-1, keepdims=True))
    return dv, jnp.sum(dz * vhat, axis=0, keepdims=True)


def _prenorm(h, g, *, name):
    m, d = h.shape
    tm = _tile(m, 640, 16)

    def body(h_ref, g_ref, o_ref):
        v = h_ref[...]
        o_ref[...] = (v * _rms(v) * g_ref[...]).astype(BF16)

    return _pcall(
        body, name=name, grid=(m // tm,),
        in_specs=[pl.BlockSpec((tm, d), lambda i: (i, 0)), pl.BlockSpec((1, d), lambda i: (0, 0))],
        out_specs=pl.BlockSpec((tm, d), lambda i: (i, 0)),
        out_shape=jax.ShapeDtypeStruct((m, d), BF16),
        compiler_params=_params("arbitrary"),
    )(h, g)


SUBLANES = 8


def _shift_down(u, k):
    r = pltpu.roll(u, k, 0)
    rows = lax.broadcasted_iota(jnp.int32, (SUBLANES, u.shape[1]), 0)
    return jnp.concatenate([jnp.where(rows >= k, r[:SUBLANES], 0.0), r[SUBLANES:]], axis=0)


def _shift_up(u, k):
    n = u.shape[0]
    r = pltpu.roll(u, n - k, 0)
    rows = lax.broadcasted_iota(jnp.int32, (SUBLANES, u.shape[1]), 0)
    return jnp.concatenate([r[:n - SUBLANES], jnp.where(rows < SUBLANES - k, r[n - SUBLANES:], 0.0)], axis=0)


def _taps(u):
    return _shift_down(u, 2), _shift_down(u, 1), u


def _conv(taps, w):
    return w[0:1] * taps[0] + w[1:2] * taps[1] + w[2:3] * taps[2]


def _conv_bwd(dcu, taps, w):
    du = w[2:3] * dcu + w[1:2] * _shift_up(dcu, 1) + w[0:1] * _shift_up(dcu, 2)
    return du, [jnp.sum(dcu * tap, axis=0, keepdims=True) for tap in taps]


def _strip(arr, t, cb0):
    return pl.BlockSpec((t, LANES), functools.partial(lambda s, cb0: (0, cb0 + s), cb0=cb0))


def _mixer_conv_fwd(hin, w, d):
    t, ns = hin.shape[0], d // LANES

    def body(b_ref, c_ref, h_ref, w_ref, y_ref):
        u = c_ref[...].astype(F32) * h_ref[...].astype(F32)
        y_ref[...] = (b_ref[...].astype(F32) * _conv(_taps(u), w_ref[...])).astype(BF16)

    return _pcall(
        body, name="mixer_conv_fwd", grid=(ns,),
        in_specs=[_strip(hin, t, 0), _strip(hin, t, ns), _strip(hin, t, 2 * ns), pl.BlockSpec((3, LANES), lambda s: (0, s))],
        out_specs=pl.BlockSpec((t, LANES), lambda s: (0, s)),
        out_shape=jax.ShapeDtypeStruct((t, d), BF16),
        compiler_params=_params("arbitrary"),
    )(hin, hin, hin, w)


def _mixer_conv_bwd(hin, dy, w, d):
    t, ns = hin.shape[0], d // LANES

    def body(b_ref, c_ref, h_ref, dy_ref, w_ref, db_ref, dc_ref, dh_ref, dw_ref):
        b, c, h, g = (r[...].astype(F32) for r in (b_ref, c_ref, h_ref, dy_ref))
        wv = w_ref[...]
        taps = _taps(c * h)
        db_ref[...] = (g * _conv(taps, wv)).astype(BF16)
        du, dw = _conv_bwd(g * b, taps, wv)
        dc_ref[...] = (du * h).astype(BF16)
        dh_ref[...] = (du * c).astype(BF16)
        for k in range(3):
            dw_ref[k:k + 1, :] = dw[k]

    col = pl.BlockSpec((t, LANES), lambda s: (0, s))
    act = jax.ShapeDtypeStruct((t, d), BF16)
    return _pcall(
        body, name="mixer_conv_bwd", grid=(ns,),
        in_specs=[_strip(hin, t, 0), _strip(hin, t, ns), _strip(hin, t, 2 * ns), col, pl.BlockSpec((3, LANES), lambda s: (0, s))],
        out_specs=[col, col, col, pl.BlockSpec((3, LANES), lambda s: (0, s))],
        out_shape=[act, act, act, jax.ShapeDtypeStruct((3, d), F32)],
        compiler_params=_params("arbitrary"),
    )(hin, hin, hin, dy, w)


GELU_C = math.sqrt(2.0 / math.pi)
GELU_A = 0.044715


def _gelu_tanh(x):
    return jnp.tanh(GELU_C * (x + GELU_A * x * x * x))


def _ffn_conv_fwd(ug, w, f):
    t, ns = ug.shape[0], f // LANES

    def body(u_ref, g_ref, w_ref, o_ref):
        cu = _conv(_taps(u_ref[...].astype(F32)), w_ref[...])
        o_ref[...] = (0.5 * cu * (1.0 + _gelu_tanh(cu)) * g_ref[...].astype(F32)).astype(BF16)

    return _pcall(
        body, name="ffn_conv_fwd", grid=(ns,),
        in_specs=[_strip(ug, t, 0), _strip(ug, t, ns), pl.BlockSpec((3, LANES), lambda s: (0, s))],
        out_specs=pl.BlockSpec((t, LANES), lambda s: (0, s)),
        out_shape=jax.ShapeDtypeStruct((t, f), BF16),
        compiler_params=_params("arbitrary"),
    )(ug, ug, w)


def _ffn_conv_bwd(ug, df, w, f):
    t, ns = ug.shape[0], f // LANES

    def body(u_ref, g_ref, df_ref, w_ref, du_ref, dg_ref, dw_ref):
        u, g, d = (r[...].astype(F32) for r in (u_ref, g_ref, df_ref))
        wv = w_ref[...]
        taps = _taps(u)
        cu = _conv(taps, wv)
        th = _gelu_tanh(cu)
        half = 0.5 * (1.0 + th)
        dg_ref[...] = (d * (cu * half)).astype(BF16)
        dgelu = half + (0.5 * GELU_C) * cu * (1.0 - th * th) * (1.0 + (3.0 * GELU_A) * (cu * cu))
        du, dw = _conv_bwd(d * g * dgelu, taps, wv)
        du_ref[...] = du.astype(BF16)
        for k in range(3):
            dw_ref[k:k + 1, :] = dw[k]

    col = pl.BlockSpec((t, LANES), lambda s: (0, s))
    act = jax.ShapeDtypeStruct((t, f), BF16)
    return _pcall(
        body, name="ffn_conv_bwd", grid=(ns,),
        in_specs=[_strip(ug, t, 0), _strip(ug, t, ns), col, pl.BlockSpec((3, LANES), lambda s: (0, s))],
        out_specs=[col, col, pl.BlockSpec((3, LANES), lambda s: (0, s))],
        out_shape=[act, act, jax.ShapeDtypeStruct((3, f), F32)],
        compiler_params=_params("arbitrary"),
    )(ug, ug, df, w)


def _log_terms(z):
    minus_abs = lax.bitcast_convert_type(lax.bitcast_convert_type(z, jnp.uint32) | jnp.uint32(0x80000000), F32)
    lb = jnp.minimum(z, 0.0) - jnp.log(1.0 + jnp.exp(minus_abs))
    return lb, lb - z


SUM_TERMS = 1


def _split_cat(v, axis):
    terms = [v.astype(BF16)]
    for _ in range(SUM_TERMS - 1):
        v = v - terms[-1].astype(F32)
        terms.append(v.astype(BF16))
    return terms[0] if SUM_TERMS == 1 else jnp.concatenate(terms, axis=axis)


def _head_masks():
    lane = lax.broadcasted_iota(jnp.int32, (1, LANES), 1)
    return lane < HEAD_DIM, lane >= HEAD_DIM


def _attention_fwd(hin, d, shards, kinds):
    t, blk, nlb = hin.shape[0], ATT_BLOCK, min(ATT_FWD_LANE_BLOCKS, d // LANES)
    width = nlb * LANES
    nd = d // width
    nq = t // blk
    scale = 1.0 / math.sqrt(HEAD_DIM)
    tri = jnp.tril(jnp.ones((blk, blk), F32)).astype(BF16)
    tri = jnp.concatenate([tri] * SUM_TERMS, axis=0)
    ns = len(shards)
    shapes = [s.shape for s in shards]
    heads = [(lb, half) for lb in range(nlb) for half in range(2)]

    def body(q_ref, k_ref, v_ref, tri_ref, *rest):
        shard_refs, (o_ref, tot_ref), full_refs, sems = rest[:ns], rest[ns:ns + 2], rest[ns + 2:2 * ns + 2], rest[2 * ns + 2:]
        p, i = pl.program_id(0), pl.program_id(1)
        start, forward, finish = _gather_phases(shard_refs, full_refs, kinds, shapes, *sems)
        pl.when((p == 0) & (i == 0))(start)
        pl.when((p == nd // 2) & (i == 0))(forward)
        masks = _head_masks()
        lanes = [slice(lb * LANES, (lb + 1) * LANES) for lb in range(nlb)]
        q = q_ref[...].astype(F32) * scale
        qh = [jnp.where(masks[half], q[:, lanes[lb]], 0.0).astype(BF16) for lb, half in heads]
        after = lax.broadcasted_iota(jnp.int32, (blk, blk), 0) > lax.broadcasted_iota(jnp.int32, (blk, blk), 1)
        tri_v = tri_ref[...]

        def scores(j):
            j0 = pl.multiple_of(j * blk, blk)
            kj = k_ref[pl.ds(j0, blk), :]
            return tuple(_dot(qh[h], kj[:, lanes[lb]], NT) for h, (lb, _) in enumerate(heads))

        def weights(z, c, diagonal):
            halves = [slice(r, r + blk // 2) for r in range(0, blk, blk // 2)]
            sums, a = {}, {}

            def stage_sums(h):
                parts = []
                for rows in halves:
                    lf = _log_terms(z[h][rows])[1]
                    if diagonal:
                        lf = jnp.where(after[rows], lf, 0.0)
                    parts.append(_dot(_split_cat(lf, 1), tri_v, NN))
                sums[h] = jnp.concatenate(parts, axis=0)

            def stage_weights(h):
                ah = jnp.exp(z[h] + sums[h] + c[h])
                a[h] = (jnp.where(after, ah, 0.0) if diagonal else ah).astype(BF16)

            order = range(len(heads))
            for h in order:
                stage_sums(h)
            for h in order:
                stage_weights(h)
            return tuple(a[h] for h in order), tuple(c[h] + sums[h][:, 0:1] for h in order)

        def apply(acc, a, j):
            j0 = pl.multiple_of(j * blk, blk)
            vj = v_ref[pl.ds(j0, blk), :]
            acc = list(acc)
            for h, (lb, half) in enumerate(heads):
                vh = vj[:, lanes[lb]]
                acc[lb] = acc[lb] + _dot(a[h], jnp.where(masks[half], vh, jnp.zeros_like(vh)), NN)
            return tuple(acc)

        zero = jnp.zeros((blk, 1), F32)

        def tile(j, c, acc, diagonal):
            a, c = weights(scores(j), c, diagonal)
            return c, apply(acc, a, j)

        c, acc = tile(i, (zero,) * len(heads), (jnp.zeros((blk, LANES), F32),) * nlb, True)
        c, acc = lax.fori_loop(0, i, lambda m, carry: tile(i - 1 - m, *carry, False), (c, acc))
        o_ref[...] = jnp.concatenate(acc, axis=1).astype(BF16)
        tot_ref[...] = jnp.concatenate([jnp.where(masks[0], c[2 * lb], c[2 * lb + 1]) for lb in range(nlb)], axis=1)
        pl.when((p == nd - 1) & (i == nq - 1))(finish)

    def whole(cb0):
        return pl.BlockSpec((t, width), functools.partial(lambda p, i, cb0: (0, cb0 + p), cb0=cb0))

    tile = pl.BlockSpec((blk, width), lambda p, i: (i, p))
    res = _pcall(
        body, name="attention_fwd", grid=(nd, nq),
        in_specs=[pl.BlockSpec((blk, width), lambda p, i: (i, 3 * nd + p)), whole(4 * nd), whole(5 * nd),
                  pl.BlockSpec((SUM_TERMS * blk, blk), lambda p, i: (0, 0))] + [HBM] * ns,
        out_specs=[tile, tile] + [HBM] * ns,
        out_shape=[jax.ShapeDtypeStruct((t, d), BF16), jax.ShapeDtypeStruct((t, d), F32)]
        + [jax.ShapeDtypeStruct(_full_shape(k, s), BF16) for k, s in zip(kinds, shapes)],
        scratch_shapes=_gather_scratch(ns),
        compiler_params=_params("arbitrary", "arbitrary"),
    )(hin, hin, hin, tri, *shards)
    return res[0], res[1], res[2:]


def _attention_bwd(hin, tot, do, d, parts):
    t, blk, nlb = hin.shape[0], ATT_BLOCK, min(ATT_BWD_LANE_BLOCKS, d // LANES)
    width = nlb * LANES
    nd = d // width
    nq = t // blk
    scale = 1.0 / math.sqrt(HEAD_DIM)
    upper = jnp.triu(jnp.ones((blk, blk), F32)).astype(BF16)
    upper = jnp.concatenate([upper] * SUM_TERMS, axis=1)
    lower = jnp.tril(jnp.ones((blk, blk), F32), -1).astype(BF16)
    npart = len(parts)
    heads = [(lb, half) for lb in range(nlb) for half in range(2)]
    nh = len(heads)

    def body(q_ref, k_ref, v_ref, tot_ref, do_ref, up_ref, low_ref, *rest):
        part_refs, (dq_ref, dk_ref, dv_ref) = rest[:npart], rest[npart:npart + 3]
        land_refs, (send_sems, recv_sems, dk_acc, dv_acc) = rest[npart + 3:2 * npart + 3], rest[2 * npart + 3:]
        p, i = pl.program_id(0), pl.program_id(1)
        start, finish = _exchange_phases(part_refs, land_refs, send_sems, recv_sems)
        pl.when((p == 0) & (i == 0))(start)
        masks = _head_masks()
        lanes = [slice(lb * LANES, (lb + 1) * LANES) for lb in range(nlb)]

        @pl.when(i == 0)
        def _():
            dk_acc[...] = jnp.zeros_like(dk_acc)
            dv_acc[...] = jnp.zeros_like(dv_acc)

        q = q_ref[...].astype(F32) * scale
        dout = do_ref[...]
        qh = [jnp.where(masks[half], q[:, lanes[lb]], 0.0).astype(BF16) for lb, half in heads]
        doh = [jnp.where(masks[half], dout[:, lanes[lb]], jnp.zeros((blk, LANES), BF16)) for lb, half in heads]
        lane = lax.broadcasted_iota(jnp.int32, (8, LANES), 1)
        totals = [lax.dot_general(jnp.where(lane == half * HEAD_DIM, 1.0, 0.0), tot_ref[:, lanes[lb]], (NT, ((), ())),
                                  precision=lax.Precision.HIGHEST, preferred_element_type=F32)[0:1, :] for lb, half in heads]
        after = lax.broadcasted_iota(jnp.int32, (blk, blk), 1) > lax.broadcasted_iota(jnp.int32, (blk, blk), 0)
        up, low = up_ref[...], low_ref[...]

        def products(j):
            j0 = pl.multiple_of(j * blk, blk)
            kj, vj = k_ref[pl.ds(j0, blk), :], v_ref[pl.ds(j0, blk), :]
            return tuple((_dot(kj[:, lanes[lb]], qh[h], NT), _dot(vj[:, lanes[lb]], doh[h], NT)) for h, (lb, _) in enumerate(heads))

        def stage_sums(z, diagonal):
            lb, lf = _log_terms(z)
            if diagonal:
                lf = jnp.where(after, lf, 0.0)
            return jnp.exp(lb).astype(BF16), _dot(up, _split_cat(lf, 0), NN)

        def stage_weights(z, da, sums, rest, diagonal):
            rest = rest - sums[0:1, :]
            a = jnp.exp(z + sums + rest)
            if diagonal:
                a = jnp.where(after, a, 0.0)
            g = da * a
            return a.astype(BF16), g, _dot(low, g.astype(BF16), NN), rest

        def stage_dz(g, sig, earlier, before, diagonal):
            dlf = earlier + before
            dz = g - sig.astype(F32) * (g + dlf)
            if diagonal:
                dz = jnp.where(after, dz, 0.0)
            return dz.astype(BF16), dlf[blk - 1:blk, :] + g[blk - 1:blk, :]

        def grads(cur, rest, before, diagonal):
            sig, sums = zip(*[stage_sums(cur[h][0], diagonal) for h in range(nh)])
            a, g, earlier, rest = zip(*[stage_weights(*cur[h], sums[h], rest[h], diagonal) for h in range(nh)])
            dz, before = zip(*[stage_dz(g[h], sig[h], earlier[h], before[h], diagonal) for h in range(nh)])
            return dz, a, rest, before

        def flush(j, dz, a, dq):
            j0 = pl.multiple_of(j * blk, blk)
            kj = k_ref[pl.ds(j0, blk), :]
            dq = list(dq)
            dk_new = [jnp.zeros((blk, LANES), F32)] * nlb
            dv_new = [jnp.zeros((blk, LANES), F32)] * nlb
            for h, (lb, half) in enumerate(heads):
                kh = kj[:, lanes[lb]]
                dk_new[lb] = dk_new[lb] + _dot(dz[h], qh[h], NN)
                dv_new[lb] = dv_new[lb] + _dot(a[h], doh[h], NN)
                dq[lb] = dq[lb] + _dot(dz[h], jnp.where(masks[half], kh, jnp.zeros_like(kh)), TN)
            for lb in range(nlb):
                dk_acc[pl.ds(j0, blk), lanes[lb]] += dk_new[lb]
                dv_acc[pl.ds(j0, blk), lanes[lb]] += dv_new[lb]
            return tuple(dq)

        def tile(j, rest, before, dq, diagonal):
            dz, a, rest, before = grads(products(j), rest, before, diagonal)
            return rest, before, flush(j, dz, a, dq)

        zero = jnp.zeros((1, blk), F32)
        carry = lax.fori_loop(0, i, lambda j, carry: tile(j, *carry, False),
                              (tuple(totals), (zero,) * nh, (jnp.zeros((blk, LANES), F32),) * nlb))
        dq_ref[...] = (jnp.concatenate(tile(i, *carry, True)[2], axis=1) * scale).astype(BF16)

        @pl.when(i == nq - 1)
        def _():
            dk_ref[...] = dk_acc[...].astype(BF16)
            dv_ref[...] = dv_acc[...].astype(BF16)

        pl.when((p == nd - 1) & (i == nq - 1))(finish)

    def whole(cb0):
        return pl.BlockSpec((t, width), functools.partial(lambda p, i, cb0: (0, cb0 + p), cb0=cb0))

    tile = pl.BlockSpec((blk, width), lambda p, i: (i, p))
    const = pl.BlockSpec((blk, blk), lambda p, i: (0, 0))
    act = jax.ShapeDtypeStruct((t, d), BF16)
    res = _pcall(
        body, name="attention_bwd", grid=(nd, nq),
        in_specs=[pl.BlockSpec((blk, width), lambda p, i: (i, 3 * nd + p)), whole(4 * nd), whole(5 * nd), tile, tile,
                  pl.BlockSpec((blk, SUM_TERMS * blk), lambda p, i: (0, 0)), const] + [HBM] * npart,
        out_specs=[tile, whole(0), whole(0)] + [HBM] * npart,
        out_shape=[act, act, act] + _exchange_shapes(parts),
        scratch_shapes=[pltpu.SemaphoreType.DMA((npart, 3)), pltpu.SemaphoreType.DMA((npart, 3)),
                        pltpu.VMEM((t, width), F32), pltpu.VMEM((t, width), F32)],
        compiler_params=_params("arbitrary", "arbitrary"),
    )(hin, hin, hin, tot, do, upper, lower, *parts)
    return res[0], res[1], res[2], res[3:]


def _adamw(w, g, m, v, *, name):
    r, c = w.shape
    tr = _tile(r, 256, 8) if r % 8 == 0 and r * c * 4 > (1 << 20) else r
    c1, c2 = 1.0 - ADAM_B1 ** ADAM_STEP, 1.0 - ADAM_B2 ** ADAM_STEP

    def body(w_ref, g_ref, m_ref, v_ref, d_ref, nm_ref, nv_ref):
        gv = g_ref[...]
        nm = ADAM_B1 * m_ref[...] + (1.0 - ADAM_B1) * gv
        nv = ADAM_B2 * v_ref[...] + (1.0 - ADAM_B2) * (gv * gv)
        d_ref[...] = -ADAM_LR * ((nm / c1) / (jnp.sqrt(nv / c2) + ADAM_EPS) + ADAM_WD * w_ref[...])
        nm_ref[...] = nm
        nv_ref[...] = nv

    spec = pl.BlockSpec((tr, c), lambda i: (i, 0))
    shape = jax.ShapeDtypeStruct((r, c), F32)
    return _pcall(
        body, name=name, grid=(r // tr,), in_specs=[spec] * 4, out_specs=[spec] * 3, out_shape=[shape] * 3,
        compiler_params=_params("arbitrary"),
    )(w, g, m, v)


HBM = pl.BlockSpec(memory_space=pltpu.HBM)


def _place():
    x, y, c = lax.axis_index("x"), lax.axis_index("y"), lax.axis_index("c")
    return x, y, c, [(1 - x, y), (x, 1 - y), (1 - x, 1 - y)]


def _window(ref, kind, shard_shape, j, half):
    r, w = shard_shape
    if kind == "col":
        return ref.at[pl.ds(half * (r // 2), r // 2), pl.ds(j * w, w)]
    return ref.at[pl.ds(j * r + half * (r // 2), r // 2), :]


def _full_shape(kind, shard_shape):
    r, w = shard_shape
    return (r, 4 * w) if kind == "col" else (4 * r, w)


def _remote(src, dst, send_sem, recv_sem, device):
    return pltpu.make_async_remote_copy(src_ref=src, dst_ref=dst, send_sem=send_sem, recv_sem=recv_sem,
                                        device_id=device, device_id_type=MESH)


def _gather_phases(big_in, big_out, kinds, shapes, send_sems, recv_sems, local_sems):
    x, y, c, chips = _place()
    me, sibling = 2 * x + y, (x, y, 1 - c)
    n = len(big_in)

    def local(a):
        r, w = shapes[a]
        own = big_out[a].at[:, pl.ds(me * w, w)] if kinds[a] == "col" else big_out[a].at[pl.ds(me * r, r), :]
        return pltpu.make_async_copy(big_in[a], own, local_sems.at[a])

    def over_ici(a, k):
        r = shapes[a][0]
        return _remote(big_in[a].at[pl.ds(c * (r // 2), r // 2), :], _window(big_out[a], kinds[a], shapes[a], me, c),
                       send_sems.at[a, k], recv_sems.at[a, k], (*chips[k], c))

    def landed(a, k, half, slot):
        win = _window(big_out[a], kinds[a], shapes[a], 2 * chips[k][0] + chips[k][1], half)
        return _remote(win, win, send_sems.at[a, slot], recv_sems.at[a, slot], sibling)

    def start():
        for a in range(n):
            local(a).start()
            for k in range(3):
                over_ici(a, k).start()

    def forward():
        for a in range(n):
            for k in range(3):
                landed(a, k, c, k).wait_recv()
                landed(a, k, c, 3 + k).start()

    def finish():
        for a in range(n):
            for k in range(3):
                landed(a, k, 1 - c, 3 + k).wait_recv()
        for a in range(n):
            for k in range(3):
                over_ici(a, k).wait_send()
                landed(a, k, c, 3 + k).wait_send()
            local(a).wait()

    return start, forward, finish


def _gather_scratch(n):
    return [pltpu.SemaphoreType.DMA((n, 6)), pltpu.SemaphoreType.DMA((n, 6)), pltpu.SemaphoreType.DMA((n,))]


def _gather_weights(bigs, kinds, smalls):
    nb, ns = len(bigs), len(smalls)
    shapes = [b.shape for b in bigs]

    def body(*refs):
        big_in, small_in = refs[:nb], refs[nb:nb + ns]
        big_out, small_out = refs[nb + ns:2 * nb + ns], refs[2 * nb + ns:2 * (nb + ns)]
        send_sems, recv_sems, local_sems, small_send, small_recv, small_local = refs[2 * (nb + ns):]
        x, y, c, chips = _place()
        me = 2 * x + y
        start, forward, finish = _gather_phases(big_in, big_out, kinds, shapes, send_sems, recv_sems, local_sems)
        start()
        local, sent = [], []
        for s in range(ns):
            local.append(pltpu.make_async_copy(small_in[s], small_out[s].at[me], small_local.at[s]))
            for k, chip in enumerate(chips):
                sent.append(_remote(small_in[s], small_out[s].at[me], small_send.at[s, k], small_recv.at[s, k], (*chip, c)))
        for cp in local + sent:
            cp.start()
        forward()
        for s in range(ns):
            for k, (px, py) in enumerate(chips):
                dst = small_out[s].at[2 * px + py]
                _remote(dst, dst, small_send.at[s, k], small_recv.at[s, k], (px, py, c)).wait_recv()
        finish()
        for cp in sent:
            cp.wait_send()
        for cp in local:
            cp.wait()

    out_shape = [jax.ShapeDtypeStruct(_full_shape(k, s), b.dtype) for b, k, s in zip(bigs, kinds, shapes)]
    out_shape += [jax.ShapeDtypeStruct((4, *s.shape), s.dtype) for s in smalls]
    return _pcall(
        body, name="gather_weights", in_specs=[HBM] * (nb + ns), out_specs=[HBM] * (nb + ns), out_shape=out_shape,
        scratch_shapes=_gather_scratch(nb) + [pltpu.SemaphoreType.DMA((ns, 3)), pltpu.SemaphoreType.DMA((ns, 3)), pltpu.SemaphoreType.DMA((ns,))],
    )(*bigs, *smalls)


def _swap_other_halves(grads, kinds, shapes, *, name):
    n = len(grads)

    def body(*refs):
        g_in, land = refs[:n], refs[n:2 * n]
        send_sems, recv_sems = refs[2 * n:]
        x, y, c, _ = _place()
        sibling = (x, y, 1 - c)
        sent = []
        for a in range(n):
            for j in range(4):
                cp = _remote(_window(g_in[a], kinds[a], shapes[a], j, 1 - c), land[a].at[j], send_sems.at[a, j], recv_sems.at[a, j], sibling)
                cp.start()
                sent.append(cp)
        for a in range(n):
            for j in range(4):
                _remote(land[a].at[j], land[a].at[j], send_sems.at[a, j], recv_sems.at[a, j], sibling).wait_recv()
        for cp in sent:
            cp.wait_send()

    return _pcall(
        body, name=name, in_specs=[HBM] * n, out_specs=[HBM] * n,
        out_shape=[jax.ShapeDtypeStruct((4, s[0] // 2, s[1]), F32) for s in shapes],
        scratch_shapes=[pltpu.SemaphoreType.DMA((n, 4)), pltpu.SemaphoreType.DMA((n, 4))],
    )(*grads)


def _pair_sum(place, g, land, kind, shard_shape, *, name):
    r, w = shard_shape
    hr = r // 2
    tr = _tile(hr, 256, 16)
    nr = hr // tr

    def body(place_ref, g_ref, l_ref, o_ref, own_ref):
        total = g_ref[...] + l_ref[...]
        o_ref[...] = total.astype(BF16)

        @pl.when(pl.program_id(1) == place_ref[1])
        def _():
            own_ref[...] = total

    if kind == "col":
        g_spec = pl.BlockSpec((tr, w), lambda i, j, p: (p[0] * nr + i, j))
    else:
        g_spec = pl.BlockSpec((tr, w), lambda i, j, p: ((2 * j + p[0]) * nr + i, 0))
    part = pl.BlockSpec((None, tr, w), lambda i, j, p: (j, i, 0))
    return _pcall(
        body, name=name,
        grid_spec=pltpu.PrefetchScalarGridSpec(num_scalar_prefetch=1, grid=(nr, 4), in_specs=[g_spec, part],
                                               out_specs=[part, pl.BlockSpec((tr, w), lambda i, j, p: (i, 0))]),
        out_shape=[jax.ShapeDtypeStruct((4, hr, w), BF16), jax.ShapeDtypeStruct((hr, w), F32)],
        compiler_params=_params("arbitrary", "arbitrary"),
    )(place, g, land)


def _exchange_phases(p_in, land, send_sems, recv_sems):
    x, y, c, chips = _place()

    def copy(a, k):
        px, py = chips[k]
        return _remote(p_in[a].at[2 * px + py], land[a].at[k], send_sems.at[a, k], recv_sems.at[a, k], (px, py, c))

    def start():
        for a in range(len(p_in)):
            for k in range(3):
                copy(a, k).start()

    def finish():
        for a in range(len(p_in)):
            for k in range(3):
                copy(a, k).wait()

    return start, finish


def _exchange_shapes(parts):
    return [jax.ShapeDtypeStruct((3, *p.shape[1:]), p.dtype) for p in parts]


def _chip_sum(place, own, land, *, name):
    hr, w = own.shape
    tr = _tile(hr, 256, 16)
    nr = hr // tr

    def body(place_ref, p_ref, l_ref, o_ref):
        o_ref[...] = ((p_ref[...] + l_ref[0].astype(F32)) + l_ref[1].astype(F32)) + l_ref[2].astype(F32)

    return _pcall(
        body, name=name,
        grid_spec=pltpu.PrefetchScalarGridSpec(
            num_scalar_prefetch=1, grid=(nr,),
            in_specs=[pl.BlockSpec((tr, w), lambda i, p: (i, 0)), pl.BlockSpec((3, tr, w), lambda i, p: (0, i, 0))],
            out_specs=pl.BlockSpec((tr, w), lambda i, p: (p[0] * nr + i, 0))),
        out_shape=jax.ShapeDtypeStruct((2 * hr, w), F32),
        compiler_params=_params("arbitrary"),
    )(place, own, land)


def _swap_reduced_halves(halves):
    n = len(halves)

    def body(*refs):
        src, out = refs[:n], refs[n:2 * n]
        send_sems, recv_sems = refs[2 * n:]
        x, y, c, _ = _place()
        sibling = (x, y, 1 - c)
        sent = []
        for a in range(n):
            hr = out[a].shape[0] // 2
            cp = _remote(src[a].at[pl.ds(c * hr, hr), :], out[a].at[pl.ds(c * hr, hr), :], send_sems.at[a], recv_sems.at[a], sibling)
            cp.start()
            sent.append(cp)
        for a in range(n):
            hr = out[a].shape[0] // 2
            other = out[a].at[pl.ds((1 - c) * hr, hr), :]
            _remote(other, other, send_sems.at[a], recv_sems.at[a], sibling).wait_recv()
        for cp in sent:
            cp.wait_send()

    return _pcall(
        body, name="swap_reduced_halves", in_specs=[HBM] * n, out_specs=[HBM] * n,
        out_shape=[jax.ShapeDtypeStruct(h.shape, F32) for h in halves],
        input_output_aliases={a: a for a in range(n)},
        scratch_shapes=[pltpu.SemaphoreType.DMA((n,)), pltpu.SemaphoreType.DMA((n,))],
    )(*halves)


def _sum_small(packed):
    rows = packed.shape[0]

    def body(in_ref, out_ref, land, send_sems, recv_sems, local_sem):
        x, y, c, _ = _place()
        me = 4 * x + 2 * y + c
        own = pltpu.make_async_copy(in_ref, land.at[me], local_sem)
        own.start()
        flips = [(r >> 2, (r >> 1) & 1, r & 1) for r in range(1, 8)]
        sent = []
        for k, (fx, fy, fc) in enumerate(flips):
            cp = _remote(in_ref, land.at[me], send_sems.at[k], recv_sems.at[k], (x ^ fx, y ^ fy, c ^ fc))
            cp.start()
            sent.append(cp)
        for k, (fx, fy, fc) in enumerate(flips):
            src = land.at[4 * (x ^ fx) + 2 * (y ^ fy) + (c ^ fc)]
            _remote(src, src, send_sems.at[k], recv_sems.at[k], (x ^ fx, y ^ fy, c ^ fc)).wait_recv()
        for cp in sent:
            cp.wait_send()
        own.wait()
        total = land[0]
        for dev in range(1, 8):
            total = total + land[dev]
        out_ref[...] = total

    vmem = pl.BlockSpec(memory_space=pltpu.VMEM)
    return _pcall(
        body, name="sum_small", in_specs=[vmem], out_specs=vmem, out_shape=jax.ShapeDtypeStruct(packed.shape, F32),
        scratch_shapes=[pltpu.VMEM((8, rows, LANES), F32), pltpu.SemaphoreType.DMA((7,)), pltpu.SemaphoreType.DMA((7,)), pltpu.SemaphoreType.DMA],
    )(packed)


def _sigmoid(v):
    return 1.0 / (1.0 + jnp.exp(-v))


def kernel(x, meta_tokens, g_pre_mix, w_in, conv_w_mix, w_proj_conv, w_proj_attn, b_gate, w_out, g_post_mix, g_pre_ffn, w_up_gate, conv_w_ffn, w_down, g_post_ffn, loss_target, m_meta_tokens, m_g_pre_mix, m_w_in, m_conv_w_mix, m_w_proj_conv, m_w_proj_attn, m_b_gate, m_w_out, m_g_post_mix, m_g_pre_ffn, m_w_up_gate, m_conv_w_ffn, m_w_down, m_g_post_ffn, v_meta_tokens, v_g_pre_mix, v_w_in, v_conv_w_mix, v_w_proj_conv, v_w_proj_attn, v_b_gate, v_w_out, v_g_post_mix, v_g_pre_ffn, v_w_up_gate, v_conv_w_ffn, v_w_down, v_g_post_ffn):
    seq, d = x.shape[1], x.shape[2]
    f = w_down.shape[1] * 4
    real = N_META + seq
    t = -(-real // ATT_BLOCK) * ATT_BLOCK
    nd = d // LANES
    cx, cy, cc = lax.axis_index("x"), lax.axis_index("y"), lax.axis_index("c")
    chip = 2 * cx + cy
    place = jnp.stack([cc, chip]).astype(jnp.int32)

    big_names = ["w_in", "w_proj_conv", "w_proj_attn", "w_out", "w_up_gate", "w_down"]
    kinds = ["col", "row", "row", "row", "col", "row"]
    big_w = [w_in[0], w_proj_conv[0], w_proj_attn[0], w_out[0], w_up_gate[0], w_down[0]]
    small_w = [meta_tokens, conv_w_mix[0], b_gate[0], conv_w_ffn[0]]
    shards = [w.astype(BF16) for w in big_w]
    shapes = [w.shape for w in big_w]
    gathered = _gather_weights(shards[:1], kinds[:1], small_w)
    wi = gathered[0]
    meta_f, cwm_f, bg_f, cwf_f = (jnp.moveaxis(s, 0, 1).reshape(s.shape[1], -1) for s in gathered[1:])

    h0 = jnp.concatenate([meta_f, x[0], jnp.zeros((t - real, d), F32)], axis=0)
    target = jnp.concatenate([jnp.zeros((N_META, d), F32), loss_target[0], jnp.zeros((t - real, d), F32)], axis=0)
    xn1 = _prenorm(h0, g_pre_mix, name="prenorm_mix")
    hin = _matmul(xn1, wi, name="in_proj")
    yconv = _mixer_conv_fwd(hin, cwm_f, d)
    o, att_tot, (wpc, wpa, wo, wug, wd) = _attention_fwd(hin, d, shards[1:], kinds[1:])

    def gates(gpre, bias):
        return _sigmoid(gpre[:, :d].astype(F32) + bias[0:1]), _sigmoid(gpre[:, d:].astype(F32) + bias[1:2])

    def merge_epi(accs, rows, vecs, row0):
        gate_c, gate_a = gates(rows[0], vecs[0])
        return [gate_c * accs[0] + gate_a * accs[1], accs[0], accs[1]]

    act = ("row", d, BF16)
    merged, bconv, battn = _matmul_rows([(yconv, wpc), (o, wpa)], [(hin, 2 * d, 3)], [bg_f], merge_epi, [act, act, act], name="branch_proj")

    def mix_epi(accs, rows, vecs, row0):
        mix = accs[0]
        h1 = rows[0] + mix * _rms(mix) * vecs[0]
        return [mix, h1, h1 * _rms(h1) * vecs[1]]

    mix, h1, xn2 = _matmul_rows([(merged, wo)], [(h0, d, 0)], [g_post_mix, g_pre_ffn], mix_epi,
                                [("row", d, F32), ("row", d, F32), act], name="out_proj")
    ug = _matmul(xn2, wug, name="up_proj")
    fact = _ffn_conv_fwd(ug, cwf_f, f)

    def loss_epi(accs, rows, vecs, row0):
        ffn = accs[0]
        r = _rms(ffn)
        fh = ffn * r
        h2 = rows[0] + fh * vecs[0]
        rid = row0 + lax.broadcasted_iota(jnp.int32, (ffn.shape[0], 1), 0)
        err = jnp.where((rid >= N_META) & (rid < real), h2 - rows[1], 0.0)
        dy = err * (1.0 / d)
        dffn, dg = _rms_bwd(dy, fh, r, vecs[0])
        loss = jnp.zeros((1, LANES), F32) + 0.5 * jnp.sum(err * err) / d
        return [dffn, dy, dg, loss]

    dffn, dh2, dg_post_ffn, loss_part = _matmul_rows(
        [(fact, wd)], [(h1, d, 0), (target, d, 0)], [g_post_ffn], loss_epi,
        [act, ("row", d, F32), ("acc", 1, d), ("acc", 1, LANES)], name="down_proj_loss")

    gw_down = _matmul_tn(fact, dffn, name="grad_w_down")
    df = _matmul(dffn, wd, nt=True, name="down_proj_bwd")
    du2, dg2, gcw_ffn = _ffn_conv_bwd(ug, df, cwf_f, f)

    def ffn_in_epi(accs, rows, vecs, row0):
        h1v, dyv, mixv = rows
        r3 = _rms(h1v)
        dh1n, dg3 = _rms_bwd(accs[0], h1v * r3, r3, vecs[0])
        dh1 = dyv + dh1n
        r2 = _rms(mixv)
        dmix, dgm = _rms_bwd(dh1, mixv * r2, r2, vecs[1])
        return [dh1, dmix, dg3, dgm]

    dh1, dmix, dg_pre_ffn, dg_post_mix, dug = _matmul_rows_joined(
        [du2, dg2], wug, [(h1, d, 0), (dh2, d, 0), (mix, d, 0)], [g_pre_ffn, g_post_mix], ffn_in_epi,
        [("row", d, F32), act, ("acc", 1, d), ("acc", 1, d)], tk=f // 2, name="up_proj_bwd")
    gw_up = _matmul_tn(xn2, dug, name="grad_w_up_gate")
    gw_out = _matmul_tn(merged, dmix, name="grad_w_out")

    def merge_bwd_epi(accs, rows, vecs, row0):
        dm = accs[0]
        gate_c, gate_a = gates(rows[0], vecs[0])
        dpre_c = dm * rows[1].astype(F32) * gate_c * (1.0 - gate_c)
        dpre_a = dm * rows[2].astype(F32) * gate_a * (1.0 - gate_a)
        dbias = jnp.concatenate([jnp.sum(dpre_c, axis=0, keepdims=True), jnp.sum(dpre_a, axis=0, keepdims=True)], axis=0)
        return [dm * gate_c, dm * gate_a, jnp.concatenate([dpre_c, dpre_a], axis=1), dbias]

    dbconv, dbattn, dgates, gb_gate = _matmul_rows(
        [(dmix, wo)], [(hin, 2 * d, 3), (bconv, d, 0), (battn, d, 0)], [bg_f], merge_bwd_epi,
        [act, act, ("row", 2 * d, BF16), ("acc", 2, d)], nt=True, name="out_proj_bwd")
    gw_pc = _matmul_tn(yconv, dbconv, name="grad_w_proj_conv")
    gw_pa = _matmul_tn(o, dbattn, name="grad_w_proj_attn")
    dyconv = _matmul(dbconv, wpc, nt=True, name="proj_conv_bwd")
    do = _matmul(dbattn, wpa, nt=True, name="proj_attn_bwd")
    db, dc, dhc, gcw_mix = _mixer_conv_bwd(hin, dyconv, cwm_f, d)

    def pair_sums(grads, first):
        landed = _swap_other_halves(grads, kinds[first:first + len(grads)], shapes[first:first + len(grads)],
                                    name=f"swap_other_halves_{big_names[first]}")
        return zip(*[_pair_sum(place, g, l, kinds[first + a], shapes[first + a], name=f"pair_sum_{big_names[first + a]}")
                     for a, (g, l) in enumerate(zip(grads, landed))])

    parts, own = pair_sums([gw_pc, gw_pa, gw_out, gw_up, gw_down], 1)
    dq, dk, dv, landed = _attention_bwd(hin, att_tot, do, d, parts)

    def in_epi(accs, rows, vecs, row0):
        h0v, dh1v = rows
        r1 = _rms(h0v)
        dh0n, dg1 = _rms_bwd(accs[0], h0v * r1, r1, vecs[0])
        return [dh1v + dh0n, dg1]

    dhin = [db, dc, dhc, dq, dk, dv, dgates]
    gw_in = _matmul_tn_segments(xn1, dhin, name="grad_w_in")
    parts_in, own_in = pair_sums([gw_in], 0)
    dh0, dg_pre_mix, landed_in = _matmul_rows_joined(
        dhin, wi, [(h0, d, 0), (dh1, d, 0)], [g_pre_mix], in_epi,
        [("row", d, F32), ("acc", 1, d)], tk=d, name="in_proj_bwd", joined=False, parts=parts_in)
    grad_x = dh0[N_META:real][None]

    landed = [landed_in] + list(landed)
    halves = [_chip_sum(place, p, l, name=f"chip_sum_{n}") for p, l, n in zip(list(own_in) + list(own), landed, big_names)]
    big_r = _swap_reduced_halves(halves)

    small_g = [dh0[:N_META], dg_pre_mix, gcw_mix, gb_gate, dg_post_mix, dg_pre_ffn, gcw_ffn, dg_post_ffn, loss_part]
    flat = jnp.concatenate([s.reshape(-1) for s in small_g])
    pad = -flat.shape[0] % (8 * LANES)
    summed = _sum_small(jnp.pad(flat, (0, pad)).reshape(-1, LANES)).reshape(-1)
    small_r, pos = [], 0
    for s in small_g:
        small_r.append(summed[pos:pos + s.size].reshape(s.shape))
        pos += s.size
    g_meta, g_g1, g_cwm, g_bg, g_g2, g_g3, g_cwf, g_g4, loss_row = small_r
    loss = loss_row[0, 0]

    def my_cols(full, width):
        return lax.dynamic_slice_in_dim(full, chip * width, width, axis=1)

    grads = {
        "meta_tokens": my_cols(g_meta, d // 4), "g_pre_mix": g_g1, "w_in": big_r[0][None],
        "conv_w_mix": my_cols(g_cwm, d // 4)[None], "w_proj_conv": big_r[1][None], "w_proj_attn": big_r[2][None],
        "b_gate": my_cols(g_bg, d // 4)[None], "w_out": big_r[3][None], "g_post_mix": g_g2, "g_pre_ffn": g_g3,
        "w_up_gate": big_r[4][None], "conv_w_ffn": my_cols(g_cwf, f // 4)[None], "w_down": big_r[5][None], "g_post_ffn": g_g4,
    }
    weights = dict(meta_tokens=meta_tokens, g_pre_mix=g_pre_mix, w_in=w_in, conv_w_mix=conv_w_mix, w_proj_conv=w_proj_conv,
                   w_proj_attn=w_proj_attn, b_gate=b_gate, w_out=w_out, g_post_mix=g_post_mix, g_pre_ffn=g_pre_ffn,
                   w_up_gate=w_up_gate, conv_w_ffn=conv_w_ffn, w_down=w_down, g_post_ffn=g_post_ffn)
    m_in = dict(meta_tokens=m_meta_tokens, g_pre_mix=m_g_pre_mix, w_in=m_w_in, conv_w_mix=m_conv_w_mix, w_proj_conv=m_w_proj_conv,
                w_proj_attn=m_w_proj_attn, b_gate=m_b_gate, w_out=m_w_out, g_post_mix=m_g_post_mix, g_pre_ffn=m_g_pre_ffn,
                w_up_gate=m_w_up_gate, conv_w_ffn=m_conv_w_ffn, w_down=m_w_down, g_post_ffn=m_g_post_ffn)
    v_in = dict(meta_tokens=v_meta_tokens, g_pre_mix=v_g_pre_mix, w_in=v_w_in, conv_w_mix=v_conv_w_mix, w_proj_conv=v_w_proj_conv,
                w_proj_attn=v_w_proj_attn, b_gate=v_b_gate, w_out=v_w_out, g_post_mix=v_g_post_mix, g_pre_ffn=v_g_pre_ffn,
                w_up_gate=v_w_up_gate, conv_w_ffn=v_conv_w_ffn, w_down=v_w_down, g_post_ffn=v_g_post_ffn)
    names = list(weights)
    deltas, new_m, new_v = [], [], []
    for n in names:
        shape = weights[n].shape
        two_d = (-1, shape[-1])
        dl, nm, nv = _adamw(weights[n].reshape(two_d), grads[n].reshape(two_d), m_in[n].reshape(two_d), v_in[n].reshape(two_d),
                            name=f"adamw_{n}")
        deltas.append(dl.reshape(shape))
        new_m.append(nm.reshape(shape))
        new_v.append(nv.reshape(shape))
    return (loss, grad_x, *[grads[n].reshape(weights[n].shape) for n in names], *deltas, *new_m, *new_v)
```

```python
import functools
import math

import jax
import jax.numpy as jnp
from jax import lax
from jax.experimental import pallas as pl
from jax.experimental.pallas import tpu as pltpu

F32 = jnp.float32
BF16 = jnp.bfloat16

N_META = 16
HEAD_DIM = 64
LANES = 128
RMS_EPS = 1e-6
ATT_BLOCK = 256
ATT_FWD_LANE_BLOCKS = 8
ATT_BWD_LANE_BLOCKS = 2
VMEM_LIMIT = 56 * 1024 * 1024

ADAM_LR = 0.001
ADAM_B1 = 0.9
ADAM_B2 = 0.999
ADAM_EPS = 1e-08
ADAM_WD = 0.01
ADAM_STEP = 10

MESH = pl.DeviceIdType.MESH


def _tile(n, cap, unit=LANES):
    d = (min(cap, n) // unit) * unit
    while d >= unit:
        if n % d == 0:
            return d
        d -= unit
    raise ValueError(f"no tile for {n} under {cap}")


MXU_EDGE = 256


def _mxu_tile(n, cap):
    if n <= cap:
        return n
    return _tile(n, cap, MXU_EDGE) if n % MXU_EDGE == 0 else _tile(n, cap)


def _params(*sem):
    return pltpu.CompilerParams(dimension_semantics=sem, vmem_limit_bytes=VMEM_LIMIT)


def _pcall(body, **kw):
    return pl.pallas_call(body, **kw)


def _dot(a, b, dims):
    return lax.dot_general(a, b, (dims, ((), ())), preferred_element_type=F32)


NN = ((1,), (0,))
NT = ((1,), (1,))
TN = ((0,), (0,))


def _matmul(a, b, *, nt=False, out_dtype=BF16, name):
    m, kdim = a.shape
    n = b.shape[0] if nt else b.shape[1]
    tm, tn, tk = _tile(m, 640, 16), _mxu_tile(n, 2816), _mxu_tile(kdim, 2816)
    nk = kdim // tk

    def body(a_ref, b_ref, o_ref, *scratch):
        p = _dot(a_ref[...], b_ref[...], NT if nt else NN)
        if nk == 1:
            o_ref[...] = p.astype(o_ref.dtype)
            return
        acc, k = scratch[0], pl.program_id(2)

        @pl.when(k == 0)
        def _():
            acc[...] = p

        @pl.when(k > 0)
        def _():
            acc[...] += p

        @pl.when(k == nk - 1)
        def _():
            o_ref[...] = acc[...].astype(o_ref.dtype)

    b_spec = pl.BlockSpec((tn, tk), lambda j, i, k: (j, k)) if nt else pl.BlockSpec((tk, tn), lambda j, i, k: (k, j))
    return _pcall(
        body, name=name, grid=(n // tn, m // tm, nk),
        in_specs=[pl.BlockSpec((tm, tk), lambda j, i, k: (i, k)), b_spec],
        out_specs=pl.BlockSpec((tm, tn), lambda j, i, k: (i, j)),
        out_shape=jax.ShapeDtypeStruct((m, n), out_dtype),
        scratch_shapes=[pltpu.VMEM((tm, tn), F32)] if nk > 1 else [],
        compiler_params=_params("arbitrary", "arbitrary", "arbitrary"),
    )(a, b)


def _matmul_tn(a, b, *, name):
    t, ka = a.shape
    nb = b.shape[1]
    tb = _mxu_tile(nb, 1024)
    budget = VMEM_LIMIT * 5 // 7
    ta = next(c for c in range(ka, 0, -LANES) if ka % c == 0 and 2 * (2 * t * (c + tb) + 4 * c * tb) <= budget)

    def body(a_ref, b_ref, o_ref):
        o_ref[...] = _dot(a_ref[...], b_ref[...], TN)

    return _pcall(
        body, name=name, grid=(ka // ta, nb // tb),
        in_specs=[pl.BlockSpec((t, ta), lambda i, j: (0, i)), pl.BlockSpec((t, tb), lambda i, j: (0, j))],
        out_specs=pl.BlockSpec((ta, tb), lambda i, j: (i, j)),
        out_shape=jax.ShapeDtypeStruct((ka, nb), F32),
        compiler_params=_params("arbitrary", "arbitrary"),
    )(a, b)


def _matmul_rows(pairs, rows, vecs, epi, outs, *, nt=False, name):
    m, kdim = pairs[0][0].shape
    n = pairs[0][1].shape[0] if nt else pairs[0][1].shape[1]
    tm, tk = _tile(m, 640, 16), _mxu_tile(kdim, 2816)
    nk, npair = kdim // tk, len(pairs)

    def body(*refs):
        a_refs, b_refs = refs[0:2 * npair:2], refs[1:2 * npair:2]
        pos = 2 * npair
        row_refs = refs[pos:pos + len(rows)]
        pos += len(rows)
        vec_refs = refs[pos:pos + len(vecs)]
        pos += len(vecs)
        out_refs = refs[pos:pos + len(outs)]
        accs = refs[pos + len(outs):]
        i, k = pl.program_id(0), pl.program_id(1)
        prods = [_dot(a[...], b[...], NT if nt else NN) for a, b in zip(a_refs, b_refs)]
        if nk > 1:
            @pl.when(k == 0)
            def _():
                for acc, p in zip(accs, prods):
                    acc[...] = p

            @pl.when(k > 0)
            def _():
                for acc, p in zip(accs, prods):
                    acc[...] += p

        @pl.when(k == nk - 1)
        def _():
            vals = [acc[...] for acc in accs] if nk > 1 else prods
            res = epi(vals, [r[...] for r in row_refs], [v[...] for v in vec_refs], i * tm)
            for o_ref, spec, val in zip(out_refs, outs, res):
                if spec[0] == "row":
                    o_ref[...] = val.astype(o_ref.dtype)
                else:
                    @pl.when(i == 0)
                    def _():
                        o_ref[...] = val

                    @pl.when(i > 0)
                    def _():
                        o_ref[...] += val

    in_specs, args = [], []
    for a, b in pairs:
        in_specs += [pl.BlockSpec((tm, tk), lambda i, k: (i, k)),
                     pl.BlockSpec((n, tk), lambda i, k: (0, k)) if nt else pl.BlockSpec((tk, n), lambda i, k: (k, 0))]
        args += [a, b]
    for arr, width, cb in rows:
        in_specs.append(pl.BlockSpec((tm, width), functools.partial(lambda i, k, cb: (i, cb), cb=cb)))
        args.append(arr)
    for v in vecs:
        in_specs.append(pl.BlockSpec(v.shape, lambda i, k: (0, 0)))
        args.append(v)
    out_specs, out_shape = [], []
    for spec in outs:
        if spec[0] == "row":
            out_specs.append(pl.BlockSpec((tm, spec[1]), lambda i, k: (i, 0)))
            out_shape.append(jax.ShapeDtypeStruct((m, spec[1]), spec[2]))
        else:
            out_specs.append(pl.BlockSpec((spec[1], spec[2]), lambda i, k: (0, 0)))
            out_shape.append(jax.ShapeDtypeStruct((spec[1], spec[2]), F32))
    return _pcall(
        body, name=name, grid=(m // tm, nk), in_specs=in_specs, out_specs=out_specs, out_shape=out_shape,
        scratch_shapes=[pltpu.VMEM((tm, n), F32) for _ in pairs] if nk > 1 else [],
        compiler_params=_params("arbitrary", "arbitrary"),
    )(*args)


def _matmul_rows_joined(segments, b, rows, vecs, epi, outs, *, tk, name, joined=True, parts=()):
    m = segments[0].shape[0]
    n, kdim = b.shape
    tm = _tile(m, 640, 16)
    nm, nk, npart = m // tm, kdim // tk, len(parts)
    first = [0]
    for s in segments:
        assert s.shape[1] % tk == 0
        first.append(first[-1] + s.shape[1] // tk)
    assert first[-1] == nk

    def body(*refs):
        seg_refs, b_ref = refs[:len(segments)], refs[len(segments)]
        pos = len(segments) + 1
        row_refs = refs[pos:pos + len(rows)]
        pos += len(rows)
        vec_refs = refs[pos:pos + len(vecs)]
        pos += len(vecs)
        part_refs = refs[pos:pos + npart]
        pos += npart
        out_refs = refs[pos:pos + len(outs)]
        pos += len(outs)
        joined_ref = refs[pos] if joined else None
        pos += int(joined)
        land_refs, acc = refs[pos:pos + npart], refs[pos + npart]
        i, k = pl.program_id(0), pl.program_id(1)
        if npart:
            start, finish = _exchange_phases(part_refs, land_refs, *refs[pos + npart + 1:])
            pl.when((i == 0) & (k == 0))(start)

        @pl.when(k == 0)
        def _():
            acc[...] = jnp.zeros_like(acc)

        for s, seg in enumerate(seg_refs):
            @pl.when((k >= first[s]) & (k < first[s + 1]))
            def _():
                if joined:
                    joined_ref[...] = seg[...]
                acc[...] += _dot(seg[...], b_ref[...], NT)

        @pl.when(k == nk - 1)
        def _():
            res = epi([acc[...]], [r[...] for r in row_refs], [v[...] for v in vec_refs], i * tm)
            for o_ref, spec, val in zip(out_refs, outs, res):
                if spec[0] == "row":
                    o_ref[...] = val.astype(o_ref.dtype)
                else:
                    @pl.when(i == 0)
                    def _():
                        o_ref[...] = val

                    @pl.when(i > 0)
                    def _():
                        o_ref[...] += val

        if npart:
            pl.when((i == nm - 1) & (k == nk - 1))(finish)

    def seg_spec(s):
        last = first[s + 1] - first[s] - 1
        return pl.BlockSpec((tm, tk), lambda i, k: (i, jnp.clip(k - first[s], 0, last)))

    in_specs = [seg_spec(s) for s in range(len(segments))] + [pl.BlockSpec((n, tk), lambda i, k: (0, k))]
    args = [*segments, b]
    for arr, width, cb in rows:
        in_specs.append(pl.BlockSpec((tm, width), functools.partial(lambda i, k, cb: (i, cb), cb=cb)))
        args.append(arr)
    for v in vecs:
        in_specs.append(pl.BlockSpec(v.shape, lambda i, k: (0, 0)))
        args.append(v)
    in_specs += [HBM] * npart
    args += list(parts)
    out_specs, out_shape = [], []
    for spec in outs:
        if spec[0] == "row":
            out_specs.append(pl.BlockSpec((tm, spec[1]), lambda i, k: (i, 0)))
            out_shape.append(jax.ShapeDtypeStruct((m, spec[1]), spec[2]))
        else:
            out_specs.append(pl.BlockSpec((spec[1], spec[2]), lambda i, k: (0, 0)))
            out_shape.append(jax.ShapeDtypeStruct((spec[1], spec[2]), F32))
    if joined:
        out_specs.append(pl.BlockSpec((tm, tk), lambda i, k: (i, k)))
        out_shape.append(jax.ShapeDtypeStruct((m, kdim), segments[0].dtype))
    out_specs += [HBM] * npart
    out_shape += _exchange_shapes(parts)
    sems = [pltpu.SemaphoreType.DMA((npart, 3)), pltpu.SemaphoreType.DMA((npart, 3))] if npart else []
    return _pcall(
        body, name=name, grid=(nm, nk), in_specs=in_specs, out_specs=out_specs, out_shape=out_shape,
        scratch_shapes=[pltpu.VMEM((tm, n), F32)] + sems,
        compiler_params=_params("arbitrary", "arbitrary"),
    )(*args)


def _matmul_tn_segments(a, segments, *, name):
    t, ka = a.shape
    tb = MXU_EDGE
    first = [0]
    for s in segments:
        assert s.shape[1] % tb == 0
        first.append(first[-1] + s.shape[1] // tb)
    budget = VMEM_LIMIT * 13 // 14
    ta = next(c for c in range(ka, 0, -LANES) if ka % c == 0 and 2 * (2 * t * (c + len(segments) * tb) + 4 * c * tb) <= budget)

    def body(a_ref, *refs):
        seg_refs, o_ref = refs[:-1], refs[-1]
        j = pl.program_id(1)
        for s, seg in enumerate(seg_refs):
            @pl.when((j >= first[s]) & (j < first[s + 1]))
            def _():
                o_ref[...] = _dot(a_ref[...], seg[...], TN)

    def seg_spec(s):
        last = first[s + 1] - first[s] - 1
        return pl.BlockSpec((t, tb), lambda i, j: (0, jnp.clip(j - first[s], 0, last)))

    return _pcall(
        body, name=name, grid=(ka // ta, first[-1]),
        in_specs=[pl.BlockSpec((t, ta), lambda i, j: (0, i))] + [seg_spec(s) for s in range(len(segments))],
        out_specs=pl.BlockSpec((ta, tb), lambda i, j: (i, j)),
        out_shape=jax.ShapeDtypeStruct((ka, first[-1] * tb), F32),
        compiler_params=_params("arbitrary", "arbitrary"),
    )(a, *segments)


def _rms(v):
    return lax.rsqrt(jnp.mean(v * v, axis=-1, keepdims=True) + RMS_EPS)


def _rms_bwd(dz, vhat, r, g):
    t = dz * g
    dv = r * (t - vhat * jnp.mean(t * vhat, axis=-1, keepdims=True))
    return dv, jnp.sum(dz * vhat, axis=0, keepdims=True)


def _prenorm(h, g, *, name):
    m, d = h.shape
    tm = _tile(m, 640, 16)

    def body(h_ref, g_ref, o_ref):
        v = h_ref[...]
        o_ref[...] = (v * _rms(v) * g_ref[...]).astype(BF16)

    return _pcall(
        body, name=name, grid=(m // tm,),
        in_specs=[pl.BlockSpec((tm, d), lambda i: (i, 0)), pl.BlockSpec((1, d), lambda i: (0, 0))],
        out_specs=pl.BlockSpec((tm, d), lambda i: (i, 0)),
        out_shape=jax.ShapeDtypeStruct((m, d), BF16),
        compiler_params=_params("arbitrary"),
    )(h, g)


SUBLANES = 8


def _shift_down(u, k):
    r = pltpu.roll(u, k, 0)
    rows = lax.broadcasted_iota(jnp.int32, (SUBLANES, u.shape[1]), 0)
    return jnp.concatenate([jnp.where(rows >= k, r[:SUBLANES], 0.0), r[SUBLANES:]], axis=0)


def _shift_up(u, k):
    n = u.shape[0]
    r = pltpu.roll(u, n - k, 0)
    rows = lax.broadcasted_iota(jnp.int32, (SUBLANES, u.shape[1]), 0)
    return jnp.concatenate([r[:n - SUBLANES], jnp.where(rows < SUBLANES - k, r[n - SUBLANES:], 0.0)], axis=0)


def _taps(u):
    return _shift_down(u, 2), _shift_down(u, 1), u


def _conv(taps, w):
    return w[0:1] * taps[0] + w[1:2] * taps[1] + w[2:3] * taps[2]


def _conv_bwd(dcu, taps, w):
    du = w[2:3] * dcu + w[1:2] * _shift_up(dcu, 1) + w[0:1] * _shift_up(dcu, 2)
    return du, [jnp.sum(dcu * tap, axis=0, keepdims=True) for tap in taps]


def _strip(arr, t, cb0):
    return pl.BlockSpec((t, LANES), functools.partial(lambda s, cb0: (0, cb0 + s), cb0=cb0))


def _mixer_conv_fwd(hin, w, d):
    t, ns = hin.shape[0], d // LANES

    def body(b_ref, c_ref, h_ref, w_ref, y_ref):
        u = c_ref[...].astype(F32) * h_ref[...].astype(F32)
        y_ref[...] = (b_ref[...].astype(F32) * _conv(_taps(u), w_ref[...])).astype(BF16)

    return _pcall(
        body, name="mixer_conv_fwd", grid=(ns,),
        in_specs=[_strip(hin, t, 0), _strip(hin, t, ns), _strip(hin, t, 2 * ns), pl.BlockSpec((3, LANES), lambda s: (0, s))],
        out_specs=pl.BlockSpec((t, LANES), lambda s: (0, s)),
        out_shape=jax.ShapeDtypeStruct((t, d), BF16),
        compiler_params=_params("arbitrary"),
    )(hin, hin, hin, w)


def _mixer_conv_bwd(hin, dy, w, d):
    t, ns = hin.shape[0], d // LANES

    def body(b_ref, c_ref, h_ref, dy_ref, w_ref, db_ref, dc_ref, dh_ref, dw_ref):
        b, c, h, g = (r[...].astype(F32) for r in (b_ref, c_ref, h_ref, dy_ref))
        wv = w_ref[...]
        taps = _taps(c * h)
        db_ref[...] = (g * _conv(taps, wv)).astype(BF16)
        du, dw = _conv_bwd(g * b, taps, wv)
        dc_ref[...] = (du * h).astype(BF16)
        dh_ref[...] = (du * c).astype(BF16)
        for k in range(3):
            dw_ref[k:k + 1, :] = dw[k]

    col = pl.BlockSpec((t, LANES), lambda s: (0, s))
    act = jax.ShapeDtypeStruct((t, d), BF16)
    return _pcall(
        body, name="mixer_conv_bwd", grid=(ns,),
        in_specs=[_strip(hin, t, 0), _strip(hin, t, ns), _strip(hin, t, 2 * ns), col, pl.BlockSpec((3, LANES), lambda s: (0, s))],
        out_specs=[col, col, col, pl.BlockSpec((3, LANES), lambda s: (0, s))],
        out_shape=[act, act, act, jax.ShapeDtypeStruct((3, d), F32)],
        compiler_params=_params("arbitrary"),
    )(hin, hin, hin, dy, w)


GELU_C = math.sqrt(2.0 / math.pi)
GELU_A = 0.044715


def _gelu_tanh(x):
    return jnp.tanh(GELU_C * (x + GELU_A * x * x * x))


def _ffn_conv_fwd(ug, w, f):
    t, ns = ug.shape[0], f // LANES

    def body(u_ref, g_ref, w_ref, o_ref):
        cu = _conv(_taps(u_ref[...].astype(F32)), w_ref[...])
        o_ref[...] = (0.5 * cu * (1.0 + _gelu_tanh(cu)) * g_ref[...].astype(F32)).astype(BF16)

    return _pcall(
        body, name="ffn_conv_fwd", grid=(ns,),
        in_specs=[_strip(ug, t, 0), _strip(ug, t, ns), pl.BlockSpec((3, LANES), lambda s: (0, s))],
        out_specs=pl.BlockSpec((t, LANES), lambda s: (0, s)),
        out_shape=jax.ShapeDtypeStruct((t, f), BF16),
        compiler_params=_params("arbitrary"),
    )(ug, ug, w)


def _ffn_conv_bwd(ug, df, w, f):
    t, ns = ug.shape[0], f // LANES

    def body(u_ref, g_ref, df_ref, w_ref, du_ref, dg_ref, dw_ref):
        u, g, d = (r[...].astype(F32) for r in (u_ref, g_ref, df_ref))
        wv = w_ref[...]
        taps = _taps(u)
        cu = _conv(taps, wv)
        th = _gelu_tanh(cu)
        half = 0.5 * (1.0 + th)
        dg_ref[...] = (d * (cu * half)).astype(BF16)
        dgelu = half + (0.5 * GELU_C) * cu * (1.0 - th * th) * (1.0 + (3.0 * GELU_A) * (cu * cu))
        du, dw = _conv_bwd(d * g * dgelu, taps, wv)
        du_ref[...] = du.astype(BF16)
        for k in range(3):
            dw_ref[k:k + 1, :] = dw[k]

    col = pl.BlockSpec((t, LANES), lambda s: (0, s))
    act = jax.ShapeDtypeStruct((t, f), BF16)
    return _pcall(
        body, name="ffn_conv_bwd", grid=(ns,),
        in_specs=[_strip(ug, t, 0), _strip(ug, t, ns), col, pl.BlockSpec((3, LANES), lambda s: (0, s))],
        out_specs=[col, col, pl.BlockSpec((3, LANES), lambda s: (0, s))],
        out_shape=[act, act, jax.ShapeDtypeStruct((3, f), F32)],
        compiler_params=_params("arbitrary"),
    )(ug, ug, df, w)


def _log_terms(z):
    minus_abs = lax.bitcast_convert_type(lax.bitcast_convert_type(z, jnp.uint32) | jnp.uint32(0x80000000), F32)
    lb = jnp.minimum(z, 0.0) - jnp.log(1.0 + jnp.exp(minus_abs))
    return lb, lb - z


SUM_TERMS = 1


def _split_cat(v, axis):
    terms = [v.astype(BF16)]
    for _ in range(SUM_TERMS - 1):
        v = v - terms[-1].astype(F32)
        terms.append(v.astype(BF16))
    return terms[0] if SUM_TERMS == 1 else jnp.concatenate(terms, axis=axis)


def _head_masks():
    lane = lax.broadcasted_iota(jnp.int32, (1, LANES), 1)
    return lane < HEAD_DIM, lane >= HEAD_DIM


def _attention_fwd(hin, d, shards, kinds):
    t, blk, nlb = hin.shape[0], ATT_BLOCK, min(ATT_FWD_LANE_BLOCKS, d // LANES)
    width = nlb * LANES
    nd = d // width
    nq = t // blk
    scale = 1.0 / math.sqrt(HEAD_DIM)
    tri = jnp.tril(jnp.ones((blk, blk), F32)).astype(BF16)
    tri = jnp.concatenate([tri] * SUM_TERMS, axis=0)
    ns = len(shards)
    shapes = [s.shape for s in shards]
    heads = [(lb, half) for lb in range(nlb) for half in range(2)]

    def body(q_ref, k_ref, v_ref, tri_ref, *rest):
        shard_refs, (o_ref, tot_ref), full_refs, sems = rest[:ns], rest[ns:ns + 2], rest[ns + 2:2 * ns + 2], rest[2 * ns + 2:]
        p, i = pl.program_id(0), pl.program_id(1)
        start, forward, finish = _gather_phases(shard_refs, full_refs, kinds, shapes, *sems)
        pl.when((p == 0) & (i == 0))(start)
        pl.when((p == nd // 2) & (i == (nq // 2 if nd == 1 else 0)))(forward)
        masks = _head_masks()
        lanes = [slice(lb * LANES, (lb + 1) * LANES) for lb in range(nlb)]
        q = q_ref[...].astype(F32) * scale
        qh = [jnp.where(masks[half], q[:, lanes[lb]], 0.0).astype(BF16) for lb, half in heads]
        after = lax.broadcasted_iota(jnp.int32, (blk, blk), 0) > lax.broadcasted_iota(jnp.int32, (blk, blk), 1)
        tri_v = tri_ref[...]

        def scores(j):
            j0 = pl.multiple_of(j * blk, blk)
            kj = k_ref[pl.ds(j0, blk), :]
            return tuple(_dot(qh[h], kj[:, lanes[lb]], NT) for h, (lb, _) in enumerate(heads))

        def weights(z, c, diagonal):
            halves = [slice(r, r + blk // 2) for r in range(0, blk, blk // 2)]
            sums, a = {}, {}

            def stage_sums(h):
                parts = []
                for rows in halves:
                    lf = _log_terms(z[h][rows])[1]
                    if diagonal:
                        lf = jnp.where(after[rows], lf, 0.0)
                    parts.append(_dot(_split_cat(lf, 1), tri_v, NN))
                sums[h] = jnp.concatenate(parts, axis=0)

            def stage_weights(h):
                ah = jnp.exp(z[h] + sums[h] + c[h])
                a[h] = (jnp.where(after, ah, 0.0) if diagonal else ah).astype(BF16)

            order = range(len(heads))
            for h in order:
                stage_sums(h)
            for h in order:
                stage_weights(h)
            return tuple(a[h] for h in order), tuple(c[h] + sums[h][:, 0:1] for h in order)

        def apply(acc, a, j):
            j0 = pl.multiple_of(j * blk, blk)
            vj = v_ref[pl.ds(j0, blk), :]
            acc = list(acc)
            for h, (lb, half) in enumerate(heads):
                vh = vj[:, lanes[lb]]
                acc[lb] = acc[lb] + _dot(a[h], jnp.where(masks[half], vh, jnp.zeros_like(vh)), NN)
            return tuple(acc)

        zero = jnp.zeros((blk, 1), F32)

        def tile(j, c, acc, diagonal):
            a, c = weights(scores(j), c, diagonal)
            return c, apply(acc, a, j)

        c, acc = tile(i, (zero,) * len(heads), (jnp.zeros((blk, LANES), F32),) * nlb, True)
        c, acc = lax.fori_loop(0, i, lambda m, carry: tile(i - 1 - m, *carry, False), (c, acc))
        o_ref[...] = jnp.concatenate(acc, axis=1).astype(BF16)
        tot_ref[...] = jnp.concatenate([jnp.where(masks[0], c[2 * lb], c[2 * lb + 1]) for lb in range(nlb)], axis=1)
        pl.when((p == nd - 1) & (i == nq - 1))(finish)

    def whole(cb0):
        return pl.BlockSpec((t, width), functools.partial(lambda p, i, cb0: (0, cb0 + p), cb0=cb0))

    tile = pl.BlockSpec((blk, width), lambda p, i: (i, p))
    res = _pcall(
        body, name="attention_fwd", grid=(nd, nq),
        in_specs=[pl.BlockSpec((blk, width), lambda p, i: (i, 3 * nd + p)), whole(4 * nd), whole(5 * nd),
                  pl.BlockSpec((SUM_TERMS * blk, blk), lambda p, i: (0, 0))] + [HBM] * ns,
        out_specs=[tile, tile] + [HBM] * ns,
        out_shape=[jax.ShapeDtypeStruct((t, d), BF16), jax.ShapeDtypeStruct((t, d), F32)]
        + [jax.ShapeDtypeStruct(_full_shape(k, s), BF16) for k, s in zip(kinds, shapes)],
        scratch_shapes=_gather_scratch(ns),
        compiler_params=_params("arbitrary", "arbitrary"),
    )(hin, hin, hin, tri, *shards)
    return res[0], res[1], res[2:]


def _attention_bwd(hin, tot, do, d, parts):
    t, blk, nlb = hin.shape[0], ATT_BLOCK, min(ATT_BWD_LANE_BLOCKS, d // LANES)
    width = nlb * LANES
    nd = d // width
    nq = t // blk
    scale = 1.0 / math.sqrt(HEAD_DIM)
    upper = jnp.triu(jnp.ones((blk, blk), F32)).astype(BF16)
    upper = jnp.concatenate([upper] * SUM_TERMS, axis=1)
    lower = jnp.tril(jnp.ones((blk, blk), F32), -1).astype(BF16)
    npart = len(parts)
    heads = [(lb, half) for lb in range(nlb) for half in range(2)]
    nh = len(heads)

    def body(q_ref, k_ref, v_ref, tot_ref, do_ref, up_ref, low_ref, *rest):
        part_refs, (dq_ref, dk_ref, dv_ref) = rest[:npart], rest[npart:npart + 3]
        land_refs, (send_sems, recv_sems, dk_acc, dv_acc) = rest[npart + 3:2 * npart + 3], rest[2 * npart + 3:]
        p, i = pl.program_id(0), pl.program_id(1)
        start, finish = _exchange_phases(part_refs, land_refs, send_sems, recv_sems)
        pl.when((p == 0) & (i == 0))(start)
        masks = _head_masks()
        lanes = [slice(lb * LANES, (lb + 1) * LANES) for lb in range(nlb)]

        @pl.when(i == 0)
        def _():
            dk_acc[...] = jnp.zeros_like(dk_acc)
            dv_acc[...] = jnp.zeros_like(dv_acc)

        q = q_ref[...].astype(F32) * scale
        dout = do_ref[...]
        qh = [jnp.where(masks[half], q[:, lanes[lb]], 0.0).astype(BF16) for lb, half in heads]
        doh = [jnp.where(masks[half], dout[:, lanes[lb]], jnp.zeros((blk, LANES), BF16)) for lb, half in heads]
        lane = lax.broadcasted_iota(jnp.int32, (8, LANES), 1)
        totals = [lax.dot_general(jnp.where(lane == half * HEAD_DIM, 1.0, 0.0), tot_ref[:, lanes[lb]], (NT, ((), ())),
                                  precision=lax.Precision.HIGHEST, preferred_element_type=F32)[0:1, :] for lb, half in heads]
        after = lax.broadcasted_iota(jnp.int32, (blk, blk), 1) > lax.broadcasted_iota(jnp.int32, (blk, blk), 0)
        up, low = up_ref[...], low_ref[...]

        def products(j):
            j0 = pl.multiple_of(j * blk, blk)
            kj, vj = k_ref[pl.ds(j0, blk), :], v_ref[pl.ds(j0, blk), :]
            return tuple((_dot(kj[:, lanes[lb]], qh[h], NT), _dot(vj[:, lanes[lb]], doh[h], NT)) for h, (lb, _) in enumerate(heads))

        def stage_sums(z, diagonal):
            lb, lf = _log_terms(z)
            if diagonal:
                lf = jnp.where(after, lf, 0.0)
            return jnp.exp(lb).astype(BF16), _dot(up, _split_cat(lf, 0), NN)

        def stage_weights(z, da, sums, rest, diagonal):
            rest = rest - sums[0:1, :]
            a = jnp.exp(z + sums + rest)
            if diagonal:
                a = jnp.where(after, a, 0.0)
            g = da * a
            return a.astype(BF16), g, _dot(low, g.astype(BF16), NN), rest

        def stage_dz(g, sig, earlier, before, diagonal):
            dlf = earlier + before
            dz = g - sig.astype(F32) * (g + dlf)
            if diagonal:
                dz = jnp.where(after, dz, 0.0)
            return dz.astype(BF16), dlf[blk - 1:blk, :] + g[blk - 1:blk, :]

        def grads(cur, rest, before, diagonal):
            sig, sums = zip(*[stage_sums(cur[h][0], diagonal) for h in range(nh)])
            a, g, earlier, rest = zip(*[stage_weights(*cur[h], sums[h], rest[h], diagonal) for h in range(nh)])
            dz, before = zip(*[stage_dz(g[h], sig[h], earlier[h], before[h], diagonal) for h in range(nh)])
            return dz, a, rest, before

        def flush(j, dz, a, dq):
            j0 = pl.multiple_of(j * blk, blk)
            kj = k_ref[pl.ds(j0, blk), :]
            dq = list(dq)
            dk_new = [jnp.zeros((blk, LANES), F32)] * nlb
            dv_new = [jnp.zeros((blk, LANES), F32)] * nlb
            for h, (lb, half) in enumerate(heads):
                kh = kj[:, lanes[lb]]
                dk_new[lb] = dk_new[lb] + _dot(dz[h], qh[h], NN)
                dv_new[lb] = dv_new[lb] + _dot(a[h], doh[h], NN)
                dq[lb] = dq[lb] + _dot(dz[h], jnp.where(masks[half], kh, jnp.zeros_like(kh)), TN)
            for lb in range(nlb):
                dk_acc[pl.ds(j0, blk), lanes[lb]] += dk_new[lb]
                dv_acc[pl.ds(j0, blk), lanes[lb]] += dv_new[lb]
            return tuple(dq)

        def tile(j, rest, before, dq, diagonal):
            dz, a, rest, before = grads(products(j), rest, before, diagonal)
            return rest, before, flush(j, dz, a, dq)

        zero = jnp.zeros((1, blk), F32)
        carry = lax.fori_loop(0, i, lambda j, carry: tile(j, *carry, False),
                              (tuple(totals), (zero,) * nh, (jnp.zeros((blk, LANES), F32),) * nlb))
        dq_ref[...] = (jnp.concatenate(tile(i, *carry, True)[2], axis=1) * scale).astype(BF16)

        @pl.when(i == nq - 1)
        def _():
            dk_ref[...] = dk_acc[...].astype(BF16)
            dv_ref[...] = dv_acc[...].astype(BF16)

        pl.when((p == nd - 1) & (i == nq - 1))(finish)

    def whole(cb0):
        return pl.BlockSpec((t, width), functools.partial(lambda p, i, cb0: (0, cb0 + p), cb0=cb0))

    tile = pl.BlockSpec((blk, width), lambda p, i: (i, p))
    const = pl.BlockSpec((blk, blk), lambda p, i: (0, 0))
    act = jax.ShapeDtypeStruct((t, d), BF16)
    res = _pcall(
        body, name="attention_bwd", grid=(nd, nq),
        in_specs=[pl.BlockSpec((blk, width), lambda p, i: (i, 3 * nd + p)), whole(4 * nd), whole(5 * nd), tile, tile,
                  pl.BlockSpec((blk, SUM_TERMS * blk), lambda p, i: (0, 0)), const] + [HBM] * npart,
        out_specs=[tile, whole(0), whole(0)] + [HBM] * npart,
        out_shape=[act, act, act] + _exchange_shapes(parts),
        scratch_shapes=[pltpu.SemaphoreType.DMA((npart, 3)), pltpu.SemaphoreType.DMA((npart, 3)),
                        pltpu.VMEM((t, width), F32), pltpu.VMEM((t, width), F32)],
        compiler_params=_params("arbitrary", "arbitrary"),
    )(hin, hin, hin, tot, do, upper, lower, *parts)
    return res[0], res[1], res[2], res[3:]


def _adamw(w, g, m, v, *, name):
    r, c = w.shape
    tr = _tile(r, 256, 8) if r % 8 == 0 and r * c * 4 > (1 << 20) else r
    c1, c2 = 1.0 - ADAM_B1 ** ADAM_STEP, 1.0 - ADAM_B2 ** ADAM_STEP

    def body(w_ref, g_ref, m_ref, v_ref, d_ref, nm_ref, nv_ref):
        gv = g_ref[...]
        nm = ADAM_B1 * m_ref[...] + (1.0 - ADAM_B1) * gv
        nv = ADAM_B2 * v_ref[...] + (1.0 - ADAM_B2) * (gv * gv)
        d_ref[...] = -ADAM_LR * ((nm / c1) / (jnp.sqrt(nv / c2) + ADAM_EPS) + ADAM_WD * w_ref[...])
        nm_ref[...] = nm
        nv_ref[...] = nv

    spec = pl.BlockSpec((tr, c), lambda i: (i, 0))
    shape = jax.ShapeDtypeStruct((r, c), F32)
    return _pcall(
        body, name=name, grid=(r // tr,), in_specs=[spec] * 4, out_specs=[spec] * 3, out_shape=[shape] * 3,
        compiler_params=_params("arbitrary"),
    )(w, g, m, v)


HBM = pl.BlockSpec(memory_space=pltpu.HBM)


def _place():
    x, y, c = lax.axis_index("x"), lax.axis_index("y"), lax.axis_index("c")
    return x, y, c, [(1 - x, y), (x, 1 - y), (1 - x, 1 - y)]


def _window(ref, kind, shard_shape, j, half):
    r, w = shard_shape
    if kind == "col":
        return ref.at[pl.ds(half * (r // 2), r // 2), pl.ds(j * w, w)]
    return ref.at[pl.ds(j * r + half * (r // 2), r // 2), :]


def _full_shape(kind, shard_shape):
    r, w = shard_shape
    return (r, 4 * w) if kind == "col" else (4 * r, w)


def _remote(src, dst, send_sem, recv_sem, device):
    return pltpu.make_async_remote_copy(src_ref=src, dst_ref=dst, send_sem=send_sem, recv_sem=recv_sem,
                                        device_id=device, device_id_type=MESH)


def _gather_phases(big_in, big_out, kinds, shapes, send_sems, recv_sems, local_sems):
    x, y, c, chips = _place()
    me, sibling = 2 * x + y, (x, y, 1 - c)
    n = len(big_in)

    def local(a):
        r, w = shapes[a]
        own = big_out[a].at[:, pl.ds(me * w, w)] if kinds[a] == "col" else big_out[a].at[pl.ds(me * r, r), :]
        return pltpu.make_async_copy(big_in[a], own, local_sems.at[a])

    def over_ici(a, k):
        r = shapes[a][0]
        return _remote(big_in[a].at[pl.ds(c * (r // 2), r // 2), :], _window(big_out[a], kinds[a], shapes[a], me, c),
                       send_sems.at[a, k], recv_sems.at[a, k], (*chips[k], c))

    def landed(a, k, half, slot):
        win = _window(big_out[a], kinds[a], shapes[a], 2 * chips[k][0] + chips[k][1], half)
        return _remote(win, win, send_sems.at[a, slot], recv_sems.at[a, slot], sibling)

    def start():
        for a in range(n):
            local(a).start()
            for k in range(3):
                over_ici(a, k).start()

    def forward():
        for a in range(n):
            for k in range(3):
                landed(a, k, c, k).wait_recv()
                landed(a, k, c, 3 + k).start()

    def finish():
        for a in range(n):
            for k in range(3):
                landed(a, k, 1 - c, 3 + k).wait_recv()
        for a in range(n):
            for k in range(3):
                over_ici(a, k).wait_send()
                landed(a, k, c, 3 + k).wait_send()
            local(a).wait()

    return start, forward, finish


def _gather_scratch(n):
    return [pltpu.SemaphoreType.DMA((n, 6)), pltpu.SemaphoreType.DMA((n, 6)), pltpu.SemaphoreType.DMA((n,))]


def _gather_weights(bigs, kinds, smalls):
    nb, ns = len(bigs), len(smalls)
    shapes = [b.shape for b in bigs]

    def body(*refs):
        big_in, small_in = refs[:nb], refs[nb:nb + ns]
        big_out, small_out = refs[nb + ns:2 * nb + ns], refs[2 * nb + ns:2 * (nb + ns)]
        send_sems, recv_sems, local_sems, small_send, small_recv, small_local = refs[2 * (nb + ns):]
        x, y, c, chips = _place()
        me = 2 * x + y
        start, forward, finish = _gather_phases(big_in, big_out, kinds, shapes, send_sems, recv_sems, local_sems)
        start()
        local, sent = [], []
        for s in range(ns):
            local.append(pltpu.make_async_copy(small_in[s], small_out[s].at[me], small_local.at[s]))
            for k, chip in enumerate(chips):
                sent.append(_remote(small_in[s], small_out[s].at[me], small_send.at[s, k], small_recv.at[s, k], (*chip, c)))
        for cp in local + sent:
            cp.start()
        forward()
        for s in range(ns):
            for k, (px, py) in enumerate(chips):
                dst = small_out[s].at[2 * px + py]
                _remote(dst, dst, small_send.at[s, k], small_recv.at[s, k], (px, py, c)).wait_recv()
        finish()
        for cp in sent:
            cp.wait_send()
        for cp in local:
            cp.wait()

    out_shape = [jax.ShapeDtypeStruct(_full_shape(k, s), b.dtype) for b, k, s in zip(bigs, kinds, shapes)]
    out_shape += [jax.ShapeDtypeStruct((4, *s.shape), s.dtype) for s in smalls]
    return _pcall(
        body, name="gather_weights", in_specs=[HBM] * (nb + ns), out_specs=[HBM] * (nb + ns), out_shape=out_shape,
        scratch_shapes=_gather_scratch(nb) + [pltpu.SemaphoreType.DMA((ns, 3)), pltpu.SemaphoreType.DMA((ns, 3)), pltpu.SemaphoreType.DMA((ns,))],
    )(*bigs, *smalls)


def _swap_other_halves(grads, kinds, shapes, *, name):
    n = len(grads)

    def body(*refs):
        g_in, land = refs[:n], refs[n:2 * n]
        send_sems, recv_sems = refs[2 * n:]
        x, y, c, _ = _place()
        sibling = (x, y, 1 - c)
        sent = []
        for a in range(n):
            for j in range(4):
                cp = _remote(_window(g_in[a], kinds[a], shapes[a], j, 1 - c), land[a].at[j], send_sems.at[a, j], recv_sems.at[a, j], sibling)
                cp.start()
                sent.append(cp)
        for a in range(n):
            for j in range(4):
                _remote(land[a].at[j], land[a].at[j], send_sems.at[a, j], recv_sems.at[a, j], sibling).wait_recv()
        for cp in sent:
            cp.wait_send()

    return _pcall(
        body, name=name, in_specs=[HBM] * n, out_specs=[HBM] * n,
        out_shape=[jax.ShapeDtypeStruct((4, s[0] // 2, s[1]), F32) for s in shapes],
        scratch_shapes=[pltpu.SemaphoreType.DMA((n, 4)), pltpu.SemaphoreType.DMA((n, 4))],
    )(*grads)


def _pair_sum(place, g, land, kind, shard_shape, *, name):
    r, w = shard_shape
    hr = r // 2
    tr = _tile(hr, 256, 16)
    nr = hr // tr

    def body(place_ref, g_ref, l_ref, o_ref, own_ref):
        total = g_ref[...] + l_ref[...]
        o_ref[...] = total.astype(BF16)

        @pl.when(pl.program_id(1) == place_ref[1])
        def _():
            own_ref[...] = total

    if kind == "col":
        g_spec = pl.BlockSpec((tr, w), lambda i, j, p: (p[0] * nr + i, j))
    else:
        g_spec = pl.BlockSpec((tr, w), lambda i, j, p: ((2 * j + p[0]) * nr + i, 0))
    part = pl.BlockSpec((None, tr, w), lambda i, j, p: (j, i, 0))
    return _pcall(
        body, name=name,
        grid_spec=pltpu.PrefetchScalarGridSpec(num_scalar_prefetch=1, grid=(nr, 4), in_specs=[g_spec, part],
                                               out_specs=[part, pl.BlockSpec((tr, w), lambda i, j, p: (i, 0))]),
        out_shape=[jax.ShapeDtypeStruct((4, hr, w), BF16), jax.ShapeDtypeStruct((hr, w), F32)],
        compiler_params=_params("arbitrary", "arbitrary"),
    )(place, g, land)


def _exchange_phases(p_in, land, send_sems, recv_sems):
    x, y, c, chips = _place()

    def copy(a, k):
        px, py = chips[k]
        return _remote(p_in[a].at[2 * px + py], land[a].at[k], send_sems.at[a, k], recv_sems.at[a, k], (px, py, c))

    def start():
        for a in range(len(p_in)):
            for k in range(3):
                copy(a, k).start()

    def finish():
        for a in range(len(p_in)):
            for k in range(3):
                copy(a, k).wait()

    return start, finish


def _exchange_shapes(parts):
    return [jax.ShapeDtypeStruct((3, *p.shape[1:]), p.dtype) for p in parts]


def _chip_sum(place, own, land, *, name):
    hr, w = own.shape
    tr = _tile(hr, 256, 16)
    nr = hr // tr

    def body(place_ref, p_ref, l_ref, o_ref):
        o_ref[...] = ((p_ref[...] + l_ref[0].astype(F32)) + l_ref[1].astype(F32)) + l_ref[2].astype(F32)

    return _pcall(
        body, name=name,
        grid_spec=pltpu.PrefetchScalarGridSpec(
            num_scalar_prefetch=1, grid=(nr,),
            in_specs=[pl.BlockSpec((tr, w), lambda i, p: (i, 0)), pl.BlockSpec((3, tr, w), lambda i, p: (0, i, 0))],
            out_specs=pl.BlockSpec((tr, w), lambda i, p: (p[0] * nr + i, 0))),
        out_shape=jax.ShapeDtypeStruct((2 * hr, w), F32),
        compiler_params=_params("arbitrary"),
    )(place, own, land)


def _swap_reduced_halves(halves):
    n = len(halves)

    def body(*refs):
        src, out = refs[:n], refs[n:2 * n]
        send_sems, recv_sems = refs[2 * n:]
        x, y, c, _ = _place()
        sibling = (x, y, 1 - c)
        sent = []
        for a in range(n):
            hr = out[a].shape[0] // 2
            cp = _remote(src[a].at[pl.ds(c * hr, hr), :], out[a].at[pl.ds(c * hr, hr), :], send_sems.at[a], recv_sems.at[a], sibling)
            cp.start()
            sent.append(cp)
        for a in range(n):
            hr = out[a].shape[0] // 2
            other = out[a].at[pl.ds((1 - c) * hr, hr), :]
            _remote(other, other, send_sems.at[a], recv_sems.at[a], sibling).wait_recv()
        for cp in sent:
            cp.wait_send()

    return _pcall(
        body, name="swap_reduced_halves", in_specs=[HBM] * n, out_specs=[HBM] * n,
        out_shape=[jax.ShapeDtypeStruct(h.shape, F32) for h in halves],
        input_output_aliases={a: a for a in range(n)},
        scratch_shapes=[pltpu.SemaphoreType.DMA((n,)), pltpu.SemaphoreType.DMA((n,))],
    )(*halves)


def _sum_small(packed):
    rows = packed.shape[0]

    def body(in_ref, out_ref, land, send_sems, recv_sems, local_sem):
        x, y, c, _ = _place()
        me = 4 * x + 2 * y + c
        own = pltpu.make_async_copy(in_ref, land.at[me], local_sem)
        own.start()
        flips = [(r >> 2, (r >> 1) & 1, r & 1) for r in range(1, 8)]
        sent = []
        for k, (fx, fy, fc) in enumerate(flips):
            cp = _remote(in_ref, land.at[me], send_sems.at[k], recv_sems.at[k], (x ^ fx, y ^ fy, c ^ fc))
            cp.start()
            sent.append(cp)
        for k, (fx, fy, fc) in enumerate(flips):
            src = land.at[4 * (x ^ fx) + 2 * (y ^ fy) + (c ^ fc)]
            _remote(src, src, send_sems.at[k], recv_sems.at[k], (x ^ fx, y ^ fy, c ^ fc)).wait_recv()
        for cp in sent:
            cp.wait_send()
        own.wait()
        total = land[0]
        for dev in range(1, 8):
            total = total + land[dev]
        out_ref[...] = total

    vmem = pl.BlockSpec(memory_space=pltpu.VMEM)
    return _pcall(
        body, name="sum_small", in_specs=[vmem], out_specs=vmem, out_shape=jax.ShapeDtypeStruct(packed.shape, F32),
        scratch_shapes=[pltpu.VMEM((8, rows, LANES), F32), pltpu.SemaphoreType.DMA((7,)), pltpu.SemaphoreType.DMA((7,)), pltpu.SemaphoreType.DMA],
    )(packed)


def _sigmoid(v):
    return 1.0 / (1.0 + jnp.exp(-v))


def kernel(x, meta_tokens, g_pre_mix, w_in, conv_w_mix, w_proj_conv, w_proj_attn, b_gate, w_out, g_post_mix, g_pre_ffn, w_up_gate, conv_w_ffn, w_down, g_post_ffn, loss_target, m_meta_tokens, m_g_pre_mix, m_w_in, m_conv_w_mix, m_w_proj_conv, m_w_proj_attn, m_b_gate, m_w_out, m_g_post_mix, m_g_pre_ffn, m_w_up_gate, m_conv_w_ffn, m_w_down, m_g_post_ffn, v_meta_tokens, v_g_pre_mix, v_w_in, v_conv_w_mix, v_w_proj_conv, v_w_proj_attn, v_b_gate, v_w_out, v_g_post_mix, v_g_pre_ffn, v_w_up_gate, v_conv_w_ffn, v_w_down, v_g_post_ffn):
    seq, d = x.shape[1], x.shape[2]
    f = w_down.shape[1] * 4
    real = N_META + seq
    t = -(-real // ATT_BLOCK) * ATT_BLOCK
    nd = d // LANES
    cx, cy, cc = lax.axis_index("x"), lax.axis_index("y"), lax.axis_index("c")
    chip = 2 * cx + cy
    place = jnp.stack([cc, chip]).astype(jnp.int32)

    big_names = ["w_in", "w_proj_conv", "w_proj_attn", "w_out", "w_up_gate", "w_down"]
    kinds = ["col", "row", "row", "row", "col", "row"]
    big_w = [w_in[0], w_proj_conv[0], w_proj_attn[0], w_out[0], w_up_gate[0], w_down[0]]
    small_w = [meta_tokens, conv_w_mix[0], b_gate[0], conv_w_ffn[0]]
    shards = [w.astype(BF16) for w in big_w]
    shapes = [w.shape for w in big_w]
    gathered = _gather_weights(shards[:1], kinds[:1], small_w)
    wi = gathered[0]
    meta_f, cwm_f, bg_f, cwf_f = (jnp.moveaxis(s, 0, 1).reshape(s.shape[1], -1) for s in gathered[1:])

    h0 = jnp.concatenate([meta_f, x[0], jnp.zeros((t - real, d), F32)], axis=0)
    target = jnp.concatenate([jnp.zeros((N_META, d), F32), loss_target[0], jnp.zeros((t - real, d), F32)], axis=0)
    xn1 = _prenorm(h0, g_pre_mix, name="prenorm_mix")
    hin = _matmul(xn1, wi, name="in_proj")
    yconv = _mixer_conv_fwd(hin, cwm_f, d)
    o, att_tot, (wpc, wpa, wo, wug, wd) = _attention_fwd(hin, d, shards[1:], kinds[1:])

    def gates(gpre, bias):
        return _sigmoid(gpre[:, :d].astype(F32) + bias[0:1]), _sigmoid(gpre[:, d:].astype(F32) + bias[1:2])

    def merge_epi(accs, rows, vecs, row0):
        gate_c, gate_a = gates(rows[0], vecs[0])
        return [gate_c * accs[0] + gate_a * accs[1], accs[0], accs[1]]

    act = ("row", d, BF16)
    merged, bconv, battn = _matmul_rows([(yconv, wpc), (o, wpa)], [(hin, 2 * d, 3)], [bg_f], merge_epi, [act, act, act], name="branch_proj")

    def mix_epi(accs, rows, vecs, row0):
        mix = accs[0]
        h1 = rows[0] + mix * _rms(mix) * vecs[0]
        return [mix, h1, h1 * _rms(h1) * vecs[1]]

    mix, h1, xn2 = _matmul_rows([(merged, wo)], [(h0, d, 0)], [g_post_mix, g_pre_ffn], mix_epi,
                                [("row", d, F32), ("row", d, F32), act], name="out_proj")
    ug = _matmul(xn2, wug, name="up_proj")
    fact = _ffn_conv_fwd(ug, cwf_f, f)

    def loss_epi(accs, rows, vecs, row0):
        ffn = accs[0]
        r = _rms(ffn)
        fh = ffn * r
        h2 = rows[0] + fh * vecs[0]
        rid = row0 + lax.broadcasted_iota(jnp.int32, (ffn.shape[0], 1), 0)
        err = jnp.where((rid >= N_META) & (rid < real), h2 - rows[1], 0.0)
        dy = err * (1.0 / d)
        dffn, dg = _rms_bwd(dy, fh, r, vecs[0])
        loss = jnp.zeros((1, LANES), F32) + 0.5 * jnp.sum(err * err) / d
        return [dffn, dy, dg, loss]

    dffn, dh2, dg_post_ffn, loss_part = _matmul_rows(
        [(fact, wd)], [(h1, d, 0), (target, d, 0)], [g_post_ffn], loss_epi,
        [act, ("row", d, F32), ("acc", 1, d), ("acc", 1, LANES)], name="down_proj_loss")

    gw_down = _matmul_tn(fact, dffn, name="grad_w_down")
    df = _matmul(dffn, wd, nt=True, name="down_proj_bwd")
    du2, dg2, gcw_ffn = _ffn_conv_bwd(ug, df, cwf_f, f)

    def ffn_in_epi(accs, rows, vecs, row0):
        h1v, dyv, mixv = rows
        r3 = _rms(h1v)
        dh1n, dg3 = _rms_bwd(accs[0], h1v * r3, r3, vecs[0])
        dh1 = dyv + dh1n
        r2 = _rms(mixv)
        dmix, dgm = _rms_bwd(dh1, mixv * r2, r2, vecs[1])
        return [dh1, dmix, dg3, dgm]

    dh1, dmix, dg_pre_ffn, dg_post_mix, dug = _matmul_rows_joined(
        [du2, dg2], wug, [(h1, d, 0), (dh2, d, 0), (mix, d, 0)], [g_pre_ffn, g_post_mix], ffn_in_epi,
        [("row", d, F32), act, ("acc", 1, d), ("acc", 1, d)], tk=f // 2, name="up_proj_bwd")
    gw_up = _matmul_tn(xn2, dug, name="grad_w_up_gate")
    gw_out = _matmul_tn(merged, dmix, name="grad_w_out")

    def merge_bwd_epi(accs, rows, vecs, row0):
        dm = accs[0]
        gate_c, gate_a = gates(rows[0], vecs[0])
        dpre_c = dm * rows[1].astype(F32) * gate_c * (1.0 - gate_c)
        dpre_a = dm * rows[2].astype(F32) * gate_a * (1.0 - gate_a)
        dbias = jnp.concatenate([jnp.sum(dpre_c, axis=0, keepdims=True), jnp.sum(dpre_a, axis=0, keepdims=True)], axis=0)
        return [dm * gate_c, dm * gate_a, jnp.concatenate([dpre_c, dpre_a], axis=1), dbias]

    dbconv, dbattn, dgates, gb_gate = _matmul_rows(
        [(dmix, wo)], [(hin, 2 * d, 3), (bconv, d, 0), (battn, d, 0)], [bg_f], merge_bwd_epi,
        [act, act, ("row", 2 * d, BF16), ("acc", 2, d)], nt=True, name="out_proj_bwd")
    gw_pc = _matmul_tn(yconv, dbconv, name="grad_w_proj_conv")
    gw_pa = _matmul_tn(o, dbattn, name="grad_w_proj_attn")
    dyconv = _matmul(dbconv, wpc, nt=True, name="proj_conv_bwd")
    do = _matmul(dbattn, wpa, nt=True, name="proj_attn_bwd")
    db, dc, dhc, gcw_mix = _mixer_conv_bwd(hin, dyconv, cwm_f, d)

    def pair_sums(grads, first):
        landed = _swap_other_halves(grads, kinds[first:first + len(grads)], shapes[first:first + len(grads)],
                                    name=f"swap_other_halves_{big_names[first]}")
        return zip(*[_pair_sum(place, g, l, kinds[first + a], shapes[first + a], name=f"pair_sum_{big_names[first + a]}")
                     for a, (g, l) in enumerate(zip(grads, landed))])

    parts, own = pair_sums([gw_pc, gw_pa, gw_out, gw_up, gw_down], 1)
    dq, dk, dv, landed = _attention_bwd(hin, att_tot, do, d, parts)

    def in_epi(accs, rows, vecs, row0):
        h0v, dh1v = rows
        r1 = _rms(h0v)
        dh0n, dg1 = _rms_bwd(accs[0], h0v * r1, r1, vecs[0])
        return [dh1v + dh0n, dg1]

    dhin = [db, dc, dhc, dq, dk, dv, dgates]
    gw_in = _matmul_tn_segments(xn1, dhin, name="grad_w_in")
    parts_in, own_in = pair_sums([gw_in], 0)
    dh0, dg_pre_mix, landed_in = _matmul_rows_joined(
        dhin, wi, [(h0, d, 0), (dh1, d, 0)], [g_pre_mix], in_epi,
        [("row", d, F32), ("acc", 1, d)], tk=d, name="in_proj_bwd", joined=False, parts=parts_in)
    grad_x = dh0[N_META:real][None]

    landed = [landed_in] + list(landed)
    halves = [_chip_sum(place, p, l, name=f"chip_sum_{n}") for p, l, n in zip(list(own_in) + list(own), landed, big_names)]
    big_r = _swap_reduced_halves(halves)

    small_g = [dh0[:N_META], dg_pre_mix, gcw_mix, gb_gate, dg_post_mix, dg_pre_ffn, gcw_ffn, dg_post_ffn, loss_part]
    flat = jnp.concatenate([s.reshape(-1) for s in small_g])
    pad = -flat.shape[0] % (8 * LANES)
    summed = _sum_small(jnp.pad(flat, (0, pad)).reshape(-1, LANES)).reshape(-1)
    small_r, pos = [], 0
    for s in small_g:
        small_r.append(summed[pos:pos + s.size].reshape(s.shape))
        pos += s.size
    g_meta, g_g1, g_cwm, g_bg, g_g2, g_g3, g_cwf, g_g4, loss_row = small_r
    loss = loss_row[0, 0]

    def my_cols(full, width):
        return lax.dynamic_slice_in_dim(full, chip * width, width, axis=1)

    grads = {
        "meta_tokens": my_cols(g_meta, d // 4), "g_pre_mix": g_g1, "w_in": big_r[0][None],
        "conv_w_mix": my_cols(g_cwm, d // 4)[None], "w_proj_conv": big_r[1][None], "w_proj_attn": big_r[2][None],
        "b_gate": my_cols(g_bg, d // 4)[None], "w_out": big_r[3][None], "g_post_mix": g_g2, "g_pre_ffn": g_g3,
        "w_up_gate": big_r[4][None], "conv_w_ffn": my_cols(g_cwf, f // 4)[None], "w_down": big_r[5][None], "g_post_ffn": g_g4,
    }
    weights = dict(meta_tokens=meta_tokens, g_pre_mix=g_pre_mix, w_in=w_in, conv_w_mix=conv_w_mix, w_proj_conv=w_proj_conv,
                   w_proj_attn=w_proj_attn, b_gate=b_gate, w_out=w_out, g_post_mix=g_post_mix, g_pre_ffn=g_pre_ffn,
                   w_up_gate=w_up_gate, conv_w_ffn=conv_w_ffn, w_down=w_down, g_post_ffn=g_post_ffn)
    m_in = dict(meta_tokens=m_meta_tokens, g_pre_mix=m_g_pre_mix, w_in=m_w_in, conv_w_mix=m_conv_w_mix, w_proj_conv=m_w_proj_conv,
                w_proj_attn=m_w_proj_attn, b_gate=m_b_gate, w_out=m_w_out, g_post_mix=m_g_post_mix, g_pre_ffn=m_g_pre_ffn,
                w_up_gate=m_w_up_gate, conv_w_ffn=m_conv_w_ffn, w_down=m_w_down, g_post_ffn=m_g_post_ffn)
    v_in = dict(meta_tokens=v_meta_tokens, g_pre_mix=v_g_pre_mix, w_in=v_w_in, conv_w_mix=v_conv_w_mix, w_proj_conv=v_w_proj_conv,
                w_proj_attn=v_w_proj_attn, b_gate=v_b_gate, w_out=v_w_out, g_post_mix=v_g_post_mix, g_pre_ffn=v_g_pre_ffn,
                w_up_gate=v_w_up_gate, conv_w_ffn=v_conv_w_ffn, w_down=v_w_down, g_post_ffn=v_g_post_ffn)
    names = list(weights)
    deltas, new_m, new_v = [], [], []
    for n in names:
        shape = weights[n].shape
        two_d = (-1, shape[-1])
        dl, nm, nv = _adamw(weights[n].reshape(two_d), grads[n].reshape(two_d), m_in[n].reshape(two_d), v_in[n].reshape(two_d),
                            name=f"adamw_{n}")
        deltas.append(dl.reshape(shape))
        new_m.append(nm.reshape(shape))
        new_v.append(nv.reshape(shape))
    return (loss, grad_x, *[grads[n].reshape(weights[n].shape) for n in names], *deltas, *new_m, *new_v)
```
